```python
import math
import jax, jax.numpy as jnp
from jax import lax
import numpy as np

D_MODEL = 1024
BATCH = 2
SEQ = 8192
DEPTH = 2
DEC_BATCH = 32
DEC_SEQ = 4
PAST_LEN = 8192
PAGE_SIZE = 128

N_MIXERS = 4
GROUP_W = D_MODEL // N_MIXERS
MIX_W = N_MIXERS * GROUP_W
LRU_W = GROUP_W
LRU_HEADS = 4
LRU_HD = LRU_W // LRU_HEADS
CONV_W = 4
LRU_C = 8.0
S5_W = GROUP_W
S5_GROUP = 16
S5_NG = S5_W // S5_GROUP
S5_N = 64
FOX_HEADS = 4
FOX_HD = GROUP_W // FOX_HEADS
DSA_HEADS = 4
DSA_HD = GROUP_W // DSA_HEADS
IDX_HEADS = 8
IDX_HD = 32
DSA_TOPK_MAX = 256
Q_BLOCK = 128
FFN_HIDDEN = -(-8 * D_MODEL // (3 * 256)) * 256
NEG_INF = -1e30
IN_WIDTHS = (LRU_W, LRU_W,
             S5_W,
             GROUP_W, GROUP_W, GROUP_W, FOX_HEADS,
             GROUP_W, GROUP_W, GROUP_W,
             IDX_HEADS * IDX_HD, IDX_HD, IDX_HEADS)
P_IN = sum(IN_WIDTHS)
IN_SPLITS = tuple(int(s) for s in np.cumsum(IN_WIDTHS)[:-1])

kernel_name = 'hybrid_rglru_s5_fox_dsa_step'

F32 = jnp.float32


def rms_norm(x, g, eps=1e-6):
    xf = x.astype(F32)
    xf = xf * lax.rsqrt(jnp.mean(xf * xf, axis=-1, keepdims=True) + eps)
    return (xf * g.astype(F32)).astype(x.dtype)


def alibi_slopes():
    return 2.0 ** (-8.0 * jnp.arange(1, DSA_HEADS + 1, dtype=F32) / DSA_HEADS)


def causal_conv(x, buf, w, b):
    T = x.shape[1]
    xp = jnp.concatenate([buf.astype(x.dtype), x], axis=1)
    y = b.astype(F32) + xp[:, 0:T].astype(F32) * w[0].astype(F32)
    for j in range(1, CONV_W):
        y = y + xp[:, j:j + T].astype(F32) * w[j].astype(F32)
    return y, xp[:, -(CONV_W - 1):]


def linear_scan(a, b, h0):
    def comb(l, r):
        return (l[0] * r[0], r[0] * l[1] + r[1])
    a_cum, h = lax.associative_scan(comb, (a, b), axis=1)
    return h + a_cum * h0[:, None]


def complex_scan(a_re, a_im, b_re, b_im, h0_re, h0_im):
    def comb(l, r):
        lar, lai, lbr, lbi = l
        rar, rai, rbr, rbi = r
        return (lar * rar - lai * rai, lar * rai + lai * rar,
                rar * lbr - rai * lbi + rbr, rar * lbi + rai * lbr + rbi)
    ar, ai, hr, hi = lax.associative_scan(comb, (a_re, a_im, b_re, b_im), axis=1)
    h0r, h0i = h0_re[:, None], h0_im[:, None]
    return hr + ar * h0r - ai * h0i, hi + ar * h0i + ai * h0r


def rglru(x, h0, wr, br, wi, bi, lam):
    Bsz, T, W = x.shape
    xh = x.reshape(Bsz, T, LRU_HEADS, LRU_HD)
    r = jax.nn.sigmoid(jnp.einsum('bthi,hij->bthj', xh, wr.astype(F32)).reshape(Bsz, T, W) + br.astype(F32))
    i = jax.nn.sigmoid(jnp.einsum('bthi,hij->bthj', xh, wi.astype(F32)).reshape(Bsz, T, W) + bi.astype(F32))
    log_a = -LRU_C * r * jax.nn.softplus(-lam.astype(F32))
    a = jnp.exp(log_a)
    b = jnp.sqrt(-jnp.expm1(2.0 * log_a)) * (i * x)
    h = linear_scan(a, b, h0.astype(F32))
    return h, h[:, -1]


def s5_ssm(u, h0_re, h0_im, log_dt, a_re, a_im, b_re, b_im, c_re, c_im, d, glu_w, glu_b):
    Bsz, T, _ = u.shape
    uf = u.astype(F32).reshape(Bsz, T, S5_NG, S5_GROUP)
    dt = jnp.exp(log_dt.astype(F32))[:, None]
    ar, ai = a_re.astype(F32), a_im.astype(F32)
    mag = jnp.exp(dt * ar)
    abar_re, abar_im = mag * jnp.cos(dt * ai), mag * jnp.sin(dt * ai)
    den = ar * ar + ai * ai
    nr, ni = abar_re - 1.0, abar_im
    coef_re = (nr * ar + ni * ai) / den
    coef_im = (ni * ar - nr * ai) / den
    br_, bi_ = b_re.astype(F32), b_im.astype(F32)
    bbar_re = coef_re[..., None] * br_ - coef_im[..., None] * bi_
    bbar_im = coef_re[..., None] * bi_ + coef_im[..., None] * br_
    bu_re = jnp.einsum('btgj,gnj->btgn', uf, bbar_re)
    bu_im = jnp.einsum('btgj,gnj->btgn', uf, bbar_im)
    hr, hi = complex_scan(jnp.broadcast_to(abar_re, bu_re.shape), jnp.broadcast_to(abar_im, bu_im.shape),
                          bu_re, bu_im, h0_re.astype(F32), h0_im.astype(F32))
    y = (jnp.einsum('btgn,gjn->btgj', hr, c_re.astype(F32))
         - jnp.einsum('btgn,gjn->btgj', hi, c_im.astype(F32))
         + d.astype(F32).reshape(S5_NG, S5_GROUP) * uf)
    g = jax.nn.gelu(y.reshape(Bsz, T, S5_W))
    out = g * jax.nn.sigmoid(g @ glu_w.astype(F32) + glu_b.astype(F32))
    return out, hr[:, -1], hi[:, -1]


def fox_attend(q, fq, q_pos, k, v, fk, k_pos):
    s = jnp.einsum('bqhd,bkhd->bhqk', q, k, preferred_element_type=F32) * FOX_HD ** -0.5
    s = s + jnp.moveaxis(fq, 2, 1)[..., :, None] - jnp.moveaxis(fk, 2, 1)[..., None, :]
    s = jnp.where(k_pos[None, :] <= q_pos[:, None], s, NEG_INF)
    p = jax.nn.softmax(s, axis=-1)
    return jnp.einsum('bhqk,bkhd->bqhd', p.astype(v.dtype), v, preferred_element_type=F32)


def dsa_attend(q, qi, wi, q_pos, k, v, ki, k_pos, topk):
    dots = jnp.einsum('bqhe,bke->bqhk', qi, ki, preferred_element_type=F32) * IDX_HD ** -0.5
    score = jnp.einsum('bqhk,bqh->bqk', jax.nn.relu(dots), wi)
    score = jnp.where(k_pos[None, :] <= q_pos[:, None], score, NEG_INF)
    _, idx = lax.top_k(score, topk)
    sel_pos = k_pos[idx]
    ok = sel_pos <= q_pos[None, :, None]
    take = jax.vmap(lambda a, ix: a[ix])
    k_sel = take(k, idx)
    v_sel = take(v, idx)
    s = jnp.einsum('bqhd,bqkhd->bhqk', q, k_sel, preferred_element_type=F32) * DSA_HD ** -0.5
    dist = (q_pos[None, :, None] - sel_pos).astype(F32)
    s = s - alibi_slopes()[None, :, None, None] * dist[:, None]
    s = jnp.where(ok[:, None], s, NEG_INF)
    p = jax.nn.softmax(s, axis=-1)
    return jnp.einsum('bhqk,bqkhd->bqhd', p.astype(v_sel.dtype), v_sel, preferred_element_type=F32)


def blocked_queries(fn, q_args, q_pos):
    T = q_pos.shape[0]
    qb = Q_BLOCK if T % Q_BLOCK == 0 else T
    nb = T // qb
    def split(a):
        return jnp.moveaxis(a.reshape((a.shape[0], nb, qb) + a.shape[2:]), 1, 0)
    out = lax.map(lambda a: fn(*a[0], a[1]), (tuple(split(a) for a in q_args), q_pos.reshape(nb, qb)))
    out = jnp.moveaxis(out, 0, 1)
    return out.reshape((out.shape[0], T) + out.shape[3:])


def token_mixers(h, lp, past):
    Bsz, T, _ = h.shape
    P = past['fox_k'].shape[1]
    q_pos = P + jnp.arange(T, dtype=jnp.int32)
    k_pos = jnp.arange(P + T, dtype=jnp.int32)
    proj = h @ lp['w_in']
    (a_x, a_g, b_u, c_q, c_k, c_v, c_f, d_q, d_k, d_v, i_q, i_k, i_w) = jnp.split(proj, IN_SPLITS, axis=-1)

    a_xc, conv_buf = causal_conv(a_x, past['lru_conv'], lp['lru_conv_w'], lp['lru_conv_b'])
    a_h, lru_last = rglru(a_xc, past['lru_h'], lp['lru_wr'], lp['lru_br'], lp['lru_wi'], lp['lru_bi'], lp['lru_lambda'])
    y_a = a_h * jax.nn.gelu(a_g.astype(F32))

    y_b, s5_re, s5_im = s5_ssm(b_u, past['s5_re'], past['s5_im'], lp['s5_log_dt'], lp['s5_a_re'], lp['s5_a_im'],
                               lp['s5_b_re'], lp['s5_b_im'], lp['s5_c_re'], lp['s5_c_im'], lp['s5_d'],
                               lp['s5_glu_w'], lp['s5_glu_b'])

    cq = c_q.reshape(Bsz, T, FOX_HEADS, FOX_HD)
    ck = c_k.reshape(Bsz, T, FOX_HEADS, FOX_HD).astype(past['fox_k'].dtype)
    cv = c_v.reshape(Bsz, T, FOX_HEADS, FOX_HD).astype(past['fox_v'].dtype)
    logf = jax.nn.log_sigmoid(c_f.astype(F32) + lp['fox_f_bias'].astype(F32))
    ck_all = jnp.concatenate([past['fox_k'], ck], axis=1)
    cv_all = jnp.concatenate([past['fox_v'], cv], axis=1)
    F = jnp.cumsum(jnp.concatenate([past['fox_logf'].astype(F32), logf], axis=1), axis=1)
    Fq = F[:, P:]
    y_c = blocked_queries(lambda q, fq, qp: fox_attend(q, fq, qp, ck_all, cv_all, F, k_pos), (cq, Fq), q_pos)

    dq = d_q.reshape(Bsz, T, DSA_HEADS, DSA_HD)
    dk = d_k.reshape(Bsz, T, DSA_HEADS, DSA_HD).astype(past['dsa_k'].dtype)
    dv = d_v.reshape(Bsz, T, DSA_HEADS, DSA_HD).astype(past['dsa_v'].dtype)
    iq = i_q.reshape(Bsz, T, IDX_HEADS, IDX_HD)
    ik = i_k.astype(past['dsa_kidx'].dtype)
    iw = i_w.astype(F32) * IDX_HEADS ** -0.5
    dk_all = jnp.concatenate([past['dsa_k'], dk], axis=1)
    dv_all = jnp.concatenate([past['dsa_v'], dv], axis=1)
    ik_all = jnp.concatenate([past['dsa_kidx'], ik], axis=1)
    topk = max(1, min(DSA_TOPK_MAX, (P + T) // 4))
    y_d = blocked_queries(lambda q, qi, wi, qp: dsa_attend(q, qi, wi, qp, dk_all, dv_all, ik_all, k_pos, topk),
                          (dq, iq, iw), q_pos)

    y = jnp.concatenate([y_a, y_b, y_c.reshape(Bsz, T, GROUP_W), y_d.reshape(Bsz, T, GROUP_W)], axis=-1)
    y = rms_norm(y.reshape(Bsz, T, N_MIXERS, GROUP_W), lp['grp_norm'].reshape(N_MIXERS, GROUP_W))
    out = y.reshape(Bsz, T, MIX_W).astype(h.dtype) @ lp['w_out']
    new_state = dict(lru_h=lru_last, lru_conv=conv_buf, s5_re=s5_re, s5_im=s5_im,
                     fox_k=ck, fox_v=cv, fox_logf=logf.astype(past['fox_logf'].dtype),
                     dsa_k=dk, dsa_v=dv, dsa_kidx=ik)
    return out, new_state


def trunk_layer(x, c, lp, past):
    mod = jax.nn.silu(c) @ lp['ada_w'] + lp['ada_b']
    sh1, sc1, g1, sh2, sc2, g2 = jnp.split(mod[:, None, :], 6, axis=-1)
    h = rms_norm(x, lp['norm_mix_pre']) * (1.0 + sc1) + sh1
    y, new_state = token_mixers(h, lp, past)
    x = x + g1 * rms_norm(y, lp['norm_mix_post'])
    h = rms_norm(x, lp['norm_ffn_pre']) * (1.0 + sc2) + sh2
    f = (jax.nn.silu(h @ lp['ffn_w_gate']) * (h @ lp['ffn_w_up'])) @ lp['ffn_w_down']
    x = x + g2 * rms_norm(f, lp['norm_ffn_post'])
    return x, new_state


def empty_past(bsz, dtype):
    return dict(lru_h=jnp.zeros((bsz, LRU_W), F32),
                lru_conv=jnp.zeros((bsz, CONV_W - 1, LRU_W), dtype),
                s5_re=jnp.zeros((bsz, S5_NG, S5_N), F32),
                s5_im=jnp.zeros((bsz, S5_NG, S5_N), F32),
                fox_k=jnp.zeros((bsz, 0, FOX_HEADS, FOX_HD), dtype),
                fox_v=jnp.zeros((bsz, 0, FOX_HEADS, FOX_HD), dtype),
                fox_logf=jnp.zeros((bsz, 0, FOX_HEADS), dtype),
                dsa_k=jnp.zeros((bsz, 0, DSA_HEADS, DSA_HD), dtype),
                dsa_v=jnp.zeros((bsz, 0, DSA_HEADS, DSA_HD), dtype),
                dsa_kidx=jnp.zeros((bsz, 0, IDX_HD), dtype))


def setup_inputs(seed: int = 0) -> dict:
    key = jax.random.key(seed)
    ks = iter(jax.random.split(key, 64))
    def nrm(shape, scale):
        return jax.random.normal(next(ks), shape, F32) * scale
    def unif(shape, lo, hi):
        return jax.random.uniform(next(ks), shape, F32, lo, hi)
    n_pages = PAST_LEN // PAGE_SIZE
    n_used = DEC_BATCH * n_pages
    n_pool = n_used + max(1, n_used // 4)
    L = DEPTH
    inp = {}
    inp['x_prompt'] = nrm((BATCH, SEQ, D_MODEL), 1.0)
    inp['x_sample'] = nrm((DEC_BATCH, DEC_SEQ, D_MODEL), 1.0)
    inp['state_lru_h'] = nrm((L, DEC_BATCH, LRU_W), 0.5)
    inp['state_lru_conv'] = nrm((L, DEC_BATCH, CONV_W - 1, LRU_W), 1.0)
    inp['state_s5_re'] = nrm((L, DEC_BATCH, S5_NG, S5_N), 0.5)
    inp['state_s5_im'] = nrm((L, DEC_BATCH, S5_NG, S5_N), 0.5)
    inp['cache_fox_k'] = nrm((L, n_pool, PAGE_SIZE, FOX_HEADS, FOX_HD), 1.0)
    inp['cache_fox_v'] = nrm((L, n_pool, PAGE_SIZE, FOX_HEADS, FOX_HD), 1.0)
    inp['cache_fox_logf'] = jax.nn.log_sigmoid(3.5 + nrm((L, n_pool, PAGE_SIZE, FOX_HEADS), 1.0))
    inp['cache_dsa_k'] = nrm((L, n_pool, PAGE_SIZE, DSA_HEADS, DSA_HD), 1.0)
    inp['cache_dsa_v'] = nrm((L, n_pool, PAGE_SIZE, DSA_HEADS, DSA_HD), 1.0)
    inp['cache_dsa_kidx'] = nrm((L, n_pool, PAGE_SIZE, IDX_HD), 1.0)
    perm = jax.random.permutation(next(ks), n_pool)
    inp['page_table'] = perm[:n_used].reshape(DEC_BATCH, n_pages).astype(jnp.int32)
    inp['c_prompt'] = nrm((BATCH, D_MODEL), 1.0)
    inp['c_sample'] = nrm((DEC_BATCH, D_MODEL), 1.0)
    inp['ada_w'] = nrm((L, D_MODEL, 6 * D_MODEL), 0.5 * D_MODEL ** -0.5)
    inp['ada_b'] = nrm((L, 6 * D_MODEL), 0.01)
    inp['norm_mix_pre'] = 1.0 + nrm((L, D_MODEL), 0.01)
    inp['norm_mix_post'] = 1.0 + nrm((L, D_MODEL), 0.01)
    inp['norm_ffn_pre'] = 1.0 + nrm((L, D_MODEL), 0.01)
    inp['norm_ffn_post'] = 1.0 + nrm((L, D_MODEL), 0.01)
    inp['w_in'] = nrm((L, D_MODEL, P_IN), D_MODEL ** -0.5)
    inp['lru_conv_w'] = nrm((L, CONV_W, LRU_W), CONV_W ** -0.5)
    inp['lru_conv_b'] = nrm((L, LRU_W), 0.01)
    inp['lru_wr'] = nrm((L, LRU_HEADS, LRU_HD, LRU_HD), LRU_HD ** -0.5)
    inp['lru_br'] = nrm((L, LRU_W), 0.01)
    inp['lru_wi'] = nrm((L, LRU_HEADS, LRU_HD, LRU_HD), LRU_HD ** -0.5)
    inp['lru_bi'] = nrm((L, LRU_W), 0.01)
    a8 = unif((L, LRU_W), 0.9, 0.999) ** (1.0 / LRU_C)
    inp['lru_lambda'] = jnp.log(a8) - jnp.log1p(-a8)
    inp['s5_log_dt'] = unif((L, S5_NG), math.log(0.001), math.log(0.1))
    inp['s5_a_re'] = -0.5 + nrm((L, S5_NG, S5_N), 0.01)
    inp['s5_a_im'] = math.pi * jnp.arange(S5_N, dtype=F32) + nrm((L, S5_NG, S5_N), 0.01)
    inp['s5_b_re'] = nrm((L, S5_NG, S5_N, S5_GROUP), S5_GROUP ** -0.5)
    inp['s5_b_im'] = nrm((L, S5_NG, S5_N, S5_GROUP), S5_GROUP ** -0.5)
    inp['s5_c_re'] = nrm((L, S5_NG, S5_GROUP, S5_N), S5_N ** -0.5)
    inp['s5_c_im'] = nrm((L, S5_NG, S5_GROUP, S5_N), S5_N ** -0.5)
    inp['s5_d'] = nrm((L, S5_W), 1.0)
    inp['s5_glu_w'] = nrm((L, S5_W, S5_W), S5_W ** -0.5)
    inp['s5_glu_b'] = nrm((L, S5_W), 0.01)
    inp['fox_f_bias'] = unif((L, FOX_HEADS), 2.0, 5.0)
    inp['grp_norm'] = 1.0 + nrm((L, MIX_W), 0.01)
    inp['w_out'] = nrm((L, MIX_W, D_MODEL), MIX_W ** -0.5)
    inp['ffn_w_gate'] = nrm((L, D_MODEL, FFN_HIDDEN), D_MODEL ** -0.5)
    inp['ffn_w_up'] = nrm((L, D_MODEL, FFN_HIDDEN), D_MODEL ** -0.5)
    inp['ffn_w_down'] = nrm((L, FFN_HIDDEN, D_MODEL), FFN_HIDDEN ** -0.5)
    return inp


def reference(x_prompt, x_sample, state_lru_h, state_lru_conv, state_s5_re, state_s5_im,
              cache_fox_k, cache_fox_v, cache_fox_logf, cache_dsa_k, cache_dsa_v, cache_dsa_kidx,
              page_table, c_prompt, c_sample,
              ada_w, ada_b, norm_mix_pre, norm_mix_post, norm_ffn_pre, norm_ffn_post, w_in,
              lru_conv_w, lru_conv_b, lru_wr, lru_br, lru_wi, lru_bi, lru_lambda,
              s5_log_dt, s5_a_re, s5_a_im, s5_b_re, s5_b_im, s5_c_re, s5_c_im, s5_d, s5_glu_w, s5_glu_b,
              fox_f_bias, grp_norm, w_out, ffn_w_gate, ffn_w_up, ffn_w_down):
    dec_b, n_pages = page_table.shape
    past_len = n_pages * PAGE_SIZE

    def gather_pages(pool):
        rows = pool[page_table]
        return rows.reshape((dec_b, past_len) + pool.shape[2:])

    xp, xs = x_prompt, x_sample
    st_p, st_s = [], []
    for l in range(DEPTH):
        lp = dict(ada_w=ada_w[l], ada_b=ada_b[l], norm_mix_pre=norm_mix_pre[l], norm_mix_post=norm_mix_post[l],
                  norm_ffn_pre=norm_ffn_pre[l], norm_ffn_post=norm_ffn_post[l], w_in=w_in[l],
                  lru_conv_w=lru_conv_w[l], lru_conv_b=lru_conv_b[l], lru_wr=lru_wr[l], lru_br=lru_br[l],
                  lru_wi=lru_wi[l], lru_bi=lru_bi[l], lru_lambda=lru_lambda[l],
                  s5_log_dt=s5_log_dt[l], s5_a_re=s5_a_re[l], s5_a_im=s5_a_im[l], s5_b_re=s5_b_re[l],
                  s5_b_im=s5_b_im[l], s5_c_re=s5_c_re[l], s5_c_im=s5_c_im[l], s5_d=s5_d[l],
                  s5_glu_w=s5_glu_w[l], s5_glu_b=s5_glu_b[l], fox_f_bias=fox_f_bias[l], grp_norm=grp_norm[l],
                  w_out=w_out[l], ffn_w_gate=ffn_w_gate[l], ffn_w_up=ffn_w_up[l], ffn_w_down=ffn_w_down[l])
        past_p = empty_past(x_prompt.shape[0], x_prompt.dtype)
        past_s = dict(lru_h=state_lru_h[l], lru_conv=state_lru_conv[l],
                      s5_re=state_s5_re[l], s5_im=state_s5_im[l],
                      fox_k=gather_pages(cache_fox_k[l]), fox_v=gather_pages(cache_fox_v[l]),
                      fox_logf=gather_pages(cache_fox_logf[l]),
                      dsa_k=gather_pages(cache_dsa_k[l]), dsa_v=gather_pages(cache_dsa_v[l]),
                      dsa_kidx=gather_pages(cache_dsa_kidx[l]))
        xp, new_p = trunk_layer(xp, c_prompt, lp, past_p)
        xs, new_s = trunk_layer(xs, c_sample, lp, past_s)
        st_p.append(new_p)
        st_s.append(new_s)

    def stk(outs, name):
        return jnp.stack([o[name] for o in outs])

    return (xp, xs,
            stk(st_p, 'lru_h'), stk(st_s, 'lru_h'),
            stk(st_p, 'lru_conv'), stk(st_s, 'lru_conv'),
            stk(st_p, 's5_re'), stk(st_s, 's5_re'),
            stk(st_p, 's5_im'), stk(st_s, 's5_im'),
            stk(st_p, 'fox_k'), stk(st_s, 'fox_k'),
            stk(st_p, 'fox_v'), stk(st_s, 'fox_v'),
            stk(st_p, 'fox_logf'), stk(st_s, 'fox_logf'),
            stk(st_p, 'dsa_k'), stk(st_s, 'dsa_k'),
            stk(st_p, 'dsa_v'), stk(st_s, 'dsa_v'),
            stk(st_p, 'dsa_kidx'), stk(st_s, 'dsa_kidx'))
```

```python
import functools
import math

import jax
import jax.numpy as jnp
from jax import lax
from jax.experimental import pallas as pl
from jax.experimental.pallas import tpu as pltpu

F32 = jnp.float32
BF16 = jnp.bfloat16

N_MIXERS = 4
GROUP_W = 256
LRU_HEADS = 4
CONV_W = 4
LRU_C = 8.0
S5_GROUP = 16
S5_NG = 16
S5_N = 64
S5_STATE = S5_NG * S5_N
ATT_HEADS = 4
ATT_HD = 64
IDX_HEADS = 8
IDX_HD = 32
DSA_TOPK_MAX = 256
PAGE = 128
NEG_INF = -1e30
M_INIT = -1e20
BIG = 3e38
ADM_CUT = -1e29
EPS = 1e-6
ATT_SCALE = ATT_HD ** -0.5
IDX_SCALE = IDX_HD ** -0.5
IDX_W_SCALE = IDX_HEADS ** -0.5
ALIBI = tuple(2.0 ** (-8.0 * (h + 1) / ATT_HEADS) for h in range(ATT_HEADS))
BISECT_ITERS = 26

LANES = 128
SUBLANES = 8
VMEM_LIMIT = 56 * 1024 * 1024

C_AX, C_AG, C_BU, C_CQ, C_CK, C_CV, C_DQ, C_DK, C_DV = (i * GROUP_W for i in range(9))
P_CAT = 9 * GROUP_W
C_IQ = 0
C_IK = C_IQ + IDX_HEADS * LANES
C_SM = C_IK + LANES
P_IDX = C_SM + LANES
SM_IK, SM_CF, SM_IW = 0, IDX_HD, IDX_HD + ATT_HEADS

NT_DIMS = (((1,), (1,)), ((), ()))


def _nt(a, b):
    return lax.dot_general(a, b, NT_DIMS, preferred_element_type=F32)


def _dot(a, b):
    return jnp.dot(a, b, preferred_element_type=F32)


def _rms(x, g):
    return x * lax.rsqrt(jnp.mean(x * x, axis=-1, keepdims=True) + EPS) * g


def _softplus(z):
    return jnp.maximum(z, 0.0) + jnp.log1p(jnp.exp(-jnp.abs(z)))


def _log_sigmoid(z):
    return -_softplus(-z)


def _expm1(z):
    e = jnp.exp(z)
    one = e == 1.0
    return jnp.where(one, z, (e - 1.0) * z / jnp.where(one, 1.0, jnp.log(e)))


def _sigmoid(z):
    return jax.nn.sigmoid(z)


def _gelu(z):
    return jax.nn.gelu(z)


def _params(sem):
    return pltpu.CompilerParams(dimension_semantics=sem, vmem_limit_bytes=VMEM_LIMIT)


def _head_mask(h, width=GROUP_W):
    lane = lax.broadcasted_iota(jnp.int32, (1, width), 1)
    return (lane >= h * ATT_HD) & (lane < (h + 1) * ATT_HD)


def _lane_cumsum(x):
    lane = lax.broadcasted_iota(jnp.int32, x.shape, 1)
    s = 1
    while s < x.shape[1]:
        x = x + jnp.where(lane >= s, pltpu.roll(x, s, axis=1), 0.0)
        s *= 2
    return x


def _ada_kernel(c_ref, w_ref, b_ref, o_ref):
    c = c_ref[...]
    s = c * _sigmoid(c)
    o_ref[...] = _dot(s.astype(BF16), w_ref[...]) + b_ref[...]


def _ada(c_all, ada_w_b, ada_b):
    L, D, D6 = ada_w_b.shape
    Bp = c_all.shape[0]
    tn = 1024
    return pl.pallas_call(
        _ada_kernel,
        grid=(L, D6 // tn),
        in_specs=[pl.BlockSpec((Bp, D), lambda l, j: (0, 0)),
                  pl.BlockSpec((None, D, tn), lambda l, j: (l, 0, j)),
                  pl.BlockSpec((None, 1, tn), lambda l, j: (l, 0, j))],
        out_specs=pl.BlockSpec((None, Bp, tn), lambda l, j: (l, 0, j)),
        out_shape=jax.ShapeDtypeStruct((L, Bp, D6), F32),
        compiler_params=_params(("arbitrary", "arbitrary")),
        name="ada",
    )(c_all, ada_w_b, ada_b.reshape(L, 1, D6))


def _inproj_kernel(x_ref, sh_ref, sc_ref, g_ref, w_ref, whi_ref, wlo_ref,
                   ax_ref, ag_ref, bu_ref, cq_ref, ckf_ref, ckb_ref, cvf_ref, cvb_ref,
                   dq_ref, dkf_ref, dkb_ref, dvf_ref, dvb_ref, iq_ref, ik_ref, sm_ref):
    x = x_ref[...]
    h = _rms(x, g_ref[...]) * (1.0 + sc_ref[...]) + sh_ref[...]
    hb = h.astype(BF16)

    def mm(c0, width):
        return _dot(hb, w_ref[:, c0:c0 + width])

    ax_ref[...] = mm(C_AX, GROUP_W)
    ag_ref[...] = mm(C_AG, GROUP_W)
    bu_ref[...] = mm(C_BU, GROUP_W)
    cq_ref[...] = mm(C_CQ, GROUP_W).astype(BF16)
    ck = mm(C_CK, GROUP_W)
    ckf_ref[...] = ck
    ckb_ref[...] = ck.astype(BF16)
    cv = mm(C_CV, GROUP_W)
    cvf_ref[...] = cv
    cvb_ref[...] = cv.astype(BF16)
    dq_ref[...] = mm(C_DQ, GROUP_W).astype(BF16)
    dk = mm(C_DK, GROUP_W)
    dkf_ref[...] = dk
    dkb_ref[...] = dk.astype(BF16)
    dv = mm(C_DV, GROUP_W)
    dvf_ref[...] = dv
    dvb_ref[...] = dv.astype(BF16)

    hl = (h - hb.astype(F32)).astype(BF16)

    def mm3(c0, width):
        whi = whi_ref[:, c0:c0 + width]
        return _dot(hb, whi) + _dot(hl, whi) + _dot(hb, wlo_ref[:, c0:c0 + width])

    sm_ref[...] = mm3(C_SM, LANES)

    sub = (lax.broadcasted_iota(jnp.int32, (1, LANES), 1) // IDX_HD)
    for hh in range(IDX_HEADS):
        r = mm3(C_IQ + hh * LANES, LANES)
        hi = r.astype(BF16)
        lo = (r - hi.astype(F32)).astype(BF16)
        zero = jnp.zeros_like(hi)
        iq_ref[:, hh * LANES:(hh + 1) * LANES] = jnp.where(sub == 1, lo, jnp.where(sub == 3, zero, hi))
    r = mm3(C_IK, LANES)
    hi = r.astype(BF16)
    lo = (r - hi.astype(F32)).astype(BF16)
    ik_ref[...] = jnp.where(sub == 2, lo, jnp.where(sub == 3, jnp.zeros_like(hi), hi))


def _mod_spec(rows_mod, D, tiles_per_mod, piece):
    return pl.BlockSpec((None, rows_mod, D), lambda i: (i // tiles_per_mod, 0, piece))


def _inproj(x, mod, tiles_per_mod, g, wcat, widx_hi, widx_lo, tm):
    N, D = x.shape
    rows_mod = mod.shape[1]
    tok = lambda w: pl.BlockSpec((tm, w), lambda i: (i, 0))
    widths_dtypes = [(GROUP_W, F32)] * 3 + [(GROUP_W, BF16), (GROUP_W, F32), (GROUP_W, BF16), (GROUP_W, F32),
                                            (GROUP_W, BF16), (GROUP_W, BF16), (GROUP_W, F32), (GROUP_W, BF16),
                                            (GROUP_W, F32), (GROUP_W, BF16), (IDX_HEADS * LANES, BF16),
                                            (LANES, BF16), (LANES, F32)]
    return pl.pallas_call(
        _inproj_kernel,
        grid=(N // tm,),
        in_specs=[tok(D), _mod_spec(rows_mod, D, tiles_per_mod, 0), _mod_spec(rows_mod, D, tiles_per_mod, 1),
                  pl.BlockSpec((1, D), lambda i: (0, 0)),
                  pl.BlockSpec((D, P_CAT), lambda i: (0, 0)),
                  pl.BlockSpec((D, P_IDX), lambda i: (0, 0)),
                  pl.BlockSpec((D, P_IDX), lambda i: (0, 0))],
        out_specs=[tok(w) for w, _ in widths_dtypes],
        out_shape=[jax.ShapeDtypeStruct((N, w), dt) for w, dt in widths_dtypes],
        compiler_params=_params(("arbitrary",)),
        name="inproj",
    )(x, mod, mod, g, wcat, widx_hi, widx_lo)


def _lru_kernel(ax_ref, ag_ref, buf_ref, h0_ref, cw_ref, cb_ref, wr_ref, br_ref, wi_ref, bi_ref, lam_ref,
                ya_ref, hl_ref, xs, hcar, *, tc, last_row):
    c = pl.program_id(1)

    @pl.when(c == 0)
    def _():
        xs[0:SUBLANES, :] = buf_ref[...]
        hcar[...] = h0_ref[...]

    x = ax_ref[...]
    xs[SUBLANES:SUBLANES + tc, :] = x
    w = cw_ref[...]
    y = cb_ref[...] + xs[5:5 + tc, :] * w[0:1] + xs[6:6 + tc, :] * w[1:2] + xs[7:7 + tc, :] * w[2:3] + x * w[3:4]
    xs[0:SUBLANES, :] = xs[tc:tc + SUBLANES, :]

    yb = y.astype(BF16)
    r = _sigmoid(_dot(yb, wr_ref[...]) + br_ref[...])
    i = _sigmoid(_dot(yb, wi_ref[...]) + bi_ref[...])
    log_a = -LRU_C * r * _softplus(-lam_ref[...])
    a = jnp.exp(log_a)
    b = jnp.sqrt(-_expm1(2.0 * log_a)) * (i * y)

    row = lax.broadcasted_iota(jnp.int32, (tc, GROUP_W), 0)
    s = 1
    while s < tc:
        keep = row >= s
        a_sh = jnp.where(keep, pltpu.roll(a, s, axis=0), 1.0)
        b_sh = jnp.where(keep, pltpu.roll(b, s, axis=0), 0.0)
        b = b + a * b_sh
        a = a * a_sh
        s *= 2
    h = b + a * hcar[...]
    hcar[...] = h[tc - 1:tc, :]
    ya_ref[...] = h * _gelu(ag_ref[...])

    @pl.when(c == pl.num_programs(1) - 1)
    def _():
        hl_ref[...] = h[last_row:last_row + 1, :]


def _lru(ax, ag, buf8, h0, lw, tc, t_real):
    B, Tp, W = ax.shape
    seq = pl.BlockSpec((None, tc, W), lambda b, c: (b, c, 0))
    full = lambda shape: pl.BlockSpec(shape, lambda b, c: (0,) * len(shape))
    return pl.pallas_call(
        functools.partial(_lru_kernel, tc=tc, last_row=(t_real - 1) % tc),
        grid=(B, Tp // tc),
        in_specs=[seq, seq,
                  pl.BlockSpec((None, SUBLANES, W), lambda b, c: (b, 0, 0)),
                  pl.BlockSpec((None, 1, W), lambda b, c: (b, 0, 0)),
                  full((CONV_W, W)), full((1, W)), full((W, W)), full((1, W)), full((W, W)), full((1, W)),
                  full((1, W))],
        out_specs=[seq, pl.BlockSpec((None, 1, W), lambda b, c: (b, 0, 0))],
        out_shape=[jax.ShapeDtypeStruct((B, Tp, W), F32), jax.ShapeDtypeStruct((B, 1, W), F32)],
        scratch_shapes=[pltpu.VMEM((tc + SUBLANES, W), F32), pltpu.VMEM((1, W), F32)],
        compiler_params=_params(("arbitrary", "arbitrary")),
        name="lru",
    )(ax, ag, buf8, h0, lw["conv_w"], lw["conv_b"], lw["wr"], lw["br"], lw["wi"], lw["bi"], lw["lam"])


def _s5_kernel(u_ref, h0r_ref, h0i_ref, ldt_ref, are_ref, aim_ref, bre_ref, bim_ref, cre_ref, cim_ref,
               d_ref, gw_ref, gb_ref, y_ref, hlr_ref, hli_ref, hr_s, hi_s, car_r, car_i, *, tc, last_row):
    c = pl.program_id(1)

    @pl.when(c == 0)
    def _():
        car_r[...] = h0r_ref[...]
        car_i[...] = h0i_ref[...]

    dt = jnp.exp(ldt_ref[...])
    ar, ai = are_ref[...], aim_ref[...]
    mag = jnp.exp(dt * ar)
    abr, abi = mag * jnp.cos(dt * ai), mag * jnp.sin(dt * ai)
    den = ar * ar + ai * ai
    nr, ni = abr - 1.0, abi
    cr = (nr * ar + ni * ai) / den
    ci = (ni * ar - nr * ai) / den

    u = u_ref[...]
    ub = u.astype(BF16)
    pre = _dot(ub, bre_ref[...])
    pim = _dot(ub, bim_ref[...])
    hr_s[...] = cr * pre - ci * pim
    hi_s[...] = cr * pim + ci * pre

    def step(t, carry):
        hr, hi = carry
        nhr = abr * hr - abi * hi + hr_s[pl.ds(t, 1), :]
        nhi = abr * hi + abi * hr + hi_s[pl.ds(t, 1), :]
        hr_s[pl.ds(t, 1), :] = nhr
        hi_s[pl.ds(t, 1), :] = nhi
        return nhr, nhi

    hr, hi = lax.fori_loop(0, tc, step, (car_r[...], car_i[...]), unroll=8)
    car_r[...] = hr
    car_i[...] = hi

    hrv, hiv = hr_s[...], hi_s[...]
    y = _dot(hrv.astype(BF16), cre_ref[...]) - _dot(hiv.astype(BF16), cim_ref[...]) + d_ref[...] * u
    g = _gelu(y)
    y_ref[...] = g * _sigmoid(_dot(g.astype(BF16), gw_ref[...]) + gb_ref[...])

    @pl.when(c == pl.num_programs(1) - 1)
    def _():
        hlr_ref[...] = hr_s[last_row:last_row + 1, :]
        hli_ref[...] = hi_s[last_row:last_row + 1, :]


def _s5(u, h0r, h0i, sw, tc, t_real):
    B, Tp, W = u.shape
    S = S5_STATE
    seq = pl.BlockSpec((None, tc, W), lambda b, c: (b, c, 0))
    st = pl.BlockSpec((None, 1, S), lambda b, c: (b, 0, 0))
    full = lambda shape: pl.BlockSpec(shape, lambda b, c: (0,) * len(shape))
    return pl.pallas_call(
        functools.partial(_s5_kernel, tc=tc, last_row=(t_real - 1) % tc),
        grid=(B, Tp // tc),
        in_specs=[seq, st, st, full((1, S)), full((1, S)), full((1, S)), full((W, S)), full((W, S)),
                  full((S, W)), full((S, W)), full((1, W)), full((W, W)), full((1, W))],
        out_specs=[seq, st, st],
        out_shape=[jax.ShapeDtypeStruct((B, Tp, W), F32), jax.ShapeDtypeStruct((B, 1, S), F32),
                   jax.ShapeDtypeStruct((B, 1, S), F32)],
        scratch_shapes=[pltpu.VMEM((tc, S), F32), pltpu.VMEM((tc, S), F32),
                        pltpu.VMEM((1, S), F32), pltpu.VMEM((1, S), F32)],
        compiler_params=_params(("arbitrary", "arbitrary")),
        name="s5",
    )(u, h0r, h0i, sw["ldt"], sw["are"], sw["aim"], sw["bre"], sw["bim"], sw["cre"], sw["cim"],
      sw["d"], sw["gw"], sw["gb"])


def _fcum_kernel(cf_ref, fb_ref, lf_ref, f_ref, *, nb):
    lf = _log_sigmoid(cf_ref[...] + fb_ref[...])
    lf_ref[...] = lf
    cs = _lane_cumsum(lf)
    tot = jnp.broadcast_to(cs[:, LANES - 1:LANES], cs.shape)
    row = lax.broadcasted_iota(jnp.int32, cs.shape, 0) % nb
    inc = tot
    s = 1
    while s < nb:
        inc = inc + jnp.where(row >= s, pltpu.roll(inc, s, axis=0), 0.0)
        s *= 2
    f_ref[...] = cs + (inc - tot)


def _fcum(cf_rows, fb_rows, nb):
    B, R, _ = cf_rows.shape
    blk = pl.BlockSpec((None, R, LANES), lambda b: (b, 0, 0))
    return pl.pallas_call(
        functools.partial(_fcum_kernel, nb=nb),
        grid=(B,),
        in_specs=[blk, pl.BlockSpec((R, 1), lambda b: (0, 0))],
        out_specs=[blk, blk],
        out_shape=[jax.ShapeDtypeStruct((B, R, LANES), F32)] * 2,
        compiler_params=_params(("arbitrary",)),
        name="fcum",
    )(cf_rows, fb_rows)


def _fox_kernel(q_ref, k_ref, v_ref, fcol_ref, frow_ref, o_ref, qm_s, acc_s, m_s, l_s, *, tq, tk):
    qi, kj = pl.program_id(1), pl.program_id(2)

    @pl.when(kj == 0)
    def _():
        q = q_ref[...]
        for h in range(ATT_HEADS):
            qm_s[h] = jnp.where(_head_mask(h), q, jnp.zeros_like(q))
        acc_s[...] = jnp.zeros_like(acc_s)
        m_s[...] = jnp.full_like(m_s, M_INIT)
        l_s[...] = jnp.zeros_like(l_s)

    @pl.when(kj * tk <= qi * tq + tq - 1)
    def _():
        k, v = k_ref[...], v_ref[...]
        row = qi * tq + lax.broadcasted_iota(jnp.int32, (tq, tk), 0)
        col = kj * tk + lax.broadcasted_iota(jnp.int32, (tq, tk), 1)
        causal = col <= row
        pv = jnp.zeros((tq, GROUP_W), F32)
        af = jnp.zeros((tq, GROUP_W), F32)
        for h in range(ATT_HEADS):
            hm = _head_mask(h)
            s = _nt(qm_s[h], k) * ATT_SCALE
            s = s + fcol_ref[:, h:h + 1] - frow_ref[h:h + 1, :]
            s = jnp.where(causal, s, NEG_INF)
            m_prev = m_s[h]
            m_new = jnp.maximum(m_prev, jnp.max(s, axis=1, keepdims=True))
            p = jnp.exp(s - m_new)
            alpha = jnp.exp(m_prev - m_new)
            l_s[h] = alpha * l_s[h] + jnp.sum(p, axis=1, keepdims=True)
            m_s[h] = m_new
            pv = pv + _dot(p.astype(BF16), jnp.where(hm, v, jnp.zeros_like(v)))
            af = jnp.where(hm, alpha, af)
        acc_s[...] = acc_s[...] * af + pv

    @pl.when(kj == pl.num_programs(2) - 1)
    def _():
        lf = jnp.zeros((tq, GROUP_W), F32)
        for h in range(ATT_HEADS):
            lf = jnp.where(_head_mask(h), l_s[h], lf)
        o_ref[...] = acc_s[...] / lf


def _fox(q, k, v, fcol, frow, tq, tk):
    B, T, W = q.shape
    last = lambda i: (i * tq + tq - 1) // tk
    qspec = pl.BlockSpec((None, tq, W), lambda b, i, j: (b, i, 0))
    kspec = pl.BlockSpec((None, tk, W), lambda b, i, j: (b, jnp.minimum(j, last(i)), 0))
    return pl.pallas_call(
        functools.partial(_fox_kernel, tq=tq, tk=tk),
        grid=(B, T // tq, T // tk),
        in_specs=[qspec, kspec, kspec,
                  pl.BlockSpec((None, tq, ATT_HEADS), lambda b, i, j: (b, i, 0)),
                  pl.BlockSpec((None, ATT_HEADS, tk), lambda b, i, j: (b, 0, jnp.minimum(j, last(i))))],
        out_specs=qspec,
        out_shape=jax.ShapeDtypeStruct((B, T, W), F32),
        scratch_shapes=[pltpu.VMEM((ATT_HEADS, tq, W), BF16), pltpu.VMEM((tq, W), F32),
                        pltpu.VMEM((ATT_HEADS, tq, 1), F32), pltpu.VMEM((ATT_HEADS, tq, 1), F32)],
        compiler_params=_params(("arbitrary", "arbitrary", "arbitrary")),
        name="fox",
    )(q, k, v, fcol, frow)


def _topk_to_bias(S, nblk, kb, R, k, nadm, w_bits):
    nt = kb // LANES
    kf = float(k)

    def tiles(j):
        k0 = pl.multiple_of(j * kb, kb)
        blk = S[:, pl.ds(k0, kb)]
        return [blk[:, t * LANES:(t + 1) * LANES] for t in range(nt)], k0

    def reduce_blocks(fn, init):
        def body(j, c):
            ts, k0 = tiles(j)
            for t, x in enumerate(ts):
                c = fn(c, x, k0 + t * LANES)
            return c
        return lax.fori_loop(0, nblk, body, init)

    def count(pred):
        c = reduce_blocks(lambda c, x, c0: c + jnp.where(pred(x, c0), 1.0, 0.0), jnp.zeros((R, LANES), F32))
        return jnp.sum(c, axis=1, keepdims=True)

    mx, mn = reduce_blocks(
        lambda c, x, c0: (jnp.maximum(c[0], x), jnp.minimum(c[1], jnp.where(x > ADM_CUT, x, BIG))),
        (jnp.full((R, LANES), -BIG, F32), jnp.full((R, LANES), BIG, F32)))
    rmax = jnp.max(mx, axis=1, keepdims=True)
    rmin = jnp.min(mn, axis=1, keepdims=True)
    small = nadm <= kf

    def bis(_, c):
        lo, hi = c
        mid = 0.5 * (lo + hi)
        ge = count(lambda x, c0: x >= mid) >= kf
        return jnp.where(ge, mid, lo), jnp.where(ge, hi, mid)

    lo, hi = lax.fori_loop(0, BISECT_ITERS, bis, (rmin, rmax + jnp.maximum(1.0, jnp.abs(rmax))))

    def unsat(cnt):
        return (cnt < kf) & jnp.logical_not(small)

    def fix_cond(c):
        return jnp.max(jnp.where(unsat(c[1]), 1.0, 0.0)) > 0.5

    def fix_body(c):
        thr, cnt = c
        below = reduce_blocks(lambda m, x, c0: jnp.maximum(m, jnp.where(x < thr, x, -BIG)),
                              jnp.full((R, LANES), -BIG, F32))
        nthr = jnp.where(unsat(cnt), jnp.max(below, axis=1, keepdims=True), thr)
        return nthr, count(lambda x, c0: x >= nthr)

    thr, cge = lax.while_loop(fix_cond, fix_body, (hi, jnp.zeros((R, 1), F32)))
    thr = jnp.where(small, rmin, thr)
    cgt = count(lambda x, c0: x > thr)
    need = kf - cgt
    ties = (cge > kf) & jnp.logical_not(small)
    lane = lax.broadcasted_iota(jnp.int32, (R, LANES), 1)
    big_col = jnp.int32(2 ** 30)

    def tie_search(_):
        def it(_, c):
            lo_i, hi_i = c
            mid = (lo_i + hi_i) >> 1
            ok = count(lambda x, c0: (x == thr) & (c0 + lane <= mid)) >= need
            return jnp.where(ok, lo_i, mid), jnp.where(ok, mid, hi_i)
        width = nblk * kb
        _, hi_i = lax.fori_loop(0, w_bits + 1, it,
                                (jnp.full((R, 1), -1, jnp.int32), jnp.zeros((R, 1), jnp.int32) + (width - 1)))
        return jnp.where(ties, hi_i, big_col)

    any_ties = jnp.max(jnp.where(ties, 1.0, 0.0)) > 0.5
    jcut = lax.cond(any_ties, tie_search, lambda _: jnp.full((R, 1), big_col, jnp.int32), 0)

    def write(j, _):
        ts, k0 = tiles(j)
        out = []
        for t, x in enumerate(ts):
            sel = (x > thr) | ((x == thr) & (k0 + t * LANES + lane <= jcut))
            out.append(jnp.where(sel, 0.0, NEG_INF))
        S[:, pl.ds(k0, kb)] = jnp.concatenate(out, axis=1)
        return 0

    lax.fori_loop(0, nblk, write, 0)


def _dsa_kernel(iq_ref, ik_ref, sm_ref, q_ref, k_ref, v_ref, o_ref, S, qm_s, acc_s, m_s, l_s,
                *, tq, kb, topk, w_bits):
    qi = pl.program_id(1)
    q0 = qi * tq
    nkb = (q0 + tq - 1) // kb + 1
    rowpos = q0 + lax.broadcasted_iota(jnp.int32, (tq, kb), 0)
    coloff = lax.broadcasted_iota(jnp.int32, (tq, kb), 1)
    wv = sm_ref[...] * IDX_W_SCALE

    def scores(j, _):
        k0 = pl.multiple_of(j * kb, kb)
        kblk = ik_ref[pl.ds(k0, kb), :]
        sc = jnp.zeros((tq, kb), F32)
        for hh in range(IDX_HEADS):
            d = _nt(iq_ref[:, hh * LANES:(hh + 1) * LANES], kblk)
            sc = sc + wv[:, SM_IW + hh:SM_IW + hh + 1] * jnp.maximum(d * IDX_SCALE, 0.0)
        S[:, pl.ds(k0, kb)] = jnp.where(k0 + coloff <= rowpos, sc, NEG_INF)
        return 0

    lax.fori_loop(0, nkb, scores, 0)

    nadm = (q0 + 1 + lax.broadcasted_iota(jnp.int32, (tq, 1), 0)).astype(F32)
    _topk_to_bias(S, nkb, kb, tq, topk, nadm, w_bits)

    q = q_ref[...]
    for h in range(ATT_HEADS):
        qm_s[h] = jnp.where(_head_mask(h), q, jnp.zeros_like(q))
    acc_s[...] = jnp.zeros_like(acc_s)
    m_s[...] = jnp.full_like(m_s, M_INIT)
    l_s[...] = jnp.zeros_like(l_s)

    def attend(j, _):
        k0 = pl.multiple_of(j * kb, kb)
        kblk = k_ref[pl.ds(k0, kb), :]
        vblk = v_ref[pl.ds(k0, kb), :]
        bias = S[:, pl.ds(k0, kb)]
        dist = (rowpos - (k0 + coloff)).astype(F32)
        pv = jnp.zeros((tq, GROUP_W), F32)
        af = jnp.zeros((tq, GROUP_W), F32)
        for h in range(ATT_HEADS):
            hm = _head_mask(h)
            s = _nt(qm_s[h], kblk) * ATT_SCALE + (bias - ALIBI[h] * dist)
            m_prev = m_s[h]
            m_new = jnp.maximum(m_prev, jnp.max(s, axis=1, keepdims=True))
            p = jnp.exp(s - m_new)
            alpha = jnp.exp(m_prev - m_new)
            l_s[h] = alpha * l_s[h] + jnp.sum(p, axis=1, keepdims=True)
            m_s[h] = m_new
            pv = pv + _dot(p.astype(BF16), jnp.where(hm, vblk, jnp.zeros_like(vblk)))
            af = jnp.where(hm, alpha, af)
        acc_s[...] = acc_s[...] * af + pv
        return 0

    lax.fori_loop(0, nkb, attend, 0)
    lf = jnp.zeros((tq, GROUP_W), F32)
    for h in range(ATT_HEADS):
        lf = jnp.where(_head_mask(h), l_s[h], lf)
    o_ref[...] = acc_s[...] / lf


def _dsa(iqp, ikp, sm, q, k, v, tq, kb, topk):
    B, T, W = q.shape
    tok = lambda w: pl.BlockSpec((None, tq, w), lambda b, i: (b, i, 0))
    res = lambda w: pl.BlockSpec((None, T, w), lambda b, i: (b, 0, 0))
    return pl.pallas_call(
        functools.partial(_dsa_kernel, tq=tq, kb=kb, topk=topk, w_bits=max(1, math.ceil(math.log2(T)))),
        grid=(B, T // tq),
        in_specs=[tok(IDX_HEADS * LANES), res(LANES), tok(LANES), tok(W), res(W), res(W)],
        out_specs=tok(W),
        out_shape=jax.ShapeDtypeStruct((B, T, W), F32),
        scratch_shapes=[pltpu.VMEM((tq, T), F32), pltpu.VMEM((ATT_HEADS, tq, W), BF16), pltpu.VMEM((tq, W), F32),
                        pltpu.VMEM((ATT_HEADS, tq, 1), F32), pltpu.VMEM((ATT_HEADS, tq, 1), F32)],
        compiler_params=_params(("arbitrary", "arbitrary")),
        name="dsa",
    )(iqp, ikp, sm, q, k, v)


def _sa_kernel(pt_ref, q_ref, w_ref, cfn_ref, fb_ref, ikn_ref, *rest, pps, tn):
    ki_refs, lf_refs = rest[:pps], rest[pps:2 * pps]
    s_ref, f_ref, sn_ref, fn_ref, lfn_ref, carry_s = rest[2 * pps:]
    step = pl.program_id(1)

    @pl.when(step == 0)
    def _():
        carry_s[...] = jnp.zeros_like(carry_s)

    qp = q_ref[...]
    qh = qp[:, 0:IDX_HD]
    ql = qp[:, IDX_HD:2 * IDX_HD]
    w = w_ref[...] * IDX_W_SCALE
    nq = qp.shape[0] // IDX_HEADS

    def scores(kf):
        kh = kf.astype(BF16)
        kl = (kf - kh.astype(F32)).astype(BF16)
        d = _nt(qh, kh) + _nt(ql, kh) + _nt(qh, kl)
        r = jnp.maximum(d * IDX_SCALE, 0.0) * w
        return jnp.sum(r.reshape(nq, IDX_HEADS, kf.shape[0]), axis=1)

    carry = carry_s[...]
    for i in range(pps):
        s_ref[:, i * PAGE:(i + 1) * PAGE] = scores(ki_refs[i][...])
        cs = _lane_cumsum(lf_refs[i][...]) + carry
        f_ref[:, i * PAGE:(i + 1) * PAGE] = cs
        carry = cs[:, PAGE - 1:PAGE]
    carry_s[...] = carry

    @pl.when(step == pl.num_programs(1) - 1)
    def _():
        lane = lax.broadcasted_iota(jnp.int32, (ATT_HEADS, PAGE), 1)
        lfn = _log_sigmoid(cfn_ref[...] + fb_ref[...])
        lfn_ref[...] = lfn
        fn_ref[...] = _lane_cumsum(jnp.where(lane < tn, lfn, 0.0)) + carry
        col = lax.broadcasted_iota(jnp.int32, (nq, PAGE), 1)
        rowq = lax.broadcasted_iota(jnp.int32, (nq, PAGE), 0)
        sn_ref[...] = jnp.where((col <= rowq) & (col < tn), scores(ikn_ref[...]), NEG_INF)


def _sa(page_table, l, qrows, wrows, cfn, fb, ikn, kidx_pool, lfT_pool, pps, tn):
    B, n_pages = page_table.shape
    nsteps = n_pages // pps
    P = n_pages * PAGE
    nq = qrows.shape[1] // IDX_HEADS
    per_b = lambda shape: pl.BlockSpec((None,) + shape, lambda b, s, pt: (b,) + (0,) * len(shape))
    ki_specs = [pl.BlockSpec((None, None, PAGE, IDX_HD), lambda b, s, pt, i=i: (l, pt[b, s * pps + i], 0, 0))
                for i in range(pps)]
    lf_specs = [pl.BlockSpec((None, None, ATT_HEADS, PAGE), lambda b, s, pt, i=i: (l, pt[b, s * pps + i], 0, 0))
                for i in range(pps)]
    gs = pltpu.PrefetchScalarGridSpec(
        num_scalar_prefetch=1,
        grid=(B, nsteps),
        in_specs=[per_b((nq * IDX_HEADS, LANES)), per_b((nq * IDX_HEADS, 1)), per_b((ATT_HEADS, PAGE)),
                  pl.BlockSpec((ATT_HEADS, 1), lambda b, s, pt: (0, 0)), per_b((PAGE, IDX_HD))] + ki_specs + lf_specs,
        out_specs=[pl.BlockSpec((None, nq, pps * PAGE), lambda b, s, pt: (b, 0, s)),
                   pl.BlockSpec((None, ATT_HEADS, pps * PAGE), lambda b, s, pt: (b, 0, s)),
                   per_b((nq, PAGE)), per_b((ATT_HEADS, PAGE)), per_b((ATT_HEADS, PAGE))],
        scratch_shapes=[pltpu.VMEM((ATT_HEADS, 1), F32)])
    return pl.pallas_call(
        functools.partial(_sa_kernel, pps=pps, tn=tn),
        grid_spec=gs,
        out_shape=[jax.ShapeDtypeStruct((B, nq, P), F32), jax.ShapeDtypeStruct((B, ATT_HEADS, P), F32),
                   jax.ShapeDtypeStruct((B, nq, PAGE), F32), jax.ShapeDtypeStruct((B, ATT_HEADS, PAGE), F32),
                   jax.ShapeDtypeStruct((B, ATT_HEADS, PAGE), F32)],
        compiler_params=_params(("arbitrary", "arbitrary")),
        name="sample_scores",
    )(page_table, qrows, wrows, cfn, fb, ikn, *([kidx_pool] * pps), *([lfT_pool] * pps))


def _sb_kernel(s_ref, o_ref, *, kb, nblk, topk, past, nq, w_bits):
    R = s_ref.shape[0]
    o_ref[...] = s_ref[...]
    nadm = (past + 1 + lax.broadcasted_iota(jnp.int32, (R, 1), 0) % nq).astype(F32)
    _topk_to_bias(o_ref, nblk, kb, R, topk, nadm, w_bits)


def _sb(s_all, topk, past, nq):
    R, Wt = s_all.shape
    nl = Wt // LANES
    div = max(d for d in range(1, 9) if nl % d == 0)
    kb = div * LANES
    return pl.pallas_call(
        functools.partial(_sb_kernel, kb=kb, nblk=Wt // kb, topk=topk, past=past, nq=nq,
                          w_bits=max(1, math.ceil(math.log2(Wt)))),
        grid=(1,),
        in_specs=[pl.BlockSpec((R, Wt), lambda i: (0, 0))],
        out_specs=pl.BlockSpec((R, Wt), lambda i: (0, 0)),
        out_shape=jax.ShapeDtypeStruct((R, Wt), F32),
        compiler_params=_params(("arbitrary",)),
        name="sample_topk",
    )(s_all)


def _sc_kernel(pt_ref, fq_ref, dq_ref, fqc_ref, fp_ref, bp_ref, fn_ref, bn_ref,
               ckn_ref, cvn_ref, dkn_ref, dvn_ref, *rest, pps, past, tn, nq):
    fk_refs, fv_refs = rest[0:pps], rest[pps:2 * pps]
    dk_refs, dv_refs = rest[2 * pps:3 * pps], rest[3 * pps:4 * pps]
    yc_ref, yd_ref, qf_s, qd_s, accf, mf, lf, accd, md, ld = rest[4 * pps:]
    step = pl.program_id(1)
    R = ATT_HEADS * SUBLANES

    def stack_heads(q8):
        return jnp.concatenate([jnp.where(_head_mask(h), q8, jnp.zeros_like(q8)) for h in range(ATT_HEADS)], axis=0)

    @pl.when(step == 0)
    def _():
        qf_s[...] = stack_heads(fq_ref[...])
        qd_s[...] = stack_heads(dq_ref[...])
        for acc, m, l in ((accf, mf, lf), (accd, md, ld)):
            acc[...] = jnp.zeros_like(acc)
            m[...] = jnp.full_like(m, M_INIT)
            l[...] = jnp.zeros_like(l)

    def update(q_s, kb, vb, bias, acc, m, l):
        s = _nt(q_s[...], kb) * ATT_SCALE + bias
        m_prev = m[...]
        m_new = jnp.maximum(m_prev, jnp.max(s, axis=1, keepdims=True))
        p = jnp.exp(s - m_new)
        alpha = jnp.exp(m_prev - m_new)
        l[...] = alpha * l[...] + jnp.sum(p, axis=1, keepdims=True)
        m[...] = m_new
        acc[...] = alpha * acc[...] + _dot(p.astype(BF16), vb)

    def per_head_rows(x4):
        return jnp.concatenate([jnp.broadcast_to(x4[h:h + 1, :], (SUBLANES, x4.shape[1]))
                                for h in range(ATT_HEADS)], axis=0)

    def per_query_rows(xq):
        r8 = lax.broadcasted_iota(jnp.int32, (SUBLANES, xq.shape[1]), 0)
        x8 = jnp.zeros((SUBLANES, xq.shape[1]), F32)
        for qq in range(nq):
            x8 = jnp.where(r8 == qq, jnp.broadcast_to(xq[qq:qq + 1, :], x8.shape), x8)
        return jnp.concatenate([x8] * ATT_HEADS, axis=0)

    rowi = lax.broadcasted_iota(jnp.int32, (R, 1), 0)
    qidx = rowi % SUBLANES
    slope = jnp.zeros((R, 1), F32)
    for h in range(ATT_HEADS):
        slope = jnp.where(rowi // SUBLANES == h, ALIBI[h], slope)

    def blocks(refs):
        return jnp.concatenate([r[...].astype(BF16) for r in refs], axis=0)

    kw = pps * PAGE
    col = lax.broadcasted_iota(jnp.int32, (R, kw), 1)
    update(qf_s, blocks(fk_refs), blocks(fv_refs), fqc_ref[...] - per_head_rows(fp_ref[...]), accf, mf, lf)
    dist = (past + qidx - (step * kw + col)).astype(F32)
    update(qd_s, blocks(dk_refs), blocks(dv_refs), per_query_rows(bp_ref[...]) - slope * dist, accd, md, ld)

    @pl.when(step == pl.num_programs(1) - 1)
    def _():
        coln = lax.broadcasted_iota(jnp.int32, (R, PAGE), 1)
        ok = (coln <= qidx) & (coln < tn)
        bias_f = jnp.where(ok, fqc_ref[...] - per_head_rows(fn_ref[...]), NEG_INF)
        update(qf_s, ckn_ref[...], cvn_ref[...], bias_f, accf, mf, lf)
        distn = (qidx - coln).astype(F32)
        bias_d = jnp.where(coln < tn, per_query_rows(bn_ref[...]), NEG_INF) - slope * distn
        update(qd_s, dkn_ref[...], dvn_ref[...], bias_d, accd, md, ld)
        for acc, l, out in ((accf, lf, yc_ref), (accd, ld, yd_ref)):
            o = acc[...] / l[...]
            y = jnp.zeros((SUBLANES, GROUP_W), F32)
            for h in range(ATT_HEADS):
                y = jnp.where(_head_mask(h), o[h * SUBLANES:(h + 1) * SUBLANES, :], y)
            out[...] = y


def _sc(page_table, l, fq8, dq8, fqc, f_past, b_past, f_new, b_new, ckn, cvn, dkn, dvn,
        fk_pool, fv_pool, dk_pool, dv_pool, pps, tn):
    B, n_pages = page_table.shape
    nsteps = n_pages // pps
    past = n_pages * PAGE
    nq = b_past.shape[1]
    W = GROUP_W
    R = ATT_HEADS * SUBLANES
    per_b = lambda shape: pl.BlockSpec((None,) + shape, lambda b, s, pt: (b,) + (0,) * len(shape))
    page = lambda i: pl.BlockSpec((None, None, PAGE, W), lambda b, s, pt, i=i: (l, pt[b, s * pps + i], 0, 0))
    pages = [page(i) for i in range(pps)]
    gs = pltpu.PrefetchScalarGridSpec(
        num_scalar_prefetch=1,
        grid=(B, nsteps),
        in_specs=[per_b((SUBLANES, W)), per_b((SUBLANES, W)), per_b((R, 1)),
                  pl.BlockSpec((None, ATT_HEADS, pps * PAGE), lambda b, s, pt: (b, 0, s)),
                  pl.BlockSpec((None, nq, pps * PAGE), lambda b, s, pt: (b, 0, s)),
                  per_b((ATT_HEADS, PAGE)), per_b((nq, PAGE)),
                  per_b((PAGE, W)), per_b((PAGE, W)), per_b((PAGE, W)), per_b((PAGE, W))] + pages * 4,
        out_specs=[per_b((SUBLANES, W)), per_b((SUBLANES, W))],
        scratch_shapes=[pltpu.VMEM((R, W), BF16), pltpu.VMEM((R, W), BF16),
                        pltpu.VMEM((R, W), F32), pltpu.VMEM((R, 1), F32), pltpu.VMEM((R, 1), F32),
                        pltpu.VMEM((R, W), F32), pltpu.VMEM((R, 1), F32), pltpu.VMEM((R, 1), F32)])
    return pl.pallas_call(
        functools.partial(_sc_kernel, pps=pps, past=past, tn=tn, nq=nq),
        grid_spec=gs,
        out_shape=[jax.ShapeDtypeStruct((B, SUBLANES, W), F32)] * 2,
        compiler_params=_params(("arbitrary", "arbitrary")),
        name="sample_attn",
    )(page_table, fq8, dq8, fqc, f_past, b_past, f_new, b_new, ckn, cvn, dkn, dvn,
      *([fk_pool] * pps), *([fv_pool] * pps), *([dk_pool] * pps), *([dv_pool] * pps))


def _post_kernel(x_ref, ya_ref, yb_ref, yc_ref, yd_ref, g1_ref, sh2_ref, sc2_ref, g2_ref,
                 gn_ref, wo_ref, npost_ref, nfpre_ref, wg_ref, wu_ref, wd_ref, nfpost_ref, o_ref, *, chunks):
    o = None
    for i, r in enumerate((ya_ref, yb_ref, yc_ref, yd_ref)):
        part = _rms(r[...], gn_ref[:, i * GROUP_W:(i + 1) * GROUP_W]).astype(BF16)
        d = _dot(part, wo_ref[i * GROUP_W:(i + 1) * GROUP_W, :])
        o = d if o is None else o + d
    x1 = x_ref[...] + g1_ref[...] * _rms(o, npost_ref[...])
    h2 = (_rms(x1, nfpre_ref[...]) * (1.0 + sc2_ref[...]) + sh2_ref[...]).astype(BF16)
    f = None
    for c0, c1 in chunks:
        gate = _dot(h2, wg_ref[:, c0:c1])
        up = _dot(h2, wu_ref[:, c0:c1])
        d = _dot((gate * _sigmoid(gate) * up).astype(BF16), wd_ref[c0:c1, :])
        f = d if f is None else f + d
    o_ref[...] = x1 + g2_ref[...] * _rms(f, nfpost_ref[...])


def _post(x, ya, yb, yc, yd, mod, tiles_per_mod, pw, tm):
    N, D = x.shape
    H = pw["wg"].shape[1]
    rows_mod = mod.shape[1]
    step = 1024
    chunks = tuple((c, min(c + step, H)) for c in range(0, H, step))
    tok = lambda w: pl.BlockSpec((tm, w), lambda i: (i, 0))
    const = lambda shape: pl.BlockSpec(shape, lambda i: (0,) * len(shape), pipeline_mode=pl.Buffered(1))
    return pl.pallas_call(
        functools.partial(_post_kernel, chunks=chunks),
        grid=(N // tm,),
        in_specs=[tok(D), tok(GROUP_W), tok(GROUP_W), tok(GROUP_W), tok(GROUP_W),
                  _mod_spec(rows_mod, D, tiles_per_mod, 2), _mod_spec(rows_mod, D, tiles_per_mod, 3),
                  _mod_spec(rows_mod, D, tiles_per_mod, 4), _mod_spec(rows_mod, D, tiles_per_mod, 5),
                  const((1, D)), const((D, D)), const((1, D)), const((1, D)),
                  const((D, H)), const((D, H)), const((H, D)), const((1, D))],
        out_specs=tok(D),
        out_shape=jax.ShapeDtypeStruct((N, D), F32),
        compiler_params=_params(("arbitrary",)),
        name="post",
    )(x, ya, yb, yc, yd, mod, mod, mod, mod, pw["gn"], pw["wo"], pw["npost"], pw["nfpre"],
      pw["wg"], pw["wu"], pw["wd"], pw["nfpost"])


def _block_diag(blocks):
    G, r, c = blocks.shape
    eye = jnp.eye(G, dtype=blocks.dtype)
    return (blocks[:, :, None, :] * eye[:, None, :, None]).reshape(G * r, G * c)


def _cat_weight(w):
    D = w.shape[0]
    widths = (GROUP_W, GROUP_W, GROUP_W, GROUP_W, GROUP_W, GROUP_W, ATT_HEADS, GROUP_W, GROUP_W, GROUP_W,
              IDX_HEADS * IDX_HD, IDX_HD, IDX_HEADS)
    offs = [0]
    for wd in widths:
        offs.append(offs[-1] + wd)
    pc = [w[:, offs[i]:offs[i + 1]] for i in range(len(widths))]
    a_x, a_g, b_u, c_q, c_k, c_v, c_f, d_q, d_k, d_v, i_q, i_k, i_w = pc
    iq_rep = jnp.tile(i_q.reshape(D, IDX_HEADS, 1, IDX_HD), (1, 1, LANES // IDX_HD, 1)).reshape(D, IDX_HEADS * LANES)
    ik_rep = jnp.tile(i_k, (1, LANES // IDX_HD))
    small = jnp.concatenate([i_k, c_f, i_w, jnp.zeros((D, LANES - IDX_HD - ATT_HEADS - IDX_HEADS), w.dtype)], axis=1)
    wcat = jnp.concatenate([a_x, a_g, b_u, c_q, c_k, c_v, d_q, d_k, d_v], axis=1).astype(BF16)
    widx = jnp.concatenate([iq_rep, ik_rep, small], axis=1)
    widx_hi = widx.astype(BF16)
    widx_lo = (widx - widx_hi.astype(F32)).astype(BF16)
    return wcat, widx_hi, widx_lo


def _layer_weights(l, p):
    row = lambda a: a[l].reshape(1, -1)
    lw = dict(conv_w=p["lru_conv_w"][l], conv_b=row(p["lru_conv_b"]),
              wr=_block_diag(p["lru_wr"][l]).astype(BF16), br=row(p["lru_br"]),
              wi=_block_diag(p["lru_wi"][l]).astype(BF16), bi=row(p["lru_bi"]), lam=row(p["lru_lambda"]))
    sw = dict(ldt=jnp.repeat(p["s5_log_dt"][l], S5_N).reshape(1, S5_STATE),
              are=p["s5_a_re"][l].reshape(1, S5_STATE), aim=p["s5_a_im"][l].reshape(1, S5_STATE),
              bre=_block_diag(jnp.swapaxes(p["s5_b_re"][l], 1, 2)).astype(BF16),
              bim=_block_diag(jnp.swapaxes(p["s5_b_im"][l], 1, 2)).astype(BF16),
              cre=_block_diag(jnp.swapaxes(p["s5_c_re"][l], 1, 2)).astype(BF16),
              cim=_block_diag(jnp.swapaxes(p["s5_c_im"][l], 1, 2)).astype(BF16),
              d=row(p["s5_d"]), gw=p["s5_glu_w"][l].astype(BF16), gb=row(p["s5_glu_b"]))
    pw = dict(gn=row(p["grp_norm"]), wo=p["w_out"][l].astype(BF16), npost=row(p["norm_mix_post"]),
              nfpre=row(p["norm_ffn_pre"]), wg=p["ffn_w_gate"][l].astype(BF16), wu=p["ffn_w_up"][l].astype(BF16),
              wd=p["ffn_w_down"][l].astype(BF16), nfpost=row(p["norm_ffn_post"]))
    return dict(win=_cat_weight(p["w_in"][l]), npre=row(p["norm_mix_pre"]), lru=lw, s5=sw, post=pw,
                fbias=p["fox_f_bias"][l])


def _largest_tile(n, cap):
    t = min(n, cap)
    while n % t:
        t //= 2
    return t


def _recurrent(ax, ag, bu, lru_buf, lru_h0, s5_r0, s5_i0, wts, t_real):
    B, Tp, _ = ax.shape
    tc = _largest_tile(Tp, 256)
    buf8 = jnp.pad(lru_buf, ((0, 0), (SUBLANES - (CONV_W - 1), 0), (0, 0)))
    ya, lru_h = _lru(ax, ag, buf8, lru_h0.reshape(B, 1, GROUP_W), wts["lru"], tc, t_real)
    yb, s5r, s5i = _s5(bu, s5_r0.reshape(B, 1, S5_STATE), s5_i0.reshape(B, 1, S5_STATE), wts["s5"], tc, t_real)
    return ya, yb, lru_h.reshape(B, GROUP_W), s5r.reshape(B, S5_NG, S5_N), s5i.reshape(B, S5_NG, S5_N)


def _prompt_layer(x, mod, wts, B, T):
    N, D = x.shape
    tm = _largest_tile(T, 512)
    (ax, ag, bu, cq, ckf, ckb, cvf, cvb, dq, dkf, dkb, dvf, dvb, iqp, ikp, sm) = _inproj(
        x, mod, T // tm, wts["npre"], *wts["win"], tm)
    seq = lambda a: a.reshape(B, T, a.shape[-1])
    zeros = lambda *s: jnp.zeros(s, F32)
    ax3 = seq(ax)
    ya, yb, lru_h, s5r, s5i = _recurrent(ax3, seq(ag), seq(bu), zeros(B, CONV_W - 1, GROUP_W), zeros(B, GROUP_W),
                                         zeros(B, S5_NG, S5_N), zeros(B, S5_NG, S5_N), wts, T)
    nb = T // LANES
    cf_rows = jnp.swapaxes(seq(sm)[:, :, SM_CF:SM_CF + ATT_HEADS], 1, 2).reshape(B, ATT_HEADS * nb, LANES)
    fb_rows = jnp.repeat(wts["fbias"], nb).reshape(ATT_HEADS * nb, 1)
    lf_rows, f_rows = _fcum(cf_rows, fb_rows, nb)
    f_row = f_rows.reshape(B, ATT_HEADS, T)
    f_col = jnp.swapaxes(f_row, 1, 2)
    logf = jnp.swapaxes(lf_rows.reshape(B, ATT_HEADS, T), 1, 2)
    ta = _largest_tile(T, 512)
    yc = _fox(seq(cq), seq(ckb), seq(cvb), f_col, f_row, ta, ta)
    topk = max(1, min(DSA_TOPK_MAX, T // 4))
    yd = _dsa(seq(iqp), seq(ikp), seq(sm), seq(dq), seq(dkb), seq(dvb), _largest_tile(T, 256), ta, topk)
    flat = lambda a: a.reshape(N, GROUP_W)
    x2 = _post(x, flat(ya), flat(yb), flat(yc), flat(yd), mod, T // tm, wts["post"], tm)
    heads = lambda a: a.reshape(B, T, ATT_HEADS, ATT_HD)
    state = dict(lru_h=lru_h, lru_conv=ax3[:, T - (CONV_W - 1):, :], s5_re=s5r, s5_im=s5i,
                 fox_k=heads(ckf), fox_v=heads(cvf), fox_logf=logf, dsa_k=heads(dkf), dsa_v=heads(dvf),
                 dsa_kidx=seq(sm)[:, :, SM_IK:SM_IK + IDX_HD])
    return x2, state


def _sample_layer(x, mod, wts, l, B, T, past, caches, page_table):
    N, D = x.shape
    (ax, ag, bu, cq, ckf, ckb, cvf, cvb, dq, dkf, dkb, dvf, dvb, iqp, ikp, sm) = _inproj(
        x, mod, 1, wts["npre"], *wts["win"], N)
    seq = lambda a: a.reshape(B, T, a.shape[-1])
    tp = -(-T // SUBLANES) * SUBLANES
    padt = lambda a, n=tp: jnp.pad(seq(a), ((0, 0), (0, n - T), (0, 0)))
    ax3 = seq(ax)
    ya, yb, lru_h, s5r, s5i = _recurrent(padt(ax), padt(ag), padt(bu), past["lru_conv"], past["lru_h"],
                                         past["s5_re"], past["s5_im"], wts, T)
    sm3 = seq(sm)
    n_pages = page_table.shape[1]
    plen = n_pages * PAGE
    pps = _largest_tile(n_pages, 8)
    iq_rows = iqp.reshape(B, T * IDX_HEADS, LANES)
    wrows = sm3[:, :, SM_IW:SM_IW + IDX_HEADS].reshape(B, T * IDX_HEADS, 1)
    cfn = jnp.pad(jnp.swapaxes(sm3[:, :, SM_CF:SM_CF + ATT_HEADS], 1, 2), ((0, 0), (0, 0), (0, PAGE - T)))
    ikn = jnp.pad(sm3[:, :, SM_IK:SM_IK + IDX_HD], ((0, 0), (0, PAGE - T), (0, 0)))
    s_past, f_past, s_new, f_new, lf_new = _sa(page_table, l, iq_rows, wrows, cfn,
                                               wts["fbias"].reshape(ATT_HEADS, 1), ikn, caches["kidx"],
                                               caches["logfT"], pps, T)
    topk = max(1, min(DSA_TOPK_MAX, (plen + T) // 4))
    s_all = jnp.concatenate([s_past, s_new], axis=2).reshape(B * T, plen + PAGE)
    bias_all = _sb(s_all, topk, plen, T).reshape(B, T, plen + PAGE)
    fqc = jnp.pad(f_new[:, :, :T], ((0, 0), (0, 0), (0, SUBLANES - T))).reshape(B, ATT_HEADS * SUBLANES, 1)
    newpage = lambda a: jnp.pad(seq(a), ((0, 0), (0, PAGE - T), (0, 0)))
    yc8, yd8 = _sc(page_table, l, padt(cq, SUBLANES), padt(dq, SUBLANES), fqc, f_past, bias_all[:, :, :plen],
                   f_new, bias_all[:, :, plen:], newpage(ckb), newpage(cvb), newpage(dkb), newpage(dvb),
                   caches["fox_k"], caches["fox_v"], caches["dsa_k"], caches["dsa_v"], pps, T)
    flat = lambda a: a[:, :T, :].reshape(N, GROUP_W)
    x2 = _post(x, flat(ya), flat(yb), flat(yc8), flat(yd8), mod, 1, wts["post"], N)
    heads = lambda a: a.reshape(B, T, ATT_HEADS, ATT_HD)
    state = dict(lru_h=lru_h, lru_conv=ax3[:, T - (CONV_W - 1):, :], s5_re=s5r, s5_im=s5i,
                 fox_k=heads(ckf), fox_v=heads(cvf), fox_logf=jnp.swapaxes(lf_new[:, :, :T], 1, 2),
                 dsa_k=heads(dkf), dsa_v=heads(dvf), dsa_kidx=sm3[:, :, SM_IK:SM_IK + IDX_HD])
    return x2, state


def kernel(x_prompt, x_sample, state_lru_h, state_lru_conv, state_s5_re, state_s5_im, cache_fox_k, cache_fox_v, cache_fox_logf, cache_dsa_k, cache_dsa_v, cache_dsa_kidx, page_table, c_prompt, c_sample, ada_w, ada_b, norm_mix_pre, norm_mix_post, norm_ffn_pre, norm_ffn_post, w_in, lru_conv_w, lru_conv_b, lru_wr, lru_br, lru_wi, lru_bi, lru_lambda, s5_log_dt, s5_a_re, s5_a_im, s5_b_re, s5_b_im, s5_c_re, s5_c_im, s5_d, s5_glu_w, s5_glu_b, fox_f_bias, grp_norm, w_out, ffn_w_gate, ffn_w_up, ffn_w_down):
    p = dict(norm_mix_pre=norm_mix_pre, norm_mix_post=norm_mix_post, norm_ffn_pre=norm_ffn_pre,
             norm_ffn_post=norm_ffn_post, w_in=w_in, lru_conv_w=lru_conv_w, lru_conv_b=lru_conv_b, lru_wr=lru_wr,
             lru_br=lru_br, lru_wi=lru_wi, lru_bi=lru_bi, lru_lambda=lru_lambda, s5_log_dt=s5_log_dt,
             s5_a_re=s5_a_re, s5_a_im=s5_a_im, s5_b_re=s5_b_re, s5_b_im=s5_b_im, s5_c_re=s5_c_re, s5_c_im=s5_c_im,
             s5_d=s5_d, s5_glu_w=s5_glu_w, s5_glu_b=s5_glu_b, fox_f_bias=fox_f_bias, grp_norm=grp_norm,
             w_out=w_out, ffn_w_gate=ffn_w_gate, ffn_w_up=ffn_w_up, ffn_w_down=ffn_w_down)
    B, T, D = x_prompt.shape
    Bs, Ts, _ = x_sample.shape
    L = ada_w.shape[0]
    n_pool = cache_fox_k.shape[1]
    assert T % LANES == 0 and CONV_W - 1 <= Ts <= SUBLANES and cache_fox_k.shape[2] == PAGE

    c_all = jnp.concatenate([c_prompt, c_sample], axis=0)
    c_all = jnp.pad(c_all, ((0, -c_all.shape[0] % SUBLANES), (0, 0)))
    mod_all = _ada(c_all, ada_w.astype(BF16), ada_b)

    pool = lambda a: a.reshape(L, n_pool, PAGE, GROUP_W)
    caches = dict(fox_k=pool(cache_fox_k), fox_v=pool(cache_fox_v), dsa_k=pool(cache_dsa_k), dsa_v=pool(cache_dsa_v),
                  kidx=cache_dsa_kidx, logfT=jnp.swapaxes(cache_fox_logf, 2, 3))

    xp = x_prompt.reshape(B * T, D)
    xs = x_sample.reshape(Bs * Ts, D)
    st_p, st_s = [], []
    for l in range(L):
        wts = _layer_weights(l, p)
        mod_p = mod_all[l, :B].reshape(B, 1, 6 * D)
        mod_s = jnp.repeat(mod_all[l, B:B + Bs], Ts, axis=0).reshape(1, Bs * Ts, 6 * D)
        xp, new_p = _prompt_layer(xp, mod_p, wts, B, T)
        past = dict(lru_h=state_lru_h[l], lru_conv=state_lru_conv[l], s5_re=state_s5_re[l], s5_im=state_s5_im[l])
        xs, new_s = _sample_layer(xs, mod_s, wts, l, Bs, Ts, past, caches, page_table)
        st_p.append(new_p)
        st_s.append(new_s)

    stk = lambda outs, name: jnp.stack([o[name] for o in outs])
    names = ("lru_h", "lru_conv", "s5_re", "s5_im", "fox_k", "fox_v", "fox_logf", "dsa_k", "dsa_v", "dsa_kidx")
    res = [xp.reshape(B, T, D), xs.reshape(Bs, Ts, D)]
    for name in names:
        res += [stk(st_p, name), stk(st_s, name)]
    return tuple(res)
```

```python
import functools
import math

import jax
import jax.numpy as jnp
from jax import lax
from jax.experimental import pallas as pl
from jax.experimental.pallas import tpu as pltpu

F32 = jnp.float32
BF16 = jnp.bfloat16

N_MIXERS = 4
GROUP_W = 256
LRU_HEADS = 4
CONV_W = 4
LRU_C = 8.0
S5_GROUP = 16
S5_NG = 16
S5_N = 64
S5_STATE = S5_NG * S5_N
ATT_HEADS = 4
ATT_HD = 64
IDX_HEADS = 8
IDX_HD = 32
DSA_TOPK_MAX = 256
PAGE = 128
NEG_INF = -1e30
M_INIT = -1e20
BIG = 3e38
ADM_CUT = -1e29
EPS = 1e-6
ATT_SCALE = ATT_HD ** -0.5
IDX_SCALE = IDX_HD ** -0.5
IDX_W_SCALE = IDX_HEADS ** -0.5
ALIBI = tuple(2.0 ** (-8.0 * (h + 1) / ATT_HEADS) for h in range(ATT_HEADS))
LOG2E = math.log2(math.e)
Q_SCALE = ATT_SCALE * LOG2E
BISECT_ITERS = 19
SEARCH_ITERS = 48

LANES = 128
SUBLANES = 8
VMEM_LIMIT = 56 * 1024 * 1024

C_AX, C_AG, C_BU, C_CQ, C_CK, C_CV, C_DQ, C_DK, C_DV = (i * GROUP_W for i in range(9))
P_CAT = 9 * GROUP_W
C_IQ = 0
C_IK = C_IQ + IDX_HEADS * LANES
C_SM = C_IK + LANES
P_IDX = C_SM + LANES
SM_IK, SM_CF, SM_IW = 0, IDX_HD, IDX_HD + ATT_HEADS

NT_DIMS = (((1,), (1,)), ((), ()))


def _nt(a, b):
    return lax.dot_general(a, b, NT_DIMS, preferred_element_type=F32)


def _dot(a, b):
    return jnp.dot(a, b, preferred_element_type=F32)


def _rms(x, g):
    return x * lax.rsqrt(jnp.mean(x * x, axis=-1, keepdims=True) + EPS) * g


def _softplus(z):
    return jnp.maximum(z, 0.0) + jnp.log1p(jnp.exp(-jnp.abs(z)))


def _log_sigmoid(z):
    return -_softplus(-z)


def _expm1(z):
    e = jnp.exp(z)
    one = e == 1.0
    return jnp.where(one, z, (e - 1.0) * z / jnp.where(one, 1.0, jnp.log(e)))


def _sigmoid(z):
    return jax.nn.sigmoid(z)


def _gelu(z):
    return jax.nn.gelu(z)


def _params(sem):
    return pltpu.CompilerParams(dimension_semantics=sem, vmem_limit_bytes=VMEM_LIMIT)


def _head_mask(h, width=GROUP_W):
    lane = lax.broadcasted_iota(jnp.int32, (1, width), 1)
    return (lane >= h * ATT_HD) & (lane < (h + 1) * ATT_HD)


FOLD_CHAINS = 4


def _fold_rows(x, axis_op):
    n, c = x.shape
    g = FOLD_CHAINS if n % (FOLD_CHAINS * SUBLANES) == 0 else 1
    x = axis_op(x.reshape(g, n // (g * SUBLANES), SUBLANES, c), axis=1)
    return axis_op(x, axis=0)


def _col_max(x):
    return jnp.max(_fold_rows(x, jnp.max), axis=0, keepdims=True)


def _lane_cumsum(x):
    lane = lax.broadcasted_iota(jnp.int32, x.shape, 1)
    s = 1
    while s < x.shape[1]:
        x = x + jnp.where(lane >= s, pltpu.roll(x, s, axis=1), 0.0)
        s *= 2
    return x


def _split(x):
    hi = x.astype(BF16)
    return hi, (x - hi.astype(F32)).astype(BF16)


def _ada_kernel(c_ref, w_ref, b_ref, o_ref):
    c = c_ref[...]
    sh, sl = _split(c * _sigmoid(c))
    wh, wl = _split(w_ref[...])
    o_ref[...] = (_dot(sl, wl) + _dot(sl, wh) + _dot(sh, wl)) + _dot(sh, wh) + b_ref[...]


def _ada(c_all, ada_w, ada_b):
    L, D, D6 = ada_w.shape
    Bp = c_all.shape[0]
    tn = 1024
    return pl.pallas_call(
        _ada_kernel,
        grid=(L, D6 // tn),
        in_specs=[pl.BlockSpec((Bp, D), lambda l, j: (0, 0)),
                  pl.BlockSpec((None, D, tn), lambda l, j: (l, 0, j)),
                  pl.BlockSpec((None, 1, tn), lambda l, j: (l, 0, j))],
        out_specs=pl.BlockSpec((None, Bp, tn), lambda l, j: (l, 0, j)),
        out_shape=jax.ShapeDtypeStruct((L, Bp, D6), F32),
        compiler_params=_params(("arbitrary", "arbitrary")),
        name="ada",
    )(c_all, ada_w, ada_b.reshape(L, 1, D6))


def _inproj_kernel(x_ref, sh_ref, sc_ref, g_ref, w_ref, whi_ref, wlo_ref,
                   ax_ref, ag_ref, bu_ref, cq_ref, ckf_ref, ckb_ref, cvf_ref, cvb_ref,
                   dq_ref, dkf_ref, dkb_ref, dvf_ref, dvb_ref, iq_ref, ik_ref, sm_ref):
    x = x_ref[...]
    h = _rms(x, g_ref[...]) * (1.0 + sc_ref[...]) + sh_ref[...]
    hb = h.astype(BF16)

    def mm(c0, width):
        return _dot(hb, w_ref[:, c0:c0 + width])

    ax_ref[...] = mm(C_AX, GROUP_W)
    ag_ref[...] = mm(C_AG, GROUP_W)
    bu_ref[...] = mm(C_BU, GROUP_W)
    cq_ref[...] = (mm(C_CQ, GROUP_W) * Q_SCALE).astype(BF16)
    ck = mm(C_CK, GROUP_W)
    ckf_ref[...] = ck
    ckb_ref[...] = ck.astype(BF16)
    cv = mm(C_CV, GROUP_W)
    cvf_ref[...] = cv
    cvb_ref[...] = cv.astype(BF16)
    dq_ref[...] = (mm(C_DQ, GROUP_W) * Q_SCALE).astype(BF16)
    dk = mm(C_DK, GROUP_W)
    dkf_ref[...] = dk
    dkb_ref[...] = dk.astype(BF16)
    dv = mm(C_DV, GROUP_W)
    dvf_ref[...] = dv
    dvb_ref[...] = dv.astype(BF16)

    hl = (h - hb.astype(F32)).astype(BF16)

    def mm3(c0, width):
        whi = whi_ref[:, c0:c0 + width]
        return _dot(hb, whi) + _dot(hl, whi) + _dot(hb, wlo_ref[:, c0:c0 + width])

    sm_ref[...] = mm3(C_SM, LANES)

    sub = (lax.broadcasted_iota(jnp.int32, (1, LANES), 1) // IDX_HD)
    for hh in range(IDX_HEADS):
        r = mm3(C_IQ + hh * LANES, LANES)
        hi = r.astype(BF16)
        lo = (r - hi.astype(F32)).astype(BF16)
        zero = jnp.zeros_like(hi)
        iq_ref[:, hh * LANES:(hh + 1) * LANES] = jnp.where(sub == 1, lo, jnp.where(sub == 3, zero, hi))
    r = mm3(C_IK, LANES)
    hi = r.astype(BF16)
    lo = (r - hi.astype(F32)).astype(BF16)
    ik_ref[...] = jnp.where(sub == 2, lo, jnp.where(sub == 3, jnp.zeros_like(hi), hi))


def _mod_spec(rows_mod, D, tiles_per_mod, piece):
    return pl.BlockSpec((None, rows_mod, D), lambda i: (i // tiles_per_mod, 0, piece))


def _inproj(x, mod, tiles_per_mod, g, wcat, widx_hi, widx_lo, tm):
    N, D = x.shape
    rows_mod = mod.shape[1]
    tok = lambda w: pl.BlockSpec((tm, w), lambda i: (i, 0))
    widths_dtypes = [(GROUP_W, F32)] * 3 + [(GROUP_W, BF16), (GROUP_W, F32), (GROUP_W, BF16), (GROUP_W, F32),
                                            (GROUP_W, BF16), (GROUP_W, BF16), (GROUP_W, F32), (GROUP_W, BF16),
                                            (GROUP_W, F32), (GROUP_W, BF16), (IDX_HEADS * LANES, BF16),
                                            (LANES, BF16), (LANES, F32)]
    return pl.pallas_call(
        _inproj_kernel,
        grid=(N // tm,),
        in_specs=[tok(D), _mod_spec(rows_mod, D, tiles_per_mod, 0), _mod_spec(rows_mod, D, tiles_per_mod, 1),
                  pl.BlockSpec((1, D), lambda i: (0, 0)),
                  pl.BlockSpec((D, P_CAT), lambda i: (0, 0)),
                  pl.BlockSpec((D, P_IDX), lambda i: (0, 0)),
                  pl.BlockSpec((D, P_IDX), lambda i: (0, 0))],
        out_specs=[tok(w) for w, _ in widths_dtypes],
        out_shape=[jax.ShapeDtypeStruct((N, w), dt) for w, dt in widths_dtypes],
        compiler_params=_params(("arbitrary",)),
        name="inproj",
    )(x, mod, mod, g, wcat, widx_hi, widx_lo)


def _lru_kernel(ax_ref, ag_ref, buf_ref, h0_ref, cw_ref, cb_ref, wr_ref, br_ref, wi_ref, bi_ref, lam_ref,
                ya_ref, hl_ref, xs, hcar, *, tc, last_row):
    c = pl.program_id(1)

    @pl.when(c == 0)
    def _():
        xs[0:SUBLANES, :] = buf_ref[...]
        hcar[...] = h0_ref[...]

    x = ax_ref[...]
    xs[SUBLANES:SUBLANES + tc, :] = x
    w = cw_ref[...]
    y = cb_ref[...] + xs[5:5 + tc, :] * w[0:1] + xs[6:6 + tc, :] * w[1:2] + xs[7:7 + tc, :] * w[2:3] + x * w[3:4]
    xs[0:SUBLANES, :] = xs[tc:tc + SUBLANES, :]

    yb = y.astype(BF16)
    r = _sigmoid(_dot(yb, wr_ref[...]) + br_ref[...])
    i = _sigmoid(_dot(yb, wi_ref[...]) + bi_ref[...])
    log_a = -LRU_C * r * _softplus(-lam_ref[...])
    a = jnp.exp(log_a)
    b = jnp.sqrt(-_expm1(2.0 * log_a)) * (i * y)

    row = lax.broadcasted_iota(jnp.int32, (tc, GROUP_W), 0)
    s = 1
    while s < tc:
        keep = row >= s
        a_sh = jnp.where(keep, pltpu.roll(a, s, axis=0), 1.0)
        b_sh = jnp.where(keep, pltpu.roll(b, s, axis=0), 0.0)
        b = b + a * b_sh
        a = a * a_sh
        s *= 2
    h = b + a * hcar[...]
    hcar[...] = h[tc - 1:tc, :]
    ya_ref[...] = h * _gelu(ag_ref[...])

    @pl.when(c == pl.num_programs(1) - 1)
    def _():
        hl_ref[...] = h[last_row:last_row + 1, :]


def _lru(ax, ag, buf8, h0, lw, tc, t_real):
    B, Tp, W = ax.shape
    seq = pl.BlockSpec((None, tc, W), lambda b, c: (b, c, 0))
    full = lambda shape: pl.BlockSpec(shape, lambda b, c: (0,) * len(shape))
    return pl.pallas_call(
        functools.partial(_lru_kernel, tc=tc, last_row=(t_real - 1) % tc),
        grid=(B, Tp // tc),
        in_specs=[seq, seq,
                  pl.BlockSpec((None, SUBLANES, W), lambda b, c: (b, 0, 0)),
                  pl.BlockSpec((None, 1, W), lambda b, c: (b, 0, 0)),
                  full((CONV_W, W)), full((1, W)), full((W, W)), full((1, W)), full((W, W)), full((1, W)),
                  full((1, W))],
        out_specs=[seq, pl.BlockSpec((None, 1, W), lambda b, c: (b, 0, 0))],
        out_shape=[jax.ShapeDtypeStruct((B, Tp, W), F32), jax.ShapeDtypeStruct((B, 1, W), F32)],
        scratch_shapes=[pltpu.VMEM((tc + SUBLANES, W), F32), pltpu.VMEM((1, W), F32)],
        compiler_params=_params(("arbitrary", "arbitrary")),
        name="lru",
    )(ax, ag, buf8, h0, lw["conv_w"], lw["conv_b"], lw["wr"], lw["br"], lw["wi"], lw["bi"], lw["lam"])


def _s5_kernel(u_ref, h0r_ref, h0i_ref, ldt_ref, are_ref, aim_ref, bre_ref, bim_ref, cre_ref, cim_ref,
               d_ref, gw_ref, gb_ref, y_ref, hlr_ref, hli_ref, hr_s, hi_s, car_r, car_i, *, tc, last_row):
    c = pl.program_id(1)

    @pl.when(c == 0)
    def _():
        car_r[...] = h0r_ref[...]
        car_i[...] = h0i_ref[...]

    dt = jnp.exp(ldt_ref[...])
    ar, ai = are_ref[...], aim_ref[...]
    mag = jnp.exp(dt * ar)
    abr, abi = mag * jnp.cos(dt * ai), mag * jnp.sin(dt * ai)
    den = ar * ar + ai * ai
    nr, ni = abr - 1.0, abi
    cr = (nr * ar + ni * ai) / den
    ci = (ni * ar - nr * ai) / den

    u = u_ref[...]
    ub = u.astype(BF16)
    pre = _dot(ub, bre_ref[...])
    pim = _dot(ub, bim_ref[...])
    hr_s[...] = cr * pre - ci * pim
    hi_s[...] = cr * pim + ci * pre

    def step(t, carry):
        hr, hi = carry
        nhr = abr * hr - abi * hi + hr_s[pl.ds(t, 1), :]
        nhi = abr * hi + abi * hr + hi_s[pl.ds(t, 1), :]
        hr_s[pl.ds(t, 1), :] = nhr
        hi_s[pl.ds(t, 1), :] = nhi
        return nhr, nhi

    hr, hi = lax.fori_loop(0, tc, step, (car_r[...], car_i[...]), unroll=8)
    car_r[...] = hr
    car_i[...] = hi

    hrv, hiv = hr_s[...], hi_s[...]
    y = _dot(hrv.astype(BF16), cre_ref[...]) - _dot(hiv.astype(BF16), cim_ref[...]) + d_ref[...] * u
    g = _gelu(y)
    y_ref[...] = g * _sigmoid(_dot(g.astype(BF16), gw_ref[...]) + gb_ref[...])

    @pl.when(c == pl.num_programs(1) - 1)
    def _():
        hlr_ref[...] = hr_s[last_row:last_row + 1, :]
        hli_ref[...] = hi_s[last_row:last_row + 1, :]


def _s5(u, h0r, h0i, sw, tc, t_real):
    B, Tp, W = u.shape
    S = S5_STATE
    seq = pl.BlockSpec((None, tc, W), lambda b, c: (b, c, 0))
    st = pl.BlockSpec((None, 1, S), lambda b, c: (b, 0, 0))
    full = lambda shape: pl.BlockSpec(shape, lambda b, c: (0,) * len(shape))
    return pl.pallas_call(
        functools.partial(_s5_kernel, tc=tc, last_row=(t_real - 1) % tc),
        grid=(B, Tp // tc),
        in_specs=[seq, st, st, full((1, S)), full((1, S)), full((1, S)), full((W, S)), full((W, S)),
                  full((S, W)), full((S, W)), full((1, W)), full((W, W)), full((1, W))],
        out_specs=[seq, st, st],
        out_shape=[jax.ShapeDtypeStruct((B, Tp, W), F32), jax.ShapeDtypeStruct((B, 1, S), F32),
                   jax.ShapeDtypeStruct((B, 1, S), F32)],
        scratch_shapes=[pltpu.VMEM((tc, S), F32), pltpu.VMEM((tc, S), F32),
                        pltpu.VMEM((1, S), F32), pltpu.VMEM((1, S), F32)],
        compiler_params=_params(("arbitrary", "arbitrary")),
        name="s5",
    )(u, h0r, h0i, sw["ldt"], sw["are"], sw["aim"], sw["bre"], sw["bim"], sw["cre"], sw["cim"],
      sw["d"], sw["gw"], sw["gb"])


def _fcum_kernel(cf_ref, fb_ref, lf_ref, f_ref, *, nb):
    lf = _log_sigmoid(cf_ref[...] + fb_ref[...])
    lf_ref[...] = lf
    cs = _lane_cumsum(lf)
    tot = jnp.broadcast_to(cs[:, LANES - 1:LANES], cs.shape)
    row = lax.broadcasted_iota(jnp.int32, cs.shape, 0) % nb
    inc = tot
    s = 1
    while s < nb:
        inc = inc + jnp.where(row >= s, pltpu.roll(inc, s, axis=0), 0.0)
        s *= 2
    f_ref[...] = (cs + (inc - tot)) * LOG2E


def _fcum(cf_rows, fb_rows, nb):
    B, R, _ = cf_rows.shape
    blk = pl.BlockSpec((None, R, LANES), lambda b: (b, 0, 0))
    return pl.pallas_call(
        functools.partial(_fcum_kernel, nb=nb),
        grid=(B,),
        in_specs=[blk, pl.BlockSpec((R, 1), lambda b: (0, 0))],
        out_specs=[blk, blk],
        out_shape=[jax.ShapeDtypeStruct((B, R, LANES), F32)] * 2,
        compiler_params=_params(("arbitrary",)),
        name="fcum",
    )(cf_rows, fb_rows)


ONES_ROWS = 16
VROWS = ATT_HD + ONES_ROWS


def _init_heads(qT_ref, qm_s, acc_s, m_s):
    qT = qT_ref[...]
    row = lax.broadcasted_iota(jnp.int32, (qT.shape[0], 1), 0)
    for h in range(ATT_HEADS):
        qm_s[h] = jnp.where(row // ATT_HD == h, qT, jnp.zeros_like(qT))
    acc_s[...] = jnp.zeros_like(acc_s)
    m_s[...] = jnp.full_like(m_s, M_INIT)


def _attend_block(kblk, vT_ref, k0, kb, qm_s, acc_s, m_s, s_s, p_s, bias_fn):
    for h in range(ATT_HEADS):
        s_s[h] = _dot(kblk, qm_s[h])
    m_new = []
    for h in range(ATT_HEADS):
        s = bias_fn(h, s_s[h])
        s_s[h] = s
        m_new.append(jnp.maximum(m_s[h], _col_max(s)))
    for h in range(ATT_HEADS):
        p_s[h] = jnp.exp2(s_s[h] - m_new[h]).astype(BF16)
    for h in range(ATT_HEADS):
        rows = slice(h * VROWS, (h + 1) * VROWS)
        pv = _dot(vT_ref[rows, pl.ds(k0, kb)], p_s[h])
        acc_s[rows, :] = jnp.exp2(m_s[h] - m_new[h]) * acc_s[rows, :] + pv
        m_s[h] = m_new[h]


def _finish_heads(o_ref, acc_s):
    for h in range(ATT_HEADS):
        base = h * VROWS
        o_ref[h * ATT_HD:(h + 1) * ATT_HD, :] = (acc_s[base:base + ATT_HD, :]
                                                 / acc_s[base + ATT_HD:base + ATT_HD + 1, :])


def _fox_kernel(qT_ref, k_ref, vT_ref, f2_ref, o_ref, qm_s, acc_s, m_s, s_s, p_s, *, tq, kb):
    q0 = pl.program_id(1) * tq
    _init_heads(qT_ref, qm_s, acc_s, m_s)
    kio = lax.broadcasted_iota(jnp.int32, (kb, tq), 0)
    qpos = q0 + lax.broadcasted_iota(jnp.int32, (kb, tq), 1)

    def block(j, masked):
        k0 = pl.multiple_of(j * kb, kb)
        fk = f2_ref[pl.ds(k0, kb), :]

        def bias(h, s):
            s = s - fk[:, h:h + 1]
            return jnp.where(k0 + kio <= qpos, s, NEG_INF) if masked else s

        _attend_block(k_ref[pl.ds(k0, kb), :], vT_ref, k0, kb, qm_s, acc_s, m_s, s_s, p_s, bias)
        return 0

    n_full = (q0 + 1) // kb
    n_blk = (q0 + tq - 1) // kb + 1
    lax.fori_loop(0, n_full, lambda j, c: block(j, False), 0)
    lax.fori_loop(n_full, n_blk, lambda j, c: block(j, True), 0)
    _finish_heads(o_ref, acc_s)


def _att_scratch(tq, kb):
    return [pltpu.VMEM((ATT_HEADS, GROUP_W, tq), BF16), pltpu.VMEM((ATT_HEADS * VROWS, tq), F32),
            pltpu.VMEM((ATT_HEADS, 1, tq), F32),
            pltpu.VMEM((ATT_HEADS, kb, tq), F32), pltpu.VMEM((ATT_HEADS, kb, tq), BF16)]


def _with_ones_rows(vT):
    B, _, T = vT.shape
    v4 = vT.reshape(B, ATT_HEADS, ATT_HD, T)
    return jnp.concatenate([v4, jnp.ones((B, ATT_HEADS, ONES_ROWS, T), vT.dtype)], axis=2).reshape(
        B, ATT_HEADS * VROWS, T)


def _resident(shape):
    return pl.BlockSpec((None,) + shape, lambda b, i: (b,) + (0,) * len(shape), pipeline_mode=pl.Buffered(1))


def _fox(qT, k, vT, f2col, tq, kb):
    B, W, T = qT.shape
    colblk = pl.BlockSpec((None, W, tq), lambda b, i: (b, 0, i))
    return pl.pallas_call(
        functools.partial(_fox_kernel, tq=tq, kb=kb),
        grid=(B, T // tq),
        in_specs=[colblk, _resident((T, W)), _resident((ATT_HEADS * VROWS, T)), _resident((T, ATT_HEADS))],
        out_specs=colblk,
        out_shape=jax.ShapeDtypeStruct((B, W, T), F32),
        scratch_shapes=_att_scratch(tq, kb),
        compiler_params=_params(("arbitrary", "arbitrary")),
        name="fox",
    )(qT, k, vT, f2col)


def _topk_to_bias(S, nblk, kb, C, k, nadm, w_bits):
    kf = float(k)
    kio = lax.broadcasted_iota(jnp.int32, (kb, C), 0)

    def reduce_blocks(fn, init):
        def body(j, c):
            k0 = pl.multiple_of(j * kb, kb)
            return fn(c, S[pl.ds(k0, kb), :], k0)
        return lax.fori_loop(0, nblk, body, init)

    fold_sum = lambda x: _fold_rows(x, jnp.sum)
    fold_max = lambda x: _fold_rows(x, jnp.max)
    fold_min = lambda x: _fold_rows(x, jnp.min)

    def count(pred):
        c = reduce_blocks(lambda c, x, k0: c + fold_sum(jnp.where(pred(x, k0), 1.0, 0.0)),
                          jnp.zeros((SUBLANES, C), F32))
        return jnp.sum(c, axis=0, keepdims=True)

    mx, mn = reduce_blocks(
        lambda c, x, k0: (jnp.maximum(c[0], fold_max(x)),
                          jnp.minimum(c[1], fold_min(jnp.where(x > ADM_CUT, x, BIG)))),
        (jnp.full((SUBLANES, C), -BIG, F32), jnp.full((SUBLANES, C), BIG, F32)))
    cmax = jnp.max(mx, axis=0, keepdims=True)
    cmin = jnp.min(mn, axis=0, keepdims=True)
    small = nadm <= kf

    def any_col(flag):
        return jnp.max(jnp.where(flag, 1.0, 0.0)) > 0.5

    def probe(c, frac):
        lo, hi, clo, chi = c
        mid = lo + (hi - lo) * frac
        cm = count(lambda x, k0: x >= mid)
        ge = cm >= kf
        return jnp.where(ge, mid, lo), jnp.where(ge, hi, mid), jnp.where(ge, cm, clo), jnp.where(ge, chi, cm)

    lo, hi, clo, chi = lax.fori_loop(
        0, BISECT_ITERS, lambda _, c: probe(c, 0.5),
        (cmin, cmax + jnp.maximum(1.0, jnp.abs(cmax)), nadm, jnp.zeros((1, C), F32)))

    top, bot = reduce_blocks(
        lambda c, x, k0: (jnp.maximum(c[0], fold_max(jnp.where(x < hi, x, -BIG))),
                          jnp.minimum(c[1], fold_min(jnp.where(x >= lo, x, BIG)))),
        (jnp.full((SUBLANES, C), -BIG, F32), jnp.full((SUBLANES, C), BIG, F32)))
    thr = jnp.where(kf - chi < 1.5, jnp.max(top, axis=0, keepdims=True), jnp.min(bot, axis=0, keepdims=True))
    thr = jnp.where(small, cmin, thr)
    cge = count(lambda x, k0: x >= thr)
    cgt = count(lambda x, k0: x > thr)
    missed = ((cgt >= kf) | (cge < kf)) & jnp.logical_not(small)

    def slow(_):
        def open_(clo, chi):
            return (clo - chi > 1.5) & jnp.logical_not(small)

        def search_body(c):
            it, lo, hi, clo, chi = c
            frac = jnp.clip((clo - kf + 0.5) / jnp.maximum(clo - chi, 1.0), 1.0 / 64, 63.0 / 64)
            return (it + 1,) + probe((lo, hi, clo, chi), jnp.where(it % 2 == 0, frac, 0.5))

        _, _, hi2, _, _ = lax.while_loop(lambda c: (c[0] < SEARCH_ITERS) & any_col(open_(c[3], c[4])), search_body,
                                         (jnp.int32(0), lo, hi, clo, chi))

        def unsat(cnt):
            return (cnt < kf) & jnp.logical_not(small)

        def fix_body(c):
            t, cnt = c
            below = reduce_blocks(lambda m, x, k0: jnp.maximum(m, fold_max(jnp.where(x < t, x, -BIG))),
                                  jnp.full((SUBLANES, C), -BIG, F32))
            nt = jnp.where(unsat(cnt), jnp.max(below, axis=0, keepdims=True), t)
            return nt, count(lambda x, k0: x >= nt)

        t2, cge2 = lax.while_loop(lambda c: any_col(unsat(c[1])), fix_body, (hi2, jnp.zeros((1, C), F32)))
        cgt2 = count(lambda x, k0: x > t2)
        return jnp.where(missed, t2, thr), jnp.where(missed, cge2, cge), jnp.where(missed, cgt2, cgt)

    thr, cge, cgt = lax.cond(any_col(missed), slow, lambda _: (thr, cge, cgt), 0)
    need = kf - cgt
    ties = (cge > kf) & jnp.logical_not(small)
    big_pos = jnp.int32(2 ** 30)

    def tie_search(_):
        def it(_, c):
            lo_i, hi_i = c
            mid = (lo_i + hi_i) >> 1
            ok = count(lambda x, k0: (x == thr) & (k0 + kio <= mid)) >= need
            return jnp.where(ok, lo_i, mid), jnp.where(ok, mid, hi_i)
        _, hi_i = lax.fori_loop(0, w_bits + 1, it, (jnp.full((1, C), -1, jnp.int32),
                                                    jnp.zeros((1, C), jnp.int32) + (nblk * kb - 1)))
        return jnp.where(ties, hi_i, big_pos)

    jcut = lax.cond(any_col(ties), tie_search, lambda _: jnp.full((1, C), big_pos, jnp.int32), 0)

    def write(j, _):
        k0 = pl.multiple_of(j * kb, kb)
        x = S[pl.ds(k0, kb), :]
        sel = (x > thr) | ((x == thr) & (k0 + kio <= jcut))
        S[pl.ds(k0, kb), :] = jnp.where(sel, 0.0, NEG_INF)
        return 0

    lax.fori_loop(0, nblk, write, 0)


def _dsa_kernel(iqT_ref, ik_ref, smT_ref, qT_ref, k_ref, vT_ref, o_ref, S, qm_s, acc_s, m_s, s_s, p_s,
                *, tq, kb, topk, w_bits):
    q0 = pl.program_id(1) * tq
    n_blk = (q0 + tq - 1) // kb + 1
    kio = lax.broadcasted_iota(jnp.int32, (kb, tq), 0)
    qpos = q0 + lax.broadcasted_iota(jnp.int32, (kb, tq), 1)
    wc = smT_ref[SM_IW:SM_IW + IDX_HEADS, :] * (IDX_W_SCALE * IDX_SCALE)

    def scores(j, _):
        k0 = pl.multiple_of(j * kb, kb)
        kblk = ik_ref[pl.ds(k0, kb), :]
        sc = jnp.zeros((kb, tq), F32)
        for hh in range(IDX_HEADS):
            d = _dot(kblk, iqT_ref[hh * LANES:(hh + 1) * LANES, :])
            sc = sc + wc[hh:hh + 1, :] * jnp.maximum(d, 0.0)
        S[pl.ds(k0, kb), :] = jnp.where(k0 + kio <= qpos, sc, NEG_INF)
        return 0

    lax.fori_loop(0, n_blk, scores, 0)

    nadm = (q0 + 1 + lax.broadcasted_iota(jnp.int32, (1, tq), 1)).astype(F32)
    _topk_to_bias(S, n_blk, kb, tq, topk, nadm, w_bits)

    _init_heads(qT_ref, qm_s, acc_s, m_s)
    kcol =lax.broadcasted_iota(jnp.int32, (kb, 1), 0)

    def attend(j, _):
        k0 = pl.multiple_of(j * kb, kb)
        sel = S[pl.ds(k0, kb), :]
        kpos = (k0 + kcol).astype(F32)

        def bias(h, s):
            return s + (sel + (ALIBI[h] * LOG2E) * kpos)

        _attend_block(k_ref[pl.ds(k0, kb), :], vT_ref, k0, kb, qm_s, acc_s, m_s, s_s, p_s, bias)
        return 0

    lax.fori_loop(0, n_blk, attend, 0)
    _finish_heads(o_ref, acc_s)


def _dsa(iqT, ikp, smT, qT, k, vT, tq, kb, topk):
    B, W, T = qT.shape
    col = lambda r: pl.BlockSpec((None, r, tq), lambda b, i: (b, 0, i))
    return pl.pallas_call(
        functools.partial(_dsa_kernel, tq=tq, kb=kb, topk=topk, w_bits=max(1, math.ceil(math.log2(T)))),
        grid=(B, T // tq),
        in_specs=[col(IDX_HEADS * LANES), _resident((T, LANES)), col(LANES), col(W),
                  _resident((T, W)), _resident((ATT_HEADS * VROWS, T))],
        out_specs=col(W),
        out_shape=jax.ShapeDtypeStruct((B, W, T), F32),
        scratch_shapes=[pltpu.VMEM((T, tq), F32)] + _att_scratch(tq, kb),
        compiler_params=_params(("arbitrary", "arbitrary")),
        name="dsa",
    )(iqT, ikp, smT, qT, k, vT)


def _sa_kernel(pt_ref, q_ref, w_ref, cfn_ref, fb_ref, ikn_ref, *rest, pps, tn):
    ki_refs, lf_refs = rest[:pps], rest[pps:2 * pps]
    s_ref, f_ref, sn_ref, fn_ref, lfn_ref, carry_s = rest[2 * pps:]
    step = pl.program_id(1)

    @pl.when(step == 0)
    def _():
        carry_s[...] = jnp.zeros_like(carry_s)

    qp = q_ref[...]
    qh = qp[:, 0:IDX_HD]
    ql = qp[:, IDX_HD:2 * IDX_HD]
    w = w_ref[...] * IDX_W_SCALE
    nq = qp.shape[0] // IDX_HEADS

    def scores(kf):
        kh = kf.astype(BF16)
        kl = (kf - kh.astype(F32)).astype(BF16)
        d = _nt(qh, kh) + _nt(ql, kh) + _nt(qh, kl)
        r = jnp.maximum(d * IDX_SCALE, 0.0) * w
        return jnp.sum(r.reshape(nq, IDX_HEADS, kf.shape[0]), axis=1)

    carry = carry_s[...]
    for i in range(pps):
        s_ref[:, i * PAGE:(i + 1) * PAGE] = scores(ki_refs[i][...])
        cs = _lane_cumsum(lf_refs[i][...]) + carry
        f_ref[:, i * PAGE:(i + 1) * PAGE] = cs
        carry = cs[:, PAGE - 1:PAGE]
    carry_s[...] = carry

    @pl.when(step == pl.num_programs(1) - 1)
    def _():
        lane = lax.broadcasted_iota(jnp.int32, (ATT_HEADS, PAGE), 1)
        lfn = _log_sigmoid(cfn_ref[...] + fb_ref[...])
        lfn_ref[...] = lfn
        fn_ref[...] = _lane_cumsum(jnp.where(lane < tn, lfn, 0.0)) + carry
        col = lax.broadcasted_iota(jnp.int32, (nq, PAGE), 1)
        rowq = lax.broadcasted_iota(jnp.int32, (nq, PAGE), 0)
        sn_ref[...] = jnp.where((col <= rowq) & (col < tn), scores(ikn_ref[...]), NEG_INF)


def _sa(page_table, l, qrows, wrows, cfn, fb, ikn, kidx_pool, lfT_pool, pps, tn):
    B, n_pages = page_table.shape
    nsteps = n_pages // pps
    P = n_pages * PAGE
    nq = qrows.shape[1] // IDX_HEADS
    per_b = lambda shape: pl.BlockSpec((None,) + shape, lambda b, s, pt: (b,) + (0,) * len(shape))
    ki_specs = [pl.BlockSpec((None, None, PAGE, IDX_HD), lambda b, s, pt, i=i: (l, pt[b, s * pps + i], 0, 0))
                for i in range(pps)]
    lf_specs = [pl.BlockSpec((None, None, ATT_HEADS, PAGE), lambda b, s, pt, i=i: (l, pt[b, s * pps + i], 0, 0))
                for i in range(pps)]
    gs = pltpu.PrefetchScalarGridSpec(
        num_scalar_prefetch=1,
        grid=(B, nsteps),
        in_specs=[per_b((nq * IDX_HEADS, LANES)), per_b((nq * IDX_HEADS, 1)), per_b((ATT_HEADS, PAGE)),
                  pl.BlockSpec((ATT_HEADS, 1), lambda b, s, pt: (0, 0)), per_b((PAGE, IDX_HD))] + ki_specs + lf_specs,
        out_specs=[pl.BlockSpec((None, nq, pps * PAGE), lambda b, s, pt: (b, 0, s)),
                   pl.BlockSpec((None, ATT_HEADS, pps * PAGE), lambda b, s, pt: (b, 0, s)),
                   per_b((nq, PAGE)), per_b((ATT_HEADS, PAGE)), per_b((ATT_HEADS, PAGE))],
        scratch_shapes=[pltpu.VMEM((ATT_HEADS, 1), F32)])
    return pl.pallas_call(
        functools.partial(_sa_kernel, pps=pps, tn=tn),
        grid_spec=gs,
        out_shape=[jax.ShapeDtypeStruct((B, nq, P), F32), jax.ShapeDtypeStruct((B, ATT_HEADS, P), F32),
                   jax.ShapeDtypeStruct((B, nq, PAGE), F32), jax.ShapeDtypeStruct((B, ATT_HEADS, PAGE), F32),
                   jax.ShapeDtypeStruct((B, ATT_HEADS, PAGE), F32)],
        compiler_params=_params(("arbitrary", "arbitrary")),
        name="sample_scores",
    )(page_table, qrows, wrows, cfn, fb, ikn, *([kidx_pool] * pps), *([lfT_pool] * pps))


def _sb_kernel(s_ref, o_ref, *, kb, nblk, topk, past, nq, w_bits):
    C = s_ref.shape[1]
    o_ref[...] = s_ref[...]
    nadm = (past + 1 + lax.broadcasted_iota(jnp.int32, (1, C), 1) % nq).astype(F32)
    _topk_to_bias(o_ref, nblk, kb, C, topk, nadm, w_bits)


def _sb(s_allT, topk, past, nq):
    Wt, C = s_allT.shape
    nl = Wt // LANES
    div = max(d for d in range(1, 9) if nl % d == 0)
    kb = div * LANES
    return pl.pallas_call(
        functools.partial(_sb_kernel, kb=kb, nblk=Wt // kb, topk=topk, past=past, nq=nq,
                          w_bits=max(1, math.ceil(math.log2(Wt)))),
        grid=(1,),
        in_specs=[pl.BlockSpec((Wt, C), lambda i: (0, 0))],
        out_specs=pl.BlockSpec((Wt, C), lambda i: (0, 0)),
        out_shape=jax.ShapeDtypeStruct((Wt, C), F32),
        compiler_params=_params(("arbitrary",)),
        name="sample_topk",
    )(s_allT)


def _sc_kernel(pt_ref, fq_ref, dq_ref, fqc_ref, fp_ref, bp_ref, fn_ref, bn_ref,
               ckn_ref, cvn_ref, dkn_ref, dvn_ref, *rest, pps, past, tn, nq):
    fk_refs, fv_refs = rest[0:pps], rest[pps:2 * pps]
    dk_refs, dv_refs = rest[2 * pps:3 * pps], rest[3 * pps:4 * pps]
    yc_ref, yd_ref, qf_s, qd_s, accf, mf, lf, accd, md, ld = rest[4 * pps:]
    step = pl.program_id(1)
    R = ATT_HEADS * SUBLANES

    def stack_heads(q8):
        return jnp.concatenate([jnp.where(_head_mask(h), q8, jnp.zeros_like(q8)) for h in range(ATT_HEADS)], axis=0)

    @pl.when(step == 0)
    def _():
        qf_s[...] = stack_heads(fq_ref[...])
        qd_s[...] = stack_heads(dq_ref[...])
        for acc, m, l in ((accf, mf, lf), (accd, md, ld)):
            acc[...] = jnp.zeros_like(acc)
            m[...] = jnp.full_like(m, M_INIT)
            l[...] = jnp.zeros_like(l)

    def update(q_s, kb, vb, bias, acc, m, l):
        s = _nt(q_s[...], kb) + bias * LOG2E
        m_prev = m[...]
        m_new = jnp.maximum(m_prev, jnp.max(s, axis=1, keepdims=True))
        p = jnp.exp2(s - m_new)
        alpha = jnp.exp2(m_prev - m_new)
        l[...] = alpha * l[...] + jnp.sum(p, axis=1, keepdims=True)
        m[...] = m_new
        acc[...] = alpha * acc[...] + _dot(p.astype(BF16), vb)

    def per_head_rows(x4):
        return jnp.concatenate([jnp.broadcast_to(x4[h:h + 1, :], (SUBLANES, x4.shape[1]))
                                for h in range(ATT_HEADS)], axis=0)

    def per_query_rows(xq):
        r8 = lax.broadcasted_iota(jnp.int32, (SUBLANES, xq.shape[1]), 0)
        x8 = jnp.zeros((SUBLANES, xq.shape[1]), F32)
        for qq in range(nq):
            x8 = jnp.where(r8 == qq, jnp.broadcast_to(xq[qq:qq + 1, :], x8.shape), x8)
        return jnp.concatenate([x8] * ATT_HEADS, axis=0)

    rowi = lax.broadcasted_iota(jnp.int32, (R, 1), 0)
    qidx = rowi % SUBLANES
    slope = jnp.zeros((R, 1), F32)
    for h in range(ATT_HEADS):
        slope = jnp.where(rowi // SUBLANES == h, ALIBI[h], slope)

    def blocks(refs):
        return jnp.concatenate([r[...].astype(BF16) for r in refs], axis=0)

    kw = pps * PAGE
    col = lax.broadcasted_iota(jnp.int32, (R, kw), 1)
    update(qf_s, blocks(fk_refs), blocks(fv_refs), fqc_ref[...] - per_head_rows(fp_ref[...]), accf, mf, lf)
    dist = (past + qidx - (step * kw + col)).astype(F32)
    update(qd_s, blocks(dk_refs), blocks(dv_refs), per_query_rows(bp_ref[...]) - slope * dist, accd, md, ld)

    @pl.when(step == pl.num_programs(1) - 1)
    def _():
        coln = lax.broadcasted_iota(jnp.int32, (R, PAGE), 1)
        ok = (coln <= qidx) & (coln < tn)
        bias_f = jnp.where(ok, fqc_ref[...] - per_head_rows(fn_ref[...]), NEG_INF)
        update(qf_s, ckn_ref[...], cvn_ref[...], bias_f, accf, mf, lf)
        distn = (qidx - coln).astype(F32)
        bias_d = jnp.where(coln < tn, per_query_rows(bn_ref[...]), NEG_INF) - slope * distn
        update(qd_s, dkn_ref[...], dvn_ref[...], bias_d, accd, md, ld)
        for acc, l, out in ((accf, lf, yc_ref), (accd, ld, yd_ref)):
            o = acc[...] / l[...]
            y = jnp.zeros((SUBLANES, GROUP_W), F32)
            for h in range(ATT_HEADS):
                y = jnp.where(_head_mask(h), o[h * SUBLANES:(h + 1) * SUBLANES, :], y)
            out[...] = y


def _sc(page_table, l, fq8, dq8, fqc, f_past, b_past, f_new, b_new, ckn, cvn, dkn, dvn,
        fk_pool, fv_pool, dk_pool, dv_pool, pps, tn):
    B, n_pages = page_table.shape
    nsteps = n_pages // pps
    past = n_pages * PAGE
    nq = b_past.shape[1]
    W = GROUP_W
    R = ATT_HEADS * SUBLANES
    per_b = lambda shape: pl.BlockSpec((None,) + shape, lambda b, s, pt: (b,) + (0,) * len(shape))
    page = lambda i: pl.BlockSpec((None, None, PAGE, W), lambda b, s, pt, i=i: (l, pt[b, s * pps + i], 0, 0))
    pages = [page(i) for i in range(pps)]
    gs = pltpu.PrefetchScalarGridSpec(
        num_scalar_prefetch=1,
        grid=(B, nsteps),
        in_specs=[per_b((SUBLANES, W)), per_b((SUBLANES, W)), per_b((R, 1)),
                  pl.BlockSpec((None, ATT_HEADS, pps * PAGE), lambda b, s, pt: (b, 0, s)),
                  pl.BlockSpec((None, nq, pps * PAGE), lambda b, s, pt: (b, 0, s)),
                  per_b((ATT_HEADS, PAGE)), per_b((nq, PAGE)),
                  per_b((PAGE, W)), per_b((PAGE, W)), per_b((PAGE, W)), per_b((PAGE, W))] + pages * 4,
        out_specs=[per_b((SUBLANES, W)), per_b((SUBLANES, W))],
        scratch_shapes=[pltpu.VMEM((R, W), BF16), pltpu.VMEM((R, W), BF16),
                        pltpu.VMEM((R, W), F32), pltpu.VMEM((R, 1), F32), pltpu.VMEM((R, 1), F32),
                        pltpu.VMEM((R, W), F32), pltpu.VMEM((R, 1), F32), pltpu.VMEM((R, 1), F32)])
    return pl.pallas_call(
        functools.partial(_sc_kernel, pps=pps, past=past, tn=tn, nq=nq),
        grid_spec=gs,
        out_shape=[jax.ShapeDtypeStruct((B, SUBLANES, W), F32)] * 2,
        compiler_params=_params(("arbitrary", "arbitrary")),
        name="sample_attn",
    )(page_table, fq8, dq8, fqc, f_past, b_past, f_new, b_new, ckn, cvn, dkn, dvn,
      *([fk_pool] * pps), *([fv_pool] * pps), *([dk_pool] * pps), *([dv_pool] * pps))


def _post_kernel(x_ref, ya_ref, yb_ref, yc_ref, yd_ref, g1_ref, sh2_ref, sc2_ref, g2_ref,
                 gn_ref, wo_ref, npost_ref, nfpre_ref, wg_ref, wu_ref, wd_ref, nfpost_ref, o_ref, *, chunks):
    o = None
    for i, r in enumerate((ya_ref, yb_ref, yc_ref, yd_ref)):
        part = _rms(r[...], gn_ref[:, i * GROUP_W:(i + 1) * GROUP_W]).astype(BF16)
        d = _dot(part, wo_ref[i * GROUP_W:(i + 1) * GROUP_W, :])
        o = d if o is None else o + d
    x1 = x_ref[...] + g1_ref[...] * _rms(o, npost_ref[...])
    h2 = (_rms(x1, nfpre_ref[...]) * (1.0 + sc2_ref[...]) + sh2_ref[...]).astype(BF16)
    f = None
    for c0, c1 in chunks:
        gate = _dot(h2, wg_ref[:, c0:c1])
        up = _dot(h2, wu_ref[:, c0:c1])
        d = _dot((gate * _sigmoid(gate) * up).astype(BF16), wd_ref[c0:c1, :])
        f = d if f is None else f + d
    o_ref[...] = x1 + g2_ref[...] * _rms(f, nfpost_ref[...])


def _post(x, ya, yb, yc, yd, mod, tiles_per_mod, pw, tm):
    N, D = x.shape
    H = pw["wg"].shape[1]
    rows_mod = mod.shape[1]
    step = 1024
    chunks = tuple((c, min(c + step, H)) for c in range(0, H, step))
    tok = lambda w: pl.BlockSpec((tm, w), lambda i: (i, 0))
    const = lambda shape: pl.BlockSpec(shape, lambda i: (0,) * len(shape), pipeline_mode=pl.Buffered(1))
    return pl.pallas_call(
        functools.partial(_post_kernel, chunks=chunks),
        grid=(N // tm,),
        in_specs=[tok(D), tok(GROUP_W), tok(GROUP_W), tok(GROUP_W), tok(GROUP_W),
                  _mod_spec(rows_mod, D, tiles_per_mod, 2), _mod_spec(rows_mod, D, tiles_per_mod, 3),
                  _mod_spec(rows_mod, D, tiles_per_mod, 4), _mod_spec(rows_mod, D, tiles_per_mod, 5),
                  const((1, D)), const((D, D)), const((1, D)), const((1, D)),
                  const((D, H)), const((D, H)), const((H, D)), const((1, D))],
        out_specs=tok(D),
        out_shape=jax.ShapeDtypeStruct((N, D), F32),
        compiler_params=_params(("arbitrary",)),
        name="post",
    )(x, ya, yb, yc, yd, mod, mod, mod, mod, pw["gn"], pw["wo"], pw["npost"], pw["nfpre"],
      pw["wg"], pw["wu"], pw["wd"], pw["nfpost"])


def _block_diag(blocks):
    G, r, c = blocks.shape
    eye = jnp.eye(G, dtype=blocks.dtype)
    return (blocks[:, :, None, :] * eye[:, None, :, None]).reshape(G * r, G * c)


def _cat_weight(w):
    D = w.shape[0]
    widths = (GROUP_W, GROUP_W, GROUP_W, GROUP_W, GROUP_W, GROUP_W, ATT_HEADS, GROUP_W, GROUP_W, GROUP_W,
              IDX_HEADS * IDX_HD, IDX_HD, IDX_HEADS)
    offs = [0]
    for wd in widths:
        offs.append(offs[-1] + wd)
    pc = [w[:, offs[i]:offs[i + 1]] for i in range(len(widths))]
    a_x, a_g, b_u, c_q, c_k, c_v, c_f, d_q, d_k, d_v, i_q, i_k, i_w = pc
    iq_rep = jnp.tile(i_q.reshape(D, IDX_HEADS, 1, IDX_HD), (1, 1, LANES // IDX_HD, 1)).reshape(D, IDX_HEADS * LANES)
    ik_rep = jnp.tile(i_k, (1, LANES // IDX_HD))
    small = jnp.concatenate([i_k, c_f, i_w, jnp.zeros((D, LANES - IDX_HD - ATT_HEADS - IDX_HEADS), w.dtype)], axis=1)
    wcat = jnp.concatenate([a_x, a_g, b_u, c_q, c_k, c_v, d_q, d_k, d_v], axis=1).astype(BF16)
    widx = jnp.concatenate([iq_rep, ik_rep, small], axis=1)
    hi = lax.reduce_precision(widx, exponent_bits=8, mantissa_bits=7)
    return wcat, hi.astype(BF16), (widx - hi).astype(BF16)


def _layer_weights(l, p):
    row = lambda a: a[l].reshape(1, -1)
    lw = dict(conv_w=p["lru_conv_w"][l], conv_b=row(p["lru_conv_b"]),
              wr=_block_diag(p["lru_wr"][l]).astype(BF16), br=row(p["lru_br"]),
              wi=_block_diag(p["lru_wi"][l]).astype(BF16), bi=row(p["lru_bi"]), lam=row(p["lru_lambda"]))
    sw = dict(ldt=jnp.repeat(p["s5_log_dt"][l], S5_N).reshape(1, S5_STATE),
              are=p["s5_a_re"][l].reshape(1, S5_STATE), aim=p["s5_a_im"][l].reshape(1, S5_STATE),
              bre=_block_diag(jnp.swapaxes(p["s5_b_re"][l], 1, 2)).astype(BF16),
              bim=_block_diag(jnp.swapaxes(p["s5_b_im"][l], 1, 2)).astype(BF16),
              cre=_block_diag(jnp.swapaxes(p["s5_c_re"][l], 1, 2)).astype(BF16),
              cim=_block_diag(jnp.swapaxes(p["s5_c_im"][l], 1, 2)).astype(BF16),
              d=row(p["s5_d"]), gw=p["s5_glu_w"][l].astype(BF16), gb=row(p["s5_glu_b"]))
    pw = dict(gn=row(p["grp_norm"]), wo=p["w_out"][l].astype(BF16), npost=row(p["norm_mix_post"]),
              nfpre=row(p["norm_ffn_pre"]), wg=p["ffn_w_gate"][l].astype(BF16), wu=p["ffn_w_up"][l].astype(BF16),
              wd=p["ffn_w_down"][l].astype(BF16), nfpost=row(p["norm_ffn_post"]))
    return dict(win=_cat_weight(p["w_in"][l]), npre=row(p["norm_mix_pre"]), lru=lw, s5=sw, post=pw,
                fbias=p["fox_f_bias"][l])


def _largest_tile(n, cap):
    t = min(n, cap)
    while n % t:
        t //= 2
    return t


def _recurrent(ax, ag, bu, lru_buf, lru_h0, s5_r0, s5_i0, wts, t_real):
    B, Tp, _ = ax.shape
    tc = _largest_tile(Tp, 256)
    buf8 = jnp.pad(lru_buf, ((0, 0), (SUBLANES - (CONV_W - 1), 0), (0, 0)))
    ya, lru_h = _lru(ax, ag, buf8, lru_h0.reshape(B, 1, GROUP_W), wts["lru"], tc, t_real)
    yb, s5r, s5i = _s5(bu, s5_r0.reshape(B, 1, S5_STATE), s5_i0.reshape(B, 1, S5_STATE), wts["s5"], tc, t_real)
    return ya, yb, lru_h.reshape(B, GROUP_W), s5r.reshape(B, S5_NG, S5_N), s5i.reshape(B, S5_NG, S5_N)


def _prompt_layer(x, mod, wts, B, T):
    N, D = x.shape
    tm = _largest_tile(T, 512)
    (ax, ag, bu, cq, ckf, ckb, cvf, cvb, dq, dkf, dkb, dvf, dvb, iqp, ikp, sm) = _inproj(
        x, mod, T // tm, wts["npre"], *wts["win"], tm)
    seq = lambda a: a.reshape(B, T, a.shape[-1])
    zeros = lambda *s: jnp.zeros(s, F32)
    ax3 = seq(ax)
    ya, yb, lru_h, s5r, s5i = _recurrent(ax3, seq(ag), seq(bu), zeros(B, CONV_W - 1, GROUP_W), zeros(B, GROUP_W),
                                         zeros(B, S5_NG, S5_N), zeros(B, S5_NG, S5_N), wts, T)
    nb = T // LANES
    cf_rows = jnp.swapaxes(seq(sm)[:, :, SM_CF:SM_CF + ATT_HEADS], 1, 2).reshape(B, ATT_HEADS * nb, LANES)
    fb_rows = jnp.repeat(wts["fbias"], nb).reshape(ATT_HEADS * nb, 1)
    lf_rows, f_rows = _fcum(cf_rows, fb_rows, nb)
    f2_col = jnp.swapaxes(f_rows.reshape(B, ATT_HEADS, T), 1, 2)
    logf = jnp.swapaxes(lf_rows.reshape(B, ATT_HEADS, T), 1, 2)
    tr = lambda a: jnp.swapaxes(seq(a), 1, 2)
    tq = kb = _largest_tile(T, 512)
    yc = jnp.swapaxes(_fox(tr(cq), seq(ckb), _with_ones_rows(tr(cvb)), f2_col, tq, kb), 1, 2)
    topk = max(1, min(DSA_TOPK_MAX, T // 4))
    yd = jnp.swapaxes(_dsa(tr(iqp), seq(ikp), tr(sm), tr(dq), seq(dkb), _with_ones_rows(tr(dvb)), tq, kb, topk),
                      1, 2)
    flat = lambda a: a.reshape(N, GROUP_W)
    x2 = _post(x, flat(ya), flat(yb), flat(yc), flat(yd), mod, T // tm, wts["post"], tm)
    heads = lambda a: a.reshape(B, T, ATT_HEADS, ATT_HD)
    state = dict(lru_h=lru_h, lru_conv=ax3[:, T - (CONV_W - 1):, :], s5_re=s5r, s5_im=s5i,
                 fox_k=heads(ckf), fox_v=heads(cvf), fox_logf=logf, dsa_k=heads(dkf), dsa_v=heads(dvf),
                 dsa_kidx=seq(sm)[:, :, SM_IK:SM_IK + IDX_HD])
    return x2, state


def _sample_layer(x, mod, wts, l, B, T, past, caches, page_table):
    N, D = x.shape
    (ax, ag, bu, cq, ckf, ckb, cvf, cvb, dq, dkf, dkb, dvf, dvb, iqp, ikp, sm) = _inproj(
        x, mod, 1, wts["npre"], *wts["win"], N)
    seq = lambda a: a.reshape(B, T, a.shape[-1])
    tp = -(-T // SUBLANES) * SUBLANES
    padt = lambda a, n=tp: jnp.pad(seq(a), ((0, 0), (0, n - T), (0, 0)))
    ax3 = seq(ax)
    ya, yb, lru_h, s5r, s5i = _recurrent(padt(ax), padt(ag), padt(bu), past["lru_conv"], past["lru_h"],
                                         past["s5_re"], past["s5_im"], wts, T)
    sm3 = seq(sm)
    n_pages = page_table.shape[1]
    plen = n_pages * PAGE
    pps = _largest_tile(n_pages, 8)
    iq_rows = iqp.reshape(B, T * IDX_HEADS, LANES)
    wrows = sm3[:, :, SM_IW:SM_IW + IDX_HEADS].reshape(B, T * IDX_HEADS, 1)
    cfn = jnp.pad(jnp.swapaxes(sm3[:, :, SM_CF:SM_CF + ATT_HEADS], 1, 2), ((0, 0), (0, 0), (0, PAGE - T)))
    ikn = jnp.pad(sm3[:, :, SM_IK:SM_IK + IDX_HD], ((0, 0), (0, PAGE - T), (0, 0)))
    s_past, f_past, s_new, f_new, lf_new = _sa(page_table, l, iq_rows, wrows, cfn,
                                               wts["fbias"].reshape(ATT_HEADS, 1), ikn, caches["kidx"],
                                               caches["logfT"], pps, T)
    topk = max(1, min(DSA_TOPK_MAX, (plen + T) // 4))
    s_allT = jnp.transpose(jnp.concatenate([s_past, s_new], axis=2), (2, 0, 1)).reshape(plen + PAGE, B * T)
    bias_all = jnp.transpose(_sb(s_allT, topk, plen, T).reshape(plen + PAGE, B, T), (1, 2, 0))
    fqc = jnp.pad(f_new[:, :, :T], ((0, 0), (0, 0), (0, SUBLANES - T))).reshape(B, ATT_HEADS * SUBLANES, 1)
    newpage = lambda a: jnp.pad(seq(a), ((0, 0), (0, PAGE - T), (0, 0)))
    yc8, yd8 = _sc(page_table, l, padt(cq, SUBLANES), padt(dq, SUBLANES), fqc, f_past, bias_all[:, :, :plen],
                   f_new, bias_all[:, :, plen:], newpage(ckb), newpage(cvb), newpage(dkb), newpage(dvb),
                   caches["fox_k"], caches["fox_v"], caches["dsa_k"], caches["dsa_v"], pps, T)
    flat = lambda a: a[:, :T, :].reshape(N, GROUP_W)
    x2 = _post(x, flat(ya), flat(yb), flat(yc8), flat(yd8), mod, 1, wts["post"], N)
    heads = lambda a: a.reshape(B, T, ATT_HEADS, ATT_HD)
    state = dict(lru_h=lru_h, lru_conv=ax3[:, T - (CONV_W - 1):, :], s5_re=s5r, s5_im=s5i,
                 fox_k=heads(ckf), fox_v=heads(cvf), fox_logf=jnp.swapaxes(lf_new[:, :, :T], 1, 2),
                 dsa_k=heads(dkf), dsa_v=heads(dvf), dsa_kidx=sm3[:, :, SM_IK:SM_IK + IDX_HD])
    return x2, state


def kernel(x_prompt, x_sample, state_lru_h, state_lru_conv, state_s5_re, state_s5_im, cache_fox_k, cache_fox_v, cache_fox_logf, cache_dsa_k, cache_dsa_v, cache_dsa_kidx, page_table, c_prompt, c_sample, ada_w, ada_b, norm_mix_pre, norm_mix_post, norm_ffn_pre, norm_ffn_post, w_in, lru_conv_w, lru_conv_b, lru_wr, lru_br, lru_wi, lru_bi, lru_lambda, s5_log_dt, s5_a_re, s5_a_im, s5_b_re, s5_b_im, s5_c_re, s5_c_im, s5_d, s5_glu_w, s5_glu_b, fox_f_bias, grp_norm, w_out, ffn_w_gate, ffn_w_up, ffn_w_down):
    p = dict(norm_mix_pre=norm_mix_pre, norm_mix_post=norm_mix_post, norm_ffn_pre=norm_ffn_pre,
             norm_ffn_post=norm_ffn_post, w_in=w_in, lru_conv_w=lru_conv_w, lru_conv_b=lru_conv_b, lru_wr=lru_wr,
             lru_br=lru_br, lru_wi=lru_wi, lru_bi=lru_bi, lru_lambda=lru_lambda, s5_log_dt=s5_log_dt,
             s5_a_re=s5_a_re, s5_a_im=s5_a_im, s5_b_re=s5_b_re, s5_b_im=s5_b_im, s5_c_re=s5_c_re, s5_c_im=s5_c_im,
             s5_d=s5_d, s5_glu_w=s5_glu_w, s5_glu_b=s5_glu_b, fox_f_bias=fox_f_bias, grp_norm=grp_norm,
             w_out=w_out, ffn_w_gate=ffn_w_gate, ffn_w_up=ffn_w_up, ffn_w_down=ffn_w_down)
    B, T, D = x_prompt.shape
    Bs, Ts, _ = x_sample.shape
    L = ada_w.shape[0]
    n_pool = cache_fox_k.shape[1]
    assert T % LANES == 0 and CONV_W - 1 <= Ts <= SUBLANES and cache_fox_k.shape[2] == PAGE

    c_all = jnp.concatenate([c_prompt, c_sample], axis=0)
    c_all = jnp.pad(c_all, ((0, -c_all.shape[0] % SUBLANES), (0, 0)))
    mod_all = _ada(c_all, ada_w, ada_b)

    pool = lambda a: a.reshape(L, n_pool, PAGE, GROUP_W)
    caches = dict(fox_k=pool(cache_fox_k), fox_v=pool(cache_fox_v), dsa_k=pool(cache_dsa_k), dsa_v=pool(cache_dsa_v),
                  kidx=cache_dsa_kidx, logfT=jnp.swapaxes(cache_fox_logf, 2, 3))

    xp = x_prompt.reshape(B * T, D)
    xs = x_sample.reshape(Bs * Ts, D)
    st_p, st_s = [], []
    for l in range(L):
        wts = _layer_weights(l, p)
        mod_p = mod_all[l, :B].reshape(B, 1, 6 * D)
        mod_s = jnp.repeat(mod_all[l, B:B + Bs], Ts, axis=0).reshape(1, Bs * Ts, 6 * D)
        xp, new_p = _prompt_layer(xp, mod_p, wts, B, T)
        past = dict(lru_h=state_lru_h[l], lru_conv=state_lru_conv[l], s5_re=state_s5_re[l], s5_im=state_s5_im[l])
        xs, new_s = _sample_layer(xs, mod_s, wts, l, Bs, Ts, past, caches, page_table)
        st_p.append(new_p)
        st_s.append(new_s)

    stk = lambda outs, name: jnp.stack([o[name] for o in outs])
    names = ("lru_h", "lru_conv", "s5_re", "s5_im", "fox_k", "fox_v", "fox_logf", "dsa_k", "dsa_v", "dsa_kidx")
    res = [xp.reshape(B, T, D), xs.reshape(Bs, Ts, D)]
    for name in names:
        res += [stk(st_p, name), stk(st_s, name)]
    return tuple(res)
```

```python
import functools
import math

import jax
import jax.numpy as jnp
from jax import lax
from jax.experimental import pallas as pl
from jax.experimental.pallas import tpu as pltpu

F32 = jnp.float32
BF16 = jnp.bfloat16

N_MIXERS = 4
GROUP_W = 256
LRU_HEADS = 4
CONV_W = 4
LRU_C = 8.0
S5_GROUP = 16
S5_NG = 16
S5_N = 64
S5_STATE = S5_NG * S5_N
ATT_HEADS = 4
ATT_HD = 64
IDX_HEADS = 8
IDX_HD = 32
DSA_TOPK_MAX = 256
PAGE = 128
NEG_INF = -1e30
M_INIT = -1e20
BIG = 3e38
ADM_CUT = -1e29
EPS = 1e-6
ATT_SCALE = ATT_HD ** -0.5
IDX_SCALE = IDX_HD ** -0.5
IDX_W_SCALE = IDX_HEADS ** -0.5
ALIBI = tuple(2.0 ** (-8.0 * (h + 1) / ATT_HEADS) for h in range(ATT_HEADS))
LOG2E = math.log2(math.e)
Q_SCALE = ATT_SCALE * LOG2E
BISECT_ITERS = 40
SEARCH_ITERS = 48

LANES = 128
SUBLANES = 8
VMEM_LIMIT = 56 * 1024 * 1024

C_AX, C_AG, C_BU, C_CQ, C_CK, C_CV, C_DQ, C_DK, C_DV = (i * GROUP_W for i in range(9))
P_CAT = 9 * GROUP_W
C_IQ = 0
C_IK = C_IQ + IDX_HEADS * LANES
C_SM = C_IK + LANES
P_IDX = C_SM + LANES
SM_IK, SM_CF, SM_IW = 0, IDX_HD, IDX_HD + ATT_HEADS

NT_DIMS = (((1,), (1,)), ((), ()))


def _nt(a, b):
    return lax.dot_general(a, b, NT_DIMS, preferred_element_type=F32)


def _dot(a, b):
    return jnp.dot(a, b, preferred_element_type=F32)


def _rms(x, g):
    return x * lax.rsqrt(jnp.mean(x * x, axis=-1, keepdims=True) + EPS) * g


def _softplus(z):
    return jnp.maximum(z, 0.0) + jnp.log1p(jnp.exp(-jnp.abs(z)))


def _log_sigmoid(z):
    return -_softplus(-z)


def _expm1(z):
    e = jnp.exp(z)
    one = e == 1.0
    return jnp.where(one, z, (e - 1.0) * z / jnp.where(one, 1.0, jnp.log(e)))


def _sigmoid(z):
    return jax.nn.sigmoid(z)


def _gelu(z):
    return jax.nn.gelu(z)


def _params(sem):
    return pltpu.CompilerParams(dimension_semantics=sem, vmem_limit_bytes=VMEM_LIMIT)


def _head_mask(h, width=GROUP_W):
    lane = lax.broadcasted_iota(jnp.int32, (1, width), 1)
    return (lane >= h * ATT_HD) & (lane < (h + 1) * ATT_HD)


FOLD_CHAINS = 4


def _fold_rows(x, axis_op):
    n, c = x.shape
    g = FOLD_CHAINS if n % (FOLD_CHAINS * SUBLANES) == 0 else 1
    x = axis_op(x.reshape(g, n // (g * SUBLANES), SUBLANES, c), axis=1)
    return axis_op(x, axis=0)


def _col_max(x):
    return jnp.max(_fold_rows(x, jnp.max), axis=0, keepdims=True)


def _lane_cumsum(x):
    lane = lax.broadcasted_iota(jnp.int32, x.shape, 1)
    s = 1
    while s < x.shape[1]:
        x = x + jnp.where(lane >= s, pltpu.roll(x, s, axis=1), 0.0)
        s *= 2
    return x


def _split(x):
    hi = x.astype(BF16)
    return hi, (x - hi.astype(F32)).astype(BF16)


def _ada_kernel(c_ref, w_ref, b_ref, o_ref):
    c = c_ref[...]
    sh, sl = _split(c * _sigmoid(c))
    wh, wl = _split(w_ref[...])
    o_ref[...] = (_dot(sl, wl) + _dot(sl, wh) + _dot(sh, wl)) + _dot(sh, wh) + b_ref[...]


def _ada(c_all, ada_w, ada_b):
    L, D, D6 = ada_w.shape
    Bp = c_all.shape[0]
    tn = 1024
    return pl.pallas_call(
        _ada_kernel,
        grid=(L, D6 // tn),
        in_specs=[pl.BlockSpec((Bp, D), lambda l, j: (0, 0)),
                  pl.BlockSpec((None, D, tn), lambda l, j: (l, 0, j)),
                  pl.BlockSpec((None, 1, tn), lambda l, j: (l, 0, j))],
        out_specs=pl.BlockSpec((None, Bp, tn), lambda l, j: (l, 0, j)),
        out_shape=jax.ShapeDtypeStruct((L, Bp, D6), F32),
        compiler_params=_params(("arbitrary", "arbitrary")),
        name="ada",
    )(c_all, ada_w, ada_b.reshape(L, 1, D6))


def _inproj_kernel(x_ref, sh_ref, sc_ref, g_ref, w_ref, whi_ref, wlo_ref,
                   ax_ref, ag_ref, bu_ref, cq_ref, ckf_ref, ckb_ref, cvf_ref, cvb_ref,
                   dq_ref, dkf_ref, dkb_ref, dvf_ref, dvb_ref, iq_ref, ik_ref, sm_ref):
    x = x_ref[...]
    h = _rms(x, g_ref[...]) * (1.0 + sc_ref[...]) + sh_ref[...]
    hb = h.astype(BF16)

    def mm(c0, width):
        return _dot(hb, w_ref[:, c0:c0 + width])

    ax_ref[...] = mm(C_AX, GROUP_W)
    ag_ref[...] = mm(C_AG, GROUP_W)
    bu_ref[...] = mm(C_BU, GROUP_W)
    cq_ref[...] = (mm(C_CQ, GROUP_W) * Q_SCALE).astype(BF16)
    ck = mm(C_CK, GROUP_W)
    ckf_ref[...] = ck
    ckb_ref[...] = ck.astype(BF16)
    cv = mm(C_CV, GROUP_W)
    cvf_ref[...] = cv
    cvb_ref[...] = cv.astype(BF16)
    dq_ref[...] = (mm(C_DQ, GROUP_W) * Q_SCALE).astype(BF16)
    dk = mm(C_DK, GROUP_W)
    dkf_ref[...] = dk
    dkb_ref[...] = dk.astype(BF16)
    dv = mm(C_DV, GROUP_W)
    dvf_ref[...] = dv
    dvb_ref[...] = dv.astype(BF16)

    hl = (h - hb.astype(F32)).astype(BF16)

    def mm3(c0, width):
        whi = whi_ref[:, c0:c0 + width]
        return _dot(hb, whi) + _dot(hl, whi) + _dot(hb, wlo_ref[:, c0:c0 + width])

    sm_ref[...] = mm3(C_SM, LANES)

    sub = (lax.broadcasted_iota(jnp.int32, (1, LANES), 1) // IDX_HD)
    for hh in range(IDX_HEADS):
        r = mm3(C_IQ + hh * LANES, LANES)
        hi = r.astype(BF16)
        lo = (r - hi.astype(F32)).astype(BF16)
        zero = jnp.zeros_like(hi)
        iq_ref[:, hh * LANES:(hh + 1) * LANES] = jnp.where(sub == 1, lo, jnp.where(sub == 3, zero, hi))
    r = mm3(C_IK, LANES)
    hi = r.astype(BF16)
    lo = (r - hi.astype(F32)).astype(BF16)
    ik_ref[...] = jnp.where(sub == 2, lo, jnp.where(sub == 3, jnp.zeros_like(hi), hi))


def _mod_spec(rows_mod, D, tiles_per_mod, piece):
    return pl.BlockSpec((None, rows_mod, D), lambda i: (i // tiles_per_mod, 0, piece))


def _inproj(x, mod, tiles_per_mod, g, wcat, widx_hi, widx_lo, tm):
    N, D = x.shape
    rows_mod = mod.shape[1]
    tok = lambda w: pl.BlockSpec((tm, w), lambda i: (i, 0))
    widths_dtypes = [(GROUP_W, F32)] * 3 + [(GROUP_W, BF16), (GROUP_W, F32), (GROUP_W, BF16), (GROUP_W, F32),
                                            (GROUP_W, BF16), (GROUP_W, BF16), (GROUP_W, F32), (GROUP_W, BF16),
                                            (GROUP_W, F32), (GROUP_W, BF16), (IDX_HEADS * LANES, BF16),
                                            (LANES, BF16), (LANES, F32)]
    return pl.pallas_call(
        _inproj_kernel,
        grid=(N // tm,),
        in_specs=[tok(D), _mod_spec(rows_mod, D, tiles_per_mod, 0), _mod_spec(rows_mod, D, tiles_per_mod, 1),
                  pl.BlockSpec((1, D), lambda i: (0, 0)),
                  pl.BlockSpec((D, P_CAT), lambda i: (0, 0)),
                  pl.BlockSpec((D, P_IDX), lambda i: (0, 0)),
                  pl.BlockSpec((D, P_IDX), lambda i: (0, 0))],
        out_specs=[tok(w) for w, _ in widths_dtypes],
        out_shape=[jax.ShapeDtypeStruct((N, w), dt) for w, dt in widths_dtypes],
        compiler_params=_params(("arbitrary",)),
        name="inproj",
    )(x, mod, mod, g, wcat, widx_hi, widx_lo)


def _lru_kernel(ax_ref, ag_ref, buf_ref, h0_ref, cw_ref, cb_ref, wr_ref, br_ref, wi_ref, bi_ref, lam_ref,
                ya_ref, hl_ref, xs, hcar, *, tc, last_row):
    c = pl.program_id(1)

    @pl.when(c == 0)
    def _():
        xs[0:SUBLANES, :] = buf_ref[...]
        hcar[...] = h0_ref[...]

    x = ax_ref[...]
    xs[SUBLANES:SUBLANES + tc, :] = x
    w = cw_ref[...]
    y = cb_ref[...] + xs[5:5 + tc, :] * w[0:1] + xs[6:6 + tc, :] * w[1:2] + xs[7:7 + tc, :] * w[2:3] + x * w[3:4]
    xs[0:SUBLANES, :] = xs[tc:tc + SUBLANES, :]

    yb = y.astype(BF16)
    r = _sigmoid(_dot(yb, wr_ref[...]) + br_ref[...])
    i = _sigmoid(_dot(yb, wi_ref[...]) + bi_ref[...])
    log_a = -LRU_C * r * _softplus(-lam_ref[...])
    a = jnp.exp(log_a)
    b = jnp.sqrt(-_expm1(2.0 * log_a)) * (i * y)

    row = lax.broadcasted_iota(jnp.int32, (tc, GROUP_W), 0)
    s = 1
    while s < tc:
        keep = row >= s
        a_sh = jnp.where(keep, pltpu.roll(a, s, axis=0), 1.0)
        b_sh = jnp.where(keep, pltpu.roll(b, s, axis=0), 0.0)
        b = b + a * b_sh
        a = a * a_sh
        s *= 2
    h = b + a * hcar[...]
    hcar[...] = h[tc - 1:tc, :]
    ya_ref[...] = h * _gelu(ag_ref[...])

    @pl.when(c == pl.num_programs(1) - 1)
    def _():
        hl_ref[...] = h[last_row:last_row + 1, :]


def _lru(ax, ag, buf8, h0, lw, tc, t_real):
    B, Tp, W = ax.shape
    seq = pl.BlockSpec((None, tc, W), lambda b, c: (b, c, 0))
    full = lambda shape: pl.BlockSpec(shape, lambda b, c: (0,) * len(shape))
    return pl.pallas_call(
        functools.partial(_lru_kernel, tc=tc, last_row=(t_real - 1) % tc),
        grid=(B, Tp // tc),
        in_specs=[seq, seq,
                  pl.BlockSpec((None, SUBLANES, W), lambda b, c: (b, 0, 0)),
                  pl.BlockSpec((None, 1, W), lambda b, c: (b, 0, 0)),
                  full((CONV_W, W)), full((1, W)), full((W, W)), full((1, W)), full((W, W)), full((1, W)),
                  full((1, W))],
        out_specs=[seq, pl.BlockSpec((None, 1, W), lambda b, c: (b, 0, 0))],
        out_shape=[jax.ShapeDtypeStruct((B, Tp, W), F32), jax.ShapeDtypeStruct((B, 1, W), F32)],
        scratch_shapes=[pltpu.VMEM((tc + SUBLANES, W), F32), pltpu.VMEM((1, W), F32)],
        compiler_params=_params(("arbitrary", "arbitrary")),
        name="lru",
    )(ax, ag, buf8, h0, lw["conv_w"], lw["conv_b"], lw["wr"], lw["br"], lw["wi"], lw["bi"], lw["lam"])


def _s5_kernel(u_ref, h0r_ref, h0i_ref, ldt_ref, are_ref, aim_ref, bre_ref, bim_ref, cre_ref, cim_ref,
               d_ref, gw_ref, gb_ref, y_ref, hlr_ref, hli_ref, hr_s, hi_s, car_r, car_i, *, tc, last_row):
    c = pl.program_id(1)

    @pl.when(c == 0)
    def _():
        car_r[...] = h0r_ref[...]
        car_i[...] = h0i_ref[...]

    dt = jnp.exp(ldt_ref[...])
    ar, ai = are_ref[...], aim_ref[...]
    mag = jnp.exp(dt * ar)
    abr, abi = mag * jnp.cos(dt * ai), mag * jnp.sin(dt * ai)
    den = ar * ar + ai * ai
    nr, ni = abr - 1.0, abi
    cr = (nr * ar + ni * ai) / den
    ci = (ni * ar - nr * ai) / den

    u = u_ref[...]
    ub = u.astype(BF16)
    pre = _dot(ub, bre_ref[...])
    pim = _dot(ub, bim_ref[...])
    hr_s[...] = cr * pre - ci * pim
    hi_s[...] = cr * pim + ci * pre

    def step(t, carry):
        hr, hi = carry
        nhr = abr * hr - abi * hi + hr_s[pl.ds(t, 1), :]
        nhi = abr * hi + abi * hr + hi_s[pl.ds(t, 1), :]
        hr_s[pl.ds(t, 1), :] = nhr
        hi_s[pl.ds(t, 1), :] = nhi
        return nhr, nhi

    hr, hi = lax.fori_loop(0, tc, step, (car_r[...], car_i[...]), unroll=8)
    car_r[...] = hr
    car_i[...] = hi

    hrv, hiv = hr_s[...], hi_s[...]
    y = _dot(hrv.astype(BF16), cre_ref[...]) - _dot(hiv.astype(BF16), cim_ref[...]) + d_ref[...] * u
    g = _gelu(y)
    y_ref[...] = g * _sigmoid(_dot(g.astype(BF16), gw_ref[...]) + gb_ref[...])

    @pl.when(c == pl.num_programs(1) - 1)
    def _():
        hlr_ref[...] = hr_s[last_row:last_row + 1, :]
        hli_ref[...] = hi_s[last_row:last_row + 1, :]


def _s5(u, h0r, h0i, sw, tc, t_real):
    B, Tp, W = u.shape
    S = S5_STATE
    seq = pl.BlockSpec((None, tc, W), lambda b, c: (b, c, 0))
    st = pl.BlockSpec((None, 1, S), lambda b, c: (b, 0, 0))
    full = lambda shape: pl.BlockSpec(shape, lambda b, c: (0,) * len(shape))
    return pl.pallas_call(
        functools.partial(_s5_kernel, tc=tc, last_row=(t_real - 1) % tc),
        grid=(B, Tp // tc),
        in_specs=[seq, st, st, full((1, S)), full((1, S)), full((1, S)), full((W, S)), full((W, S)),
                  full((S, W)), full((S, W)), full((1, W)), full((W, W)), full((1, W))],
        out_specs=[seq, st, st],
        out_shape=[jax.ShapeDtypeStruct((B, Tp, W), F32), jax.ShapeDtypeStruct((B, 1, S), F32),
                   jax.ShapeDtypeStruct((B, 1, S), F32)],
        scratch_shapes=[pltpu.VMEM((tc, S), F32), pltpu.VMEM((tc, S), F32),
                        pltpu.VMEM((1, S), F32), pltpu.VMEM((1, S), F32)],
        compiler_params=_params(("arbitrary", "arbitrary")),
        name="s5",
    )(u, h0r, h0i, sw["ldt"], sw["are"], sw["aim"], sw["bre"], sw["bim"], sw["cre"], sw["cim"],
      sw["d"], sw["gw"], sw["gb"])


def _fcum_kernel(cf_ref, fb_ref, lf_ref, f_ref, *, nb):
    lf = _log_sigmoid(cf_ref[...] + fb_ref[...])
    lf_ref[...] = lf
    cs = _lane_cumsum(lf)
    tot = jnp.broadcast_to(cs[:, LANES - 1:LANES], cs.shape)
    row = lax.broadcasted_iota(jnp.int32, cs.shape, 0) % nb
    inc = tot
    s = 1
    while s < nb:
        inc = inc + jnp.where(row >= s, pltpu.roll(inc, s, axis=0), 0.0)
        s *= 2
    f_ref[...] = (cs + (inc - tot)) * LOG2E


def _fcum(cf_rows, fb_rows, nb):
    B, R, _ = cf_rows.shape
    blk = pl.BlockSpec((None, R, LANES), lambda b: (b, 0, 0))
    return pl.pallas_call(
        functools.partial(_fcum_kernel, nb=nb),
        grid=(B,),
        in_specs=[blk, pl.BlockSpec((R, 1), lambda b: (0, 0))],
        out_specs=[blk, blk],
        out_shape=[jax.ShapeDtypeStruct((B, R, LANES), F32)] * 2,
        compiler_params=_params(("arbitrary",)),
        name="fcum",
    )(cf_rows, fb_rows)


ONES_ROWS = 16
VROWS = ATT_HD + ONES_ROWS


def _init_heads(qT_ref, qm_s, acc_s, m_s):
    qT = qT_ref[...]
    row = lax.broadcasted_iota(jnp.int32, (qT.shape[0], 1), 0)
    for h in range(ATT_HEADS):
        qm_s[h] = jnp.where(row // ATT_HD == h, qT, jnp.zeros_like(qT))
    acc_s[...] = jnp.zeros_like(acc_s)
    m_s[...] = jnp.full_like(m_s, M_INIT)


def _attend_block(kblk, vT_ref, k0, kb, qm_s, acc_s, m_s, s_s, p_s, bias_fn):
    for h in range(ATT_HEADS):
        s_s[h] = _dot(kblk, qm_s[h])
    m_new = []
    for h in range(ATT_HEADS):
        s = bias_fn(h, s_s[h])
        s_s[h] = s
        m_new.append(jnp.maximum(m_s[h], _col_max(s)))
    for h in range(ATT_HEADS):
        p_s[h] = jnp.exp2(s_s[h] - m_new[h]).astype(BF16)
    for h in range(ATT_HEADS):
        rows = slice(h * VROWS, (h + 1) * VROWS)
        pv = _dot(vT_ref[rows, pl.ds(k0, kb)], p_s[h])
        acc_s[rows, :] = jnp.exp2(m_s[h] - m_new[h]) * acc_s[rows, :] + pv
        m_s[h] = m_new[h]


def _finish_heads(o_ref, acc_s):
    for h in range(ATT_HEADS):
        base = h * VROWS
        o_ref[h * ATT_HD:(h + 1) * ATT_HD, :] = (acc_s[base:base + ATT_HD, :]
                                                 / acc_s[base + ATT_HD:base + ATT_HD + 1, :])


def _fox_kernel(qT_ref, k_ref, vT_ref, f2_ref, o_ref, qm_s, acc_s, m_s, s_s, p_s, *, tq, kb):
    q0 = pl.program_id(1) * tq
    _init_heads(qT_ref, qm_s, acc_s, m_s)
    kio = lax.broadcasted_iota(jnp.int32, (kb, tq), 0)
    qpos = q0 + lax.broadcasted_iota(jnp.int32, (kb, tq), 1)

    def block(j, masked):
        k0 = pl.multiple_of(j * kb, kb)
        fk = f2_ref[pl.ds(k0, kb), :]

        def bias(h, s):
            s = s - fk[:, h:h + 1]
            return jnp.where(k0 + kio <= qpos, s, NEG_INF) if masked else s

        _attend_block(k_ref[pl.ds(k0, kb), :], vT_ref, k0, kb, qm_s, acc_s, m_s, s_s, p_s, bias)
        return 0

    n_full = (q0 + 1) // kb
    n_blk = (q0 + tq - 1) // kb + 1
    lax.fori_loop(0, n_full, lambda j, c: block(j, False), 0)
    lax.fori_loop(n_full, n_blk, lambda j, c: block(j, True), 0)
    _finish_heads(o_ref, acc_s)


def _att_scratch(tq, kb):
    return [pltpu.VMEM((ATT_HEADS, GROUP_W, tq), BF16), pltpu.VMEM((ATT_HEADS * VROWS, tq), F32),
            pltpu.VMEM((ATT_HEADS, 1, tq), F32),
            pltpu.VMEM((ATT_HEADS, kb, tq), F32), pltpu.VMEM((ATT_HEADS, kb, tq), BF16)]


def _with_ones_rows(vT):
    B, _, T = vT.shape
    v4 = vT.reshape(B, ATT_HEADS, ATT_HD, T)
    return jnp.concatenate([v4, jnp.ones((B, ATT_HEADS, ONES_ROWS, T), vT.dtype)], axis=2).reshape(
        B, ATT_HEADS * VROWS, T)


def _resident(shape):
    return pl.BlockSpec((None,) + shape, lambda b, i: (b,) + (0,) * len(shape), pipeline_mode=pl.Buffered(1))


def _fox(qT, k, vT, f2col, tq, kb):
    B, W, T = qT.shape
    colblk = pl.BlockSpec((None, W, tq), lambda b, i: (b, 0, i))
    return pl.pallas_call(
        functools.partial(_fox_kernel, tq=tq, kb=kb),
        grid=(B, T // tq),
        in_specs=[colblk, _resident((T, W)), _resident((ATT_HEADS * VROWS, T)), _resident((T, ATT_HEADS))],
        out_specs=colblk,
        out_shape=jax.ShapeDtypeStruct((B, W, T), F32),
        scratch_shapes=_att_scratch(tq, kb),
        compiler_params=_params(("arbitrary", "arbitrary")),
        name="fox",
    )(qT, k, vT, f2col)


def _topk_to_bias(S, nblk, kb, C, k, nadm, w_bits):
    kf = float(k)
    kio = lax.broadcasted_iota(jnp.int32, (kb, C), 0)

    def reduce_blocks(fn, init):
        def body(j, c):
            k0 = pl.multiple_of(j * kb, kb)
            return fn(c, S[pl.ds(k0, kb), :], k0)
        return lax.fori_loop(0, nblk, body, init)

    fold_sum = lambda x: _fold_rows(x, jnp.sum)
    fold_max = lambda x: _fold_rows(x, jnp.max)
    fold_min = lambda x: _fold_rows(x, jnp.min)

    def count(pred):
        c = reduce_blocks(lambda c, x, k0: c + fold_sum(jnp.where(pred(x, k0), 1.0, 0.0)),
                          jnp.zeros((SUBLANES, C), F32))
        return jnp.sum(c, axis=0, keepdims=True)

    mx, mn = reduce_blocks(
        lambda c, x, k0: (jnp.maximum(c[0], fold_max(x)),
                          jnp.minimum(c[1], fold_min(jnp.where(x > ADM_CUT, x, BIG)))),
        (jnp.full((SUBLANES, C), -BIG, F32), jnp.full((SUBLANES, C), BIG, F32)))
    cmax = jnp.max(mx, axis=0, keepdims=True)
    cmin = jnp.min(mn, axis=0, keepdims=True)
    small = nadm <= kf

    def any_col(flag):
        return jnp.max(jnp.where(flag, 1.0, 0.0)) > 0.5

    def probe(c, frac):
        lo, hi, clo, chi = c
        mid = lo + (hi - lo) * frac
        cm = count(lambda x, k0: x >= mid)
        ge = cm >= kf
        return jnp.where(ge, mid, lo), jnp.where(ge, hi, mid), jnp.where(ge, cm, clo), jnp.where(ge, chi, cm)

    def unresolved(clo, chi):
        g, r = clo - chi, kf - chi
        return (g > 2.5) & (r > 1.5) & (g - r > 0.5) & jnp.logical_not(small)

    _, lo, hi, clo, chi = lax.while_loop(
        lambda c: (c[0] < BISECT_ITERS) & any_col(unresolved(c[3], c[4])),
        lambda c: (c[0] + 1,) + probe(c[1:], 0.5),
        (jnp.int32(0), cmin, cmax + jnp.maximum(1.0, jnp.abs(cmax)), nadm, jnp.zeros((1, C), F32)))

    top, bot = reduce_blocks(
        lambda c, x, k0: (jnp.maximum(c[0], fold_max(jnp.where(x < hi, x, -BIG))),
                          jnp.minimum(c[1], fold_min(jnp.where(x >= lo, x, BIG)))),
        (jnp.full((SUBLANES, C), -BIG, F32), jnp.full((SUBLANES, C), BIG, F32)))
    thr = jnp.where(kf - chi < 1.5, jnp.max(top, axis=0, keepdims=True), jnp.min(bot, axis=0, keepdims=True))
    thr = jnp.where(small, cmin, thr)
    cge = count(lambda x, k0: x >= thr)
    cgt = count(lambda x, k0: x > thr)
    missed = ((cgt >= kf) | (cge < kf)) & jnp.logical_not(small)

    def slow(_):
        def open_(clo, chi):
            return (clo - chi > 1.5) & jnp.logical_not(small)

        def search_body(c):
            it, lo, hi, clo, chi = c
            frac = jnp.clip((clo - kf + 0.5) / jnp.maximum(clo - chi, 1.0), 1.0 / 64, 63.0 / 64)
            return (it + 1,) + probe((lo, hi, clo, chi), jnp.where(it % 2 == 0, frac, 0.5))

        _, _, hi2, _, _ = lax.while_loop(lambda c: (c[0] < SEARCH_ITERS) & any_col(open_(c[3], c[4])), search_body,
                                         (jnp.int32(0), lo, hi, clo, chi))

        def unsat(cnt):
            return (cnt < kf) & jnp.logical_not(small)

        def fix_body(c):
            t, cnt = c
            below = reduce_blocks(lambda m, x, k0: jnp.maximum(m, fold_max(jnp.where(x < t, x, -BIG))),
                                  jnp.full((SUBLANES, C), -BIG, F32))
            nt = jnp.where(unsat(cnt), jnp.max(below, axis=0, keepdims=True), t)
            return nt, count(lambda x, k0: x >= nt)

        t2, cge2 = lax.while_loop(lambda c: any_col(unsat(c[1])), fix_body, (hi2, jnp.zeros((1, C), F32)))
        cgt2 = count(lambda x, k0: x > t2)
        return jnp.where(missed, t2, thr), jnp.where(missed, cge2, cge), jnp.where(missed, cgt2, cgt)

    thr, cge, cgt = lax.cond(any_col(missed), slow, lambda _: (thr, cge, cgt), 0)
    need = kf - cgt
    ties = (cge > kf) & jnp.logical_not(small)
    big_pos = jnp.int32(2 ** 30)

    def tie_search(_):
        def it(_, c):
            lo_i, hi_i = c
            mid = (lo_i + hi_i) >> 1
            ok = count(lambda x, k0: (x == thr) & (k0 + kio <= mid)) >= need
            return jnp.where(ok, lo_i, mid), jnp.where(ok, mid, hi_i)
        _, hi_i = lax.fori_loop(0, w_bits + 1, it, (jnp.full((1, C), -1, jnp.int32),
                                                    jnp.zeros((1, C), jnp.int32) + (nblk * kb - 1)))
        return jnp.where(ties, hi_i, big_pos)

    jcut = lax.cond(any_col(ties), tie_search, lambda _: jnp.full((1, C), big_pos, jnp.int32), 0)

    def write(j, _):
        k0 = pl.multiple_of(j * kb, kb)
        x = S[pl.ds(k0, kb), :]
        sel = (x > thr) | ((x == thr) & (k0 + kio <= jcut))
        S[pl.ds(k0, kb), :] = jnp.where(sel, 0.0, NEG_INF)
        return 0

    lax.fori_loop(0, nblk, write, 0)


def _dsa_kernel(iqT_ref, ik_ref, smT_ref, qT_ref, k_ref, vT_ref, o_ref, S, qm_s, acc_s, m_s, s_s, p_s,
                *, tq, kb, topk, w_bits):
    q0 = pl.program_id(1) * tq
    n_blk = (q0 + tq - 1) // kb + 1
    kio = lax.broadcasted_iota(jnp.int32, (kb, tq), 0)
    qpos = q0 + lax.broadcasted_iota(jnp.int32, (kb, tq), 1)
    wc = smT_ref[SM_IW:SM_IW + IDX_HEADS, :] * (IDX_W_SCALE * IDX_SCALE)

    def scores(j, _):
        k0 = pl.multiple_of(j * kb, kb)
        kblk = ik_ref[pl.ds(k0, kb), :]
        sc = jnp.zeros((kb, tq), F32)
        for hh in range(IDX_HEADS):
            d = _dot(kblk, iqT_ref[hh * LANES:(hh + 1) * LANES, :])
            sc = sc + wc[hh:hh + 1, :] * jnp.maximum(d, 0.0)
        S[pl.ds(k0, kb), :] = jnp.where(k0 + kio <= qpos, sc, NEG_INF)
        return 0

    lax.fori_loop(0, n_blk, scores, 0)

    nadm = (q0 + 1 + lax.broadcasted_iota(jnp.int32, (1, tq), 1)).astype(F32)
    _topk_to_bias(S, n_blk, kb, tq, topk, nadm, w_bits)

    _init_heads(qT_ref, qm_s, acc_s, m_s)
    kcol =lax.broadcasted_iota(jnp.int32, (kb, 1), 0)

    def attend(j, _):
        k0 = pl.multiple_of(j * kb, kb)
        sel = S[pl.ds(k0, kb), :]
        kpos = (k0 + kcol).astype(F32)

        def bias(h, s):
            return s + (sel + (ALIBI[h] * LOG2E) * kpos)

        _attend_block(k_ref[pl.ds(k0, kb), :], vT_ref, k0, kb, qm_s, acc_s, m_s, s_s, p_s, bias)
        return 0

    lax.fori_loop(0, n_blk, attend, 0)
    _finish_heads(o_ref, acc_s)


def _dsa(iqT, ikp, smT, qT, k, vT, tq, kb, topk):
    B, W, T = qT.shape
    col = lambda r: pl.BlockSpec((None, r, tq), lambda b, i: (b, 0, i))
    return pl.pallas_call(
        functools.partial(_dsa_kernel, tq=tq, kb=kb, topk=topk, w_bits=max(1, math.ceil(math.log2(T)))),
        grid=(B, T // tq),
        in_specs=[col(IDX_HEADS * LANES), _resident((T, LANES)), col(LANES), col(W),
                  _resident((T, W)), _resident((ATT_HEADS * VROWS, T))],
        out_specs=col(W),
        out_shape=jax.ShapeDtypeStruct((B, W, T), F32),
        scratch_shapes=[pltpu.VMEM((T, tq), F32)] + _att_scratch(tq, kb),
        compiler_params=_params(("arbitrary", "arbitrary")),
        name="dsa",
    )(iqT, ikp, smT, qT, k, vT)


def _sa_kernel(pt_ref, q_ref, w_ref, cfn_ref, fb_ref, ikn_ref, *rest, pps, tn):
    ki_refs, lf_refs = rest[:pps], rest[pps:2 * pps]
    s_ref, f_ref, sn_ref, fn_ref, lfn_ref, carry_s = rest[2 * pps:]
    step = pl.program_id(1)

    @pl.when(step == 0)
    def _():
        carry_s[...] = jnp.zeros_like(carry_s)

    qp = q_ref[...]
    qh = qp[:, 0:IDX_HD]
    ql = qp[:, IDX_HD:2 * IDX_HD]
    w = w_ref[...] * IDX_W_SCALE
    nq = qp.shape[0] // IDX_HEADS

    def scores(kT):
        kh, kl = _split(kT)
        d = _dot(qh, kh) + _dot(ql, kh) + _dot(qh, kl)
        r = jnp.maximum(d * IDX_SCALE, 0.0) * w
        return jnp.sum(r.reshape(nq, IDX_HEADS, kT.shape[1]), axis=1)

    s_ref[...] = scores(jnp.concatenate([r[...] for r in ki_refs], axis=1))
    carry = carry_s[...]
    for i in range(pps):
        cs = _lane_cumsum(lf_refs[i][...]) + carry
        f_ref[:, i * PAGE:(i + 1) * PAGE] = cs
        carry = cs[:, PAGE - 1:PAGE]
    carry_s[...] = carry

    @pl.when(step == pl.num_programs(1) - 1)
    def _():
        lane = lax.broadcasted_iota(jnp.int32, (ATT_HEADS, PAGE), 1)
        lfn = _log_sigmoid(cfn_ref[...] + fb_ref[...])
        lfn_ref[...] = lfn
        fn_ref[...] = _lane_cumsum(jnp.where(lane < tn, lfn, 0.0)) + carry
        col = lax.broadcasted_iota(jnp.int32, (nq, PAGE), 1)
        rowq = lax.broadcasted_iota(jnp.int32, (nq, PAGE), 0)
        sn_ref[...] = jnp.where((col <= rowq) & (col < tn), scores(ikn_ref[...]), NEG_INF)


def _sa(page_table, l, qrows, wrows, cfn, fb, ikn, kidx_pool, lfT_pool, pps, tn):
    B, n_pages = page_table.shape
    nsteps = n_pages // pps
    P = n_pages * PAGE
    nq = qrows.shape[1] // IDX_HEADS
    per_b = lambda shape: pl.BlockSpec((None,) + shape, lambda b, s, pt: (b,) + (0,) * len(shape))
    ki_specs = [pl.BlockSpec((None, None, IDX_HD, PAGE), lambda b, s, pt, i=i: (l, pt[b, s * pps + i], 0, 0))
                for i in range(pps)]
    lf_specs = [pl.BlockSpec((None, None, ATT_HEADS, PAGE), lambda b, s, pt, i=i: (l, pt[b, s * pps + i], 0, 0))
                for i in range(pps)]
    gs = pltpu.PrefetchScalarGridSpec(
        num_scalar_prefetch=1,
        grid=(B, nsteps),
        in_specs=[per_b((nq * IDX_HEADS, LANES)), per_b((nq * IDX_HEADS, 1)), per_b((ATT_HEADS, PAGE)),
                  pl.BlockSpec((ATT_HEADS, 1), lambda b, s, pt: (0, 0)), per_b((IDX_HD, PAGE))] + ki_specs + lf_specs,
        out_specs=[pl.BlockSpec((None, nq, pps * PAGE), lambda b, s, pt: (b, 0, s)),
                   pl.BlockSpec((None, ATT_HEADS, pps * PAGE), lambda b, s, pt: (b, 0, s)),
                   per_b((nq, PAGE)), per_b((ATT_HEADS, PAGE)), per_b((ATT_HEADS, PAGE))],
        scratch_shapes=[pltpu.VMEM((ATT_HEADS, 1), F32)])
    return pl.pallas_call(
        functools.partial(_sa_kernel, pps=pps, tn=tn),
        grid_spec=gs,
        out_shape=[jax.ShapeDtypeStruct((B, nq, P), F32), jax.ShapeDtypeStruct((B, ATT_HEADS, P), F32),
                   jax.ShapeDtypeStruct((B, nq, PAGE), F32), jax.ShapeDtypeStruct((B, ATT_HEADS, PAGE), F32),
                   jax.ShapeDtypeStruct((B, ATT_HEADS, PAGE), F32)],
        compiler_params=_params(("arbitrary", "arbitrary")),
        name="sample_scores",
    )(page_table, qrows, wrows, cfn, fb, ikn, *([kidx_pool] * pps), *([lfT_pool] * pps))


def _sb_kernel(s_ref, o_ref, *, kb, nblk, topk, past, nq, w_bits):
    C = s_ref.shape[1]
    o_ref[...] = s_ref[...]
    nadm = (past + 1 + lax.broadcasted_iota(jnp.int32, (1, C), 1) % nq).astype(F32)
    _topk_to_bias(o_ref, nblk, kb, C, topk, nadm, w_bits)


def _sb(s_allT, topk, past, nq):
    Wt, C = s_allT.shape
    nl = Wt // LANES
    div = max(d for d in range(1, 9) if nl % d == 0)
    kb = div * LANES
    return pl.pallas_call(
        functools.partial(_sb_kernel, kb=kb, nblk=Wt // kb, topk=topk, past=past, nq=nq,
                          w_bits=max(1, math.ceil(math.log2(Wt)))),
        grid=(1,),
        in_specs=[pl.BlockSpec((Wt, C), lambda i: (0, 0))],
        out_specs=pl.BlockSpec((Wt, C), lambda i: (0, 0)),
        out_shape=jax.ShapeDtypeStruct((Wt, C), F32),
        compiler_params=_params(("arbitrary",)),
        name="sample_topk",
    )(s_allT)


def _sc_kernel(pt_ref, fq_ref, dq_ref, fqc_ref, fp_ref, bp_ref, fn_ref, bn_ref,
               ckn_ref, cvn_ref, dkn_ref, dvn_ref, *rest, pps, past, tn, nq):
    fk_refs, fv_refs = rest[0:pps], rest[pps:2 * pps]
    dk_refs, dv_refs = rest[2 * pps:3 * pps], rest[3 * pps:4 * pps]
    yc_ref, yd_ref, accf, mf, lf, accd, md, ld = rest[4 * pps:]
    step = pl.program_id(1)

    @pl.when(step == 0)
    def _():
        for acc, m, l in ((accf, mf, lf), (accd, md, ld)):
            acc[...] = jnp.zeros_like(acc)
            m[...] = jnp.full_like(m, M_INIT)
            l[...] = jnp.zeros_like(l)

    def update(h, q8, kT, vT, bias, acc, m, l):
        qh = q8[:, h * ATT_HD:(h + 1) * ATT_HD]
        s = _dot(qh, kT) + bias * LOG2E
        m_prev = m[h]
        m_new = jnp.maximum(m_prev, jnp.max(s, axis=1, keepdims=True))
        p = jnp.exp2(s - m_new)
        alpha = jnp.exp2(m_prev - m_new)
        l[h] = alpha * l[h] + jnp.sum(p, axis=1, keepdims=True)
        m[h] = m_new
        acc[h] = alpha * acc[h] + _nt(p.astype(BF16), vT)

    def per_query_rows(xq):
        r8 = lax.broadcasted_iota(jnp.int32, (SUBLANES, xq.shape[1]), 0)
        x8 = jnp.zeros((SUBLANES, xq.shape[1]), F32)
        for qq in range(nq):
            x8 = jnp.where(r8 == qq, jnp.broadcast_to(xq[qq:qq + 1, :], x8.shape), x8)
        return x8

    def head_block(refs, h):
        return jnp.concatenate([r[h].astype(BF16) for r in refs], axis=1)

    fq8, dq8 = fq_ref[...], dq_ref[...]
    qidx = lax.broadcasted_iota(jnp.int32, (SUBLANES, 1), 0)
    kw = pps * PAGE
    col = lax.broadcasted_iota(jnp.int32, (SUBLANES, kw), 1)
    dist = (past + qidx - (step * kw + col)).astype(F32)
    selb = per_query_rows(bp_ref[...])
    for h in range(ATT_HEADS):
        update(h, fq8, head_block(fk_refs, h), head_block(fv_refs, h), fqc_ref[h] - fp_ref[h:h + 1, :],
               accf, mf, lf)
        update(h, dq8, head_block(dk_refs, h), head_block(dv_refs, h), selb - ALIBI[h] * dist, accd, md, ld)

    @pl.when(step == pl.num_programs(1) - 1)
    def _():
        coln = lax.broadcasted_iota(jnp.int32, (SUBLANES, PAGE), 1)
        ok = (coln <= qidx) & (coln < tn)
        distn = (qidx - coln).astype(F32)
        seln = jnp.where(coln < tn, per_query_rows(bn_ref[...]), NEG_INF)
        for h in range(ATT_HEADS):
            bias_f = jnp.where(ok, fqc_ref[h] - fn_ref[h:h + 1, :], NEG_INF)
            update(h, fq8, ckn_ref[h], cvn_ref[h], bias_f, accf, mf, lf)
            update(h, dq8, dkn_ref[h], dvn_ref[h], seln - ALIBI[h] * distn, accd, md, ld)
        for acc, l, out in ((accf, lf, yc_ref), (accd, ld, yd_ref)):
            out[...] = jnp.concatenate([acc[h] / l[h] for h in range(ATT_HEADS)], axis=1)


def _sc(page_table, l, fq8, dq8, fqc, f_past, b_past, f_new, b_new, ckn, cvn, dkn, dvn,
        fk_pool, fv_pool, dk_pool, dv_pool, pps, tn):
    B, n_pages = page_table.shape
    nsteps = n_pages // pps
    past = n_pages * PAGE
    nq = b_past.shape[1]
    W = GROUP_W
    per_b = lambda shape: pl.BlockSpec((None,) + shape, lambda b, s, pt: (b,) + (0,) * len(shape))
    page = lambda i: pl.BlockSpec((None, None, ATT_HEADS, ATT_HD, PAGE),
                                  lambda b, s, pt, i=i: (l, pt[b, s * pps + i], 0, 0, 0))
    pages = [page(i) for i in range(pps)]
    newpage = per_b((ATT_HEADS, ATT_HD, PAGE))
    stat = lambda w: pltpu.VMEM((ATT_HEADS, SUBLANES, w), F32)
    gs = pltpu.PrefetchScalarGridSpec(
        num_scalar_prefetch=1,
        grid=(B, nsteps),
        in_specs=[per_b((SUBLANES, W)), per_b((SUBLANES, W)), per_b((ATT_HEADS, SUBLANES, 1)),
                  pl.BlockSpec((None, ATT_HEADS, pps * PAGE), lambda b, s, pt: (b, 0, s)),
                  pl.BlockSpec((None, nq, pps * PAGE), lambda b, s, pt: (b, 0, s)),
                  per_b((ATT_HEADS, PAGE)), per_b((nq, PAGE)),
                  newpage, newpage, newpage, newpage] + pages * 4,
        out_specs=[per_b((SUBLANES, W)), per_b((SUBLANES, W))],
        scratch_shapes=[stat(ATT_HD), stat(1), stat(1), stat(ATT_HD), stat(1), stat(1)])
    return pl.pallas_call(
        functools.partial(_sc_kernel, pps=pps, past=past, tn=tn, nq=nq),
        grid_spec=gs,
        out_shape=[jax.ShapeDtypeStruct((B, SUBLANES, W), F32)] * 2,
        compiler_params=_params(("arbitrary", "arbitrary")),
        name="sample_attn",
    )(page_table, fq8, dq8, fqc, f_past, b_past, f_new, b_new, ckn, cvn, dkn, dvn,
      *([fk_pool] * pps), *([fv_pool] * pps), *([dk_pool] * pps), *([dv_pool] * pps))


def _post_kernel(x_ref, ya_ref, yb_ref, yc_ref, yd_ref, g1_ref, sh2_ref, sc2_ref, g2_ref,
                 gn_ref, wo_ref, npost_ref, nfpre_ref, wg_ref, wu_ref, wd_ref, nfpost_ref, o_ref, *, chunks):
    o = None
    for i, r in enumerate((ya_ref, yb_ref, yc_ref, yd_ref)):
        part = _rms(r[...], gn_ref[:, i * GROUP_W:(i + 1) * GROUP_W]).astype(BF16)
        d = _dot(part, wo_ref[i * GROUP_W:(i + 1) * GROUP_W, :])
        o = d if o is None else o + d
    x1 = x_ref[...] + g1_ref[...] * _rms(o, npost_ref[...])
    h2 = (_rms(x1, nfpre_ref[...]) * (1.0 + sc2_ref[...]) + sh2_ref[...]).astype(BF16)
    f = None
    for c0, c1 in chunks:
        gate = _dot(h2, wg_ref[:, c0:c1])
        up = _dot(h2, wu_ref[:, c0:c1])
        d = _dot((gate * _sigmoid(gate) * up).astype(BF16), wd_ref[c0:c1, :])
        f = d if f is None else f + d
    o_ref[...] = x1 + g2_ref[...] * _rms(f, nfpost_ref[...])


def _post(x, ya, yb, yc, yd, mod, tiles_per_mod, pw, tm):
    N, D = x.shape
    H = pw["wg"].shape[1]
    rows_mod = mod.shape[1]
    step = 1024
    chunks = tuple((c, min(c + step, H)) for c in range(0, H, step))
    tok = lambda w: pl.BlockSpec((tm, w), lambda i: (i, 0))
    const = lambda shape: pl.BlockSpec(shape, lambda i: (0,) * len(shape), pipeline_mode=pl.Buffered(1))
    return pl.pallas_call(
        functools.partial(_post_kernel, chunks=chunks),
        grid=(N // tm,),
        in_specs=[tok(D), tok(GROUP_W), tok(GROUP_W), tok(GROUP_W), tok(GROUP_W),
                  _mod_spec(rows_mod, D, tiles_per_mod, 2), _mod_spec(rows_mod, D, tiles_per_mod, 3),
                  _mod_spec(rows_mod, D, tiles_per_mod, 4), _mod_spec(rows_mod, D, tiles_per_mod, 5),
                  const((1, D)), const((D, D)), const((1, D)), const((1, D)),
                  const((D, H)), const((D, H)), const((H, D)), const((1, D))],
        out_specs=tok(D),
        out_shape=jax.ShapeDtypeStruct((N, D), F32),
        compiler_params=_params(("arbitrary",)),
        name="post",
    )(x, ya, yb, yc, yd, mod, mod, mod, mod, pw["gn"], pw["wo"], pw["npost"], pw["nfpre"],
      pw["wg"], pw["wu"], pw["wd"], pw["nfpost"])


def _block_diag(blocks):
    G, r, c = blocks.shape
    eye = jnp.eye(G, dtype=blocks.dtype)
    return (blocks[:, :, None, :] * eye[:, None, :, None]).reshape(G * r, G * c)


def _cat_weight(w):
    D = w.shape[0]
    widths = (GROUP_W, GROUP_W, GROUP_W, GROUP_W, GROUP_W, GROUP_W, ATT_HEADS, GROUP_W, GROUP_W, GROUP_W,
              IDX_HEADS * IDX_HD, IDX_HD, IDX_HEADS)
    offs = [0]
    for wd in widths:
        offs.append(offs[-1] + wd)
    pc = [w[:, offs[i]:offs[i + 1]] for i in range(len(widths))]
    a_x, a_g, b_u, c_q, c_k, c_v, c_f, d_q, d_k, d_v, i_q, i_k, i_w = pc
    iq_rep = jnp.tile(i_q.reshape(D, IDX_HEADS, 1, IDX_HD), (1, 1, LANES // IDX_HD, 1)).reshape(D, IDX_HEADS * LANES)
    ik_rep = jnp.tile(i_k, (1, LANES // IDX_HD))
    small = jnp.concatenate([i_k, c_f, i_w, jnp.zeros((D, LANES - IDX_HD - ATT_HEADS - IDX_HEADS), w.dtype)], axis=1)
    wcat = jnp.concatenate([a_x, a_g, b_u, c_q, c_k, c_v, d_q, d_k, d_v], axis=1).astype(BF16)
    widx = jnp.concatenate([iq_rep, ik_rep, small], axis=1)
    hi = lax.reduce_precision(widx, exponent_bits=8, mantissa_bits=7)
    return wcat, hi.astype(BF16), (widx - hi).astype(BF16)


def _layer_weights(l, p):
    row = lambda a: a[l].reshape(1, -1)
    lw = dict(conv_w=p["lru_conv_w"][l], conv_b=row(p["lru_conv_b"]),
              wr=_block_diag(p["lru_wr"][l]).astype(BF16), br=row(p["lru_br"]),
              wi=_block_diag(p["lru_wi"][l]).astype(BF16), bi=row(p["lru_bi"]), lam=row(p["lru_lambda"]))
    sw = dict(ldt=jnp.repeat(p["s5_log_dt"][l], S5_N).reshape(1, S5_STATE),
              are=p["s5_a_re"][l].reshape(1, S5_STATE), aim=p["s5_a_im"][l].reshape(1, S5_STATE),
              bre=_block_diag(jnp.swapaxes(p["s5_b_re"][l], 1, 2)).astype(BF16),
              bim=_block_diag(jnp.swapaxes(p["s5_b_im"][l], 1, 2)).astype(BF16),
              cre=_block_diag(jnp.swapaxes(p["s5_c_re"][l], 1, 2)).astype(BF16),
              cim=_block_diag(jnp.swapaxes(p["s5_c_im"][l], 1, 2)).astype(BF16),
              d=row(p["s5_d"]), gw=p["s5_glu_w"][l].astype(BF16), gb=row(p["s5_glu_b"]))
    pw = dict(gn=row(p["grp_norm"]), wo=p["w_out"][l].astype(BF16), npost=row(p["norm_mix_post"]),
              nfpre=row(p["norm_ffn_pre"]), wg=p["ffn_w_gate"][l].astype(BF16), wu=p["ffn_w_up"][l].astype(BF16),
              wd=p["ffn_w_down"][l].astype(BF16), nfpost=row(p["norm_ffn_post"]))
    return dict(win=_cat_weight(p["w_in"][l]), npre=row(p["norm_mix_pre"]), lru=lw, s5=sw, post=pw,
                fbias=p["fox_f_bias"][l])


def _largest_tile(n, cap):
    t = min(n, cap)
    while n % t:
        t //= 2
    return t


def _recurrent(ax, ag, bu, lru_buf, lru_h0, s5_r0, s5_i0, wts, t_real):
    B, Tp, _ = ax.shape
    tc = _largest_tile(Tp, 256)
    buf8 = jnp.pad(lru_buf, ((0, 0), (SUBLANES - (CONV_W - 1), 0), (0, 0)))
    ya, lru_h = _lru(ax, ag, buf8, lru_h0.reshape(B, 1, GROUP_W), wts["lru"], tc, t_real)
    yb, s5r, s5i = _s5(bu, s5_r0.reshape(B, 1, S5_STATE), s5_i0.reshape(B, 1, S5_STATE), wts["s5"], tc, t_real)
    return ya, yb, lru_h.reshape(B, GROUP_W), s5r.reshape(B, S5_NG, S5_N), s5i.reshape(B, S5_NG, S5_N)


def _prompt_layer(x, mod, wts, B, T):
    N, D = x.shape
    tm = _largest_tile(T, 512)
    (ax, ag, bu, cq, ckf, ckb, cvf, cvb, dq, dkf, dkb, dvf, dvb, iqp, ikp, sm) = _inproj(
        x, mod, T // tm, wts["npre"], *wts["win"], tm)
    seq = lambda a: a.reshape(B, T, a.shape[-1])
    zeros = lambda *s: jnp.zeros(s, F32)
    ax3 = seq(ax)
    ya, yb, lru_h, s5r, s5i = _recurrent(ax3, seq(ag), seq(bu), zeros(B, CONV_W - 1, GROUP_W), zeros(B, GROUP_W),
                                         zeros(B, S5_NG, S5_N), zeros(B, S5_NG, S5_N), wts, T)
    nb = T // LANES
    cf_rows = jnp.swapaxes(seq(sm)[:, :, SM_CF:SM_CF + ATT_HEADS], 1, 2).reshape(B, ATT_HEADS * nb, LANES)
    fb_rows = jnp.repeat(wts["fbias"], nb).reshape(ATT_HEADS * nb, 1)
    lf_rows, f_rows = _fcum(cf_rows, fb_rows, nb)
    f2_col = jnp.swapaxes(f_rows.reshape(B, ATT_HEADS, T), 1, 2)
    logf = jnp.swapaxes(lf_rows.reshape(B, ATT_HEADS, T), 1, 2)
    tr = lambda a: jnp.swapaxes(seq(a), 1, 2)
    tq = kb = _largest_tile(T, 512)
    yc = jnp.swapaxes(_fox(tr(cq), seq(ckb), _with_ones_rows(tr(cvb)), f2_col, tq, kb), 1, 2)
    topk = max(1, min(DSA_TOPK_MAX, T // 4))
    yd = jnp.swapaxes(_dsa(tr(iqp), seq(ikp), tr(sm), tr(dq), seq(dkb), _with_ones_rows(tr(dvb)), tq, kb, topk),
                      1, 2)
    flat = lambda a: a.reshape(N, GROUP_W)
    x2 = _post(x, flat(ya), flat(yb), flat(yc), flat(yd), mod, T // tm, wts["post"], tm)
    heads = lambda a: a.reshape(B, T, ATT_HEADS, ATT_HD)
    state = dict(lru_h=lru_h, lru_conv=ax3[:, T - (CONV_W - 1):, :], s5_re=s5r, s5_im=s5i,
                 fox_k=heads(ckf), fox_v=heads(cvf), fox_logf=logf, dsa_k=heads(dkf), dsa_v=heads(dvf),
                 dsa_kidx=seq(sm)[:, :, SM_IK:SM_IK + IDX_HD])
    return x2, state


def _sample_layer(x, mod, wts, l, B, T, past, caches, page_table):
    N, D = x.shape
    (ax, ag, bu, cq, ckf, ckb, cvf, cvb, dq, dkf, dkb, dvf, dvb, iqp, ikp, sm) = _inproj(
        x, mod, 1, wts["npre"], *wts["win"], N)
    seq = lambda a: a.reshape(B, T, a.shape[-1])
    tp = -(-T // SUBLANES) * SUBLANES
    padt = lambda a, n=tp: jnp.pad(seq(a), ((0, 0), (0, n - T), (0, 0)))
    ax3 = seq(ax)
    ya, yb, lru_h, s5r, s5i = _recurrent(padt(ax), padt(ag), padt(bu), past["lru_conv"], past["lru_h"],
                                         past["s5_re"], past["s5_im"], wts, T)
    sm3 = seq(sm)
    n_pages = page_table.shape[1]
    plen = n_pages * PAGE
    pps = _largest_tile(n_pages, 8)
    iq_rows = iqp.reshape(B, T * IDX_HEADS, LANES)
    wrows = sm3[:, :, SM_IW:SM_IW + IDX_HEADS].reshape(B, T * IDX_HEADS, 1)
    cfn = jnp.pad(jnp.swapaxes(sm3[:, :, SM_CF:SM_CF + ATT_HEADS], 1, 2), ((0, 0), (0, 0), (0, PAGE - T)))
    ikn = jnp.pad(jnp.swapaxes(sm3[:, :, SM_IK:SM_IK + IDX_HD], 1, 2), ((0, 0), (0, 0), (0, PAGE - T)))
    s_past, f_past, s_new, f_new, lf_new = _sa(page_table, l, iq_rows, wrows, cfn,
                                               wts["fbias"].reshape(ATT_HEADS, 1), ikn, caches["kidx"],
                                               caches["logfT"], pps, T)
    topk = max(1, min(DSA_TOPK_MAX, (plen + T) // 4))
    s_allT = jnp.transpose(jnp.concatenate([s_past, s_new], axis=2), (2, 0, 1)).reshape(plen + PAGE, B * T)
    bias_all = jnp.transpose(_sb(s_allT, topk, plen, T).reshape(plen + PAGE, B, T), (1, 2, 0))
    fqc = jnp.pad(f_new[:, :, :T], ((0, 0), (0, 0), (0, SUBLANES - T))).reshape(B, ATT_HEADS, SUBLANES, 1)
    newpage = lambda a: jnp.pad(jnp.transpose(a.reshape(B, T, ATT_HEADS, ATT_HD), (0, 2, 3, 1)),
                                ((0, 0), (0, 0), (0, 0), (0, PAGE - T)))
    yc8, yd8 = _sc(page_table, l, padt(cq, SUBLANES), padt(dq, SUBLANES), fqc, f_past, bias_all[:, :, :plen],
                   f_new, bias_all[:, :, plen:], newpage(ckb), newpage(cvb), newpage(dkb), newpage(dvb),
                   caches["fox_k"], caches["fox_v"], caches["dsa_k"], caches["dsa_v"], pps, T)
    flat = lambda a: a[:, :T, :].reshape(N, GROUP_W)
    x2 = _post(x, flat(ya), flat(yb), flat(yc8), flat(yd8), mod, 1, wts["post"], N)
    heads = lambda a: a.reshape(B, T, ATT_HEADS, ATT_HD)
    state = dict(lru_h=lru_h, lru_conv=ax3[:, T - (CONV_W - 1):, :], s5_re=s5r, s5_im=s5i,
                 fox_k=heads(ckf), fox_v=heads(cvf), fox_logf=jnp.swapaxes(lf_new[:, :, :T], 1, 2),
                 dsa_k=heads(dkf), dsa_v=heads(dvf), dsa_kidx=sm3[:, :, SM_IK:SM_IK + IDX_HD])
    return x2, state


def kernel(x_prompt, x_sample, state_lru_h, state_lru_conv, state_s5_re, state_s5_im, cache_fox_k, cache_fox_v, cache_fox_logf, cache_dsa_k, cache_dsa_v, cache_dsa_kidx, page_table, c_prompt, c_sample, ada_w, ada_b, norm_mix_pre, norm_mix_post, norm_ffn_pre, norm_ffn_post, w_in, lru_conv_w, lru_conv_b, lru_wr, lru_br, lru_wi, lru_bi, lru_lambda, s5_log_dt, s5_a_re, s5_a_im, s5_b_re, s5_b_im, s5_c_re, s5_c_im, s5_d, s5_glu_w, s5_glu_b, fox_f_bias, grp_norm, w_out, ffn_w_gate, ffn_w_up, ffn_w_down):
    p = dict(norm_mix_pre=norm_mix_pre, norm_mix_post=norm_mix_post, norm_ffn_pre=norm_ffn_pre,
             norm_ffn_post=norm_ffn_post, w_in=w_in, lru_conv_w=lru_conv_w, lru_conv_b=lru_conv_b, lru_wr=lru_wr,
             lru_br=lru_br, lru_wi=lru_wi, lru_bi=lru_bi, lru_lambda=lru_lambda, s5_log_dt=s5_log_dt,
             s5_a_re=s5_a_re, s5_a_im=s5_a_im, s5_b_re=s5_b_re, s5_b_im=s5_b_im, s5_c_re=s5_c_re, s5_c_im=s5_c_im,
             s5_d=s5_d, s5_glu_w=s5_glu_w, s5_glu_b=s5_glu_b, fox_f_bias=fox_f_bias, grp_norm=grp_norm,
             w_out=w_out, ffn_w_gate=ffn_w_gate, ffn_w_up=ffn_w_up, ffn_w_down=ffn_w_down)
    B, T, D = x_prompt.shape
    Bs, Ts, _ = x_sample.shape
    L = ada_w.shape[0]
    n_pool = cache_fox_k.shape[1]
    assert T % LANES == 0 and CONV_W - 1 <= Ts <= SUBLANES and cache_fox_k.shape[2] == PAGE

    c_all = jnp.concatenate([c_prompt, c_sample], axis=0)
    c_all = jnp.pad(c_all, ((0, -c_all.shape[0] % SUBLANES), (0, 0)))
    mod_all = _ada(c_all, ada_w, ada_b)

    pool = lambda a: jnp.transpose(a, (0, 1, 3, 4, 2))
    caches = dict(fox_k=pool(cache_fox_k), fox_v=pool(cache_fox_v), dsa_k=pool(cache_dsa_k), dsa_v=pool(cache_dsa_v),
                  kidx=jnp.swapaxes(cache_dsa_kidx, 2, 3), logfT=jnp.swapaxes(cache_fox_logf, 2, 3))

    xp = x_prompt.reshape(B * T, D)
    xs = x_sample.reshape(Bs * Ts, D)
    st_p, st_s = [], []
    for l in range(L):
        wts = _layer_weights(l, p)
        mod_p = mod_all[l, :B].reshape(B, 1, 6 * D)
        mod_s = jnp.repeat(mod_all[l, B:B + Bs], Ts, axis=0).reshape(1, Bs * Ts, 6 * D)
        xp, new_p = _prompt_layer(xp, mod_p, wts, B, T)
        past = dict(lru_h=state_lru_h[l], lru_conv=state_lru_conv[l], s5_re=state_s5_re[l], s5_im=state_s5_im[l])
        xs, new_s = _sample_layer(xs, mod_s, wts, l, Bs, Ts, past, caches, page_table)
        st_p.append(new_p)
        st_s.append(new_s)

    stk = lambda outs, name: jnp.stack([o[name] for o in outs])
    names = ("lru_h", "lru_conv", "s5_re", "s5_im", "fox_k", "fox_v", "fox_logf", "dsa_k", "dsa_v", "dsa_kidx")
    res = [xp.reshape(B, T, D), xs.reshape(Bs, Ts, D)]
    for name in names:
        res += [stk(st_p, name), stk(st_s, name)]
    return tuple(res)
```

```python
import functools
import math

import jax
import jax.numpy as jnp
from jax import lax
from jax.experimental import pallas as pl
from jax.experimental.pallas import tpu as pltpu

F32 = jnp.float32
BF16 = jnp.bfloat16

N_MIXERS = 4
GROUP_W = 256
LRU_HEADS = 4
CONV_W = 4
LRU_C = 8.0
S5_GROUP = 16
S5_NG = 16
S5_N = 64
S5_STATE = S5_NG * S5_N
ATT_HEADS = 4
ATT_HD = 64
IDX_HEADS = 8
IDX_HD = 32
DSA_TOPK_MAX = 256
PAGE = 128
NEG_INF = -1e30
M_INIT = -1e20
BIG = 3e38
ADM_CUT = -1e29
EPS = 1e-6
ATT_SCALE = ATT_HD ** -0.5
IDX_SCALE = IDX_HD ** -0.5
IDX_W_SCALE = IDX_HEADS ** -0.5
ALIBI = tuple(2.0 ** (-8.0 * (h + 1) / ATT_HEADS) for h in range(ATT_HEADS))
LOG2E = math.log2(math.e)
Q_SCALE = ATT_SCALE * LOG2E
BISECT_ITERS = 28
SEARCH_ITERS = 48

LANES = 128
SUBLANES = 8
VMEM_LIMIT = 56 * 1024 * 1024

C_AX, C_AG, C_BU, C_CQ, C_CK, C_CV, C_DQ, C_DK, C_DV = (i * GROUP_W for i in range(9))
P_CAT = 9 * GROUP_W
C_IQ = 0
C_IK = C_IQ + IDX_HEADS * LANES
C_SM = C_IK + LANES
P_IDX = C_SM + LANES
SM_IK, SM_CF, SM_IW = 0, IDX_HD, IDX_HD + ATT_HEADS

NT_DIMS = (((1,), (1,)), ((), ()))


def _nt(a, b):
    return lax.dot_general(a, b, NT_DIMS, preferred_element_type=F32)


def _dot(a, b):
    return jnp.dot(a, b, preferred_element_type=F32)


def _rms(x, g):
    return x * lax.rsqrt(jnp.mean(x * x, axis=-1, keepdims=True) + EPS) * g


def _softplus(z):
    return jnp.maximum(z, 0.0) + jnp.log1p(jnp.exp(-jnp.abs(z)))


def _log_sigmoid(z):
    return -_softplus(-z)


def _expm1(z):
    e = jnp.exp(z)
    one = e == 1.0
    return jnp.where(one, z, (e - 1.0) * z / jnp.where(one, 1.0, jnp.log(e)))


def _sigmoid(z):
    return jax.nn.sigmoid(z)


def _gelu(z):
    return jax.nn.gelu(z)


def _params(sem):
    return pltpu.CompilerParams(dimension_semantics=sem, vmem_limit_bytes=VMEM_LIMIT)


def _head_mask(h, width=GROUP_W):
    lane = lax.broadcasted_iota(jnp.int32, (1, width), 1)
    return (lane >= h * ATT_HD) & (lane < (h + 1) * ATT_HD)


FOLD_CHAINS = 4


def _fold_rows(x, axis_op):
    n, c = x.shape
    g = FOLD_CHAINS if n % (FOLD_CHAINS * SUBLANES) == 0 else 1
    x = axis_op(x.reshape(g, n // (g * SUBLANES), SUBLANES, c), axis=1)
    return axis_op(x, axis=0)


def _col_max(x):
    return jnp.max(_fold_rows(x, jnp.max), axis=0, keepdims=True)


def _lane_cumsum(x):
    lane = lax.broadcasted_iota(jnp.int32, x.shape, 1)
    s = 1
    while s < x.shape[1]:
        x = x + jnp.where(lane >= s, pltpu.roll(x, s, axis=1), 0.0)
        s *= 2
    return x


def _split(x):
    hi = x.astype(BF16)
    return hi, (x - hi.astype(F32)).astype(BF16)


def _ada_kernel(c_ref, w_ref, b_ref, o_ref):
    c = c_ref[...]
    sh, sl = _split(c * _sigmoid(c))
    wh, wl = _split(w_ref[...])
    o_ref[...] = (_dot(sl, wl) + _dot(sl, wh) + _dot(sh, wl)) + _dot(sh, wh) + b_ref[...]


def _ada(c_all, ada_w, ada_b):
    L, D, D6 = ada_w.shape
    Bp = c_all.shape[0]
    tn = 1024
    return pl.pallas_call(
        _ada_kernel,
        grid=(L, D6 // tn),
        in_specs=[pl.BlockSpec((Bp, D), lambda l, j: (0, 0)),
                  pl.BlockSpec((None, D, tn), lambda l, j: (l, 0, j)),
                  pl.BlockSpec((None, 1, tn), lambda l, j: (l, 0, j))],
        out_specs=pl.BlockSpec((None, Bp, tn), lambda l, j: (l, 0, j)),
        out_shape=jax.ShapeDtypeStruct((L, Bp, D6), F32),
        compiler_params=_params(("arbitrary", "arbitrary")),
        name="ada",
    )(c_all, ada_w, ada_b.reshape(L, 1, D6))


def _inproj_kernel(x_ref, sh_ref, sc_ref, g_ref, w_ref, whi_ref, wlo_ref,
                   ax_ref, ag_ref, bu_ref, cq_ref, ckf_ref, ckb_ref, cvf_ref, cvb_ref,
                   dq_ref, dkf_ref, dkb_ref, dvf_ref, dvb_ref, iq_ref, ik_ref, sm_ref):
    x = x_ref[...]
    h = _rms(x, g_ref[...]) * (1.0 + sc_ref[...]) + sh_ref[...]
    hb = h.astype(BF16)

    def mm(c0, width):
        return _dot(hb, w_ref[:, c0:c0 + width])

    ax_ref[...] = mm(C_AX, GROUP_W)
    ag_ref[...] = mm(C_AG, GROUP_W)
    bu_ref[...] = mm(C_BU, GROUP_W)
    cq_ref[...] = (mm(C_CQ, GROUP_W) * Q_SCALE).astype(BF16)
    ck = mm(C_CK, GROUP_W)
    ckf_ref[...] = ck
    ckb_ref[...] = ck.astype(BF16)
    cv = mm(C_CV, GROUP_W)
    cvf_ref[...] = cv
    cvb_ref[...] = cv.astype(BF16)
    dq_ref[...] = (mm(C_DQ, GROUP_W) * Q_SCALE).astype(BF16)
    dk = mm(C_DK, GROUP_W)
    dkf_ref[...] = dk
    dkb_ref[...] = dk.astype(BF16)
    dv = mm(C_DV, GROUP_W)
    dvf_ref[...] = dv
    dvb_ref[...] = dv.astype(BF16)

    hl = (h - hb.astype(F32)).astype(BF16)

    def mm3(c0, width):
        whi = whi_ref[:, c0:c0 + width]
        return _dot(hb, whi) + _dot(hl, whi) + _dot(hb, wlo_ref[:, c0:c0 + width])

    sm_ref[...] = mm3(C_SM, LANES)

    sub = (lax.broadcasted_iota(jnp.int32, (1, LANES), 1) // IDX_HD)
    for hh in range(IDX_HEADS):
        r = mm3(C_IQ + hh * LANES, LANES)
        hi = r.astype(BF16)
        lo = (r - hi.astype(F32)).astype(BF16)
        zero = jnp.zeros_like(hi)
        iq_ref[:, hh * LANES:(hh + 1) * LANES] = jnp.where(sub == 1, lo, jnp.where(sub == 3, zero, hi))
    r = mm3(C_IK, LANES)
    hi = r.astype(BF16)
    lo = (r - hi.astype(F32)).astype(BF16)
    ik_ref[...] = jnp.where(sub == 2, lo, jnp.where(sub == 3, jnp.zeros_like(hi), hi))


def _mod_spec(rows_mod, D, tiles_per_mod, piece):
    return pl.BlockSpec((None, rows_mod, D), lambda i: (i // tiles_per_mod, 0, piece))


def _inproj(x, mod, tiles_per_mod, g, wcat, widx_hi, widx_lo, tm):
    N, D = x.shape
    rows_mod = mod.shape[1]
    tok = lambda w: pl.BlockSpec((tm, w), lambda i: (i, 0))
    widths_dtypes = [(GROUP_W, F32)] * 3 + [(GROUP_W, BF16), (GROUP_W, F32), (GROUP_W, BF16), (GROUP_W, F32),
                                            (GROUP_W, BF16), (GROUP_W, BF16), (GROUP_W, F32), (GROUP_W, BF16),
                                            (GROUP_W, F32), (GROUP_W, BF16), (IDX_HEADS * LANES, BF16),
                                            (LANES, BF16), (LANES, F32)]
    return pl.pallas_call(
        _inproj_kernel,
        grid=(N // tm,),
        in_specs=[tok(D), _mod_spec(rows_mod, D, tiles_per_mod, 0), _mod_spec(rows_mod, D, tiles_per_mod, 1),
                  pl.BlockSpec((1, D), lambda i: (0, 0)),
                  pl.BlockSpec((D, P_CAT), lambda i: (0, 0)),
                  pl.BlockSpec((D, P_IDX), lambda i: (0, 0)),
                  pl.BlockSpec((D, P_IDX), lambda i: (0, 0))],
        out_specs=[tok(w) for w, _ in widths_dtypes],
        out_shape=[jax.ShapeDtypeStruct((N, w), dt) for w, dt in widths_dtypes],
        compiler_params=_params(("arbitrary",)),
        name="inproj",
    )(x, mod, mod, g, wcat, widx_hi, widx_lo)


def _lru_kernel(ax_ref, ag_ref, buf_ref, h0_ref, cw_ref, cb_ref, wr_ref, br_ref, wi_ref, bi_ref, lam_ref,
                ya_ref, hl_ref, xs, hcar, *, tc, last_row):
    c = pl.program_id(1)

    @pl.when(c == 0)
    def _():
        xs[0:SUBLANES, :] = buf_ref[...]
        hcar[...] = h0_ref[...]

    x = ax_ref[...]
    xs[SUBLANES:SUBLANES + tc, :] = x
    w = cw_ref[...]
    y = cb_ref[...] + xs[5:5 + tc, :] * w[0:1] + xs[6:6 + tc, :] * w[1:2] + xs[7:7 + tc, :] * w[2:3] + x * w[3:4]
    xs[0:SUBLANES, :] = xs[tc:tc + SUBLANES, :]

    yb = y.astype(BF16)
    r = _sigmoid(_dot(yb, wr_ref[...]) + br_ref[...])
    i = _sigmoid(_dot(yb, wi_ref[...]) + bi_ref[...])
    log_a = -LRU_C * r * _softplus(-lam_ref[...])
    a = jnp.exp(log_a)
    b = jnp.sqrt(-_expm1(2.0 * log_a)) * (i * y)

    row = lax.broadcasted_iota(jnp.int32, (tc, GROUP_W), 0)
    s = 1
    while s < tc:
        keep = row >= s
        a_sh = jnp.where(keep, pltpu.roll(a, s, axis=0), 1.0)
        b_sh = jnp.where(keep, pltpu.roll(b, s, axis=0), 0.0)
        b = b + a * b_sh
        a = a * a_sh
        s *= 2
    h = b + a * hcar[...]
    hcar[...] = h[tc - 1:tc, :]
    ya_ref[...] = h * _gelu(ag_ref[...])

    @pl.when(c == pl.num_programs(1) - 1)
    def _():
        hl_ref[...] = h[last_row:last_row + 1, :]


def _lru(ax, ag, buf8, h0, lw, tc, t_real):
    B, Tp, W = ax.shape
    seq = pl.BlockSpec((None, tc, W), lambda b, c: (b, c, 0))
    full = lambda shape: pl.BlockSpec(shape, lambda b, c: (0,) * len(shape))
    return pl.pallas_call(
        functools.partial(_lru_kernel, tc=tc, last_row=(t_real - 1) % tc),
        grid=(B, Tp // tc),
        in_specs=[seq, seq,
                  pl.BlockSpec((None, SUBLANES, W), lambda b, c: (b, 0, 0)),
                  pl.BlockSpec((None, 1, W), lambda b, c: (b, 0, 0)),
                  full((CONV_W, W)), full((1, W)), full((W, W)), full((1, W)), full((W, W)), full((1, W)),
                  full((1, W))],
        out_specs=[seq, pl.BlockSpec((None, 1, W), lambda b, c: (b, 0, 0))],
        out_shape=[jax.ShapeDtypeStruct((B, Tp, W), F32), jax.ShapeDtypeStruct((B, 1, W), F32)],
        scratch_shapes=[pltpu.VMEM((tc + SUBLANES, W), F32), pltpu.VMEM((1, W), F32)],
        compiler_params=_params(("arbitrary", "arbitrary")),
        name="lru",
    )(ax, ag, buf8, h0, lw["conv_w"], lw["conv_b"], lw["wr"], lw["br"], lw["wi"], lw["bi"], lw["lam"])


def _s5_kernel(u_ref, h0r_ref, h0i_ref, ldt_ref, are_ref, aim_ref, bre_ref, bim_ref, cre_ref, cim_ref,
               d_ref, gw_ref, gb_ref, y_ref, hlr_ref, hli_ref, hr_s, hi_s, car_r, car_i, *, tc, last_row):
    c = pl.program_id(1)

    @pl.when(c == 0)
    def _():
        car_r[...] = h0r_ref[...]
        car_i[...] = h0i_ref[...]

    dt = jnp.exp(ldt_ref[...])
    ar, ai = are_ref[...], aim_ref[...]
    mag = jnp.exp(dt * ar)
    abr, abi = mag * jnp.cos(dt * ai), mag * jnp.sin(dt * ai)
    den = ar * ar + ai * ai
    nr, ni = abr - 1.0, abi
    cr = (nr * ar + ni * ai) / den
    ci = (ni * ar - nr * ai) / den

    u = u_ref[...]
    ub = u.astype(BF16)
    pre = _dot(ub, bre_ref[...])
    pim = _dot(ub, bim_ref[...])
    hr_s[...] = cr * pre - ci * pim
    hi_s[...] = cr * pim + ci * pre

    def step(t, carry):
        hr, hi = carry
        nhr = abr * hr - abi * hi + hr_s[pl.ds(t, 1), :]
        nhi = abr * hi + abi * hr + hi_s[pl.ds(t, 1), :]
        hr_s[pl.ds(t, 1), :] = nhr
        hi_s[pl.ds(t, 1), :] = nhi
        return nhr, nhi

    hr, hi = lax.fori_loop(0, tc, step, (car_r[...], car_i[...]), unroll=8)
    car_r[...] = hr
    car_i[...] = hi

    hrv, hiv = hr_s[...], hi_s[...]
    y = _dot(hrv.astype(BF16), cre_ref[...]) - _dot(hiv.astype(BF16), cim_ref[...]) + d_ref[...] * u
    g = _gelu(y)
    y_ref[...] = g * _sigmoid(_dot(g.astype(BF16), gw_ref[...]) + gb_ref[...])

    @pl.when(c == pl.num_programs(1) - 1)
    def _():
        hlr_ref[...] = hr_s[last_row:last_row + 1, :]
        hli_ref[...] = hi_s[last_row:last_row + 1, :]


def _s5(u, h0r, h0i, sw, tc, t_real):
    B, Tp, W = u.shape
    S = S5_STATE
    seq = pl.BlockSpec((None, tc, W), lambda b, c: (b, c, 0))
    st = pl.BlockSpec((None, 1, S), lambda b, c: (b, 0, 0))
    full = lambda shape: pl.BlockSpec(shape, lambda b, c: (0,) * len(shape))
    return pl.pallas_call(
        functools.partial(_s5_kernel, tc=tc, last_row=(t_real - 1) % tc),
        grid=(B, Tp // tc),
        in_specs=[seq, st, st, full((1, S)), full((1, S)), full((1, S)), full((W, S)), full((W, S)),
                  full((S, W)), full((S, W)), full((1, W)), full((W, W)), full((1, W))],
        out_specs=[seq, st, st],
        out_shape=[jax.ShapeDtypeStruct((B, Tp, W), F32), jax.ShapeDtypeStruct((B, 1, S), F32),
                   jax.ShapeDtypeStruct((B, 1, S), F32)],
        scratch_shapes=[pltpu.VMEM((tc, S), F32), pltpu.VMEM((tc, S), F32),
                        pltpu.VMEM((1, S), F32), pltpu.VMEM((1, S), F32)],
        compiler_params=_params(("arbitrary", "arbitrary")),
        name="s5",
    )(u, h0r, h0i, sw["ldt"], sw["are"], sw["aim"], sw["bre"], sw["bim"], sw["cre"], sw["cim"],
      sw["d"], sw["gw"], sw["gb"])


def _fcum_kernel(cf_ref, fb_ref, lf_ref, f_ref, *, nb):
    lf = _log_sigmoid(cf_ref[...] + fb_ref[...])
    lf_ref[...] = lf
    cs = _lane_cumsum(lf)
    tot = jnp.broadcast_to(cs[:, LANES - 1:LANES], cs.shape)
    row = lax.broadcasted_iota(jnp.int32, cs.shape, 0) % nb
    inc = tot
    s = 1
    while s < nb:
        inc = inc + jnp.where(row >= s, pltpu.roll(inc, s, axis=0), 0.0)
        s *= 2
    f_ref[...] = (cs + (inc - tot)) * LOG2E


def _fcum(cf_rows, fb_rows, nb):
    B, R, _ = cf_rows.shape
    blk = pl.BlockSpec((None, R, LANES), lambda b: (b, 0, 0))
    return pl.pallas_call(
        functools.partial(_fcum_kernel, nb=nb),
        grid=(B,),
        in_specs=[blk, pl.BlockSpec((R, 1), lambda b: (0, 0))],
        out_specs=[blk, blk],
        out_shape=[jax.ShapeDtypeStruct((B, R, LANES), F32)] * 2,
        compiler_params=_params(("arbitrary",)),
        name="fcum",
    )(cf_rows, fb_rows)


ONES_ROWS = 16
VROWS = ATT_HD + ONES_ROWS


def _init_heads(qT_ref, qm_s, acc_s, m_s):
    qT = qT_ref[...]
    row = lax.broadcasted_iota(jnp.int32, (qT.shape[0], 1), 0)
    for h in range(ATT_HEADS):
        qm_s[h] = jnp.where(row // ATT_HD == h, qT, jnp.zeros_like(qT))
    acc_s[...] = jnp.zeros_like(acc_s)
    m_s[...] = jnp.full_like(m_s, M_INIT)


def _attend_block(kblk, vT_ref, k0, kb, qm_s, acc_s, m_s, s_s, p_s, bias_fn):
    for h in range(ATT_HEADS):
        s_s[h] = _dot(kblk, qm_s[h])
    m_new = []
    for h in range(ATT_HEADS):
        s = bias_fn(h, s_s[h])
        s_s[h] = s
        m_new.append(jnp.maximum(m_s[h], _col_max(s)))
    for h in range(ATT_HEADS):
        p_s[h] = jnp.exp2(s_s[h] - m_new[h]).astype(BF16)
    for h in range(ATT_HEADS):
        rows = slice(h * VROWS, (h + 1) * VROWS)
        pv = _dot(vT_ref[rows, pl.ds(k0, kb)], p_s[h])
        acc_s[rows, :] = jnp.exp2(m_s[h] - m_new[h]) * acc_s[rows, :] + pv
        m_s[h] = m_new[h]


def _finish_heads(o_ref, acc_s):
    for h in range(ATT_HEADS):
        base = h * VROWS
        o_ref[h * ATT_HD:(h + 1) * ATT_HD, :] = (acc_s[base:base + ATT_HD, :]
                                                 / acc_s[base + ATT_HD:base + ATT_HD + 1, :])


def _fox_kernel(qT_ref, k_ref, vT_ref, f2_ref, o_ref, qm_s, acc_s, m_s, s_s, p_s, *, tq, kb):
    q0 = pl.program_id(1) * tq
    _init_heads(qT_ref, qm_s, acc_s, m_s)
    kio = lax.broadcasted_iota(jnp.int32, (kb, tq), 0)
    qpos = q0 + lax.broadcasted_iota(jnp.int32, (kb, tq), 1)

    def block(j, masked):
        k0 = pl.multiple_of(j * kb, kb)
        fk = f2_ref[pl.ds(k0, kb), :]

        def bias(h, s):
            s = s - fk[:, h:h + 1]
            return jnp.where(k0 + kio <= qpos, s, NEG_INF) if masked else s

        _attend_block(k_ref[pl.ds(k0, kb), :], vT_ref, k0, kb, qm_s, acc_s, m_s, s_s, p_s, bias)
        return 0

    n_full = (q0 + 1) // kb
    n_blk = (q0 + tq - 1) // kb + 1
    lax.fori_loop(0, n_full, lambda j, c: block(j, False), 0)
    lax.fori_loop(n_full, n_blk, lambda j, c: block(j, True), 0)
    _finish_heads(o_ref, acc_s)


def _att_scratch(tq, kb):
    return [pltpu.VMEM((ATT_HEADS, GROUP_W, tq), BF16), pltpu.VMEM((ATT_HEADS * VROWS, tq), F32),
            pltpu.VMEM((ATT_HEADS, 1, tq), F32),
            pltpu.VMEM((ATT_HEADS, kb, tq), F32), pltpu.VMEM((ATT_HEADS, kb, tq), BF16)]


def _with_ones_rows(vT):
    B, _, T = vT.shape
    v4 = vT.reshape(B, ATT_HEADS, ATT_HD, T)
    return jnp.concatenate([v4, jnp.ones((B, ATT_HEADS, ONES_ROWS, T), vT.dtype)], axis=2).reshape(
        B, ATT_HEADS * VROWS, T)


def _resident(shape):
    return pl.BlockSpec((None,) + shape, lambda b, i: (b,) + (0,) * len(shape), pipeline_mode=pl.Buffered(1))


def _fox(qT, k, vT, f2col, tq, kb):
    B, W, T = qT.shape
    colblk = pl.BlockSpec((None, W, tq), lambda b, i: (b, 0, i))
    return pl.pallas_call(
        functools.partial(_fox_kernel, tq=tq, kb=kb),
        grid=(B, T // tq),
        in_specs=[colblk, _resident((T, W)), _resident((ATT_HEADS * VROWS, T)), _resident((T, ATT_HEADS))],
        out_specs=colblk,
        out_shape=jax.ShapeDtypeStruct((B, W, T), F32),
        scratch_shapes=_att_scratch(tq, kb),
        compiler_params=_params(("arbitrary", "arbitrary")),
        name="fox",
    )(qT, k, vT, f2col)


def _topk_to_bias(S, nblk, kb, C, k, nadm, w_bits):
    kf = float(k)
    kio = lax.broadcasted_iota(jnp.int32, (kb, C), 0)

    def reduce_blocks(fn, init):
        def body(j, c):
            k0 = pl.multiple_of(j * kb, kb)
            return fn(c, S[pl.ds(k0, kb), :], k0)
        return lax.fori_loop(0, nblk, body, init)

    fold_sum = lambda x: _fold_rows(x, jnp.sum)
    fold_max = lambda x: _fold_rows(x, jnp.max)
    fold_min = lambda x: _fold_rows(x, jnp.min)

    def count(pred):
        c = reduce_blocks(lambda c, x, k0: c + fold_sum(jnp.where(pred(x, k0), 1.0, 0.0)),
                          jnp.zeros((SUBLANES, C), F32))
        return jnp.sum(c, axis=0, keepdims=True)

    big8, zero8 = jnp.full((SUBLANES, C), BIG, F32), jnp.zeros((SUBLANES, C), F32)
    mx, mn, mp, np8, nz8 = reduce_blocks(
        lambda c, x, k0: (jnp.maximum(c[0], fold_max(x)),
                          jnp.minimum(c[1], fold_min(jnp.where(x > ADM_CUT, x, BIG))),
                          jnp.minimum(c[2], fold_min(jnp.where(x > 0.0, x, BIG))),
                          c[3] + fold_sum(jnp.where(x > 0.0, 1.0, 0.0)),
                          c[4] + fold_sum(jnp.where(x == 0.0, 1.0, 0.0))),
        (-big8, big8, big8, zero8, zero8))
    cmax = jnp.max(mx, axis=0, keepdims=True)
    cmin = jnp.min(mn, axis=0, keepdims=True)
    small = nadm <= kf
    cpos = jnp.sum(np8, axis=0, keepdims=True)
    cnn = cpos + jnp.sum(nz8, axis=0, keepdims=True)
    at_zero = (cpos < kf) & (cnn >= kf) & jnp.logical_not(small)
    settled = small | at_zero
    above = cpos >= kf
    lo0 = jnp.where(above, jnp.min(mp, axis=0, keepdims=True), cmin)
    clo0 = jnp.where(above, cpos, nadm)
    hi0 = jnp.where(above, cmax + jnp.maximum(1.0, jnp.abs(cmax)), 0.0)
    chi0 = jnp.where(above, 0.0, cnn)

    def any_col(flag):
        return jnp.max(jnp.where(flag, 1.0, 0.0)) > 0.5

    def probe(c, frac):
        lo, hi, clo, chi = c
        mid = lo + (hi - lo) * frac
        cm = count(lambda x, k0: x >= mid)
        ge = cm >= kf
        return jnp.where(ge, mid, lo), jnp.where(ge, hi, mid), jnp.where(ge, cm, clo), jnp.where(ge, chi, cm)

    def unresolved(clo, chi):
        g, r = clo - chi, kf - chi
        return (g > 2.5) & (r > 1.5) & (g - r > 0.5) & jnp.logical_not(settled)

    _, lo, hi, clo, chi = lax.while_loop(
        lambda c: (c[0] < BISECT_ITERS) & any_col(unresolved(c[3], c[4])),
        lambda c: (c[0] + 1,) + probe(c[1:], 0.5),
        (jnp.int32(0), lo0, hi0, clo0, chi0))

    top, bot = reduce_blocks(
        lambda c, x, k0: (jnp.maximum(c[0], fold_max(jnp.where(x < hi, x, -BIG))),
                          jnp.minimum(c[1], fold_min(jnp.where(x >= lo, x, BIG)))),
        (jnp.full((SUBLANES, C), -BIG, F32), jnp.full((SUBLANES, C), BIG, F32)))
    thr = jnp.where(kf - chi < 1.5, jnp.max(top, axis=0, keepdims=True), jnp.min(bot, axis=0, keepdims=True))
    thr = jnp.where(small, cmin, jnp.where(at_zero, 0.0, thr))
    cge = count(lambda x, k0: x >= thr)
    cgt = count(lambda x, k0: x > thr)
    missed = ((cgt >= kf) | (cge < kf)) & jnp.logical_not(settled)

    def slow(_):
        def open_(clo, chi):
            return (clo - chi > 1.5) & jnp.logical_not(settled)

        def search_body(c):
            it, lo, hi, clo, chi = c
            frac = jnp.clip((clo - kf + 0.5) / jnp.maximum(clo - chi, 1.0), 1.0 / 64, 63.0 / 64)
            return (it + 1,) + probe((lo, hi, clo, chi), jnp.where(it % 2 == 0, frac, 0.5))

        _, _, hi2, _, _ = lax.while_loop(lambda c: (c[0] < SEARCH_ITERS) & any_col(open_(c[3], c[4])), search_body,
                                         (jnp.int32(0), lo, hi, clo, chi))

        def unsat(cnt):
            return (cnt < kf) & jnp.logical_not(settled)

        def fix_body(c):
            t, cnt = c
            below = reduce_blocks(lambda m, x, k0: jnp.maximum(m, fold_max(jnp.where(x < t, x, -BIG))),
                                  jnp.full((SUBLANES, C), -BIG, F32))
            nt = jnp.where(unsat(cnt), jnp.max(below, axis=0, keepdims=True), t)
            return nt, count(lambda x, k0: x >= nt)

        t2, cge2 = lax.while_loop(lambda c: any_col(unsat(c[1])), fix_body, (hi2, jnp.zeros((1, C), F32)))
        cgt2 = count(lambda x, k0: x > t2)
        return jnp.where(missed, t2, thr), jnp.where(missed, cge2, cge), jnp.where(missed, cgt2, cgt)

    thr, cge, cgt = lax.cond(any_col(missed), slow, lambda _: (thr, cge, cgt), 0)
    need = kf - cgt
    ties = (cge > kf) & jnp.logical_not(small)
    big_pos = jnp.int32(2 ** 30)

    def tie_search(_):
        def it(_, c):
            lo_i, hi_i = c
            mid = (lo_i + hi_i) >> 1
            ok = count(lambda x, k0: (x == thr) & (k0 + kio <= mid)) >= need
            return jnp.where(ok, lo_i, mid), jnp.where(ok, mid, hi_i)
        _, hi_i = lax.fori_loop(0, w_bits + 1, it, (jnp.full((1, C), -1, jnp.int32),
                                                    jnp.zeros((1, C), jnp.int32) + (nblk * kb - 1)))
        return jnp.where(ties, hi_i, big_pos)

    jcut = lax.cond(any_col(ties), tie_search, lambda _: jnp.full((1, C), big_pos, jnp.int32), 0)

    def write(j, _):
        k0 = pl.multiple_of(j * kb, kb)
        x = S[pl.ds(k0, kb), :]
        sel = (x > thr) | ((x == thr) & (k0 + kio <= jcut))
        S[pl.ds(k0, kb), :] = jnp.where(sel, 0.0, NEG_INF)
        return 0

    lax.fori_loop(0, nblk, write, 0)


def _dsa_kernel(iqT_ref, ik_ref, smT_ref, qT_ref, k_ref, vT_ref, o_ref, S, qm_s, acc_s, m_s, s_s, p_s,
                *, tq, kb, topk, w_bits):
    q0 = pl.program_id(1) * tq
    n_blk = (q0 + tq - 1) // kb + 1
    kio = lax.broadcasted_iota(jnp.int32, (kb, tq), 0)
    qpos = q0 + lax.broadcasted_iota(jnp.int32, (kb, tq), 1)
    wc = smT_ref[SM_IW:SM_IW + IDX_HEADS, :] * (IDX_W_SCALE * IDX_SCALE)

    def scores(j, _):
        k0 = pl.multiple_of(j * kb, kb)
        kblk = ik_ref[pl.ds(k0, kb), :]
        sc = jnp.zeros((kb, tq), F32)
        for hh in range(IDX_HEADS):
            d = _dot(kblk, iqT_ref[hh * LANES:(hh + 1) * LANES, :])
            sc = sc + wc[hh:hh + 1, :] * jnp.maximum(d, 0.0)
        S[pl.ds(k0, kb), :] = jnp.where(k0 + kio <= qpos, sc, NEG_INF)
        return 0

    lax.fori_loop(0, n_blk, scores, 0)

    nadm = (q0 + 1 + lax.broadcasted_iota(jnp.int32, (1, tq), 1)).astype(F32)
    _topk_to_bias(S, n_blk, kb, tq, topk, nadm, w_bits)

    _init_heads(qT_ref, qm_s, acc_s, m_s)
    kcol =lax.broadcasted_iota(jnp.int32, (kb, 1), 0)

    def attend(j, _):
        k0 = pl.multiple_of(j * kb, kb)
        sel = S[pl.ds(k0, kb), :]
        kpos = (k0 + kcol).astype(F32)

        def bias(h, s):
            return s + (sel + (ALIBI[h] * LOG2E) * kpos)

        _attend_block(k_ref[pl.ds(k0, kb), :], vT_ref, k0, kb, qm_s, acc_s, m_s, s_s, p_s, bias)
        return 0

    lax.fori_loop(0, n_blk, attend, 0)
    _finish_heads(o_ref, acc_s)


def _dsa(iqT, ikp, smT, qT, k, vT, tq, kb, topk):
    B, W, T = qT.shape
    col = lambda r: pl.BlockSpec((None, r, tq), lambda b, i: (b, 0, i))
    return pl.pallas_call(
        functools.partial(_dsa_kernel, tq=tq, kb=kb, topk=topk, w_bits=max(1, math.ceil(math.log2(T)))),
        grid=(B, T // tq),
        in_specs=[col(IDX_HEADS * LANES), _resident((T, LANES)), col(LANES), col(W),
                  _resident((T, W)), _resident((ATT_HEADS * VROWS, T))],
        out_specs=col(W),
        out_shape=jax.ShapeDtypeStruct((B, W, T), F32),
        scratch_shapes=[pltpu.VMEM((T, tq), F32)] + _att_scratch(tq, kb),
        compiler_params=_params(("arbitrary", "arbitrary")),
        name="dsa",
    )(iqT, ikp, smT, qT, k, vT)


def _sa_kernel(pt_ref, q_ref, w_ref, cfn_ref, fb_ref, ikn_ref, *rest, pps, tn):
    ki_refs, lf_refs = rest[:pps], rest[pps:2 * pps]
    s_ref, f_ref, sn_ref, fn_ref, lfn_ref, carry_s = rest[2 * pps:]
    step = pl.program_id(1)

    @pl.when(step == 0)
    def _():
        carry_s[...] = jnp.zeros_like(carry_s)

    qp = q_ref[...]
    qh = qp[:, 0:IDX_HD]
    ql = qp[:, IDX_HD:2 * IDX_HD]
    w = w_ref[...] * IDX_W_SCALE
    nq = qp.shape[0] // IDX_HEADS

    def scores(kT):
        kh, kl = _split(kT)
        d = _dot(qh, kh) + _dot(ql, kh) + _dot(qh, kl)
        r = jnp.maximum(d * IDX_SCALE, 0.0) * w
        return jnp.sum(r.reshape(nq, IDX_HEADS, kT.shape[1]), axis=1)

    s_ref[...] = scores(jnp.concatenate([r[...] for r in ki_refs], axis=1))
    carry = carry_s[...]
    for i in range(pps):
        cs = _lane_cumsum(lf_refs[i][...]) + carry
        f_ref[:, i * PAGE:(i + 1) * PAGE] = cs
        carry = cs[:, PAGE - 1:PAGE]
    carry_s[...] = carry

    @pl.when(step == pl.num_programs(1) - 1)
    def _():
        lane = lax.broadcasted_iota(jnp.int32, (ATT_HEADS, PAGE), 1)
        lfn = _log_sigmoid(cfn_ref[...] + fb_ref[...])
        lfn_ref[...] = lfn
        fn_ref[...] = _lane_cumsum(jnp.where(lane < tn, lfn, 0.0)) + carry
        col = lax.broadcasted_iota(jnp.int32, (nq, PAGE), 1)
        rowq = lax.broadcasted_iota(jnp.int32, (nq, PAGE), 0)
        sn_ref[...] = jnp.where((col <= rowq) & (col < tn), scores(ikn_ref[...]), NEG_INF)


def _sa(page_table, l, qrows, wrows, cfn, fb, ikn, kidx_pool, lfT_pool, pps, tn):
    B, n_pages = page_table.shape
    nsteps = n_pages // pps
    P = n_pages * PAGE
    nq = qrows.shape[1] // IDX_HEADS
    per_b = lambda shape: pl.BlockSpec((None,) + shape, lambda b, s, pt: (b,) + (0,) * len(shape))
    ki_specs = [pl.BlockSpec((None, None, IDX_HD, PAGE), lambda b, s, pt, i=i: (l, pt[b, s * pps + i], 0, 0))
                for i in range(pps)]
    lf_specs = [pl.BlockSpec((None, None, ATT_HEADS, PAGE), lambda b, s, pt, i=i: (l, pt[b, s * pps + i], 0, 0))
                for i in range(pps)]
    gs = pltpu.PrefetchScalarGridSpec(
        num_scalar_prefetch=1,
        grid=(B, nsteps),
        in_specs=[per_b((nq * IDX_HEADS, LANES)), per_b((nq * IDX_HEADS, 1)), per_b((ATT_HEADS, PAGE)),
                  pl.BlockSpec((ATT_HEADS, 1), lambda b, s, pt: (0, 0)), per_b((IDX_HD, PAGE))] + ki_specs + lf_specs,
        out_specs=[pl.BlockSpec((None, nq, pps * PAGE), lambda b, s, pt: (b, 0, s)),
                   pl.BlockSpec((None, ATT_HEADS, pps * PAGE), lambda b, s, pt: (b, 0, s)),
                   per_b((nq, PAGE)), per_b((ATT_HEADS, PAGE)), per_b((ATT_HEADS, PAGE))],
        scratch_shapes=[pltpu.VMEM((ATT_HEADS, 1), F32)])
    return pl.pallas_call(
        functools.partial(_sa_kernel, pps=pps, tn=tn),
        grid_spec=gs,
        out_shape=[jax.ShapeDtypeStruct((B, nq, P), F32), jax.ShapeDtypeStruct((B, ATT_HEADS, P), F32),
                   jax.ShapeDtypeStruct((B, nq, PAGE), F32), jax.ShapeDtypeStruct((B, ATT_HEADS, PAGE), F32),
                   jax.ShapeDtypeStruct((B, ATT_HEADS, PAGE), F32)],
        compiler_params=_params(("arbitrary", "arbitrary")),
        name="sample_scores",
    )(page_table, qrows, wrows, cfn, fb, ikn, *([kidx_pool] * pps), *([lfT_pool] * pps))


def _sb_kernel(s_ref, o_ref, *, kb, nblk, topk, past, nq, w_bits):
    C = s_ref.shape[1]
    o_ref[...] = s_ref[...]
    nadm = (past + 1 + lax.broadcasted_iota(jnp.int32, (1, C), 1) % nq).astype(F32)
    _topk_to_bias(o_ref, nblk, kb, C, topk, nadm, w_bits)


def _sb(s_allT, topk, past, nq):
    Wt, C = s_allT.shape
    nl = Wt // LANES
    div = max(d for d in range(1, 9) if nl % d == 0)
    kb = div * LANES
    return pl.pallas_call(
        functools.partial(_sb_kernel, kb=kb, nblk=Wt // kb, topk=topk, past=past, nq=nq,
                          w_bits=max(1, math.ceil(math.log2(Wt)))),
        grid=(1,),
        in_specs=[pl.BlockSpec((Wt, C), lambda i: (0, 0))],
        out_specs=pl.BlockSpec((Wt, C), lambda i: (0, 0)),
        out_shape=jax.ShapeDtypeStruct((Wt, C), F32),
        compiler_params=_params(("arbitrary",)),
        name="sample_topk",
    )(s_allT)


def _sc_kernel(pt_ref, fq_ref, dq_ref, fqc_ref, fp_ref, bp_ref, fn_ref, bn_ref,
               ckn_ref, cvn_ref, dkn_ref, dvn_ref, *rest, pps, past, tn, nq):
    fk_refs, fv_refs = rest[0:pps], rest[pps:2 * pps]
    dk_refs, dv_refs = rest[2 * pps:3 * pps], rest[3 * pps:4 * pps]
    yc_ref, yd_ref, qf_s, qd_s, accf, mf, lf, accd, md, ld = rest[4 * pps:]
    step = pl.program_id(1)
    R = ATT_HEADS * SUBLANES

    def stack_heads(q8):
        return jnp.concatenate([jnp.where(_head_mask(h), q8, jnp.zeros_like(q8)) for h in range(ATT_HEADS)], axis=0)

    @pl.when(step == 0)
    def _():
        qf_s[...] = stack_heads(fq_ref[...])
        qd_s[...] = stack_heads(dq_ref[...])
        for acc, m, l in ((accf, mf, lf), (accd, md, ld)):
            acc[...] = jnp.zeros_like(acc)
            m[...] = jnp.full_like(m, M_INIT)
            l[...] = jnp.zeros_like(l)

    def update(q_s, kT, vT, bias, acc, m, l):
        s = _dot(q_s[...], kT) + bias * LOG2E
        m_prev = m[...]
        m_new = jnp.maximum(m_prev, jnp.max(s, axis=1, keepdims=True))
        p = jnp.exp2(s - m_new)
        alpha = jnp.exp2(m_prev - m_new)
        l[...] = alpha * l[...] + jnp.sum(p, axis=1, keepdims=True)
        m[...] = m_new
        acc[...] = alpha * acc[...] + _nt(p.astype(BF16), vT)

    def per_head_rows(x4):
        return jnp.concatenate([jnp.broadcast_to(x4[h:h + 1, :], (SUBLANES, x4.shape[1]))
                                for h in range(ATT_HEADS)], axis=0)

    def per_query_rows(xq):
        r8 = lax.broadcasted_iota(jnp.int32, (SUBLANES, xq.shape[1]), 0)
        x8 = jnp.zeros((SUBLANES, xq.shape[1]), F32)
        for qq in range(nq):
            x8 = jnp.where(r8 == qq, jnp.broadcast_to(xq[qq:qq + 1, :], x8.shape), x8)
        return jnp.concatenate([x8] * ATT_HEADS, axis=0)

    rowi = lax.broadcasted_iota(jnp.int32, (R, 1), 0)
    qidx = rowi % SUBLANES
    slope = jnp.zeros((R, 1), F32)
    for h in range(ATT_HEADS):
        slope = jnp.where(rowi // SUBLANES == h, ALIBI[h], slope)

    def slab(ref):
        return ref[...].reshape(GROUP_W, ref.shape[-1]).astype(BF16)

    def blocks(refs):
        return jnp.concatenate([slab(r) for r in refs], axis=1)

    kw = pps * PAGE
    col = lax.broadcasted_iota(jnp.int32, (R, kw), 1)
    update(qf_s, blocks(fk_refs), blocks(fv_refs), fqc_ref[...] - per_head_rows(fp_ref[...]), accf, mf, lf)
    dist = (past + qidx - (step * kw + col)).astype(F32)
    update(qd_s, blocks(dk_refs), blocks(dv_refs), per_query_rows(bp_ref[...]) - slope * dist, accd, md, ld)

    @pl.when(step == pl.num_programs(1) - 1)
    def _():
        coln = lax.broadcasted_iota(jnp.int32, (R, PAGE), 1)
        ok = (coln <= qidx) & (coln < tn)
        bias_f = jnp.where(ok, fqc_ref[...] - per_head_rows(fn_ref[...]), NEG_INF)
        update(qf_s, slab(ckn_ref), slab(cvn_ref), bias_f, accf, mf, lf)
        distn = (qidx - coln).astype(F32)
        bias_d = jnp.where(coln < tn, per_query_rows(bn_ref[...]), NEG_INF) - slope * distn
        update(qd_s, slab(dkn_ref), slab(dvn_ref), bias_d, accd, md, ld)
        for acc, l, out in ((accf, lf, yc_ref), (accd, ld, yd_ref)):
            o = acc[...] / l[...]
            y = jnp.zeros((SUBLANES, GROUP_W), F32)
            for h in range(ATT_HEADS):
                y = jnp.where(_head_mask(h), o[h * SUBLANES:(h + 1) * SUBLANES, :], y)
            out[...] = y


def _sc(page_table, l, fq8, dq8, fqc, f_past, b_past, f_new, b_new, ckn, cvn, dkn, dvn,
        fk_pool, fv_pool, dk_pool, dv_pool, pps, tn):
    B, n_pages = page_table.shape
    nsteps = n_pages // pps
    past = n_pages * PAGE
    nq = b_past.shape[1]
    W = GROUP_W
    per_b = lambda shape: pl.BlockSpec((None,) + shape, lambda b, s, pt: (b,) + (0,) * len(shape))
    page = lambda i: pl.BlockSpec((None, None, ATT_HEADS, ATT_HD, PAGE),
                                  lambda b, s, pt, i=i: (l, pt[b, s * pps + i], 0, 0, 0))
    pages = [page(i) for i in range(pps)]
    newpage = per_b((ATT_HEADS, ATT_HD, PAGE))
    R = ATT_HEADS * SUBLANES
    gs = pltpu.PrefetchScalarGridSpec(
        num_scalar_prefetch=1,
        grid=(B, nsteps),
        in_specs=[per_b((SUBLANES, W)), per_b((SUBLANES, W)), per_b((R, 1)),
                  pl.BlockSpec((None, ATT_HEADS, pps * PAGE), lambda b, s, pt: (b, 0, s)),
                  pl.BlockSpec((None, nq, pps * PAGE), lambda b, s, pt: (b, 0, s)),
                  per_b((ATT_HEADS, PAGE)), per_b((nq, PAGE)),
                  newpage, newpage, newpage, newpage] + pages * 4,
        out_specs=[per_b((SUBLANES, W)), per_b((SUBLANES, W))],
        scratch_shapes=[pltpu.VMEM((R, W), BF16), pltpu.VMEM((R, W), BF16),
                        pltpu.VMEM((R, W), F32), pltpu.VMEM((R, 1), F32), pltpu.VMEM((R, 1), F32),
                        pltpu.VMEM((R, W), F32), pltpu.VMEM((R, 1), F32), pltpu.VMEM((R, 1), F32)])
    return pl.pallas_call(
        functools.partial(_sc_kernel, pps=pps, past=past, tn=tn, nq=nq),
        grid_spec=gs,
        out_shape=[jax.ShapeDtypeStruct((B, SUBLANES, W), F32)] * 2,
        compiler_params=_params(("arbitrary", "arbitrary")),
        name="sample_attn",
    )(page_table, fq8, dq8, fqc, f_past, b_past, f_new, b_new, ckn, cvn, dkn, dvn,
      *([fk_pool] * pps), *([fv_pool] * pps), *([dk_pool] * pps), *([dv_pool] * pps))


def _post_kernel(x_ref, ya_ref, yb_ref, yc_ref, yd_ref, g1_ref, sh2_ref, sc2_ref, g2_ref,
                 gn_ref, wo_ref, npost_ref, nfpre_ref, wg_ref, wu_ref, wd_ref, nfpost_ref, o_ref, *, chunks):
    o = None
    for i, r in enumerate((ya_ref, yb_ref, yc_ref, yd_ref)):
        part = _rms(r[...], gn_ref[:, i * GROUP_W:(i + 1) * GROUP_W]).astype(BF16)
        d = _dot(part, wo_ref[i * GROUP_W:(i + 1) * GROUP_W, :])
        o = d if o is None else o + d
    x1 = x_ref[...] + g1_ref[...] * _rms(o, npost_ref[...])
    h2 = (_rms(x1, nfpre_ref[...]) * (1.0 + sc2_ref[...]) + sh2_ref[...]).astype(BF16)
    f = None
    for c0, c1 in chunks:
        gate = _dot(h2, wg_ref[:, c0:c1])
        up = _dot(h2, wu_ref[:, c0:c1])
        d = _dot((gate * _sigmoid(gate) * up).astype(BF16), wd_ref[c0:c1, :])
        f = d if f is None else f + d
    o_ref[...] = x1 + g2_ref[...] * _rms(f, nfpost_ref[...])


def _post(x, ya, yb, yc, yd, mod, tiles_per_mod, pw, tm):
    N, D = x.shape
    H = pw["wg"].shape[1]
    rows_mod = mod.shape[1]
    step = 1024
    chunks = tuple((c, min(c + step, H)) for c in range(0, H, step))
    tok = lambda w: pl.BlockSpec((tm, w), lambda i: (i, 0))
    const = lambda shape: pl.BlockSpec(shape, lambda i: (0,) * len(shape), pipeline_mode=pl.Buffered(1))
    return pl.pallas_call(
        functools.partial(_post_kernel, chunks=chunks),
        grid=(N // tm,),
        in_specs=[tok(D), tok(GROUP_W), tok(GROUP_W), tok(GROUP_W), tok(GROUP_W),
                  _mod_spec(rows_mod, D, tiles_per_mod, 2), _mod_spec(rows_mod, D, tiles_per_mod, 3),
                  _mod_spec(rows_mod, D, tiles_per_mod, 4), _mod_spec(rows_mod, D, tiles_per_mod, 5),
                  const((1, D)), const((D, D)), const((1, D)), const((1, D)),
                  const((D, H)), const((D, H)), const((H, D)), const((1, D))],
        out_specs=tok(D),
        out_shape=jax.ShapeDtypeStruct((N, D), F32),
        compiler_params=_params(("arbitrary",)),
        name="post",
    )(x, ya, yb, yc, yd, mod, mod, mod, mod, pw["gn"], pw["wo"], pw["npost"], pw["nfpre"],
      pw["wg"], pw["wu"], pw["wd"], pw["nfpost"])


def _block_diag(blocks):
    G, r, c = blocks.shape
    eye = jnp.eye(G, dtype=blocks.dtype)
    return (blocks[:, :, None, :] * eye[:, None, :, None]).reshape(G * r, G * c)


def _cat_weight(w):
    D = w.shape[0]
    widths = (GROUP_W, GROUP_W, GROUP_W, GROUP_W, GROUP_W, GROUP_W, ATT_HEADS, GROUP_W, GROUP_W, GROUP_W,
              IDX_HEADS * IDX_HD, IDX_HD, IDX_HEADS)
    offs = [0]
    for wd in widths:
        offs.append(offs[-1] + wd)
    pc = [w[:, offs[i]:offs[i + 1]] for i in range(len(widths))]
    a_x, a_g, b_u, c_q, c_k, c_v, c_f, d_q, d_k, d_v, i_q, i_k, i_w = pc
    iq_rep = jnp.tile(i_q.reshape(D, IDX_HEADS, 1, IDX_HD), (1, 1, LANES // IDX_HD, 1)).reshape(D, IDX_HEADS * LANES)
    ik_rep = jnp.tile(i_k, (1, LANES // IDX_HD))
    small = jnp.concatenate([i_k, c_f, i_w, jnp.zeros((D, LANES - IDX_HD - ATT_HEADS - IDX_HEADS), w.dtype)], axis=1)
    wcat = jnp.concatenate([a_x, a_g, b_u, c_q, c_k, c_v, d_q, d_k, d_v], axis=1).astype(BF16)
    widx = jnp.concatenate([iq_rep, ik_rep, small], axis=1)
    hi = lax.reduce_precision(widx, exponent_bits=8, mantissa_bits=7)
    return wcat, hi.astype(BF16), (widx - hi).astype(BF16)


def _layer_weights(l, p):
    row = lambda a: a[l].reshape(1, -1)
    lw = dict(conv_w=p["lru_conv_w"][l], conv_b=row(p["lru_conv_b"]),
              wr=_block_diag(p["lru_wr"][l]).astype(BF16), br=row(p["lru_br"]),
              wi=_block_diag(p["lru_wi"][l]).astype(BF16), bi=row(p["lru_bi"]), lam=row(p["lru_lambda"]))
    sw = dict(ldt=jnp.repeat(p["s5_log_dt"][l], S5_N).reshape(1, S5_STATE),
              are=p["s5_a_re"][l].reshape(1, S5_STATE), aim=p["s5_a_im"][l].reshape(1, S5_STATE),
              bre=_block_diag(jnp.swapaxes(p["s5_b_re"][l], 1, 2)).astype(BF16),
              bim=_block_diag(jnp.swapaxes(p["s5_b_im"][l], 1, 2)).astype(BF16),
              cre=_block_diag(jnp.swapaxes(p["s5_c_re"][l], 1, 2)).astype(BF16),
              cim=_block_diag(jnp.swapaxes(p["s5_c_im"][l], 1, 2)).astype(BF16),
              d=row(p["s5_d"]), gw=p["s5_glu_w"][l].astype(BF16), gb=row(p["s5_glu_b"]))
    pw = dict(gn=row(p["grp_norm"]), wo=p["w_out"][l].astype(BF16), npost=row(p["norm_mix_post"]),
              nfpre=row(p["norm_ffn_pre"]), wg=p["ffn_w_gate"][l].astype(BF16), wu=p["ffn_w_up"][l].astype(BF16),
              wd=p["ffn_w_down"][l].astype(BF16), nfpost=row(p["norm_ffn_post"]))
    return dict(win=_cat_weight(p["w_in"][l]), npre=row(p["norm_mix_pre"]), lru=lw, s5=sw, post=pw,
                fbias=p["fox_f_bias"][l])


def _largest_tile(n, cap):
    t = min(n, cap)
    while n % t:
        t //= 2
    return t


def _recurrent(ax, ag, bu, lru_buf, lru_h0, s5_r0, s5_i0, wts, t_real):
    B, Tp, _ = ax.shape
    tc = _largest_tile(Tp, 256)
    buf8 = jnp.pad(lru_buf, ((0, 0), (SUBLANES - (CONV_W - 1), 0), (0, 0)))
    ya, lru_h = _lru(ax, ag, buf8, lru_h0.reshape(B, 1, GROUP_W), wts["lru"], tc, t_real)
    yb, s5r, s5i = _s5(bu, s5_r0.reshape(B, 1, S5_STATE), s5_i0.reshape(B, 1, S5_STATE), wts["s5"], tc, t_real)
    return ya, yb, lru_h.reshape(B, GROUP_W), s5r.reshape(B, S5_NG, S5_N), s5i.reshape(B, S5_NG, S5_N)


def _prompt_layer(x, mod, wts, B, T):
    N, D = x.shape
    tm = _largest_tile(T, 512)
    (ax, ag, bu, cq, ckf, ckb, cvf, cvb, dq, dkf, dkb, dvf, dvb, iqp, ikp, sm) = _inproj(
        x, mod, T // tm, wts["npre"], *wts["win"], tm)
    seq = lambda a: a.reshape(B, T, a.shape[-1])
    zeros = lambda *s: jnp.zeros(s, F32)
    ax3 = seq(ax)
    ya, yb, lru_h, s5r, s5i = _recurrent(ax3, seq(ag), seq(bu), zeros(B, CONV_W - 1, GROUP_W), zeros(B, GROUP_W),
                                         zeros(B, S5_NG, S5_N), zeros(B, S5_NG, S5_N), wts, T)
    nb = T // LANES
    cf_rows = jnp.swapaxes(seq(sm)[:, :, SM_CF:SM_CF + ATT_HEADS], 1, 2).reshape(B, ATT_HEADS * nb, LANES)
    fb_rows = jnp.repeat(wts["fbias"], nb).reshape(ATT_HEADS * nb, 1)
    lf_rows, f_rows = _fcum(cf_rows, fb_rows, nb)
    f2_col = jnp.swapaxes(f_rows.reshape(B, ATT_HEADS, T), 1, 2)
    logf = jnp.swapaxes(lf_rows.reshape(B, ATT_HEADS, T), 1, 2)
    tr = lambda a: jnp.swapaxes(seq(a), 1, 2)
    tq = kb = _largest_tile(T, 512)
    yc = jnp.swapaxes(_fox(tr(cq), seq(ckb), _with_ones_rows(tr(cvb)), f2_col, tq, kb), 1, 2)
    topk = max(1, min(DSA_TOPK_MAX, T // 4))
    yd = jnp.swapaxes(_dsa(tr(iqp), seq(ikp), tr(sm), tr(dq), seq(dkb), _with_ones_rows(tr(dvb)), tq, kb, topk),
                      1, 2)
    flat = lambda a: a.reshape(N, GROUP_W)
    x2 = _post(x, flat(ya), flat(yb), flat(yc), flat(yd), mod, T // tm, wts["post"], tm)
    heads = lambda a: a.reshape(B, T, ATT_HEADS, ATT_HD)
    state = dict(lru_h=lru_h, lru_conv=ax3[:, T - (CONV_W - 1):, :], s5_re=s5r, s5_im=s5i,
                 fox_k=heads(ckf), fox_v=heads(cvf), fox_logf=logf, dsa_k=heads(dkf), dsa_v=heads(dvf),
                 dsa_kidx=seq(sm)[:, :, SM_IK:SM_IK + IDX_HD])
    return x2, state


def _sample_layer(x, mod, wts, l, B, T, past, caches, page_table):
    N, D = x.shape
    (ax, ag, bu, cq, ckf, ckb, cvf, cvb, dq, dkf, dkb, dvf, dvb, iqp, ikp, sm) = _inproj(
        x, mod, 1, wts["npre"], *wts["win"], N)
    seq = lambda a: a.reshape(B, T, a.shape[-1])
    tp = -(-T // SUBLANES) * SUBLANES
    padt = lambda a, n=tp: jnp.pad(seq(a), ((0, 0), (0, n - T), (0, 0)))
    ax3 = seq(ax)
    ya, yb, lru_h, s5r, s5i = _recurrent(padt(ax), padt(ag), padt(bu), past["lru_conv"], past["lru_h"],
                                         past["s5_re"], past["s5_im"], wts, T)
    sm3 = seq(sm)
    n_pages = page_table.shape[1]
    plen = n_pages * PAGE
    pps = _largest_tile(n_pages, 8)
    iq_rows = iqp.reshape(B, T * IDX_HEADS, LANES)
    wrows = sm3[:, :, SM_IW:SM_IW + IDX_HEADS].reshape(B, T * IDX_HEADS, 1)
    cfn = jnp.pad(jnp.swapaxes(sm3[:, :, SM_CF:SM_CF + ATT_HEADS], 1, 2), ((0, 0), (0, 0), (0, PAGE - T)))
    ikn = jnp.pad(jnp.swapaxes(sm3[:, :, SM_IK:SM_IK + IDX_HD], 1, 2), ((0, 0), (0, 0), (0, PAGE - T)))
    s_past, f_past, s_new, f_new, lf_new = _sa(page_table, l, iq_rows, wrows, cfn,
                                               wts["fbias"].reshape(ATT_HEADS, 1), ikn, caches["kidx"],
                                               caches["logfT"], pps, T)
    topk = max(1, min(DSA_TOPK_MAX, (plen + T) // 4))
    s_allT = jnp.transpose(jnp.concatenate([s_past, s_new], axis=2), (2, 0, 1)).reshape(plen + PAGE, B * T)
    bias_all = jnp.transpose(_sb(s_allT, topk, plen, T).reshape(plen + PAGE, B, T), (1, 2, 0))
    fqc = jnp.pad(f_new[:, :, :T], ((0, 0), (0, 0), (0, SUBLANES - T))).reshape(B, ATT_HEADS * SUBLANES, 1)
    newpage = lambda a: jnp.pad(jnp.transpose(a.reshape(B, T, ATT_HEADS, ATT_HD), (0, 2, 3, 1)),
                                ((0, 0), (0, 0), (0, 0), (0, PAGE - T)))
    yc8, yd8 = _sc(page_table, l, padt(cq, SUBLANES), padt(dq, SUBLANES), fqc, f_past, bias_all[:, :, :plen],
                   f_new, bias_all[:, :, plen:], newpage(ckb), newpage(cvb), newpage(dkb), newpage(dvb),
                   caches["fox_k"], caches["fox_v"], caches["dsa_k"], caches["dsa_v"], pps, T)
    flat = lambda a: a[:, :T, :].reshape(N, GROUP_W)
    x2 = _post(x, flat(ya), flat(yb), flat(yc8), flat(yd8), mod, 1, wts["post"], N)
    heads = lambda a: a.reshape(B, T, ATT_HEADS, ATT_HD)
    state = dict(lru_h=lru_h, lru_conv=ax3[:, T - (CONV_W - 1):, :], s5_re=s5r, s5_im=s5i,
                 fox_k=heads(ckf), fox_v=heads(cvf), fox_logf=jnp.swapaxes(lf_new[:, :, :T], 1, 2),
                 dsa_k=heads(dkf), dsa_v=heads(dvf), dsa_kidx=sm3[:, :, SM_IK:SM_IK + IDX_HD])
    return x2, state


def kernel(x_prompt, x_sample, state_lru_h, state_lru_conv, state_s5_re, state_s5_im, cache_fox_k, cache_fox_v, cache_fox_logf, cache_dsa_k, cache_dsa_v, cache_dsa_kidx, page_table, c_prompt, c_sample, ada_w, ada_b, norm_mix_pre, norm_mix_post, norm_ffn_pre, norm_ffn_post, w_in, lru_conv_w, lru_conv_b, lru_wr, lru_br, lru_wi, lru_bi, lru_lambda, s5_log_dt, s5_a_re, s5_a_im, s5_b_re, s5_b_im, s5_c_re, s5_c_im, s5_d, s5_glu_w, s5_glu_b, fox_f_bias, grp_norm, w_out, ffn_w_gate, ffn_w_up, ffn_w_down):
    p = dict(norm_mix_pre=norm_mix_pre, norm_mix_post=norm_mix_post, norm_ffn_pre=norm_ffn_pre,
             norm_ffn_post=norm_ffn_post, w_in=w_in, lru_conv_w=lru_conv_w, lru_conv_b=lru_conv_b, lru_wr=lru_wr,
             lru_br=lru_br, lru_wi=lru_wi, lru_bi=lru_bi, lru_lambda=lru_lambda, s5_log_dt=s5_log_dt,
             s5_a_re=s5_a_re, s5_a_im=s5_a_im, s5_b_re=s5_b_re, s5_b_im=s5_b_im, s5_c_re=s5_c_re, s5_c_im=s5_c_im,
             s5_d=s5_d, s5_glu_w=s5_glu_w, s5_glu_b=s5_glu_b, fox_f_bias=fox_f_bias, grp_norm=grp_norm,
             w_out=w_out, ffn_w_gate=ffn_w_gate, ffn_w_up=ffn_w_up, ffn_w_down=ffn_w_down)
    B, T, D = x_prompt.shape
    Bs, Ts, _ = x_sample.shape
    L = ada_w.shape[0]
    n_pool = cache_fox_k.shape[1]
    assert T % LANES == 0 and CONV_W - 1 <= Ts <= SUBLANES and cache_fox_k.shape[2] == PAGE

    c_all = jnp.concatenate([c_prompt, c_sample], axis=0)
    c_all = jnp.pad(c_all, ((0, -c_all.shape[0] % SUBLANES), (0, 0)))
    mod_all = _ada(c_all, ada_w, ada_b)

    pool = lambda a: jnp.transpose(a, (0, 1, 3, 4, 2))
    caches = dict(fox_k=pool(cache_fox_k), fox_v=pool(cache_fox_v), dsa_k=pool(cache_dsa_k), dsa_v=pool(cache_dsa_v),
                  kidx=jnp.swapaxes(cache_dsa_kidx, 2, 3), logfT=jnp.swapaxes(cache_fox_logf, 2, 3))

    xp = x_prompt.reshape(B * T, D)
    xs = x_sample.reshape(Bs * Ts, D)
    st_p, st_s = [], []
    for l in range(L):
        wts = _layer_weights(l, p)
        mod_p = mod_all[l, :B].reshape(B, 1, 6 * D)
        mod_s = jnp.repeat(mod_all[l, B:B + Bs], Ts, axis=0).reshape(1, Bs * Ts, 6 * D)
        xp, new_p = _prompt_layer(xp, mod_p, wts, B, T)
        past = dict(lru_h=state_lru_h[l], lru_conv=state_lru_conv[l], s5_re=state_s5_re[l], s5_im=state_s5_im[l])
        xs, new_s = _sample_layer(xs, mod_s, wts, l, Bs, Ts, past, caches, page_table)
        st_p.append(new_p)
        st_s.append(new_s)

    stk = lambda outs, name: jnp.stack([o[name] for o in outs])
    names = ("lru_h", "lru_conv", "s5_re", "s5_im", "fox_k", "fox_v", "fox_logf", "dsa_k", "dsa_v", "dsa_kidx")
    res = [xp.reshape(B, T, D), xs.reshape(Bs, Ts, D)]
    for name in names:
        res += [stk(st_p, name), stk(st_s, name)]
    return tuple(res)
```

```python
import functools
import math

import jax
import jax.numpy as jnp
from jax import lax
from jax.experimental import pallas as pl
from jax.experimental.pallas import tpu as pltpu

F32 = jnp.float32
BF16 = jnp.bfloat16

N_MIXERS = 4
GROUP_W = 256
LRU_HEADS = 4
CONV_W = 4
LRU_C = 8.0
S5_GROUP = 16
S5_NG = 16
S5_N = 64
S5_STATE = S5_NG * S5_N
ATT_HEADS = 4
ATT_HD = 64
IDX_HEADS = 8
IDX_HD = 32
DSA_TOPK_MAX = 256
PAGE = 128
NEG_INF = -1e30
M_INIT = -1e20
BIG = 3e38
ADM_CUT = -1e29
EPS = 1e-6
ATT_SCALE = ATT_HD ** -0.5
IDX_SCALE = IDX_HD ** -0.5
IDX_W_SCALE = IDX_HEADS ** -0.5
ALIBI = tuple(2.0 ** (-8.0 * (h + 1) / ATT_HEADS) for h in range(ATT_HEADS))
LOG2E = math.log2(math.e)
Q_SCALE = ATT_SCALE * LOG2E
BISECT_ITERS = 28
SEARCH_ITERS = 48

LANES = 128
SUBLANES = 8
VMEM_LIMIT = 56 * 1024 * 1024

C_AX, C_AG, C_BU, C_CQ, C_CK, C_CV, C_DQ, C_DK, C_DV = (i * GROUP_W for i in range(9))
P_CAT = 9 * GROUP_W
C_IQ = 0
C_IK = C_IQ + IDX_HEADS * LANES
C_SM = C_IK + LANES
P_IDX = C_SM + LANES
SM_IK, SM_CF, SM_IW = 0, IDX_HD, IDX_HD + ATT_HEADS

NT_DIMS = (((1,), (1,)), ((), ()))


def _nt(a, b):
    return lax.dot_general(a, b, NT_DIMS, preferred_element_type=F32)


def _dot(a, b):
    return jnp.dot(a, b, preferred_element_type=F32)


def _rms(x, g):
    return x * lax.rsqrt(jnp.mean(x * x, axis=-1, keepdims=True) + EPS) * g


def _softplus(z):
    return jnp.maximum(z, 0.0) + jnp.log1p(jnp.exp(-jnp.abs(z)))


def _log_sigmoid(z):
    return -_softplus(-z)


def _expm1(z):
    e = jnp.exp(z)
    one = e == 1.0
    return jnp.where(one, z, (e - 1.0) * z / jnp.where(one, 1.0, jnp.log(e)))


def _sigmoid(z):
    return jax.nn.sigmoid(z)


def _gelu(z):
    return jax.nn.gelu(z)


def _params(sem):
    return pltpu.CompilerParams(dimension_semantics=sem, vmem_limit_bytes=VMEM_LIMIT)


def _head_mask(h, width=GROUP_W):
    lane = lax.broadcasted_iota(jnp.int32, (1, width), 1)
    return (lane >= h * ATT_HD) & (lane < (h + 1) * ATT_HD)


FOLD_CHAINS = 4


def _fold_rows(x, axis_op):
    n, c = x.shape
    g = FOLD_CHAINS if n % (FOLD_CHAINS * SUBLANES) == 0 else 1
    x = axis_op(x.reshape(g, n // (g * SUBLANES), SUBLANES, c), axis=1)
    return axis_op(x, axis=0)


def _col_max(x):
    return jnp.max(_fold_rows(x, jnp.max), axis=0, keepdims=True)


def _lane_cumsum(x):
    lane = lax.broadcasted_iota(jnp.int32, x.shape, 1)
    s = 1
    while s < x.shape[1]:
        x = x + jnp.where(lane >= s, pltpu.roll(x, s, axis=1), 0.0)
        s *= 2
    return x


def _split(x):
    hi = x.astype(BF16)
    return hi, (x - hi.astype(F32)).astype(BF16)


def _ada_kernel(c_ref, w_ref, b_ref, o_ref):
    c = c_ref[...]
    sh, sl = _split(c * _sigmoid(c))
    wh, wl = _split(w_ref[...])
    o_ref[...] = (_dot(sl, wl) + _dot(sl, wh) + _dot(sh, wl)) + _dot(sh, wh) + b_ref[...]


def _ada(c_all, ada_w, ada_b):
    L, D, D6 = ada_w.shape
    Bp = c_all.shape[0]
    tn = 1024
    return pl.pallas_call(
        _ada_kernel,
        grid=(L, D6 // tn),
        in_specs=[pl.BlockSpec((Bp, D), lambda l, j: (0, 0)),
                  pl.BlockSpec((None, D, tn), lambda l, j: (l, 0, j)),
                  pl.BlockSpec((None, 1, tn), lambda l, j: (l, 0, j))],
        out_specs=pl.BlockSpec((None, Bp, tn), lambda l, j: (l, 0, j)),
        out_shape=jax.ShapeDtypeStruct((L, Bp, D6), F32),
        compiler_params=_params(("arbitrary", "arbitrary")),
        name="ada",
    )(c_all, ada_w, ada_b.reshape(L, 1, D6))


def _inproj_kernel(x_ref, sh_ref, sc_ref, g_ref, w_ref, whi_ref, wlo_ref,
                   ax_ref, ag_ref, bu_ref, cq_ref, ckf_ref, ckb_ref, cvf_ref, cvb_ref,
                   dq_ref, dkf_ref, dkb_ref, dvf_ref, dvb_ref, iq_ref, ik_ref, sm_ref):
    x = x_ref[...]
    h = _rms(x, g_ref[...]) * (1.0 + sc_ref[...]) + sh_ref[...]
    hb = h.astype(BF16)

    def mm(c0, width):
        return _dot(hb, w_ref[:, c0:c0 + width])

    ax_ref[...] = mm(C_AX, GROUP_W)
    ag_ref[...] = mm(C_AG, GROUP_W)
    bu_ref[...] = mm(C_BU, GROUP_W)
    cq_ref[...] = (mm(C_CQ, GROUP_W) * Q_SCALE).astype(BF16)
    ck = mm(C_CK, GROUP_W)
    ckf_ref[...] = ck
    ckb_ref[...] = ck.astype(BF16)
    cv = mm(C_CV, GROUP_W)
    cvf_ref[...] = cv
    cvb_ref[...] = cv.astype(BF16)
    dq_ref[...] = (mm(C_DQ, GROUP_W) * Q_SCALE).astype(BF16)
    dk = mm(C_DK, GROUP_W)
    dkf_ref[...] = dk
    dkb_ref[...] = dk.astype(BF16)
    dv = mm(C_DV, GROUP_W)
    dvf_ref[...] = dv
    dvb_ref[...] = dv.astype(BF16)

    hl = (h - hb.astype(F32)).astype(BF16)

    def mm3(c0, width):
        whi = whi_ref[:, c0:c0 + width]
        return _dot(hb, whi) + _dot(hl, whi) + _dot(hb, wlo_ref[:, c0:c0 + width])

    sm_ref[...] = mm3(C_SM, LANES)

    sub = (lax.broadcasted_iota(jnp.int32, (1, LANES), 1) // IDX_HD)
    for hh in range(IDX_HEADS):
        r = mm3(C_IQ + hh * LANES, LANES)
        hi = r.astype(BF16)
        lo = (r - hi.astype(F32)).astype(BF16)
        zero = jnp.zeros_like(hi)
        iq_ref[:, hh * LANES:(hh + 1) * LANES] = jnp.where(sub == 1, lo, jnp.where(sub == 3, zero, hi))
    r = mm3(C_IK, LANES)
    hi = r.astype(BF16)
    lo = (r - hi.astype(F32)).astype(BF16)
    ik_ref[...] = jnp.where(sub == 2, lo, jnp.where(sub == 3, jnp.zeros_like(hi), hi))


def _mod_spec(rows_mod, D, tiles_per_mod, piece):
    return pl.BlockSpec((None, rows_mod, D), lambda i: (i // tiles_per_mod, 0, piece))


def _inproj(x, mod, tiles_per_mod, g, wcat, widx_hi, widx_lo, tm):
    N, D = x.shape
    rows_mod = mod.shape[1]
    tok = lambda w: pl.BlockSpec((tm, w), lambda i: (i, 0))
    widths_dtypes = [(GROUP_W, F32)] * 3 + [(GROUP_W, BF16), (GROUP_W, F32), (GROUP_W, BF16), (GROUP_W, F32),
                                            (GROUP_W, BF16), (GROUP_W, BF16), (GROUP_W, F32), (GROUP_W, BF16),
                                            (GROUP_W, F32), (GROUP_W, BF16), (IDX_HEADS * LANES, BF16),
                                            (LANES, BF16), (LANES, F32)]
    return pl.pallas_call(
        _inproj_kernel,
        grid=(N // tm,),
        in_specs=[tok(D), _mod_spec(rows_mod, D, tiles_per_mod, 0), _mod_spec(rows_mod, D, tiles_per_mod, 1),
                  pl.BlockSpec((1, D), lambda i: (0, 0)),
                  pl.BlockSpec((D, P_CAT), lambda i: (0, 0)),
                  pl.BlockSpec((D, P_IDX), lambda i: (0, 0)),
                  pl.BlockSpec((D, P_IDX), lambda i: (0, 0))],
        out_specs=[tok(w) for w, _ in widths_dtypes],
        out_shape=[jax.ShapeDtypeStruct((N, w), dt) for w, dt in widths_dtypes],
        compiler_params=_params(("arbitrary",)),
        name="inproj",
    )(x, mod, mod, g, wcat, widx_hi, widx_lo)


def _lru_kernel(ax_ref, ag_ref, buf_ref, h0_ref, cw_ref, cb_ref, wr_ref, br_ref, wi_ref, bi_ref, lam_ref,
                ya_ref, hl_ref, xs, hcar, *, tc, last_row):
    c = pl.program_id(1)

    @pl.when(c == 0)
    def _():
        xs[0:SUBLANES, :] = buf_ref[...]
        hcar[...] = h0_ref[...]

    x = ax_ref[...]
    xs[SUBLANES:SUBLANES + tc, :] = x
    w = cw_ref[...]
    y = cb_ref[...] + xs[5:5 + tc, :] * w[0:1] + xs[6:6 + tc, :] * w[1:2] + xs[7:7 + tc, :] * w[2:3] + x * w[3:4]
    xs[0:SUBLANES, :] = xs[tc:tc + SUBLANES, :]

    yb = y.astype(BF16)
    r = _sigmoid(_dot(yb, wr_ref[...]) + br_ref[...])
    i = _sigmoid(_dot(yb, wi_ref[...]) + bi_ref[...])
    log_a = -LRU_C * r * _softplus(-lam_ref[...])
    a = jnp.exp(log_a)
    b = jnp.sqrt(-_expm1(2.0 * log_a)) * (i * y)

    row = lax.broadcasted_iota(jnp.int32, (tc, GROUP_W), 0)
    s = 1
    while s < tc:
        keep = row >= s
        a_sh = jnp.where(keep, pltpu.roll(a, s, axis=0), 1.0)
        b_sh = jnp.where(keep, pltpu.roll(b, s, axis=0), 0.0)
        b = b + a * b_sh
        a = a * a_sh
        s *= 2
    h = b + a * hcar[...]
    hcar[...] = h[tc - 1:tc, :]
    ya_ref[...] = h * _gelu(ag_ref[...])

    @pl.when(c == pl.num_programs(1) - 1)
    def _():
        hl_ref[...] = h[last_row:last_row + 1, :]


def _lru(ax, ag, buf8, h0, lw, tc, t_real):
    B, Tp, W = ax.shape
    seq = pl.BlockSpec((None, tc, W), lambda b, c: (b, c, 0))
    full = lambda shape: pl.BlockSpec(shape, lambda b, c: (0,) * len(shape))
    return pl.pallas_call(
        functools.partial(_lru_kernel, tc=tc, last_row=(t_real - 1) % tc),
        grid=(B, Tp // tc),
        in_specs=[seq, seq,
                  pl.BlockSpec((None, SUBLANES, W), lambda b, c: (b, 0, 0)),
                  pl.BlockSpec((None, 1, W), lambda b, c: (b, 0, 0)),
                  full((CONV_W, W)), full((1, W)), full((W, W)), full((1, W)), full((W, W)), full((1, W)),
                  full((1, W))],
        out_specs=[seq, pl.BlockSpec((None, 1, W), lambda b, c: (b, 0, 0))],
        out_shape=[jax.ShapeDtypeStruct((B, Tp, W), F32), jax.ShapeDtypeStruct((B, 1, W), F32)],
        scratch_shapes=[pltpu.VMEM((tc + SUBLANES, W), F32), pltpu.VMEM((1, W), F32)],
        compiler_params=_params(("arbitrary", "arbitrary")),
        name="lru",
    )(ax, ag, buf8, h0, lw["conv_w"], lw["conv_b"], lw["wr"], lw["br"], lw["wi"], lw["bi"], lw["lam"])


def _s5_kernel(u_ref, h0r_ref, h0i_ref, ldt_ref, are_ref, aim_ref, bre_ref, bim_ref, cre_ref, cim_ref,
               d_ref, gw_ref, gb_ref, y_ref, hlr_ref, hli_ref, hr_s, hi_s, car_r, car_i, *, tc, last_row):
    c = pl.program_id(1)

    @pl.when(c == 0)
    def _():
        car_r[...] = h0r_ref[...]
        car_i[...] = h0i_ref[...]

    dt = jnp.exp(ldt_ref[...])
    ar, ai = are_ref[...], aim_ref[...]
    mag = jnp.exp(dt * ar)
    abr, abi = mag * jnp.cos(dt * ai), mag * jnp.sin(dt * ai)
    den = ar * ar + ai * ai
    nr, ni = abr - 1.0, abi
    cr = (nr * ar + ni * ai) / den
    ci = (ni * ar - nr * ai) / den

    u = u_ref[...]
    ub = u.astype(BF16)
    pre = _dot(ub, bre_ref[...])
    pim = _dot(ub, bim_ref[...])
    hr_s[...] = cr * pre - ci * pim
    hi_s[...] = cr * pim + ci * pre

    def step(t, carry):
        hr, hi = carry
        nhr = abr * hr - abi * hi + hr_s[pl.ds(t, 1), :]
        nhi = abr * hi + abi * hr + hi_s[pl.ds(t, 1), :]
        hr_s[pl.ds(t, 1), :] = nhr
        hi_s[pl.ds(t, 1), :] = nhi
        return nhr, nhi

    hr, hi = lax.fori_loop(0, tc, step, (car_r[...], car_i[...]), unroll=8)
    car_r[...] = hr
    car_i[...] = hi

    hrv, hiv = hr_s[...], hi_s[...]
    y = _dot(hrv.astype(BF16), cre_ref[...]) - _dot(hiv.astype(BF16), cim_ref[...]) + d_ref[...] * u
    g = _gelu(y)
    y_ref[...] = g * _sigmoid(_dot(g.astype(BF16), gw_ref[...]) + gb_ref[...])

    @pl.when(c == pl.num_programs(1) - 1)
    def _():
        hlr_ref[...] = hr_s[last_row:last_row + 1, :]
        hli_ref[...] = hi_s[last_row:last_row + 1, :]


def _s5(u, h0r, h0i, sw, tc, t_real):
    B, Tp, W = u.shape
    S = S5_STATE
    seq = pl.BlockSpec((None, tc, W), lambda b, c: (b, c, 0))
    st = pl.BlockSpec((None, 1, S), lambda b, c: (b, 0, 0))
    full = lambda shape: pl.BlockSpec(shape, lambda b, c: (0,) * len(shape))
    return pl.pallas_call(
        functools.partial(_s5_kernel, tc=tc, last_row=(t_real - 1) % tc),
        grid=(B, Tp // tc),
        in_specs=[seq, st, st, full((1, S)), full((1, S)), full((1, S)), full((W, S)), full((W, S)),
                  full((S, W)), full((S, W)), full((1, W)), full((W, W)), full((1, W))],
        out_specs=[seq, st, st],
        out_shape=[jax.ShapeDtypeStruct((B, Tp, W), F32), jax.ShapeDtypeStruct((B, 1, S), F32),
                   jax.ShapeDtypeStruct((B, 1, S), F32)],
        scratch_shapes=[pltpu.VMEM((tc, S), F32), pltpu.VMEM((tc, S), F32),
                        pltpu.VMEM((1, S), F32), pltpu.VMEM((1, S), F32)],
        compiler_params=_params(("arbitrary", "arbitrary")),
        name="s5",
    )(u, h0r, h0i, sw["ldt"], sw["are"], sw["aim"], sw["bre"], sw["bim"], sw["cre"], sw["cim"],
      sw["d"], sw["gw"], sw["gb"])


def _fcum_kernel(cf_ref, fb_ref, lf_ref, f_ref, *, nb):
    lf = _log_sigmoid(cf_ref[...] + fb_ref[...])
    lf_ref[...] = lf
    cs = _lane_cumsum(lf)
    tot = jnp.broadcast_to(cs[:, LANES - 1:LANES], cs.shape)
    row = lax.broadcasted_iota(jnp.int32, cs.shape, 0) % nb
    inc = tot
    s = 1
    while s < nb:
        inc = inc + jnp.where(row >= s, pltpu.roll(inc, s, axis=0), 0.0)
        s *= 2
    f_ref[...] = (cs + (inc - tot)) * LOG2E


def _fcum(cf_rows, fb_rows, nb):
    B, R, _ = cf_rows.shape
    blk = pl.BlockSpec((None, R, LANES), lambda b: (b, 0, 0))
    return pl.pallas_call(
        functools.partial(_fcum_kernel, nb=nb),
        grid=(B,),
        in_specs=[blk, pl.BlockSpec((R, 1), lambda b: (0, 0))],
        out_specs=[blk, blk],
        out_shape=[jax.ShapeDtypeStruct((B, R, LANES), F32)] * 2,
        compiler_params=_params(("arbitrary",)),
        name="fcum",
    )(cf_rows, fb_rows)


ONES_ROWS = 16
VROWS = ATT_HD + ONES_ROWS


def _init_heads(qT_ref, qm_s, acc_s, m_s):
    qT = qT_ref[...]
    row = lax.broadcasted_iota(jnp.int32, (qT.shape[0], 1), 0)
    for h in range(ATT_HEADS):
        qm_s[h] = jnp.where(row // ATT_HD == h, qT, jnp.zeros_like(qT))
    acc_s[...] = jnp.zeros_like(acc_s)
    m_s[...] = jnp.full_like(m_s, M_INIT)


def _attend_block(kblk, vT_ref, k0, kb, qm_s, acc_s, m_s, s_s, p_s, bias_fn):
    for h in range(ATT_HEADS):
        s_s[h] = _dot(kblk, qm_s[h])
    m_new = []
    for h in range(ATT_HEADS):
        s = bias_fn(h, s_s[h])
        s_s[h] = s
        m_new.append(jnp.maximum(m_s[h], _col_max(s)))
    for h in range(ATT_HEADS):
        p_s[h] = jnp.exp2(s_s[h] - m_new[h]).astype(BF16)
    for h in range(ATT_HEADS):
        rows = slice(h * VROWS, (h + 1) * VROWS)
        pv = _dot(vT_ref[rows, pl.ds(k0, kb)], p_s[h])
        acc_s[rows, :] = jnp.exp2(m_s[h] - m_new[h]) * acc_s[rows, :] + pv
        m_s[h] = m_new[h]


def _finish_heads(o_ref, acc_s):
    for h in range(ATT_HEADS):
        base = h * VROWS
        o_ref[h * ATT_HD:(h + 1) * ATT_HD, :] = (acc_s[base:base + ATT_HD, :]
                                                 / acc_s[base + ATT_HD:base + ATT_HD + 1, :])


def _fox_kernel(qT_ref, k_ref, vT_ref, f2_ref, o_ref, qm_s, acc_s, m_s, s_s, p_s, *, tq, kb):
    q0 = pl.program_id(1) * tq
    _init_heads(qT_ref, qm_s, acc_s, m_s)
    kio = lax.broadcasted_iota(jnp.int32, (kb, tq), 0)
    qpos = q0 + lax.broadcasted_iota(jnp.int32, (kb, tq), 1)

    def block(j, masked):
        k0 = pl.multiple_of(j * kb, kb)
        fk = f2_ref[pl.ds(k0, kb), :]

        def bias(h, s):
            s = s - fk[:, h:h + 1]
            return jnp.where(k0 + kio <= qpos, s, NEG_INF) if masked else s

        _attend_block(k_ref[pl.ds(k0, kb), :], vT_ref, k0, kb, qm_s, acc_s, m_s, s_s, p_s, bias)
        return 0

    n_full = (q0 + 1) // kb
    n_blk = (q0 + tq - 1) // kb + 1
    lax.fori_loop(0, n_full, lambda j, c: block(j, False), 0)
    lax.fori_loop(n_full, n_blk, lambda j, c: block(j, True), 0)
    _finish_heads(o_ref, acc_s)


def _att_scratch(tq, kb):
    return [pltpu.VMEM((ATT_HEADS, GROUP_W, tq), BF16), pltpu.VMEM((ATT_HEADS * VROWS, tq), F32),
            pltpu.VMEM((ATT_HEADS, 1, tq), F32),
            pltpu.VMEM((ATT_HEADS, kb, tq), F32), pltpu.VMEM((ATT_HEADS, kb, tq), BF16)]


def _with_ones_rows(vT):
    B, _, T = vT.shape
    v4 = vT.reshape(B, ATT_HEADS, ATT_HD, T)
    return jnp.concatenate([v4, jnp.ones((B, ATT_HEADS, ONES_ROWS, T), vT.dtype)], axis=2).reshape(
        B, ATT_HEADS * VROWS, T)


def _resident(shape):
    return pl.BlockSpec((None,) + shape, lambda b, i: (b,) + (0,) * len(shape), pipeline_mode=pl.Buffered(1))


def _fox(qT, k, vT, f2col, tq, kb):
    B, W, T = qT.shape
    colblk = pl.BlockSpec((None, W, tq), lambda b, i: (b, 0, i))
    return pl.pallas_call(
        functools.partial(_fox_kernel, tq=tq, kb=kb),
        grid=(B, T // tq),
        in_specs=[colblk, _resident((T, W)), _resident((ATT_HEADS * VROWS, T)), _resident((T, ATT_HEADS))],
        out_specs=colblk,
        out_shape=jax.ShapeDtypeStruct((B, W, T), F32),
        scratch_shapes=_att_scratch(tq, kb),
        compiler_params=_params(("arbitrary", "arbitrary")),
        name="fox",
    )(qT, k, vT, f2col)


def _topk_to_bias(S, nblk, kb, C, k, nadm):
    kf = float(k)

    def reduce_blocks(fn, init):
        def body(j, c):
            k0 = pl.multiple_of(j * kb, kb)
            return fn(c, S[pl.ds(k0, kb), :], k0)
        return lax.fori_loop(0, nblk, body, init)

    fold_sum = lambda x: _fold_rows(x, jnp.sum)
    fold_max = lambda x: _fold_rows(x, jnp.max)
    fold_min = lambda x: _fold_rows(x, jnp.min)

    def count(pred):
        c = reduce_blocks(lambda c, x, k0: c + fold_sum(jnp.where(pred(x, k0), 1.0, 0.0)),
                          jnp.zeros((SUBLANES, C), F32))
        return jnp.sum(c, axis=0, keepdims=True)

    big8, zero8 = jnp.full((SUBLANES, C), BIG, F32), jnp.zeros((SUBLANES, C), F32)
    mx, mn, mp, np8, nz8 = reduce_blocks(
        lambda c, x, k0: (jnp.maximum(c[0], fold_max(x)),
                          jnp.minimum(c[1], fold_min(jnp.where(x > ADM_CUT, x, BIG))),
                          jnp.minimum(c[2], fold_min(jnp.where(x > 0.0, x, BIG))),
                          c[3] + fold_sum(jnp.where(x > 0.0, 1.0, 0.0)),
                          c[4] + fold_sum(jnp.where(x == 0.0, 1.0, 0.0))),
        (-big8, big8, big8, zero8, zero8))
    cmax = jnp.max(mx, axis=0, keepdims=True)
    cmin = jnp.min(mn, axis=0, keepdims=True)
    small = nadm <= kf
    cpos = jnp.sum(np8, axis=0, keepdims=True)
    cnn = cpos + jnp.sum(nz8, axis=0, keepdims=True)
    at_zero = (cpos < kf) & (cnn >= kf) & jnp.logical_not(small)
    settled = small | at_zero
    above = cpos >= kf
    lo0 = jnp.where(above, jnp.min(mp, axis=0, keepdims=True), cmin)
    clo0 = jnp.where(above, cpos, nadm)
    hi0 = jnp.where(above, cmax + jnp.maximum(1.0, jnp.abs(cmax)), 0.0)
    chi0 = jnp.where(above, 0.0, cnn)

    def any_col(flag):
        return jnp.max(jnp.where(flag, 1.0, 0.0)) > 0.5

    def probe(c, frac):
        lo, hi, clo, chi = c
        mid = lo + (hi - lo) * frac
        cm = count(lambda x, k0: x >= mid)
        ge = cm >= kf
        return jnp.where(ge, mid, lo), jnp.where(ge, hi, mid), jnp.where(ge, cm, clo), jnp.where(ge, chi, cm)

    def unresolved(clo, chi):
        g, r = clo - chi, kf - chi
        return (g > 2.5) & (r > 1.5) & (g - r > 0.5) & jnp.logical_not(settled)

    _, lo, hi, clo, chi = lax.while_loop(
        lambda c: (c[0] < BISECT_ITERS) & any_col(unresolved(c[3], c[4])),
        lambda c: (c[0] + 1,) + probe(c[1:], 0.5),
        (jnp.int32(0), lo0, hi0, clo0, chi0))

    top, bot = reduce_blocks(
        lambda c, x, k0: (jnp.maximum(c[0], fold_max(jnp.where(x < hi, x, -BIG))),
                          jnp.minimum(c[1], fold_min(jnp.where(x >= lo, x, BIG)))),
        (jnp.full((SUBLANES, C), -BIG, F32), jnp.full((SUBLANES, C), BIG, F32)))
    thr = jnp.where(kf - chi < 1.5, jnp.max(top, axis=0, keepdims=True), jnp.min(bot, axis=0, keepdims=True))
    thr = jnp.where(small, cmin, jnp.where(at_zero, 0.0, thr))
    cge = count(lambda x, k0: x >= thr)
    cgt = count(lambda x, k0: x > thr)
    missed = ((cgt >= kf) | (cge < kf)) & jnp.logical_not(settled)

    def slow(_):
        def open_(clo, chi):
            return (clo - chi > 1.5) & jnp.logical_not(settled)

        def search_body(c):
            it, lo, hi, clo, chi = c
            frac = jnp.clip((clo - kf + 0.5) / jnp.maximum(clo - chi, 1.0), 1.0 / 64, 63.0 / 64)
            return (it + 1,) + probe((lo, hi, clo, chi), jnp.where(it % 2 == 0, frac, 0.5))

        _, _, hi2, _, _ = lax.while_loop(lambda c: (c[0] < SEARCH_ITERS) & any_col(open_(c[3], c[4])), search_body,
                                         (jnp.int32(0), lo, hi, clo, chi))

        def unsat(cnt):
            return (cnt < kf) & jnp.logical_not(settled)

        def fix_body(c):
            t, cnt = c
            below = reduce_blocks(lambda m, x, k0: jnp.maximum(m, fold_max(jnp.where(x < t, x, -BIG))),
                                  jnp.full((SUBLANES, C), -BIG, F32))
            nt = jnp.where(unsat(cnt), jnp.max(below, axis=0, keepdims=True), t)
            return nt, count(lambda x, k0: x >= nt)

        t2, cge2 = lax.while_loop(lambda c: any_col(unsat(c[1])), fix_body, (hi2, jnp.zeros((1, C), F32)))
        cgt2 = count(lambda x, k0: x > t2)
        return jnp.where(missed, t2, thr), jnp.where(missed, cge2, cge), jnp.where(missed, cgt2, cgt)

    thr, cge, cgt = lax.cond(any_col(missed), slow, lambda _: (thr, cge, cgt), 0)
    need = kf - cgt
    ties = (cge > kf) & jnp.logical_not(small)

    def write_plain(_):
        def body(j, _):
            k0 = pl.multiple_of(j * kb, kb)
            S[pl.ds(k0, kb), :] = jnp.where(S[pl.ds(k0, kb), :] >= thr, 0.0, NEG_INF)
            return 0
        return lax.fori_loop(0, nblk, body, 0)

    def write_ties(_):
        tri = jnp.where(lax.broadcasted_iota(jnp.int32, (kb, kb), 0) >= lax.broadcasted_iota(jnp.int32, (kb, kb), 1),
                        1.0, 0.0).astype(BF16)

        def body(j, seen):
            k0 = pl.multiple_of(j * kb, kb)
            x = S[pl.ds(k0, kb), :]
            eq = x == thr
            rank = seen + _dot(tri, jnp.where(eq, 1.0, 0.0).astype(BF16))
            S[pl.ds(k0, kb), :] = jnp.where((x > thr) | (eq & (rank <= need)), 0.0, NEG_INF)
            return rank[kb - 1:kb, :]
        lax.fori_loop(0, nblk, body, jnp.zeros((1, C), F32))
        return 0

    lax.cond(any_col(ties), write_ties, write_plain, 0)


def _dsa_kernel(iqT_ref, ik_ref, smT_ref, qT_ref, k_ref, vT_ref, o_ref, S, qm_s, acc_s, m_s, s_s, p_s,
                *, tq, kb, topk):
    q0 = pl.program_id(1) * tq
    n_blk = (q0 + tq - 1) // kb + 1
    kio = lax.broadcasted_iota(jnp.int32, (kb, tq), 0)
    qpos = q0 + lax.broadcasted_iota(jnp.int32, (kb, tq), 1)
    wc = smT_ref[SM_IW:SM_IW + IDX_HEADS, :] * (IDX_W_SCALE * IDX_SCALE)

    def scores(j, _):
        k0 = pl.multiple_of(j * kb, kb)
        kblk = ik_ref[pl.ds(k0, kb), :]
        sc = jnp.zeros((kb, tq), F32)
        for hh in range(IDX_HEADS):
            d = _dot(kblk, iqT_ref[hh * LANES:(hh + 1) * LANES, :])
            sc = sc + wc[hh:hh + 1, :] * jnp.maximum(d, 0.0)
        S[pl.ds(k0, kb), :] = jnp.where(k0 + kio <= qpos, sc, NEG_INF)
        return 0

    lax.fori_loop(0, n_blk, scores, 0)

    nadm = (q0 + 1 + lax.broadcasted_iota(jnp.int32, (1, tq), 1)).astype(F32)
    _topk_to_bias(S, n_blk, kb, tq, topk, nadm)

    _init_heads(qT_ref, qm_s, acc_s, m_s)
    kcol =lax.broadcasted_iota(jnp.int32, (kb, 1), 0)

    def attend(j, _):
        k0 = pl.multiple_of(j * kb, kb)
        sel = S[pl.ds(k0, kb), :]
        kpos = (k0 + kcol).astype(F32)

        def bias(h, s):
            return s + (sel + (ALIBI[h] * LOG2E) * kpos)

        _attend_block(k_ref[pl.ds(k0, kb), :], vT_ref, k0, kb, qm_s, acc_s, m_s, s_s, p_s, bias)
        return 0

    lax.fori_loop(0, n_blk, attend, 0)
    _finish_heads(o_ref, acc_s)


def _dsa(iqT, ikp, smT, qT, k, vT, tq, kb, topk):
    B, W, T = qT.shape
    col = lambda r: pl.BlockSpec((None, r, tq), lambda b, i: (b, 0, i))
    return pl.pallas_call(
        functools.partial(_dsa_kernel, tq=tq, kb=kb, topk=topk),
        grid=(B, T // tq),
        in_specs=[col(IDX_HEADS * LANES), _resident((T, LANES)), col(LANES), col(W),
                  _resident((T, W)), _resident((ATT_HEADS * VROWS, T))],
        out_specs=col(W),
        out_shape=jax.ShapeDtypeStruct((B, W, T), F32),
        scratch_shapes=[pltpu.VMEM((T, tq), F32)] + _att_scratch(tq, kb),
        compiler_params=_params(("arbitrary", "arbitrary")),
        name="dsa",
    )(iqT, ikp, smT, qT, k, vT)


def _sa_kernel(pt_ref, q_ref, w_ref, cfn_ref, fb_ref, ikn_ref, *rest, pps, tn):
    ki_refs, lf_refs = rest[:pps], rest[pps:2 * pps]
    s_ref, f_ref, sn_ref, fn_ref, lfn_ref, carry_s = rest[2 * pps:]
    step = pl.program_id(1)

    @pl.when(step == 0)
    def _():
        carry_s[...] = jnp.zeros_like(carry_s)

    qp = q_ref[...]
    qh = qp[:, 0:IDX_HD]
    ql = qp[:, IDX_HD:2 * IDX_HD]
    w = w_ref[...] * IDX_W_SCALE
    nq = qp.shape[0] // IDX_HEADS

    def scores(kT):
        kh, kl = _split(kT)
        d = _dot(qh, kh) + _dot(ql, kh) + _dot(qh, kl)
        r = jnp.maximum(d * IDX_SCALE, 0.0) * w
        return jnp.sum(r.reshape(nq, IDX_HEADS, kT.shape[1]), axis=1)

    s_ref[...] = scores(jnp.concatenate([r[...] for r in ki_refs], axis=1))
    cs_all = _lane_cumsum(jnp.concatenate([r[...] for r in lf_refs], axis=0))
    carry = carry_s[...]
    for i in range(pps):
        cs = cs_all[i * ATT_HEADS:(i + 1) * ATT_HEADS, :] + carry
        f_ref[:, i * PAGE:(i + 1) * PAGE] = cs
        carry = cs[:, PAGE - 1:PAGE]
    carry_s[...] = carry

    @pl.when(step == pl.num_programs(1) - 1)
    def _():
        lane = lax.broadcasted_iota(jnp.int32, (ATT_HEADS, PAGE), 1)
        lfn = _log_sigmoid(cfn_ref[...] + fb_ref[...])
        lfn_ref[...] = lfn
        fn_ref[...] = _lane_cumsum(jnp.where(lane < tn, lfn, 0.0)) + carry
        col = lax.broadcasted_iota(jnp.int32, (nq, PAGE), 1)
        rowq = lax.broadcasted_iota(jnp.int32, (nq, PAGE), 0)
        sn_ref[...] = jnp.where((col <= rowq) & (col < tn), scores(ikn_ref[...]), NEG_INF)


def _sa(page_table, l, qrows, wrows, cfn, fb, ikn, kidx_pool, lfT_pool, pps, tn):
    B, n_pages = page_table.shape
    nsteps = n_pages // pps
    P = n_pages * PAGE
    nq = qrows.shape[1] // IDX_HEADS
    per_b = lambda shape: pl.BlockSpec((None,) + shape, lambda b, s, pt: (b,) + (0,) * len(shape))
    ki_specs = [pl.BlockSpec((None, None, IDX_HD, PAGE), lambda b, s, pt, i=i: (l, pt[b, s * pps + i], 0, 0))
                for i in range(pps)]
    lf_specs = [pl.BlockSpec((None, None, ATT_HEADS, PAGE), lambda b, s, pt, i=i: (l, pt[b, s * pps + i], 0, 0))
                for i in range(pps)]
    gs = pltpu.PrefetchScalarGridSpec(
        num_scalar_prefetch=1,
        grid=(B, nsteps),
        in_specs=[per_b((nq * IDX_HEADS, LANES)), per_b((nq * IDX_HEADS, 1)), per_b((ATT_HEADS, PAGE)),
                  pl.BlockSpec((ATT_HEADS, 1), lambda b, s, pt: (0, 0)), per_b((IDX_HD, PAGE))] + ki_specs + lf_specs,
        out_specs=[pl.BlockSpec((None, nq, pps * PAGE), lambda b, s, pt: (b, 0, s)),
                   pl.BlockSpec((None, ATT_HEADS, pps * PAGE), lambda b, s, pt: (b, 0, s)),
                   per_b((nq, PAGE)), per_b((ATT_HEADS, PAGE)), per_b((ATT_HEADS, PAGE))],
        scratch_shapes=[pltpu.VMEM((ATT_HEADS, 1), F32)])
    return pl.pallas_call(
        functools.partial(_sa_kernel, pps=pps, tn=tn),
        grid_spec=gs,
        out_shape=[jax.ShapeDtypeStruct((B, nq, P), F32), jax.ShapeDtypeStruct((B, ATT_HEADS, P), F32),
                   jax.ShapeDtypeStruct((B, nq, PAGE), F32), jax.ShapeDtypeStruct((B, ATT_HEADS, PAGE), F32),
                   jax.ShapeDtypeStruct((B, ATT_HEADS, PAGE), F32)],
        compiler_params=_params(("arbitrary", "arbitrary")),
        name="sample_scores",
    )(page_table, qrows, wrows, cfn, fb, ikn, *([kidx_pool] * pps), *([lfT_pool] * pps))


def _sb_kernel(s_ref, o_ref, *, kb, nblk, topk, past, nq):
    C = s_ref.shape[1]
    o_ref[...] = s_ref[...]
    nadm = (past + 1 + lax.broadcasted_iota(jnp.int32, (1, C), 1) % nq).astype(F32)
    _topk_to_bias(o_ref, nblk, kb, C, topk, nadm)


def _sb(s_allT, topk, past, nq):
    Wt, C = s_allT.shape
    nl = Wt // LANES
    div = max(d for d in range(1, 9) if nl % d == 0)
    kb = div * LANES
    return pl.pallas_call(
        functools.partial(_sb_kernel, kb=kb, nblk=Wt // kb, topk=topk, past=past, nq=nq),
        grid=(1,),
        in_specs=[pl.BlockSpec((Wt, C), lambda i: (0, 0))],
        out_specs=pl.BlockSpec((Wt, C), lambda i: (0, 0)),
        out_shape=jax.ShapeDtypeStruct((Wt, C), F32),
        compiler_params=_params(("arbitrary",)),
        name="sample_topk",
    )(s_allT)


def _sc_kernel(pt_ref, fq_ref, dq_ref, fqc_ref, fp_ref, bp_ref, fn_ref, bn_ref,
               ckn_ref, cvn_ref, dkn_ref, dvn_ref, *rest, pps, past, tn, nq):
    fk_refs, fv_refs = rest[0:pps], rest[pps:2 * pps]
    dk_refs, dv_refs = rest[2 * pps:3 * pps], rest[3 * pps:4 * pps]
    yc_ref, yd_ref, qf_s, qd_s, accf, mf, lf, accd, md, ld = rest[4 * pps:]
    step = pl.program_id(1)
    R = ATT_HEADS * SUBLANES

    def stack_heads(q8):
        return jnp.concatenate([jnp.where(_head_mask(h), q8, jnp.zeros_like(q8)) for h in range(ATT_HEADS)], axis=0)

    @pl.when(step == 0)
    def _():
        qf_s[...] = stack_heads(fq_ref[...])
        qd_s[...] = stack_heads(dq_ref[...])
        for acc, m, l in ((accf, mf, lf), (accd, md, ld)):
            acc[...] = jnp.zeros_like(acc)
            m[...] = jnp.full_like(m, M_INIT)
            l[...] = jnp.zeros_like(l)

    def update(q_s, kT, vT, bias, acc, m, l):
        s = _dot(q_s[...], kT) + bias * LOG2E
        m_prev = m[...]
        m_new = jnp.maximum(m_prev, jnp.max(s, axis=1, keepdims=True))
        p = jnp.exp2(s - m_new)
        alpha = jnp.exp2(m_prev - m_new)
        l[...] = alpha * l[...] + jnp.sum(p, axis=1, keepdims=True)
        m[...] = m_new
        acc[...] = alpha * acc[...] + _nt(p.astype(BF16), vT)

    def per_head_rows(x4):
        return jnp.concatenate([jnp.broadcast_to(x4[h:h + 1, :], (SUBLANES, x4.shape[1]))
                                for h in range(ATT_HEADS)], axis=0)

    def per_query_rows(xq):
        r8 = lax.broadcasted_iota(jnp.int32, (SUBLANES, xq.shape[1]), 0)
        x8 = jnp.zeros((SUBLANES, xq.shape[1]), F32)
        for qq in range(nq):
            x8 = jnp.where(r8 == qq, jnp.broadcast_to(xq[qq:qq + 1, :], x8.shape), x8)
        return jnp.concatenate([x8] * ATT_HEADS, axis=0)

    rowi = lax.broadcasted_iota(jnp.int32, (R, 1), 0)
    qidx = rowi % SUBLANES
    slope = jnp.zeros((R, 1), F32)
    for h in range(ATT_HEADS):
        slope = jnp.where(rowi // SUBLANES == h, ALIBI[h], slope)

    def slab(ref):
        return ref[...].reshape(GROUP_W, ref.shape[-1]).astype(BF16)

    def blocks(refs):
        return jnp.concatenate([slab(r) for r in refs], axis=1)

    kw = pps * PAGE
    col = lax.broadcasted_iota(jnp.int32, (R, kw), 1)
    update(qf_s, blocks(fk_refs), blocks(fv_refs), fqc_ref[...] - per_head_rows(fp_ref[...]), accf, mf, lf)
    dist = (past + qidx - (step * kw + col)).astype(F32)
    update(qd_s, blocks(dk_refs), blocks(dv_refs), per_query_rows(bp_ref[...]) - slope * dist, accd, md, ld)

    @pl.when(step == pl.num_programs(1) - 1)
    def _():
        coln = lax.broadcasted_iota(jnp.int32, (R, PAGE), 1)
        ok = (coln <= qidx) & (coln < tn)
        bias_f = jnp.where(ok, fqc_ref[...] - per_head_rows(fn_ref[...]), NEG_INF)
        update(qf_s, slab(ckn_ref), slab(cvn_ref), bias_f, accf, mf, lf)
        distn = (qidx - coln).astype(F32)
        bias_d = jnp.where(coln < tn, per_query_rows(bn_ref[...]), NEG_INF) - slope * distn
        update(qd_s, slab(dkn_ref), slab(dvn_ref), bias_d, accd, md, ld)
        for acc, l, out in ((accf, lf, yc_ref), (accd, ld, yd_ref)):
            o = acc[...] / l[...]
            y = jnp.zeros((SUBLANES, GROUP_W), F32)
            for h in range(ATT_HEADS):
                y = jnp.where(_head_mask(h), o[h * SUBLANES:(h + 1) * SUBLANES, :], y)
            out[...] = y


def _sc(page_table, l, fq8, dq8, fqc, f_past, b_past, f_new, b_new, ckn, cvn, dkn, dvn,
        fk_pool, fv_pool, dk_pool, dv_pool, pps, tn):
    B, n_pages = page_table.shape
    nsteps = n_pages // pps
    past = n_pages * PAGE
    nq = b_past.shape[1]
    W = GROUP_W
    per_b = lambda shape: pl.BlockSpec((None,) + shape, lambda b, s, pt: (b,) + (0,) * len(shape))
    page = lambda i: pl.BlockSpec((None, None, ATT_HEADS, ATT_HD, PAGE),
                                  lambda b, s, pt, i=i: (l, pt[b, s * pps + i], 0, 0, 0))
    pages = [page(i) for i in range(pps)]
    newpage = per_b((ATT_HEADS, ATT_HD, PAGE))
    R = ATT_HEADS * SUBLANES
    gs = pltpu.PrefetchScalarGridSpec(
        num_scalar_prefetch=1,
        grid=(B, nsteps),
        in_specs=[per_b((SUBLANES, W)), per_b((SUBLANES, W)), per_b((R, 1)),
                  pl.BlockSpec((None, ATT_HEADS, pps * PAGE), lambda b, s, pt: (b, 0, s)),
                  pl.BlockSpec((None, nq, pps * PAGE), lambda b, s, pt: (b, 0, s)),
                  per_b((ATT_HEADS, PAGE)), per_b((nq, PAGE)),
                  newpage, newpage, newpage, newpage] + pages * 4,
        out_specs=[per_b((SUBLANES, W)), per_b((SUBLANES, W))],
        scratch_shapes=[pltpu.VMEM((R, W), BF16), pltpu.VMEM((R, W), BF16),
                        pltpu.VMEM((R, W), F32), pltpu.VMEM((R, 1), F32), pltpu.VMEM((R, 1), F32),
                        pltpu.VMEM((R, W), F32), pltpu.VMEM((R, 1), F32), pltpu.VMEM((R, 1), F32)])
    return pl.pallas_call(
        functools.partial(_sc_kernel, pps=pps, past=past, tn=tn, nq=nq),
        grid_spec=gs,
        out_shape=[jax.ShapeDtypeStruct((B, SUBLANES, W), F32)] * 2,
        compiler_params=_params(("arbitrary", "arbitrary")),
        name="sample_attn",
    )(page_table, fq8, dq8, fqc, f_past, b_past, f_new, b_new, ckn, cvn, dkn, dvn,
      *([fk_pool] * pps), *([fv_pool] * pps), *([dk_pool] * pps), *([dv_pool] * pps))


def _post_kernel(x_ref, ya_ref, yb_ref, yc_ref, yd_ref, g1_ref, sh2_ref, sc2_ref, g2_ref,
                 gn_ref, wo_ref, npost_ref, nfpre_ref, wg_ref, wu_ref, wd_ref, nfpost_ref, o_ref, *, chunks):
    o = None
    for i, r in enumerate((ya_ref, yb_ref, yc_ref, yd_ref)):
        part = _rms(r[...], gn_ref[:, i * GROUP_W:(i + 1) * GROUP_W]).astype(BF16)
        d = _dot(part, wo_ref[i * GROUP_W:(i + 1) * GROUP_W, :])
        o = d if o is None else o + d
    x1 = x_ref[...] + g1_ref[...] * _rms(o, npost_ref[...])
    h2 = (_rms(x1, nfpre_ref[...]) * (1.0 + sc2_ref[...]) + sh2_ref[...]).astype(BF16)
    f = None
    for c0, c1 in chunks:
        gate = _dot(h2, wg_ref[:, c0:c1])
        up = _dot(h2, wu_ref[:, c0:c1])
        d = _dot((gate * _sigmoid(gate) * up).astype(BF16), wd_ref[c0:c1, :])
        f = d if f is None else f + d
    o_ref[...] = x1 + g2_ref[...] * _rms(f, nfpost_ref[...])


def _post(x, ya, yb, yc, yd, mod, tiles_per_mod, pw, tm):
    N, D = x.shape
    H = pw["wg"].shape[1]
    rows_mod = mod.shape[1]
    step = 1024
    chunks = tuple((c, min(c + step, H)) for c in range(0, H, step))
    tok = lambda w: pl.BlockSpec((tm, w), lambda i: (i, 0))
    const = lambda shape: pl.BlockSpec(shape, lambda i: (0,) * len(shape), pipeline_mode=pl.Buffered(1))
    return pl.pallas_call(
        functools.partial(_post_kernel, chunks=chunks),
        grid=(N // tm,),
        in_specs=[tok(D), tok(GROUP_W), tok(GROUP_W), tok(GROUP_W), tok(GROUP_W),
                  _mod_spec(rows_mod, D, tiles_per_mod, 2), _mod_spec(rows_mod, D, tiles_per_mod, 3),
                  _mod_spec(rows_mod, D, tiles_per_mod, 4), _mod_spec(rows_mod, D, tiles_per_mod, 5),
                  const((1, D)), const((D, D)), const((1, D)), const((1, D)),
                  const((D, H)), const((D, H)), const((H, D)), const((1, D))],
        out_specs=tok(D),
        out_shape=jax.ShapeDtypeStruct((N, D), F32),
        compiler_params=_params(("arbitrary",)),
        name="post",
    )(x, ya, yb, yc, yd, mod, mod, mod, mod, pw["gn"], pw["wo"], pw["npost"], pw["nfpre"],
      pw["wg"], pw["wu"], pw["wd"], pw["nfpost"])


def _block_diag(blocks):
    G, r, c = blocks.shape
    eye = jnp.eye(G, dtype=blocks.dtype)
    return (blocks[:, :, None, :] * eye[:, None, :, None]).reshape(G * r, G * c)


def _cat_weight(w):
    D = w.shape[0]
    widths = (GROUP_W, GROUP_W, GROUP_W, GROUP_W, GROUP_W, GROUP_W, ATT_HEADS, GROUP_W, GROUP_W, GROUP_W,
              IDX_HEADS * IDX_HD, IDX_HD, IDX_HEADS)
    offs = [0]
    for wd in widths:
        offs.append(offs[-1] + wd)
    pc = [w[:, offs[i]:offs[i + 1]] for i in range(len(widths))]
    a_x, a_g, b_u, c_q, c_k, c_v, c_f, d_q, d_k, d_v, i_q, i_k, i_w = pc
    iq_rep = jnp.tile(i_q.reshape(D, IDX_HEADS, 1, IDX_HD), (1, 1, LANES // IDX_HD, 1)).reshape(D, IDX_HEADS * LANES)
    ik_rep = jnp.tile(i_k, (1, LANES // IDX_HD))
    small = jnp.concatenate([i_k, c_f, i_w, jnp.zeros((D, LANES - IDX_HD - ATT_HEADS - IDX_HEADS), w.dtype)], axis=1)
    wcat = jnp.concatenate([a_x, a_g, b_u, c_q, c_k, c_v, d_q, d_k, d_v], axis=1).astype(BF16)
    widx = jnp.concatenate([iq_rep, ik_rep, small], axis=1)
    hi = lax.reduce_precision(widx, exponent_bits=8, mantissa_bits=7)
    return wcat, hi.astype(BF16), (widx - hi).astype(BF16)


def _layer_weights(l, p):
    row = lambda a: a[l].reshape(1, -1)
    lw = dict(conv_w=p["lru_conv_w"][l], conv_b=row(p["lru_conv_b"]),
              wr=_block_diag(p["lru_wr"][l]).astype(BF16), br=row(p["lru_br"]),
              wi=_block_diag(p["lru_wi"][l]).astype(BF16), bi=row(p["lru_bi"]), lam=row(p["lru_lambda"]))
    sw = dict(ldt=jnp.repeat(p["s5_log_dt"][l], S5_N).reshape(1, S5_STATE),
              are=p["s5_a_re"][l].reshape(1, S5_STATE), aim=p["s5_a_im"][l].reshape(1, S5_STATE),
              bre=_block_diag(jnp.swapaxes(p["s5_b_re"][l], 1, 2)).astype(BF16),
              bim=_block_diag(jnp.swapaxes(p["s5_b_im"][l], 1, 2)).astype(BF16),
              cre=_block_diag(jnp.swapaxes(p["s5_c_re"][l], 1, 2)).astype(BF16),
              cim=_block_diag(jnp.swapaxes(p["s5_c_im"][l], 1, 2)).astype(BF16),
              d=row(p["s5_d"]), gw=p["s5_glu_w"][l].astype(BF16), gb=row(p["s5_glu_b"]))
    pw = dict(gn=row(p["grp_norm"]), wo=p["w_out"][l].astype(BF16), npost=row(p["norm_mix_post"]),
              nfpre=row(p["norm_ffn_pre"]), wg=p["ffn_w_gate"][l].astype(BF16), wu=p["ffn_w_up"][l].astype(BF16),
              wd=p["ffn_w_down"][l].astype(BF16), nfpost=row(p["norm_ffn_post"]))
    return dict(win=_cat_weight(p["w_in"][l]), npre=row(p["norm_mix_pre"]), lru=lw, s5=sw, post=pw,
                fbias=p["fox_f_bias"][l])


def _largest_tile(n, cap):
    t = min(n, cap)
    while n % t:
        t //= 2
    return t


def _recurrent(ax, ag, bu, lru_buf, lru_h0, s5_r0, s5_i0, wts, t_real):
    B, Tp, _ = ax.shape
    tc = _largest_tile(Tp, 256)
    buf8 = jnp.pad(lru_buf, ((0, 0), (SUBLANES - (CONV_W - 1), 0), (0, 0)))
    ya, lru_h = _lru(ax, ag, buf8, lru_h0.reshape(B, 1, GROUP_W), wts["lru"], tc, t_real)
    yb, s5r, s5i = _s5(bu, s5_r0.reshape(B, 1, S5_STATE), s5_i0.reshape(B, 1, S5_STATE), wts["s5"], tc, t_real)
    return ya, yb, lru_h.reshape(B, GROUP_W), s5r.reshape(B, S5_NG, S5_N), s5i.reshape(B, S5_NG, S5_N)


def _prompt_layer(x, mod, wts, B, T):
    N, D = x.shape
    tm = _largest_tile(T, 512)
    (ax, ag, bu, cq, ckf, ckb, cvf, cvb, dq, dkf, dkb, dvf, dvb, iqp, ikp, sm) = _inproj(
        x, mod, T // tm, wts["npre"], *wts["win"], tm)
    seq = lambda a: a.reshape(B, T, a.shape[-1])
    zeros = lambda *s: jnp.zeros(s, F32)
    ax3 = seq(ax)
    ya, yb, lru_h, s5r, s5i = _recurrent(ax3, seq(ag), seq(bu), zeros(B, CONV_W - 1, GROUP_W), zeros(B, GROUP_W),
                                         zeros(B, S5_NG, S5_N), zeros(B, S5_NG, S5_N), wts, T)
    nb = T // LANES
    cf_rows = jnp.swapaxes(seq(sm)[:, :, SM_CF:SM_CF + ATT_HEADS], 1, 2).reshape(B, ATT_HEADS * nb, LANES)
    fb_rows = jnp.repeat(wts["fbias"], nb).reshape(ATT_HEADS * nb, 1)
    lf_rows, f_rows = _fcum(cf_rows, fb_rows, nb)
    f2_col = jnp.swapaxes(f_rows.reshape(B, ATT_HEADS, T), 1, 2)
    logf = jnp.swapaxes(lf_rows.reshape(B, ATT_HEADS, T), 1, 2)
    tr = lambda a: jnp.swapaxes(seq(a), 1, 2)
    tq = kb = _largest_tile(T, 512)
    yc = jnp.swapaxes(_fox(tr(cq), seq(ckb), _with_ones_rows(tr(cvb)), f2_col, tq, kb), 1, 2)
    topk = max(1, min(DSA_TOPK_MAX, T // 4))
    yd = jnp.swapaxes(_dsa(tr(iqp), seq(ikp), tr(sm), tr(dq), seq(dkb), _with_ones_rows(tr(dvb)), tq, kb, topk),
                      1, 2)
    flat = lambda a: a.reshape(N, GROUP_W)
    x2 = _post(x, flat(ya), flat(yb), flat(yc), flat(yd), mod, T // tm, wts["post"], tm)
    heads = lambda a: a.reshape(B, T, ATT_HEADS, ATT_HD)
    state = dict(lru_h=lru_h, lru_conv=ax3[:, T - (CONV_W - 1):, :], s5_re=s5r, s5_im=s5i,
                 fox_k=heads(ckf), fox_v=heads(cvf), fox_logf=logf, dsa_k=heads(dkf), dsa_v=heads(dvf),
                 dsa_kidx=seq(sm)[:, :, SM_IK:SM_IK + IDX_HD])
    return x2, state


def _sample_layer(x, mod, wts, l, B, T, past, caches, page_table):
    N, D = x.shape
    (ax, ag, bu, cq, ckf, ckb, cvf, cvb, dq, dkf, dkb, dvf, dvb, iqp, ikp, sm) = _inproj(
        x, mod, 1, wts["npre"], *wts["win"], N)
    seq = lambda a: a.reshape(B, T, a.shape[-1])
    tp = -(-T // SUBLANES) * SUBLANES
    padt = lambda a, n=tp: jnp.pad(seq(a), ((0, 0), (0, n - T), (0, 0)))
    ax3 = seq(ax)
    ya, yb, lru_h, s5r, s5i = _recurrent(padt(ax), padt(ag), padt(bu), past["lru_conv"], past["lru_h"],
                                         past["s5_re"], past["s5_im"], wts, T)
    sm3 = seq(sm)
    n_pages = page_table.shape[1]
    plen = n_pages * PAGE
    pps = _largest_tile(n_pages, 8)
    iq_rows = iqp.reshape(B, T * IDX_HEADS, LANES)
    wrows = sm3[:, :, SM_IW:SM_IW + IDX_HEADS].reshape(B, T * IDX_HEADS, 1)
    cfn = jnp.pad(jnp.swapaxes(sm3[:, :, SM_CF:SM_CF + ATT_HEADS], 1, 2), ((0, 0), (0, 0), (0, PAGE - T)))
    ikn = jnp.pad(jnp.swapaxes(sm3[:, :, SM_IK:SM_IK + IDX_HD], 1, 2), ((0, 0), (0, 0), (0, PAGE - T)))
    s_past, f_past, s_new, f_new, lf_new = _sa(page_table, l, iq_rows, wrows, cfn,
                                               wts["fbias"].reshape(ATT_HEADS, 1), ikn, caches["kidx"],
                                               caches["logfT"], _largest_tile(n_pages, 32), T)
    topk = max(1, min(DSA_TOPK_MAX, (plen + T) // 4))
    s_allT = jnp.transpose(jnp.concatenate([s_past, s_new], axis=2), (2, 0, 1)).reshape(plen + PAGE, B * T)
    bias_all = jnp.transpose(_sb(s_allT, topk, plen, T).reshape(plen + PAGE, B, T), (1, 2, 0))
    fqc = jnp.pad(f_new[:, :, :T], ((0, 0), (0, 0), (0, SUBLANES - T))).reshape(B, ATT_HEADS * SUBLANES, 1)
    newpage = lambda a: jnp.pad(jnp.transpose(a.reshape(B, T, ATT_HEADS, ATT_HD), (0, 2, 3, 1)),
                                ((0, 0), (0, 0), (0, 0), (0, PAGE - T)))
    yc8, yd8 = _sc(page_table, l, padt(cq, SUBLANES), padt(dq, SUBLANES), fqc, f_past, bias_all[:, :, :plen],
                   f_new, bias_all[:, :, plen:], newpage(ckb), newpage(cvb), newpage(dkb), newpage(dvb),
                   caches["fox_k"], caches["fox_v"], caches["dsa_k"], caches["dsa_v"], pps, T)
    flat = lambda a: a[:, :T, :].reshape(N, GROUP_W)
    x2 = _post(x, flat(ya), flat(yb), flat(yc8), flat(yd8), mod, 1, wts["post"], N)
    heads = lambda a: a.reshape(B, T, ATT_HEADS, ATT_HD)
    state = dict(lru_h=lru_h, lru_conv=ax3[:, T - (CONV_W - 1):, :], s5_re=s5r, s5_im=s5i,
                 fox_k=heads(ckf), fox_v=heads(cvf), fox_logf=jnp.swapaxes(lf_new[:, :, :T], 1, 2),
                 dsa_k=heads(dkf), dsa_v=heads(dvf), dsa_kidx=sm3[:, :, SM_IK:SM_IK + IDX_HD])
    return x2, state


def kernel(x_prompt, x_sample, state_lru_h, state_lru_conv, state_s5_re, state_s5_im, cache_fox_k, cache_fox_v, cache_fox_logf, cache_dsa_k, cache_dsa_v, cache_dsa_kidx, page_table, c_prompt, c_sample, ada_w, ada_b, norm_mix_pre, norm_mix_post, norm_ffn_pre, norm_ffn_post, w_in, lru_conv_w, lru_conv_b, lru_wr, lru_br, lru_wi, lru_bi, lru_lambda, s5_log_dt, s5_a_re, s5_a_im, s5_b_re, s5_b_im, s5_c_re, s5_c_im, s5_d, s5_glu_w, s5_glu_b, fox_f_bias, grp_norm, w_out, ffn_w_gate, ffn_w_up, ffn_w_down):
    p = dict(norm_mix_pre=norm_mix_pre, norm_mix_post=norm_mix_post, norm_ffn_pre=norm_ffn_pre,
             norm_ffn_post=norm_ffn_post, w_in=w_in, lru_conv_w=lru_conv_w, lru_conv_b=lru_conv_b, lru_wr=lru_wr,
             lru_br=lru_br, lru_wi=lru_wi, lru_bi=lru_bi, lru_lambda=lru_lambda, s5_log_dt=s5_log_dt,
             s5_a_re=s5_a_re, s5_a_im=s5_a_im, s5_b_re=s5_b_re, s5_b_im=s5_b_im, s5_c_re=s5_c_re, s5_c_im=s5_c_im,
             s5_d=s5_d, s5_glu_w=s5_glu_w, s5_glu_b=s5_glu_b, fox_f_bias=fox_f_bias, grp_norm=grp_norm,
             w_out=w_out, ffn_w_gate=ffn_w_gate, ffn_w_up=ffn_w_up, ffn_w_down=ffn_w_down)
    B, T, D = x_prompt.shape
    Bs, Ts, _ = x_sample.shape
    L = ada_w.shape[0]
    n_pool = cache_fox_k.shape[1]
    assert T % LANES == 0 and CONV_W - 1 <= Ts <= SUBLANES and cache_fox_k.shape[2] == PAGE

    c_all = jnp.concatenate([c_prompt, c_sample], axis=0)
    c_all = jnp.pad(c_all, ((0, -c_all.shape[0] % SUBLANES), (0, 0)))
    mod_all = _ada(c_all, ada_w, ada_b)

    pool = lambda a: jnp.transpose(a, (0, 1, 3, 4, 2))
    caches = dict(fox_k=pool(cache_fox_k), fox_v=pool(cache_fox_v), dsa_k=pool(cache_dsa_k), dsa_v=pool(cache_dsa_v),
                  kidx=jnp.swapaxes(cache_dsa_kidx, 2, 3), logfT=jnp.swapaxes(cache_fox_logf, 2, 3))

    xp = x_prompt.reshape(B * T, D)
    xs = x_sample.reshape(Bs * Ts, D)
    st_p, st_s = [], []
    for l in range(L):
        wts = _layer_weights(l, p)
        mod_p = mod_all[l, :B].reshape(B, 1, 6 * D)
        mod_s = jnp.repeat(mod_all[l, B:B + Bs], Ts, axis=0).reshape(1, Bs * Ts, 6 * D)
        xp, new_p = _prompt_layer(xp, mod_p, wts, B, T)
        past = dict(lru_h=state_lru_h[l], lru_conv=state_lru_conv[l], s5_re=state_s5_re[l], s5_im=state_s5_im[l])
        xs, new_s = _sample_layer(xs, mod_s, wts, l, Bs, Ts, past, caches, page_table)
        st_p.append(new_p)
        st_s.append(new_s)

    stk = lambda outs, name: jnp.stack([o[name] for o in outs])
    names = ("lru_h", "lru_conv", "s5_re", "s5_im", "fox_k", "fox_v", "fox_logf", "dsa_k", "dsa_v", "dsa_kidx")
    res = [xp.reshape(B, T, D), xs.reshape(Bs, Ts, D)]
    for name in names:
        res += [stk(st_p, name), stk(st_s, name)]
    return tuple(res)
```

```python
import functools
import math

import jax
import jax.numpy as jnp
from jax import lax
from jax.experimental import pallas as pl
from jax.experimental.pallas import tpu as pltpu

F32 = jnp.float32
BF16 = jnp.bfloat16

N_MIXERS = 4
GROUP_W = 256
LRU_HEADS = 4
CONV_W = 4
LRU_C = 8.0
S5_GROUP = 16
S5_NG = 16
S5_N = 64
S5_STATE = S5_NG * S5_N
ATT_HEADS = 4
ATT_HD = 64
IDX_HEADS = 8
IDX_HD = 32
DSA_TOPK_MAX = 256
PAGE = 128
NEG_INF = -1e30
M_INIT = -1e20
BIG = 3e38
ADM_CUT = -1e29
EPS = 1e-6
ATT_SCALE = ATT_HD ** -0.5
IDX_SCALE = IDX_HD ** -0.5
IDX_W_SCALE = IDX_HEADS ** -0.5
ALIBI = tuple(2.0 ** (-8.0 * (h + 1) / ATT_HEADS) for h in range(ATT_HEADS))
LOG2E = math.log2(math.e)
Q_SCALE = ATT_SCALE * LOG2E
BISECT_ITERS = 28
SEARCH_ITERS = 48

LANES = 128
SUBLANES = 8
VMEM_LIMIT = 56 * 1024 * 1024

C_AX, C_AG, C_BU, C_CQ, C_CK, C_CV, C_DQ, C_DK, C_DV = (i * GROUP_W for i in range(9))
P_CAT = 9 * GROUP_W
C_IQ = 0
C_IK = C_IQ + IDX_HEADS * LANES
C_SM = C_IK + LANES
P_IDX = C_SM + LANES
SM_IK, SM_CF, SM_IW = 0, IDX_HD, IDX_HD + ATT_HEADS

NT_DIMS = (((1,), (1,)), ((), ()))


def _nt(a, b):
    return lax.dot_general(a, b, NT_DIMS, preferred_element_type=F32)


def _dot(a, b):
    return jnp.dot(a, b, preferred_element_type=F32)


def _rms(x, g):
    return x * lax.rsqrt(jnp.mean(x * x, axis=-1, keepdims=True) + EPS) * g


def _softplus(z):
    return jnp.maximum(z, 0.0) + jnp.log1p(jnp.exp(-jnp.abs(z)))


def _log_sigmoid(z):
    return -_softplus(-z)


def _expm1(z):
    e = jnp.exp(z)
    one = e == 1.0
    return jnp.where(one, z, (e - 1.0) * z / jnp.where(one, 1.0, jnp.log(e)))


def _sigmoid(z):
    return jax.nn.sigmoid(z)


def _gelu(z):
    return jax.nn.gelu(z)


def _params(sem):
    return pltpu.CompilerParams(dimension_semantics=sem, vmem_limit_bytes=VMEM_LIMIT)


def _head_mask(h, width=GROUP_W):
    lane = lax.broadcasted_iota(jnp.int32, (1, width), 1)
    return (lane >= h * ATT_HD) & (lane < (h + 1) * ATT_HD)


FOLD_CHAINS = 4


def _fold_rows(x, axis_op):
    n, c = x.shape
    g = FOLD_CHAINS if n % (FOLD_CHAINS * SUBLANES) == 0 else 1
    x = axis_op(x.reshape(g, n // (g * SUBLANES), SUBLANES, c), axis=1)
    return axis_op(x, axis=0)


def _col_max(x):
    return jnp.max(_fold_rows(x, jnp.max), axis=0, keepdims=True)


def _lane_cumsum(x):
    lane = lax.broadcasted_iota(jnp.int32, x.shape, 1)
    s = 1
    while s < x.shape[1]:
        x = x + jnp.where(lane >= s, pltpu.roll(x, s, axis=1), 0.0)
        s *= 2
    return x


def _split(x):
    hi = x.astype(BF16)
    return hi, (x - hi.astype(F32)).astype(BF16)


def _ada_kernel(c_ref, w_ref, b_ref, o_ref):
    c = c_ref[...]
    sh, sl = _split(c * _sigmoid(c))
    wh, wl = _split(w_ref[...])
    o_ref[...] = (_dot(sl, wl) + _dot(sl, wh) + _dot(sh, wl)) + _dot(sh, wh) + b_ref[...]


def _ada(c_all, ada_w, ada_b):
    L, D, D6 = ada_w.shape
    Bp = c_all.shape[0]
    tn = 1024
    return pl.pallas_call(
        _ada_kernel,
        grid=(L, D6 // tn),
        in_specs=[pl.BlockSpec((Bp, D), lambda l, j: (0, 0)),
                  pl.BlockSpec((None, D, tn), lambda l, j: (l, 0, j)),
                  pl.BlockSpec((None, 1, tn), lambda l, j: (l, 0, j))],
        out_specs=pl.BlockSpec((None, Bp, tn), lambda l, j: (l, 0, j)),
        out_shape=jax.ShapeDtypeStruct((L, Bp, D6), F32),
        compiler_params=_params(("arbitrary", "arbitrary")),
        name="ada",
    )(c_all, ada_w, ada_b.reshape(L, 1, D6))


def _inproj_kernel(x_ref, sh_ref, sc_ref, g_ref, w_ref, whi_ref, wlo_ref,
                   ax_ref, ag_ref, bu_ref, cq_ref, ckf_ref, ckb_ref, cvf_ref, cvb_ref,
                   dq_ref, dkf_ref, dkb_ref, dvf_ref, dvb_ref, iq_ref, ik_ref, sm_ref):
    x = x_ref[...]
    h = _rms(x, g_ref[...]) * (1.0 + sc_ref[...]) + sh_ref[...]
    hb = h.astype(BF16)

    def mm(c0, width):
        return _dot(hb, w_ref[:, c0:c0 + width])

    ax_ref[...] = mm(C_AX, GROUP_W)
    ag_ref[...] = mm(C_AG, GROUP_W)
    bu_ref[...] = mm(C_BU, GROUP_W)
    cq_ref[...] = (mm(C_CQ, GROUP_W) * Q_SCALE).astype(BF16)
    ck = mm(C_CK, GROUP_W)
    ckf_ref[...] = ck
    ckb_ref[...] = ck.astype(BF16)
    cv = mm(C_CV, GROUP_W)
    cvf_ref[...] = cv
    cvb_ref[...] = cv.astype(BF16)
    dq_ref[...] = (mm(C_DQ, GROUP_W) * Q_SCALE).astype(BF16)
    dk = mm(C_DK, GROUP_W)
    dkf_ref[...] = dk
    dkb_ref[...] = dk.astype(BF16)
    dv = mm(C_DV, GROUP_W)
    dvf_ref[...] = dv
    dvb_ref[...] = dv.astype(BF16)

    hl = (h - hb.astype(F32)).astype(BF16)

    def mm3(c0, width):
        whi = whi_ref[:, c0:c0 + width]
        return _dot(hb, whi) + _dot(hl, whi) + _dot(hb, wlo_ref[:, c0:c0 + width])

    sm_ref[...] = mm3(C_SM, LANES)

    sub = (lax.broadcasted_iota(jnp.int32, (1, LANES), 1) // IDX_HD)
    for hh in range(IDX_HEADS):
        r = mm3(C_IQ + hh * LANES, LANES)
        hi = r.astype(BF16)
        lo = (r - hi.astype(F32)).astype(BF16)
        zero = jnp.zeros_like(hi)
        iq_ref[:, hh * LANES:(hh + 1) * LANES] = jnp.where(sub == 1, lo, jnp.where(sub == 3, zero, hi))
    r = mm3(C_IK, LANES)
    hi = r.astype(BF16)
    lo = (r - hi.astype(F32)).astype(BF16)
    ik_ref[...] = jnp.where(sub == 2, lo, jnp.where(sub == 3, jnp.zeros_like(hi), hi))


def _mod_spec(rows_mod, D, tiles_per_mod, piece):
    return pl.BlockSpec((None, rows_mod, D), lambda i: (i // tiles_per_mod, 0, piece))


def _inproj(x, mod, tiles_per_mod, g, wcat, widx_hi, widx_lo, tm):
    N, D = x.shape
    rows_mod = mod.shape[1]
    tok = lambda w: pl.BlockSpec((tm, w), lambda i: (i, 0))
    widths_dtypes = [(GROUP_W, F32)] * 3 + [(GROUP_W, BF16), (GROUP_W, F32), (GROUP_W, BF16), (GROUP_W, F32),
                                            (GROUP_W, BF16), (GROUP_W, BF16), (GROUP_W, F32), (GROUP_W, BF16),
                                            (GROUP_W, F32), (GROUP_W, BF16), (IDX_HEADS * LANES, BF16),
                                            (LANES, BF16), (LANES, F32)]
    return pl.pallas_call(
        _inproj_kernel,
        grid=(N // tm,),
        in_specs=[tok(D), _mod_spec(rows_mod, D, tiles_per_mod, 0), _mod_spec(rows_mod, D, tiles_per_mod, 1),
                  pl.BlockSpec((1, D), lambda i: (0, 0)),
                  pl.BlockSpec((D, P_CAT), lambda i: (0, 0)),
                  pl.BlockSpec((D, P_IDX), lambda i: (0, 0)),
                  pl.BlockSpec((D, P_IDX), lambda i: (0, 0))],
        out_specs=[tok(w) for w, _ in widths_dtypes],
        out_shape=[jax.ShapeDtypeStruct((N, w), dt) for w, dt in widths_dtypes],
        compiler_params=_params(("arbitrary",)),
        name="inproj",
    )(x, mod, mod, g, wcat, widx_hi, widx_lo)


def _lru_kernel(ax_ref, ag_ref, buf_ref, h0_ref, cw_ref, cb_ref, wr_ref, br_ref, wi_ref, bi_ref, lam_ref,
                ya_ref, hl_ref, xs, hcar, *, tc, last_row):
    c = pl.program_id(1)

    @pl.when(c == 0)
    def _():
        xs[0:SUBLANES, :] = buf_ref[...]
        hcar[...] = h0_ref[...]

    x = ax_ref[...]
    xs[SUBLANES:SUBLANES + tc, :] = x
    w = cw_ref[...]
    y = cb_ref[...] + xs[5:5 + tc, :] * w[0:1] + xs[6:6 + tc, :] * w[1:2] + xs[7:7 + tc, :] * w[2:3] + x * w[3:4]
    xs[0:SUBLANES, :] = xs[tc:tc + SUBLANES, :]

    yb = y.astype(BF16)
    r = _sigmoid(_dot(yb, wr_ref[...]) + br_ref[...])
    i = _sigmoid(_dot(yb, wi_ref[...]) + bi_ref[...])
    log_a = -LRU_C * r * _softplus(-lam_ref[...])
    a = jnp.exp(log_a)
    b = jnp.sqrt(-_expm1(2.0 * log_a)) * (i * y)

    row = lax.broadcasted_iota(jnp.int32, (tc, GROUP_W), 0)
    s = 1
    while s < tc:
        keep = row >= s
        a_sh = jnp.where(keep, pltpu.roll(a, s, axis=0), 1.0)
        b_sh = jnp.where(keep, pltpu.roll(b, s, axis=0), 0.0)
        b = b + a * b_sh
        a = a * a_sh
        s *= 2
    h = b + a * hcar[...]
    hcar[...] = h[tc - 1:tc, :]
    ya_ref[...] = h * _gelu(ag_ref[...])

    @pl.when(c == pl.num_programs(1) - 1)
    def _():
        hl_ref[...] = h[last_row:last_row + 1, :]


def _lru(ax, ag, buf8, h0, lw, tc, t_real):
    B, Tp, W = ax.shape
    seq = pl.BlockSpec((None, tc, W), lambda b, c: (b, c, 0))
    full = lambda shape: pl.BlockSpec(shape, lambda b, c: (0,) * len(shape))
    return pl.pallas_call(
        functools.partial(_lru_kernel, tc=tc, last_row=(t_real - 1) % tc),
        grid=(B, Tp // tc),
        in_specs=[seq, seq,
                  pl.BlockSpec((None, SUBLANES, W), lambda b, c: (b, 0, 0)),
                  pl.BlockSpec((None, 1, W), lambda b, c: (b, 0, 0)),
                  full((CONV_W, W)), full((1, W)), full((W, W)), full((1, W)), full((W, W)), full((1, W)),
                  full((1, W))],
        out_specs=[seq, pl.BlockSpec((None, 1, W), lambda b, c: (b, 0, 0))],
        out_shape=[jax.ShapeDtypeStruct((B, Tp, W), F32), jax.ShapeDtypeStruct((B, 1, W), F32)],
        scratch_shapes=[pltpu.VMEM((tc + SUBLANES, W), F32), pltpu.VMEM((1, W), F32)],
        compiler_params=_params(("arbitrary", "arbitrary")),
        name="lru",
    )(ax, ag, buf8, h0, lw["conv_w"], lw["conv_b"], lw["wr"], lw["br"], lw["wi"], lw["bi"], lw["lam"])


def _s5_kernel(u_ref, h0r_ref, h0i_ref, ldt_ref, are_ref, aim_ref, bre_ref, bim_ref, cre_ref, cim_ref,
               d_ref, gw_ref, gb_ref, y_ref, hlr_ref, hli_ref, hr_s, hi_s, car_r, car_i, *, tc, last_row):
    c = pl.program_id(1)

    @pl.when(c == 0)
    def _():
        car_r[...] = h0r_ref[...]
        car_i[...] = h0i_ref[...]

    dt = jnp.exp(ldt_ref[...])
    ar, ai = are_ref[...], aim_ref[...]
    mag = jnp.exp(dt * ar)
    abr, abi = mag * jnp.cos(dt * ai), mag * jnp.sin(dt * ai)
    den = ar * ar + ai * ai
    nr, ni = abr - 1.0, abi
    cr = (nr * ar + ni * ai) / den
    ci = (ni * ar - nr * ai) / den

    u = u_ref[...]
    ub = u.astype(BF16)
    pre = _dot(ub, bre_ref[...])
    pim = _dot(ub, bim_ref[...])
    hr_s[...] = cr * pre - ci * pim
    hi_s[...] = cr * pim + ci * pre

    def step(t, carry):
        hr, hi = carry
        nhr = abr * hr - abi * hi + hr_s[pl.ds(t, 1), :]
        nhi = abr * hi + abi * hr + hi_s[pl.ds(t, 1), :]
        hr_s[pl.ds(t, 1), :] = nhr
        hi_s[pl.ds(t, 1), :] = nhi
        return nhr, nhi

    hr, hi = lax.fori_loop(0, tc, step, (car_r[...], car_i[...]), unroll=8)
    car_r[...] = hr
    car_i[...] = hi

    hrv, hiv = hr_s[...], hi_s[...]
    y = _dot(hrv.astype(BF16), cre_ref[...]) - _dot(hiv.astype(BF16), cim_ref[...]) + d_ref[...] * u
    g = _gelu(y)
    y_ref[...] = g * _sigmoid(_dot(g.astype(BF16), gw_ref[...]) + gb_ref[...])

    @pl.when(c == pl.num_programs(1) - 1)
    def _():
        hlr_ref[...] = hr_s[last_row:last_row + 1, :]
        hli_ref[...] = hi_s[last_row:last_row + 1, :]


def _s5(u, h0r, h0i, sw, tc, t_real):
    B, Tp, W = u.shape
    S = S5_STATE
    seq = pl.BlockSpec((None, tc, W), lambda b, c: (b, c, 0))
    st = pl.BlockSpec((None, 1, S), lambda b, c: (b, 0, 0))
    full = lambda shape: pl.BlockSpec(shape, lambda b, c: (0,) * len(shape))
    return pl.pallas_call(
        functools.partial(_s5_kernel, tc=tc, last_row=(t_real - 1) % tc),
        grid=(B, Tp // tc),
        in_specs=[seq, st, st, full((1, S)), full((1, S)), full((1, S)), full((W, S)), full((W, S)),
                  full((S, W)), full((S, W)), full((1, W)), full((W, W)), full((1, W))],
        out_specs=[seq, st, st],
        out_shape=[jax.ShapeDtypeStruct((B, Tp, W), F32), jax.ShapeDtypeStruct((B, 1, S), F32),
                   jax.ShapeDtypeStruct((B, 1, S), F32)],
        scratch_shapes=[pltpu.VMEM((tc, S), F32), pltpu.VMEM((tc, S), F32),
                        pltpu.VMEM((1, S), F32), pltpu.VMEM((1, S), F32)],
        compiler_params=_params(("arbitrary", "arbitrary")),
        name="s5",
    )(u, h0r, h0i, sw["ldt"], sw["are"], sw["aim"], sw["bre"], sw["bim"], sw["cre"], sw["cim"],
      sw["d"], sw["gw"], sw["gb"])


def _fcum_kernel(cf_ref, fb_ref, lf_ref, f_ref, *, nb):
    lf = _log_sigmoid(cf_ref[...] + fb_ref[...])
    lf_ref[...] = lf
    cs = _lane_cumsum(lf)
    tot = jnp.broadcast_to(cs[:, LANES - 1:LANES], cs.shape)
    row = lax.broadcasted_iota(jnp.int32, cs.shape, 0) % nb
    inc = tot
    s = 1
    while s < nb:
        inc = inc + jnp.where(row >= s, pltpu.roll(inc, s, axis=0), 0.0)
        s *= 2
    f_ref[...] = (cs + (inc - tot)) * LOG2E


def _fcum(cf_rows, fb_rows, nb):
    B, R, _ = cf_rows.shape
    blk = pl.BlockSpec((None, R, LANES), lambda b: (b, 0, 0))
    return pl.pallas_call(
        functools.partial(_fcum_kernel, nb=nb),
        grid=(B,),
        in_specs=[blk, pl.BlockSpec((R, 1), lambda b: (0, 0))],
        out_specs=[blk, blk],
        out_shape=[jax.ShapeDtypeStruct((B, R, LANES), F32)] * 2,
        compiler_params=_params(("arbitrary",)),
        name="fcum",
    )(cf_rows, fb_rows)


ONES_ROWS = 16
VROWS = ATT_HD + ONES_ROWS


def _init_heads(qT_ref, qm_s, acc_s, m_s):
    qT = qT_ref[...]
    row = lax.broadcasted_iota(jnp.int32, (qT.shape[0], 1), 0)
    for h in range(ATT_HEADS):
        qm_s[h] = jnp.where(row // ATT_HD == h, qT, jnp.zeros_like(qT))
    acc_s[...] = jnp.zeros_like(acc_s)
    m_s[...] = jnp.full_like(m_s, M_INIT)


def _attend_block(kblk, vT_ref, k0, kb, qm_s, acc_s, m_s, s_s, p_s, bias_fn):
    for h in range(ATT_HEADS):
        s_s[h] = _dot(kblk, qm_s[h])
    m_new = []
    for h in range(ATT_HEADS):
        s = bias_fn(h, s_s[h])
        s_s[h] = s
        m_new.append(jnp.maximum(m_s[h], _col_max(s)))
    for h in range(ATT_HEADS):
        p_s[h] = jnp.exp2(s_s[h] - m_new[h]).astype(BF16)
    for h in range(ATT_HEADS):
        rows = slice(h * VROWS, (h + 1) * VROWS)
        pv = _dot(vT_ref[rows, pl.ds(k0, kb)], p_s[h])
        acc_s[rows, :] = jnp.exp2(m_s[h] - m_new[h]) * acc_s[rows, :] + pv
        m_s[h] = m_new[h]


def _finish_heads(o_ref, acc_s):
    for h in range(ATT_HEADS):
        base = h * VROWS
        o_ref[h * ATT_HD:(h + 1) * ATT_HD, :] = (acc_s[base:base + ATT_HD, :]
                                                 / acc_s[base + ATT_HD:base + ATT_HD + 1, :])


def _fox_kernel(qT_ref, k_ref, vT_ref, f2_ref, o_ref, qm_s, acc_s, m_s, s_s, p_s, *, tq, kb):
    q0 = pl.program_id(1) * tq
    _init_heads(qT_ref, qm_s, acc_s, m_s)
    kio = lax.broadcasted_iota(jnp.int32, (kb, tq), 0)
    qpos = q0 + lax.broadcasted_iota(jnp.int32, (kb, tq), 1)

    def block(j, masked):
        k0 = pl.multiple_of(j * kb, kb)
        fk = f2_ref[pl.ds(k0, kb), :]

        def bias(h, s):
            s = s - fk[:, h:h + 1]
            return jnp.where(k0 + kio <= qpos, s, NEG_INF) if masked else s

        _attend_block(k_ref[pl.ds(k0, kb), :], vT_ref, k0, kb, qm_s, acc_s, m_s, s_s, p_s, bias)
        return 0

    n_full = (q0 + 1) // kb
    n_blk = (q0 + tq - 1) // kb + 1
    lax.fori_loop(0, n_full, lambda j, c: block(j, False), 0)
    lax.fori_loop(n_full, n_blk, lambda j, c: block(j, True), 0)
    _finish_heads(o_ref, acc_s)


def _att_scratch(tq, kb):
    return [pltpu.VMEM((ATT_HEADS, GROUP_W, tq), BF16), pltpu.VMEM((ATT_HEADS * VROWS, tq), F32),
            pltpu.VMEM((ATT_HEADS, 1, tq), F32),
            pltpu.VMEM((ATT_HEADS, kb, tq), F32), pltpu.VMEM((ATT_HEADS, kb, tq), BF16)]


def _with_ones_rows(vT):
    B, _, T = vT.shape
    v4 = vT.reshape(B, ATT_HEADS, ATT_HD, T)
    return jnp.concatenate([v4, jnp.ones((B, ATT_HEADS, ONES_ROWS, T), vT.dtype)], axis=2).reshape(
        B, ATT_HEADS * VROWS, T)


def _resident(shape):
    return pl.BlockSpec((None,) + shape, lambda b, i: (b,) + (0,) * len(shape), pipeline_mode=pl.Buffered(1))


def _fox(qT, k, vT, f2col, tq, kb):
    B, W, T = qT.shape
    colblk = pl.BlockSpec((None, W, tq), lambda b, i: (b, 0, i))
    return pl.pallas_call(
        functools.partial(_fox_kernel, tq=tq, kb=kb),
        grid=(B, T // tq),
        in_specs=[colblk, _resident((T, W)), _resident((ATT_HEADS * VROWS, T)), _resident((T, ATT_HEADS))],
        out_specs=colblk,
        out_shape=jax.ShapeDtypeStruct((B, W, T), F32),
        scratch_shapes=_att_scratch(tq, kb),
        compiler_params=_params(("arbitrary", "arbitrary")),
        name="fox",
    )(qT, k, vT, f2col)


def _topk_to_bias(S, nblk, kb, C, k, nadm):
    kf = float(k)

    sb = LANES if kb % LANES == 0 else kb

    def reduce_blocks(fn, init):
        def body(j, c):
            k0 = pl.multiple_of(j * kb, kb)
            for r in range(0, kb, sb):
                c = fn(c, S[pl.ds(k0 + r, sb), :], k0 + r)
            return c
        return lax.fori_loop(0, nblk, body, init)

    fold_sum = lambda x: _fold_rows(x, jnp.sum)
    fold_max = lambda x: _fold_rows(x, jnp.max)
    fold_min = lambda x: _fold_rows(x, jnp.min)

    def count(pred):
        c = reduce_blocks(lambda c, x, k0: c + fold_sum(jnp.where(pred(x, k0), 1.0, 0.0)),
                          jnp.zeros((SUBLANES, C), F32))
        return jnp.sum(c, axis=0, keepdims=True)

    big8, zero8 = jnp.full((SUBLANES, C), BIG, F32), jnp.zeros((SUBLANES, C), F32)
    mx, mn, mp, np8, nz8 = reduce_blocks(
        lambda c, x, k0: (jnp.maximum(c[0], fold_max(x)),
                          jnp.minimum(c[1], fold_min(jnp.where(x > ADM_CUT, x, BIG))),
                          jnp.minimum(c[2], fold_min(jnp.where(x > 0.0, x, BIG))),
                          c[3] + fold_sum(jnp.where(x > 0.0, 1.0, 0.0)),
                          c[4] + fold_sum(jnp.where(x == 0.0, 1.0, 0.0))),
        (-big8, big8, big8, zero8, zero8))
    cmax = jnp.max(mx, axis=0, keepdims=True)
    cmin = jnp.min(mn, axis=0, keepdims=True)
    small = nadm <= kf
    cpos = jnp.sum(np8, axis=0, keepdims=True)
    cnn = cpos + jnp.sum(nz8, axis=0, keepdims=True)
    at_zero = (cpos < kf) & (cnn >= kf) & jnp.logical_not(small)
    settled = small | at_zero
    above = cpos >= kf
    lo0 = jnp.where(above, jnp.min(mp, axis=0, keepdims=True), cmin)
    clo0 = jnp.where(above, cpos, nadm)
    hi0 = jnp.where(above, cmax + jnp.maximum(1e-30, 1e-6 * jnp.abs(cmax)), 0.0)
    chi0 = jnp.where(above, 0.0, cnn)

    def any_col(flag):
        return jnp.max(jnp.where(flag, 1.0, 0.0)) > 0.5

    def probe(c, frac):
        lo, hi, clo, chi = c
        mid = lo + (hi - lo) * frac
        cm = count(lambda x, k0: x >= mid)
        ge = cm >= kf
        return jnp.where(ge, mid, lo), jnp.where(ge, hi, mid), jnp.where(ge, cm, clo), jnp.where(ge, chi, cm)

    def unresolved(clo, chi):
        g, r = clo - chi, kf - chi
        return (g > 2.5) & (r > 1.5) & (g - r > 0.5) & jnp.logical_not(settled)

    _, lo, hi, clo, chi = lax.while_loop(
        lambda c: (c[0] < BISECT_ITERS) & any_col(unresolved(c[3], c[4])),
        lambda c: (c[0] + 1,) + probe(c[1:], 0.5),
        (jnp.int32(0), lo0, hi0, clo0, chi0))

    top, bot = reduce_blocks(
        lambda c, x, k0: (jnp.maximum(c[0], fold_max(jnp.where(x < hi, x, -BIG))),
                          jnp.minimum(c[1], fold_min(jnp.where(x >= lo, x, BIG)))),
        (jnp.full((SUBLANES, C), -BIG, F32), jnp.full((SUBLANES, C), BIG, F32)))
    thr = jnp.where(kf - chi < 1.5, jnp.max(top, axis=0, keepdims=True), jnp.min(bot, axis=0, keepdims=True))
    thr = jnp.where(small, cmin, jnp.where(at_zero, 0.0, thr))
    cge = count(lambda x, k0: x >= thr)
    cgt = count(lambda x, k0: x > thr)
    missed = ((cgt >= kf) | (cge < kf)) & jnp.logical_not(settled)

    def slow(_):
        def open_(clo, chi):
            return (clo - chi > 1.5) & jnp.logical_not(settled)

        def search_body(c):
            it, lo, hi, clo, chi = c
            frac = jnp.clip((clo - kf + 0.5) / jnp.maximum(clo - chi, 1.0), 1.0 / 64, 63.0 / 64)
            return (it + 1,) + probe((lo, hi, clo, chi), jnp.where(it % 2 == 0, frac, 0.5))

        _, _, hi2, _, _ = lax.while_loop(lambda c: (c[0] < SEARCH_ITERS) & any_col(open_(c[3], c[4])), search_body,
                                         (jnp.int32(0), lo, hi, clo, chi))

        def unsat(cnt):
            return (cnt < kf) & jnp.logical_not(settled)

        def fix_body(c):
            t, cnt = c
            below = reduce_blocks(lambda m, x, k0: jnp.maximum(m, fold_max(jnp.where(x < t, x, -BIG))),
                                  jnp.full((SUBLANES, C), -BIG, F32))
            nt = jnp.where(unsat(cnt), jnp.max(below, axis=0, keepdims=True), t)
            return nt, count(lambda x, k0: x >= nt)

        t2, cge2 = lax.while_loop(lambda c: any_col(unsat(c[1])), fix_body, (hi2, jnp.zeros((1, C), F32)))
        cgt2 = count(lambda x, k0: x > t2)
        return jnp.where(missed, t2, thr), jnp.where(missed, cge2, cge), jnp.where(missed, cgt2, cgt)

    thr, cge, cgt = lax.cond(any_col(missed), slow, lambda _: (thr, cge, cgt), 0)
    need = kf - cgt
    ties = (cge > kf) & jnp.logical_not(small)

    def write_plain(_):
        def body(j, _):
            k0 = pl.multiple_of(j * kb, kb)
            S[pl.ds(k0, kb), :] = jnp.where(S[pl.ds(k0, kb), :] >= thr, 0.0, NEG_INF)
            return 0
        return lax.fori_loop(0, nblk, body, 0)

    def write_ties(_):
        tri = jnp.where(lax.broadcasted_iota(jnp.int32, (kb, kb), 0) >= lax.broadcasted_iota(jnp.int32, (kb, kb), 1),
                        1.0, 0.0).astype(BF16)

        def body(j, seen):
            k0 = pl.multiple_of(j * kb, kb)
            x = S[pl.ds(k0, kb), :]
            eq = x == thr
            rank = seen + _dot(tri, jnp.where(eq, 1.0, 0.0).astype(BF16))
            S[pl.ds(k0, kb), :] = jnp.where((x > thr) | (eq & (rank <= need)), 0.0, NEG_INF)
            return rank[kb - 1:kb, :]
        lax.fori_loop(0, nblk, body, jnp.zeros((1, C), F32))
        return 0

    lax.cond(any_col(ties), write_ties, write_plain, 0)


def _dsa_kernel(iqT_ref, ik_ref, smT_ref, qT_ref, k_ref, vT_ref, o_ref, S, qm_s, acc_s, m_s, s_s, p_s,
                *, tq, kb, topk):
    q0 = pl.program_id(1) * tq
    n_blk = (q0 + tq - 1) // kb + 1
    kio = lax.broadcasted_iota(jnp.int32, (kb, tq), 0)
    qpos = q0 + lax.broadcasted_iota(jnp.int32, (kb, tq), 1)
    wc = smT_ref[SM_IW:SM_IW + IDX_HEADS, :] * (IDX_W_SCALE * IDX_SCALE)

    def scores(j, _):
        k0 = pl.multiple_of(j * kb, kb)
        kblk = ik_ref[pl.ds(k0, kb), :]
        sc = jnp.zeros((kb, tq), F32)
        for hh in range(IDX_HEADS):
            d = _dot(kblk, iqT_ref[hh * LANES:(hh + 1) * LANES, :])
            sc = sc + wc[hh:hh + 1, :] * jnp.maximum(d, 0.0)
        S[pl.ds(k0, kb), :] = jnp.where(k0 + kio <= qpos, sc, NEG_INF)
        return 0

    lax.fori_loop(0, n_blk, scores, 0)

    nadm = (q0 + 1 + lax.broadcasted_iota(jnp.int32, (1, tq), 1)).astype(F32)
    _topk_to_bias(S, n_blk, kb, tq, topk, nadm)

    _init_heads(qT_ref, qm_s, acc_s, m_s)
    kcol =lax.broadcasted_iota(jnp.int32, (kb, 1), 0)

    def attend(j, _):
        k0 = pl.multiple_of(j * kb, kb)
        sel = S[pl.ds(k0, kb), :]
        kpos = (k0 + kcol).astype(F32)

        def bias(h, s):
            return s + (sel + (ALIBI[h] * LOG2E) * kpos)

        _attend_block(k_ref[pl.ds(k0, kb), :], vT_ref, k0, kb, qm_s, acc_s, m_s, s_s, p_s, bias)
        return 0

    lax.fori_loop(0, n_blk, attend, 0)
    _finish_heads(o_ref, acc_s)


def _dsa(iqT, ikp, smT, qT, k, vT, tq, kb, topk):
    B, W, T = qT.shape
    col = lambda r: pl.BlockSpec((None, r, tq), lambda b, i: (b, 0, i))
    return pl.pallas_call(
        functools.partial(_dsa_kernel, tq=tq, kb=kb, topk=topk),
        grid=(B, T // tq),
        in_specs=[col(IDX_HEADS * LANES), _resident((T, LANES)), col(LANES), col(W),
                  _resident((T, W)), _resident((ATT_HEADS * VROWS, T))],
        out_specs=col(W),
        out_shape=jax.ShapeDtypeStruct((B, W, T), F32),
        scratch_shapes=[pltpu.VMEM((T, tq), F32)] + _att_scratch(tq, kb),
        compiler_params=_params(("arbitrary", "arbitrary")),
        name="dsa",
    )(iqT, ikp, smT, qT, k, vT)


def _sa_kernel(pt_ref, q_ref, w_ref, cfn_ref, fb_ref, ikn_ref, *rest, pps, tn):
    ki_refs, lf_refs = rest[:pps], rest[pps:2 * pps]
    s_ref, f_ref, sn_ref, fn_ref, lfn_ref, carry_s = rest[2 * pps:]
    step = pl.program_id(1)

    @pl.when(step == 0)
    def _():
        carry_s[...] = jnp.zeros_like(carry_s)

    qp = q_ref[...]
    qh = qp[:, 0:IDX_HD]
    ql = qp[:, IDX_HD:2 * IDX_HD]
    w = w_ref[...] * IDX_W_SCALE
    nq = qp.shape[0] // IDX_HEADS

    def scores(kT):
        kh, kl = _split(kT)
        d = _dot(qh, kh) + _dot(ql, kh) + _dot(qh, kl)
        r = jnp.maximum(d * IDX_SCALE, 0.0) * w
        return jnp.sum(r.reshape(nq, IDX_HEADS, kT.shape[1]), axis=1)

    s_ref[...] = scores(jnp.concatenate([r[...] for r in ki_refs], axis=1))
    cs_all = _lane_cumsum(jnp.concatenate([r[...] for r in lf_refs], axis=0))
    carry = carry_s[...]
    for i in range(pps):
        cs = cs_all[i * ATT_HEADS:(i + 1) * ATT_HEADS, :] + carry
        f_ref[:, i * PAGE:(i + 1) * PAGE] = cs
        carry = cs[:, PAGE - 1:PAGE]
    carry_s[...] = carry

    @pl.when(step == pl.num_programs(1) - 1)
    def _():
        lane = lax.broadcasted_iota(jnp.int32, (ATT_HEADS, PAGE), 1)
        lfn = _log_sigmoid(cfn_ref[...] + fb_ref[...])
        lfn_ref[...] = lfn
        fn_ref[...] = _lane_cumsum(jnp.where(lane < tn, lfn, 0.0)) + carry
        col = lax.broadcasted_iota(jnp.int32, (nq, PAGE), 1)
        rowq = lax.broadcasted_iota(jnp.int32, (nq, PAGE), 0)
        sn_ref[...] = jnp.where((col <= rowq) & (col < tn), scores(ikn_ref[...]), NEG_INF)


def _sa(page_table, l, qrows, wrows, cfn, fb, ikn, kidx_pool, lfT_pool, pps, tn):
    B, n_pages = page_table.shape
    nsteps = n_pages // pps
    P = n_pages * PAGE
    nq = qrows.shape[1] // IDX_HEADS
    per_b = lambda shape: pl.BlockSpec((None,) + shape, lambda b, s, pt: (b,) + (0,) * len(shape))
    ki_specs = [pl.BlockSpec((None, None, IDX_HD, PAGE), lambda b, s, pt, i=i: (l, pt[b, s * pps + i], 0, 0))
                for i in range(pps)]
    lf_specs = [pl.BlockSpec((None, None, ATT_HEADS, PAGE), lambda b, s, pt, i=i: (l, pt[b, s * pps + i], 0, 0))
                for i in range(pps)]
    gs = pltpu.PrefetchScalarGridSpec(
        num_scalar_prefetch=1,
        grid=(B, nsteps),
        in_specs=[per_b((nq * IDX_HEADS, LANES)), per_b((nq * IDX_HEADS, 1)), per_b((ATT_HEADS, PAGE)),
                  pl.BlockSpec((ATT_HEADS, 1), lambda b, s, pt: (0, 0)), per_b((IDX_HD, PAGE))] + ki_specs + lf_specs,
        out_specs=[pl.BlockSpec((None, nq, pps * PAGE), lambda b, s, pt: (b, 0, s)),
                   pl.BlockSpec((None, ATT_HEADS, pps * PAGE), lambda b, s, pt: (b, 0, s)),
                   per_b((nq, PAGE)), per_b((ATT_HEADS, PAGE)), per_b((ATT_HEADS, PAGE))],
        scratch_shapes=[pltpu.VMEM((ATT_HEADS, 1), F32)])
    return pl.pallas_call(
        functools.partial(_sa_kernel, pps=pps, tn=tn),
        grid_spec=gs,
        out_shape=[jax.ShapeDtypeStruct((B, nq, P), F32), jax.ShapeDtypeStruct((B, ATT_HEADS, P), F32),
                   jax.ShapeDtypeStruct((B, nq, PAGE), F32), jax.ShapeDtypeStruct((B, ATT_HEADS, PAGE), F32),
                   jax.ShapeDtypeStruct((B, ATT_HEADS, PAGE), F32)],
        compiler_params=_params(("arbitrary", "arbitrary")),
        name="sample_scores",
    )(page_table, qrows, wrows, cfn, fb, ikn, *([kidx_pool] * pps), *([lfT_pool] * pps))


def _sb_kernel(s_ref, o_ref, *, kb, nblk, topk, past, nq):
    C = s_ref.shape[1]
    o_ref[...] = s_ref[...]
    nadm = (past + 1 + lax.broadcasted_iota(jnp.int32, (1, C), 1) % nq).astype(F32)
    _topk_to_bias(o_ref, nblk, kb, C, topk, nadm)


def _sb(s_allT, topk, past, nq):
    Wt, C = s_allT.shape
    nl = Wt // LANES
    div = max(d for d in range(1, 9) if nl % d == 0)
    kb = div * LANES
    return pl.pallas_call(
        functools.partial(_sb_kernel, kb=kb, nblk=Wt // kb, topk=topk, past=past, nq=nq),
        grid=(1,),
        in_specs=[pl.BlockSpec((Wt, C), lambda i: (0, 0))],
        out_specs=pl.BlockSpec((Wt, C), lambda i: (0, 0)),
        out_shape=jax.ShapeDtypeStruct((Wt, C), F32),
        compiler_params=_params(("arbitrary",)),
        name="sample_topk",
    )(s_allT)


def _sc_kernel(pt_ref, fq_ref, dq_ref, fqc_ref, fp_ref, bp_ref, fn_ref, bn_ref,
               ckn_ref, cvn_ref, dkn_ref, dvn_ref, *rest, pps, past, tn, nq):
    fk_refs, fv_refs = rest[0:pps], rest[pps:2 * pps]
    dk_refs, dv_refs = rest[2 * pps:3 * pps], rest[3 * pps:4 * pps]
    yc_ref, yd_ref, qf_s, qd_s, accf, mf, lf, accd, md, ld = rest[4 * pps:]
    step = pl.program_id(1)
    R = ATT_HEADS * SUBLANES

    def stack_heads(q8):
        return jnp.concatenate([jnp.where(_head_mask(h), q8, jnp.zeros_like(q8)) for h in range(ATT_HEADS)], axis=0)

    @pl.when(step == 0)
    def _():
        qf_s[...] = stack_heads(fq_ref[...])
        qd_s[...] = stack_heads(dq_ref[...])
        for acc, m, l in ((accf, mf, lf), (accd, md, ld)):
            acc[...] = jnp.zeros_like(acc)
            m[...] = jnp.full_like(m, M_INIT)
            l[...] = jnp.zeros_like(l)

    def update(q_s, kT, vT, bias, acc, m, l):
        s = _dot(q_s[...], kT) + bias * LOG2E
        m_prev = m[...]
        m_new = jnp.maximum(m_prev, jnp.max(s, axis=1, keepdims=True))
        p = jnp.exp2(s - m_new)
        alpha = jnp.exp2(m_prev - m_new)
        l[...] = alpha * l[...] + jnp.sum(p, axis=1, keepdims=True)
        m[...] = m_new
        acc[...] = alpha * acc[...] + _nt(p.astype(BF16), vT)

    def per_head_rows(x4):
        return jnp.concatenate([jnp.broadcast_to(x4[h:h + 1, :], (SUBLANES, x4.shape[1]))
                                for h in range(ATT_HEADS)], axis=0)

    def per_query_rows(xq):
        r8 = lax.broadcasted_iota(jnp.int32, (SUBLANES, xq.shape[1]), 0)
        x8 = jnp.zeros((SUBLANES, xq.shape[1]), F32)
        for qq in range(nq):
            x8 = jnp.where(r8 == qq, jnp.broadcast_to(xq[qq:qq + 1, :], x8.shape), x8)
        return jnp.concatenate([x8] * ATT_HEADS, axis=0)

    rowi = lax.broadcasted_iota(jnp.int32, (R, 1), 0)
    qidx = rowi % SUBLANES
    slope = jnp.zeros((R, 1), F32)
    for h in range(ATT_HEADS):
        slope = jnp.where(rowi // SUBLANES == h, ALIBI[h], slope)

    def slab(ref):
        return ref[...].reshape(GROUP_W, ref.shape[-1]).astype(BF16)

    def blocks(refs):
        return jnp.concatenate([slab(r) for r in refs], axis=1)

    kw = pps * PAGE
    col = lax.broadcasted_iota(jnp.int32, (R, kw), 1)
    update(qf_s, blocks(fk_refs), blocks(fv_refs), fqc_ref[...] - per_head_rows(fp_ref[...]), accf, mf, lf)
    dist = (past + qidx - (step * kw + col)).astype(F32)
    update(qd_s, blocks(dk_refs), blocks(dv_refs), per_query_rows(bp_ref[...]) - slope * dist, accd, md, ld)

    @pl.when(step == pl.num_programs(1) - 1)
    def _():
        coln = lax.broadcasted_iota(jnp.int32, (R, PAGE), 1)
        ok = (coln <= qidx) & (coln < tn)
        bias_f = jnp.where(ok, fqc_ref[...] - per_head_rows(fn_ref[...]), NEG_INF)
        update(qf_s, slab(ckn_ref), slab(cvn_ref), bias_f, accf, mf, lf)
        distn = (qidx - coln).astype(F32)
        bias_d = jnp.where(coln < tn, per_query_rows(bn_ref[...]), NEG_INF) - slope * distn
        update(qd_s, slab(dkn_ref), slab(dvn_ref), bias_d, accd, md, ld)
        for acc, l, out in ((accf, lf, yc_ref), (accd, ld, yd_ref)):
            o = acc[...] / l[...]
            y = jnp.zeros((SUBLANES, GROUP_W), F32)
            for h in range(ATT_HEADS):
                y = jnp.where(_head_mask(h), o[h * SUBLANES:(h + 1) * SUBLANES, :], y)
            out[...] = y


def _sc(page_table, l, fq8, dq8, fqc, f_past, b_past, f_new, b_new, ckn, cvn, dkn, dvn,
        fk_pool, fv_pool, dk_pool, dv_pool, pps, tn):
    B, n_pages = page_table.shape
    nsteps = n_pages // pps
    past = n_pages * PAGE
    nq = b_past.shape[1]
    W = GROUP_W
    per_b = lambda shape: pl.BlockSpec((None,) + shape, lambda b, s, pt: (b,) + (0,) * len(shape))
    page = lambda i: pl.BlockSpec((None, None, ATT_HEADS, ATT_HD, PAGE),
                                  lambda b, s, pt, i=i: (l, pt[b, s * pps + i], 0, 0, 0))
    pages = [page(i) for i in range(pps)]
    newpage = per_b((ATT_HEADS, ATT_HD, PAGE))
    R = ATT_HEADS * SUBLANES
    gs = pltpu.PrefetchScalarGridSpec(
        num_scalar_prefetch=1,
        grid=(B, nsteps),
        in_specs=[per_b((SUBLANES, W)), per_b((SUBLANES, W)), per_b((R, 1)),
                  pl.BlockSpec((None, ATT_HEADS, pps * PAGE), lambda b, s, pt: (b, 0, s)),
                  pl.BlockSpec((None, nq, pps * PAGE), lambda b, s, pt: (b, 0, s)),
                  per_b((ATT_HEADS, PAGE)), per_b((nq, PAGE)),
                  newpage, newpage, newpage, newpage] + pages * 4,
        out_specs=[per_b((SUBLANES, W)), per_b((SUBLANES, W))],
        scratch_shapes=[pltpu.VMEM((R, W), BF16), pltpu.VMEM((R, W), BF16),
                        pltpu.VMEM((R, W), F32), pltpu.VMEM((R, 1), F32), pltpu.VMEM((R, 1), F32),
                        pltpu.VMEM((R, W), F32), pltpu.VMEM((R, 1), F32), pltpu.VMEM((R, 1), F32)])
    return pl.pallas_call(
        functools.partial(_sc_kernel, pps=pps, past=past, tn=tn, nq=nq),
        grid_spec=gs,
        out_shape=[jax.ShapeDtypeStruct((B, SUBLANES, W), F32)] * 2,
        compiler_params=_params(("arbitrary", "arbitrary")),
        name="sample_attn",
    )(page_table, fq8, dq8, fqc, f_past, b_past, f_new, b_new, ckn, cvn, dkn, dvn,
      *([fk_pool] * pps), *([fv_pool] * pps), *([dk_pool] * pps), *([dv_pool] * pps))


def _post_kernel(x_ref, ya_ref, yb_ref, yc_ref, yd_ref, g1_ref, sh2_ref, sc2_ref, g2_ref,
                 gn_ref, wo_ref, npost_ref, nfpre_ref, wg_ref, wu_ref, wd_ref, nfpost_ref, o_ref, *, chunks):
    o = None
    for i, r in enumerate((ya_ref, yb_ref, yc_ref, yd_ref)):
        part = _rms(r[...], gn_ref[:, i * GROUP_W:(i + 1) * GROUP_W]).astype(BF16)
        d = _dot(part, wo_ref[i * GROUP_W:(i + 1) * GROUP_W, :])
        o = d if o is None else o + d
    x1 = x_ref[...] + g1_ref[...] * _rms(o, npost_ref[...])
    h2 = (_rms(x1, nfpre_ref[...]) * (1.0 + sc2_ref[...]) + sh2_ref[...]).astype(BF16)
    f = None
    for c0, c1 in chunks:
        gate = _dot(h2, wg_ref[:, c0:c1])
        up = _dot(h2, wu_ref[:, c0:c1])
        d = _dot((gate * _sigmoid(gate) * up).astype(BF16), wd_ref[c0:c1, :])
        f = d if f is None else f + d
    o_ref[...] = x1 + g2_ref[...] * _rms(f, nfpost_ref[...])


def _post(x, ya, yb, yc, yd, mod, tiles_per_mod, pw, tm):
    N, D = x.shape
    H = pw["wg"].shape[1]
    rows_mod = mod.shape[1]
    step = 1024
    chunks = tuple((c, min(c + step, H)) for c in range(0, H, step))
    tok = lambda w: pl.BlockSpec((tm, w), lambda i: (i, 0))
    const = lambda shape: pl.BlockSpec(shape, lambda i: (0,) * len(shape), pipeline_mode=pl.Buffered(1))
    return pl.pallas_call(
        functools.partial(_post_kernel, chunks=chunks),
        grid=(N // tm,),
        in_specs=[tok(D), tok(GROUP_W), tok(GROUP_W), tok(GROUP_W), tok(GROUP_W),
                  _mod_spec(rows_mod, D, tiles_per_mod, 2), _mod_spec(rows_mod, D, tiles_per_mod, 3),
                  _mod_spec(rows_mod, D, tiles_per_mod, 4), _mod_spec(rows_mod, D, tiles_per_mod, 5),
                  const((1, D)), const((D, D)), const((1, D)), const((1, D)),
                  const((D, H)), const((D, H)), const((H, D)), const((1, D))],
        out_specs=tok(D),
        out_shape=jax.ShapeDtypeStruct((N, D), F32),
        compiler_params=_params(("arbitrary",)),
        name="post",
    )(x, ya, yb, yc, yd, mod, mod, mod, mod, pw["gn"], pw["wo"], pw["npost"], pw["nfpre"],
      pw["wg"], pw["wu"], pw["wd"], pw["nfpost"])


def _block_diag(blocks):
    G, r, c = blocks.shape
    eye = jnp.eye(G, dtype=blocks.dtype)
    return (blocks[:, :, None, :] * eye[:, None, :, None]).reshape(G * r, G * c)


def _cat_weight(w):
    D = w.shape[0]
    widths = (GROUP_W, GROUP_W, GROUP_W, GROUP_W, GROUP_W, GROUP_W, ATT_HEADS, GROUP_W, GROUP_W, GROUP_W,
              IDX_HEADS * IDX_HD, IDX_HD, IDX_HEADS)
    offs = [0]
    for wd in widths:
        offs.append(offs[-1] + wd)
    pc = [w[:, offs[i]:offs[i + 1]] for i in range(len(widths))]
    a_x, a_g, b_u, c_q, c_k, c_v, c_f, d_q, d_k, d_v, i_q, i_k, i_w = pc
    iq_rep = jnp.tile(i_q.reshape(D, IDX_HEADS, 1, IDX_HD), (1, 1, LANES // IDX_HD, 1)).reshape(D, IDX_HEADS * LANES)
    ik_rep = jnp.tile(i_k, (1, LANES // IDX_HD))
    small = jnp.concatenate([i_k, c_f, i_w, jnp.zeros((D, LANES - IDX_HD - ATT_HEADS - IDX_HEADS), w.dtype)], axis=1)
    wcat = jnp.concatenate([a_x, a_g, b_u, c_q, c_k, c_v, d_q, d_k, d_v], axis=1).astype(BF16)
    widx = jnp.concatenate([iq_rep, ik_rep, small], axis=1)
    hi = lax.reduce_precision(widx, exponent_bits=8, mantissa_bits=7)
    return wcat, hi.astype(BF16), (widx - hi).astype(BF16)


def _layer_weights(l, p):
    row = lambda a: a[l].reshape(1, -1)
    lw = dict(conv_w=p["lru_conv_w"][l], conv_b=row(p["lru_conv_b"]),
              wr=_block_diag(p["lru_wr"][l]).astype(BF16), br=row(p["lru_br"]),
              wi=_block_diag(p["lru_wi"][l]).astype(BF16), bi=row(p["lru_bi"]), lam=row(p["lru_lambda"]))
    sw = dict(ldt=jnp.repeat(p["s5_log_dt"][l], S5_N).reshape(1, S5_STATE),
              are=p["s5_a_re"][l].reshape(1, S5_STATE), aim=p["s5_a_im"][l].reshape(1, S5_STATE),
              bre=_block_diag(jnp.swapaxes(p["s5_b_re"][l], 1, 2)).astype(BF16),
              bim=_block_diag(jnp.swapaxes(p["s5_b_im"][l], 1, 2)).astype(BF16),
              cre=_block_diag(jnp.swapaxes(p["s5_c_re"][l], 1, 2)).astype(BF16),
              cim=_block_diag(jnp.swapaxes(p["s5_c_im"][l], 1, 2)).astype(BF16),
              d=row(p["s5_d"]), gw=p["s5_glu_w"][l].astype(BF16), gb=row(p["s5_glu_b"]))
    pw = dict(gn=row(p["grp_norm"]), wo=p["w_out"][l].astype(BF16), npost=row(p["norm_mix_post"]),
              nfpre=row(p["norm_ffn_pre"]), wg=p["ffn_w_gate"][l].astype(BF16), wu=p["ffn_w_up"][l].astype(BF16),
              wd=p["ffn_w_down"][l].astype(BF16), nfpost=row(p["norm_ffn_post"]))
    return dict(win=_cat_weight(p["w_in"][l]), npre=row(p["norm_mix_pre"]), lru=lw, s5=sw, post=pw,
                fbias=p["fox_f_bias"][l])


def _largest_tile(n, cap):
    t = min(n, cap)
    while n % t:
        t //= 2
    return t


def _recurrent(ax, ag, bu, lru_buf, lru_h0, s5_r0, s5_i0, wts, t_real):
    B, Tp, _ = ax.shape
    tc = _largest_tile(Tp, 256)
    buf8 = jnp.pad(lru_buf, ((0, 0), (SUBLANES - (CONV_W - 1), 0), (0, 0)))
    ya, lru_h = _lru(ax, ag, buf8, lru_h0.reshape(B, 1, GROUP_W), wts["lru"], tc, t_real)
    yb, s5r, s5i = _s5(bu, s5_r0.reshape(B, 1, S5_STATE), s5_i0.reshape(B, 1, S5_STATE), wts["s5"], tc, t_real)
    return ya, yb, lru_h.reshape(B, GROUP_W), s5r.reshape(B, S5_NG, S5_N), s5i.reshape(B, S5_NG, S5_N)


def _prompt_layer(x, mod, wts, B, T):
    N, D = x.shape
    tm = _largest_tile(T, 512)
    (ax, ag, bu, cq, ckf, ckb, cvf, cvb, dq, dkf, dkb, dvf, dvb, iqp, ikp, sm) = _inproj(
        x, mod, T // tm, wts["npre"], *wts["win"], tm)
    seq = lambda a: a.reshape(B, T, a.shape[-1])
    zeros = lambda *s: jnp.zeros(s, F32)
    ax3 = seq(ax)
    ya, yb, lru_h, s5r, s5i = _recurrent(ax3, seq(ag), seq(bu), zeros(B, CONV_W - 1, GROUP_W), zeros(B, GROUP_W),
                                         zeros(B, S5_NG, S5_N), zeros(B, S5_NG, S5_N), wts, T)
    nb = T // LANES
    cf_rows = jnp.swapaxes(seq(sm)[:, :, SM_CF:SM_CF + ATT_HEADS], 1, 2).reshape(B, ATT_HEADS * nb, LANES)
    fb_rows = jnp.repeat(wts["fbias"], nb).reshape(ATT_HEADS * nb, 1)
    lf_rows, f_rows = _fcum(cf_rows, fb_rows, nb)
    f2_col = jnp.swapaxes(f_rows.reshape(B, ATT_HEADS, T), 1, 2)
    logf = jnp.swapaxes(lf_rows.reshape(B, ATT_HEADS, T), 1, 2)
    tr = lambda a: jnp.swapaxes(seq(a), 1, 2)
    tq = kb = _largest_tile(T, 512)
    yc = jnp.swapaxes(_fox(tr(cq), seq(ckb), _with_ones_rows(tr(cvb)), f2_col, tq, kb), 1, 2)
    topk = max(1, min(DSA_TOPK_MAX, T // 4))
    yd = jnp.swapaxes(_dsa(tr(iqp), seq(ikp), tr(sm), tr(dq), seq(dkb), _with_ones_rows(tr(dvb)), tq, kb, topk),
                      1, 2)
    flat = lambda a: a.reshape(N, GROUP_W)
    x2 = _post(x, flat(ya), flat(yb), flat(yc), flat(yd), mod, T // tm, wts["post"], tm)
    heads = lambda a: a.reshape(B, T, ATT_HEADS, ATT_HD)
    state = dict(lru_h=lru_h, lru_conv=ax3[:, T - (CONV_W - 1):, :], s5_re=s5r, s5_im=s5i,
                 fox_k=heads(ckf), fox_v=heads(cvf), fox_logf=logf, dsa_k=heads(dkf), dsa_v=heads(dvf),
                 dsa_kidx=seq(sm)[:, :, SM_IK:SM_IK + IDX_HD])
    return x2, state


def _sample_layer(x, mod, wts, l, B, T, past, caches, page_table):
    N, D = x.shape
    (ax, ag, bu, cq, ckf, ckb, cvf, cvb, dq, dkf, dkb, dvf, dvb, iqp, ikp, sm) = _inproj(
        x, mod, 1, wts["npre"], *wts["win"], N)
    seq = lambda a: a.reshape(B, T, a.shape[-1])
    tp = -(-T // SUBLANES) * SUBLANES
    padt = lambda a, n=tp: jnp.pad(seq(a), ((0, 0), (0, n - T), (0, 0)))
    ax3 = seq(ax)
    ya, yb, lru_h, s5r, s5i = _recurrent(padt(ax), padt(ag), padt(bu), past["lru_conv"], past["lru_h"],
                                         past["s5_re"], past["s5_im"], wts, T)
    sm3 = seq(sm)
    n_pages = page_table.shape[1]
    plen = n_pages * PAGE
    pps = _largest_tile(n_pages, 16)
    iq_rows = iqp.reshape(B, T * IDX_HEADS, LANES)
    wrows = sm3[:, :, SM_IW:SM_IW + IDX_HEADS].reshape(B, T * IDX_HEADS, 1)
    cfn = jnp.pad(jnp.swapaxes(sm3[:, :, SM_CF:SM_CF + ATT_HEADS], 1, 2), ((0, 0), (0, 0), (0, PAGE - T)))
    ikn = jnp.pad(jnp.swapaxes(sm3[:, :, SM_IK:SM_IK + IDX_HD], 1, 2), ((0, 0), (0, 0), (0, PAGE - T)))
    s_past, f_past, s_new, f_new, lf_new = _sa(page_table, l, iq_rows, wrows, cfn,
                                               wts["fbias"].reshape(ATT_HEADS, 1), ikn, caches["kidx"],
                                               caches["logfT"], _largest_tile(n_pages, 32), T)
    topk = max(1, min(DSA_TOPK_MAX, (plen + T) // 4))
    s_allT = jnp.transpose(jnp.concatenate([s_past, s_new], axis=2), (2, 0, 1)).reshape(plen + PAGE, B * T)
    bias_all = jnp.transpose(_sb(s_allT, topk, plen, T).reshape(plen + PAGE, B, T), (1, 2, 0))
    fqc = jnp.pad(f_new[:, :, :T], ((0, 0), (0, 0), (0, SUBLANES - T))).reshape(B, ATT_HEADS * SUBLANES, 1)
    newpage = lambda a: jnp.pad(jnp.transpose(a.reshape(B, T, ATT_HEADS, ATT_HD), (0, 2, 3, 1)),
                                ((0, 0), (0, 0), (0, 0), (0, PAGE - T)))
    yc8, yd8 = _sc(page_table, l, padt(cq, SUBLANES), padt(dq, SUBLANES), fqc, f_past, bias_all[:, :, :plen],
                   f_new, bias_all[:, :, plen:], newpage(ckb), newpage(cvb), newpage(dkb), newpage(dvb),
                   caches["fox_k"], caches["fox_v"], caches["dsa_k"], caches["dsa_v"], pps, T)
    flat = lambda a: a[:, :T, :].reshape(N, GROUP_W)
    x2 = _post(x, flat(ya), flat(yb), flat(yc8), flat(yd8), mod, 1, wts["post"], N)
    heads = lambda a: a.reshape(B, T, ATT_HEADS, ATT_HD)
    state = dict(lru_h=lru_h, lru_conv=ax3[:, T - (CONV_W - 1):, :], s5_re=s5r, s5_im=s5i,
                 fox_k=heads(ckf), fox_v=heads(cvf), fox_logf=jnp.swapaxes(lf_new[:, :, :T], 1, 2),
                 dsa_k=heads(dkf), dsa_v=heads(dvf), dsa_kidx=sm3[:, :, SM_IK:SM_IK + IDX_HD])
    return x2, state


def kernel(x_prompt, x_sample, state_lru_h, state_lru_conv, state_s5_re, state_s5_im, cache_fox_k, cache_fox_v, cache_fox_logf, cache_dsa_k, cache_dsa_v, cache_dsa_kidx, page_table, c_prompt, c_sample, ada_w, ada_b, norm_mix_pre, norm_mix_post, norm_ffn_pre, norm_ffn_post, w_in, lru_conv_w, lru_conv_b, lru_wr, lru_br, lru_wi, lru_bi, lru_lambda, s5_log_dt, s5_a_re, s5_a_im, s5_b_re, s5_b_im, s5_c_re, s5_c_im, s5_d, s5_glu_w, s5_glu_b, fox_f_bias, grp_norm, w_out, ffn_w_gate, ffn_w_up, ffn_w_down):
    p = dict(norm_mix_pre=norm_mix_pre, norm_mix_post=norm_mix_post, norm_ffn_pre=norm_ffn_pre,
             norm_ffn_post=norm_ffn_post, w_in=w_in, lru_conv_w=lru_conv_w, lru_conv_b=lru_conv_b, lru_wr=lru_wr,
             lru_br=lru_br, lru_wi=lru_wi, lru_bi=lru_bi, lru_lambda=lru_lambda, s5_log_dt=s5_log_dt,
             s5_a_re=s5_a_re, s5_a_im=s5_a_im, s5_b_re=s5_b_re, s5_b_im=s5_b_im, s5_c_re=s5_c_re, s5_c_im=s5_c_im,
             s5_d=s5_d, s5_glu_w=s5_glu_w, s5_glu_b=s5_glu_b, fox_f_bias=fox_f_bias, grp_norm=grp_norm,
             w_out=w_out, ffn_w_gate=ffn_w_gate, ffn_w_up=ffn_w_up, ffn_w_down=ffn_w_down)
    B, T, D = x_prompt.shape
    Bs, Ts, _ = x_sample.shape
    L = ada_w.shape[0]
    n_pool = cache_fox_k.shape[1]
    assert T % LANES == 0 and CONV_W - 1 <= Ts <= SUBLANES and cache_fox_k.shape[2] == PAGE

    c_all = jnp.concatenate([c_prompt, c_sample], axis=0)
    c_all = jnp.pad(c_all, ((0, -c_all.shape[0] % SUBLANES), (0, 0)))
    mod_all = _ada(c_all, ada_w, ada_b)

    pool = lambda a: jnp.transpose(a, (0, 1, 3, 4, 2))
    caches = dict(fox_k=pool(cache_fox_k), fox_v=pool(cache_fox_v), dsa_k=pool(cache_dsa_k), dsa_v=pool(cache_dsa_v),
                  kidx=jnp.swapaxes(cache_dsa_kidx, 2, 3), logfT=jnp.swapaxes(cache_fox_logf, 2, 3))

    xp = x_prompt.reshape(B * T, D)
    xs = x_sample.reshape(Bs * Ts, D)
    st_p, st_s = [], []
    for l in range(L):
        wts = _layer_weights(l, p)
        mod_p = mod_all[l, :B].reshape(B, 1, 6 * D)
        mod_s = jnp.repeat(mod_all[l, B:B + Bs], Ts, axis=0).reshape(1, Bs * Ts, 6 * D)
        xp, new_p = _prompt_layer(xp, mod_p, wts, B, T)
        past = dict(lru_h=state_lru_h[l], lru_conv=state_lru_conv[l], s5_re=state_s5_re[l], s5_im=state_s5_im[l])
        xs, new_s = _sample_layer(xs, mod_s, wts, l, Bs, Ts, past, caches, page_table)
        st_p.append(new_p)
        st_s.append(new_s)

    stk = lambda outs, name: jnp.stack([o[name] for o in outs])
    names = ("lru_h", "lru_conv", "s5_re", "s5_im", "fox_k", "fox_v", "fox_logf", "dsa_k", "dsa_v", "dsa_kidx")
    res = [xp.reshape(B, T, D), xs.reshape(Bs, Ts, D)]
    for name in names:
        res += [stk(st_p, name), stk(st_s, name)]
    return tuple(res)
```

```python
import functools
import math

import jax
import jax.numpy as jnp
from jax import lax
from jax.experimental import pallas as pl
from jax.experimental.pallas import tpu as pltpu

F32 = jnp.float32
BF16 = jnp.bfloat16

N_MIXERS = 4
GROUP_W = 256
LRU_HEADS = 4
CONV_W = 4
LRU_C = 8.0
S5_GROUP = 16
S5_NG = 16
S5_N = 64
S5_STATE = S5_NG * S5_N
ATT_HEADS = 4
ATT_HD = 64
IDX_HEADS = 8
IDX_HD = 32
DSA_TOPK_MAX = 256
PAGE = 128
NEG_INF = -1e30
M_INIT = -1e20
BIG = 3e38
ADM_CUT = -1e29
EPS = 1e-6
ATT_SCALE = ATT_HD ** -0.5
IDX_SCALE = IDX_HD ** -0.5
IDX_W_SCALE = IDX_HEADS ** -0.5
ALIBI = tuple(2.0 ** (-8.0 * (h + 1) / ATT_HEADS) for h in range(ATT_HEADS))
LOG2E = math.log2(math.e)
Q_SCALE = ATT_SCALE * LOG2E
BISECT_ITERS = 28
SEARCH_ITERS = 48

LANES = 128
SUBLANES = 8
VMEM_LIMIT = 56 * 1024 * 1024

C_AX, C_AG, C_BU, C_CQ, C_CK, C_CV, C_DQ, C_DK, C_DV = (i * GROUP_W for i in range(9))
P_CAT = 9 * GROUP_W
C_IQ = 0
C_IK = C_IQ + IDX_HEADS * LANES
C_SM = C_IK + LANES
P_IDX = C_SM + LANES
SM_IK, SM_CF, SM_IW = 0, IDX_HD, IDX_HD + ATT_HEADS

NT_DIMS = (((1,), (1,)), ((), ()))


def _nt(a, b):
    return lax.dot_general(a, b, NT_DIMS, preferred_element_type=F32)


def _dot(a, b):
    return jnp.dot(a, b, preferred_element_type=F32)


def _rms(x, g):
    return x * lax.rsqrt(jnp.mean(x * x, axis=-1, keepdims=True) + EPS) * g


def _softplus(z):
    return jnp.maximum(z, 0.0) + jnp.log1p(jnp.exp(-jnp.abs(z)))


def _log_sigmoid(z):
    return -_softplus(-z)


def _expm1(z):
    e = jnp.exp(z)
    one = e == 1.0
    return jnp.where(one, z, (e - 1.0) * z / jnp.where(one, 1.0, jnp.log(e)))


def _sigmoid(z):
    return jax.nn.sigmoid(z)


def _gelu(z):
    return jax.nn.gelu(z)


def _params(sem):
    return pltpu.CompilerParams(dimension_semantics=sem, vmem_limit_bytes=VMEM_LIMIT)


def _head_mask(h, width=GROUP_W):
    lane = lax.broadcasted_iota(jnp.int32, (1, width), 1)
    return (lane >= h * ATT_HD) & (lane < (h + 1) * ATT_HD)


FOLD_CHAINS = 4


def _fold_rows(x, axis_op):
    n, c = x.shape
    g = FOLD_CHAINS if n % (FOLD_CHAINS * SUBLANES) == 0 else 1
    x = axis_op(x.reshape(g, n // (g * SUBLANES), SUBLANES, c), axis=1)
    return axis_op(x, axis=0)


def _col_max(x):
    return jnp.max(_fold_rows(x, jnp.max), axis=0, keepdims=True)


def _lane_cumsum(x):
    lane = lax.broadcasted_iota(jnp.int32, x.shape, 1)
    s = 1
    while s < x.shape[1]:
        x = x + jnp.where(lane >= s, pltpu.roll(x, s, axis=1), 0.0)
        s *= 2
    return x


def _split(x):
    hi = x.astype(BF16)
    return hi, (x - hi.astype(F32)).astype(BF16)


def _ada_kernel(c_ref, w_ref, b_ref, o_ref):
    c = c_ref[...]
    sh, sl = _split(c * _sigmoid(c))
    wh, wl = _split(w_ref[...])
    o_ref[...] = (_dot(sl, wl) + _dot(sl, wh) + _dot(sh, wl)) + _dot(sh, wh) + b_ref[...]


def _ada(c_all, ada_w, ada_b):
    L, D, D6 = ada_w.shape
    Bp = c_all.shape[0]
    tn = 1024
    return pl.pallas_call(
        _ada_kernel,
        grid=(L, D6 // tn),
        in_specs=[pl.BlockSpec((Bp, D), lambda l, j: (0, 0)),
                  pl.BlockSpec((None, D, tn), lambda l, j: (l, 0, j)),
                  pl.BlockSpec((None, 1, tn), lambda l, j: (l, 0, j))],
        out_specs=pl.BlockSpec((None, Bp, tn), lambda l, j: (l, 0, j)),
        out_shape=jax.ShapeDtypeStruct((L, Bp, D6), F32),
        compiler_params=_params(("arbitrary", "arbitrary")),
        name="ada",
    )(c_all, ada_w, ada_b.reshape(L, 1, D6))


def _inproj_kernel(x_ref, sh_ref, sc_ref, g_ref, w_ref, w3_ref,
                   ax_ref, ag_ref, bu_ref, cq_ref, ckf_ref, ckb_ref, cvf_ref, cvb_ref,
                   dq_ref, dkf_ref, dkb_ref, dvf_ref, dvb_ref, iq_ref, ik_ref, sm_ref):
    x = x_ref[...]
    h = _rms(x, g_ref[...]) * (1.0 + sc_ref[...]) + sh_ref[...]
    hb = h.astype(BF16)

    main = _dot(hb, w_ref[...])

    def mm(c0, width):
        return main[:, c0:c0 + width]

    ax_ref[...] = mm(C_AX, GROUP_W)
    ag_ref[...] = mm(C_AG, GROUP_W)
    bu_ref[...] = mm(C_BU, GROUP_W)
    cq_ref[...] = (mm(C_CQ, GROUP_W) * Q_SCALE).astype(BF16)
    ck = mm(C_CK, GROUP_W)
    ckf_ref[...] = ck
    ckb_ref[...] = ck.astype(BF16)
    cv = mm(C_CV, GROUP_W)
    cvf_ref[...] = cv
    cvb_ref[...] = cv.astype(BF16)
    dq_ref[...] = (mm(C_DQ, GROUP_W) * Q_SCALE).astype(BF16)
    dk = mm(C_DK, GROUP_W)
    dkf_ref[...] = dk
    dkb_ref[...] = dk.astype(BF16)
    dv = mm(C_DV, GROUP_W)
    dvf_ref[...] = dv
    dvb_ref[...] = dv.astype(BF16)

    hl = (h - hb.astype(F32)).astype(BF16)
    idx = _dot(jnp.concatenate([hb, hl, hb], axis=1), w3_ref[...])

    def mm3(c0, width):
        return idx[:, c0:c0 + width]

    sm_ref[...] = mm3(C_SM, LANES)

    sub = (lax.broadcasted_iota(jnp.int32, (1, LANES), 1) // IDX_HD)
    for hh in range(IDX_HEADS):
        r = mm3(C_IQ + hh * LANES, LANES)
        hi = r.astype(BF16)
        lo = (r - hi.astype(F32)).astype(BF16)
        zero = jnp.zeros_like(hi)
        iq_ref[:, hh * LANES:(hh + 1) * LANES] = jnp.where(sub == 1, lo, jnp.where(sub == 3, zero, hi))
    r = mm3(C_IK, LANES)
    hi = r.astype(BF16)
    lo = (r - hi.astype(F32)).astype(BF16)
    ik_ref[...] = jnp.where(sub == 2, lo, jnp.where(sub == 3, jnp.zeros_like(hi), hi))


def _mod_spec(rows_mod, D, tiles_per_mod, piece):
    return pl.BlockSpec((None, rows_mod, D), lambda i: (i // tiles_per_mod, 0, piece))


def _inproj(x, mod, tiles_per_mod, g, wcat, widx3, tm):
    N, D = x.shape
    rows_mod = mod.shape[1]
    tok = lambda w: pl.BlockSpec((tm, w), lambda i: (i, 0))
    widths_dtypes = [(GROUP_W, F32)] * 3 + [(GROUP_W, BF16), (GROUP_W, F32), (GROUP_W, BF16), (GROUP_W, F32),
                                            (GROUP_W, BF16), (GROUP_W, BF16), (GROUP_W, F32), (GROUP_W, BF16),
                                            (GROUP_W, F32), (GROUP_W, BF16), (IDX_HEADS * LANES, BF16),
                                            (LANES, BF16), (LANES, F32)]
    return pl.pallas_call(
        _inproj_kernel,
        grid=(N // tm,),
        in_specs=[tok(D), _mod_spec(rows_mod, D, tiles_per_mod, 0), _mod_spec(rows_mod, D, tiles_per_mod, 1),
                  pl.BlockSpec((1, D), lambda i: (0, 0)),
                  pl.BlockSpec((D, P_CAT), lambda i: (0, 0), pipeline_mode=pl.Buffered(1)),
                  pl.BlockSpec((3 * D, P_IDX), lambda i: (0, 0), pipeline_mode=pl.Buffered(1))],
        out_specs=[tok(w) for w, _ in widths_dtypes],
        out_shape=[jax.ShapeDtypeStruct((N, w), dt) for w, dt in widths_dtypes],
        compiler_params=_params(("arbitrary",)),
        name="inproj",
    )(x, mod, mod, g, wcat, widx3)


def _lru_kernel(ax_ref, ag_ref, buf_ref, h0_ref, cw_ref, cb_ref, wr_ref, br_ref, wi_ref, bi_ref, lam_ref,
                ya_ref, hl_ref, xs, hcar, *, tc, last_row):
    c = pl.program_id(1)

    @pl.when(c == 0)
    def _():
        xs[0:SUBLANES, :] = buf_ref[...]
        hcar[...] = h0_ref[...]

    x = ax_ref[...]
    xs[SUBLANES:SUBLANES + tc, :] = x
    w = cw_ref[...]
    y = cb_ref[...] + xs[5:5 + tc, :] * w[0:1] + xs[6:6 + tc, :] * w[1:2] + xs[7:7 + tc, :] * w[2:3] + x * w[3:4]
    xs[0:SUBLANES, :] = xs[tc:tc + SUBLANES, :]

    yb = y.astype(BF16)
    r = _sigmoid(_dot(yb, wr_ref[...]) + br_ref[...])
    i = _sigmoid(_dot(yb, wi_ref[...]) + bi_ref[...])
    log_a = -LRU_C * r * _softplus(-lam_ref[...])
    a = jnp.exp(log_a)
    b = jnp.sqrt(-_expm1(2.0 * log_a)) * (i * y)

    row = lax.broadcasted_iota(jnp.int32, (tc, GROUP_W), 0)
    s = 1
    while s < tc:
        keep = row >= s
        a_sh = jnp.where(keep, pltpu.roll(a, s, axis=0), 1.0)
        b_sh = jnp.where(keep, pltpu.roll(b, s, axis=0), 0.0)
        b = b + a * b_sh
        a = a * a_sh
        s *= 2
    h = b + a * hcar[...]
    hcar[...] = h[tc - 1:tc, :]
    ya_ref[...] = h * _gelu(ag_ref[...])

    @pl.when(c == pl.num_programs(1) - 1)
    def _():
        hl_ref[...] = h[last_row:last_row + 1, :]


def _lru(ax, ag, buf8, h0, lw, tc, t_real):
    B, Tp, W = ax.shape
    seq = pl.BlockSpec((None, tc, W), lambda b, c: (b, c, 0))
    full = lambda shape: pl.BlockSpec(shape, lambda b, c: (0,) * len(shape))
    return pl.pallas_call(
        functools.partial(_lru_kernel, tc=tc, last_row=(t_real - 1) % tc),
        grid=(B, Tp // tc),
        in_specs=[seq, seq,
                  pl.BlockSpec((None, SUBLANES, W), lambda b, c: (b, 0, 0)),
                  pl.BlockSpec((None, 1, W), lambda b, c: (b, 0, 0)),
                  full((CONV_W, W)), full((1, W)), full((W, W)), full((1, W)), full((W, W)), full((1, W)),
                  full((1, W))],
        out_specs=[seq, pl.BlockSpec((None, 1, W), lambda b, c: (b, 0, 0))],
        out_shape=[jax.ShapeDtypeStruct((B, Tp, W), F32), jax.ShapeDtypeStruct((B, 1, W), F32)],
        scratch_shapes=[pltpu.VMEM((tc + SUBLANES, W), F32), pltpu.VMEM((1, W), F32)],
        compiler_params=_params(("arbitrary", "arbitrary")),
        name="lru",
    )(ax, ag, buf8, h0, lw["conv_w"], lw["conv_b"], lw["wr"], lw["br"], lw["wi"], lw["bi"], lw["lam"])


def _s5_kernel(u_ref, h0r_ref, h0i_ref, ldt_ref, are_ref, aim_ref, bre_ref, bim_ref, cre_ref, cim_ref,
               d_ref, gw_ref, gb_ref, y_ref, hlr_ref, hli_ref, hr_s, hi_s, car_r, car_i, *, tc, last_row):
    c = pl.program_id(1)

    @pl.when(c == 0)
    def _():
        car_r[...] = h0r_ref[...]
        car_i[...] = h0i_ref[...]

    dt = jnp.exp(ldt_ref[...])
    ar, ai = are_ref[...], aim_ref[...]
    mag = jnp.exp(dt * ar)
    abr, abi = mag * jnp.cos(dt * ai), mag * jnp.sin(dt * ai)
    den = ar * ar + ai * ai
    nr, ni = abr - 1.0, abi
    cr = (nr * ar + ni * ai) / den
    ci = (ni * ar - nr * ai) / den

    u = u_ref[...]
    ub = u.astype(BF16)
    pre = _dot(ub, bre_ref[...])
    pim = _dot(ub, bim_ref[...])
    hr_s[...] = cr * pre - ci * pim
    hi_s[...] = cr * pim + ci * pre

    def step(t, carry):
        hr, hi = carry
        nhr = abr * hr - abi * hi + hr_s[pl.ds(t, 1), :]
        nhi = abr * hi + abi * hr + hi_s[pl.ds(t, 1), :]
        hr_s[pl.ds(t, 1), :] = nhr
        hi_s[pl.ds(t, 1), :] = nhi
        return nhr, nhi

    hr, hi = lax.fori_loop(0, tc, step, (car_r[...], car_i[...]), unroll=8)
    car_r[...] = hr
    car_i[...] = hi

    hrv, hiv = hr_s[...], hi_s[...]
    y = _dot(hrv.astype(BF16), cre_ref[...]) - _dot(hiv.astype(BF16), cim_ref[...]) + d_ref[...] * u
    g = _gelu(y)
    y_ref[...] = g * _sigmoid(_dot(g.astype(BF16), gw_ref[...]) + gb_ref[...])

    @pl.when(c == pl.num_programs(1) - 1)
    def _():
        hlr_ref[...] = hr_s[last_row:last_row + 1, :]
        hli_ref[...] = hi_s[last_row:last_row + 1, :]


def _s5(u, h0r, h0i, sw, tc, t_real):
    B, Tp, W = u.shape
    S = S5_STATE
    seq = pl.BlockSpec((None, tc, W), lambda b, c: (b, c, 0))
    st = pl.BlockSpec((None, 1, S), lambda b, c: (b, 0, 0))
    full = lambda shape: pl.BlockSpec(shape, lambda b, c: (0,) * len(shape))
    return pl.pallas_call(
        functools.partial(_s5_kernel, tc=tc, last_row=(t_real - 1) % tc),
        grid=(B, Tp // tc),
        in_specs=[seq, st, st, full((1, S)), full((1, S)), full((1, S)), full((W, S)), full((W, S)),
                  full((S, W)), full((S, W)), full((1, W)), full((W, W)), full((1, W))],
        out_specs=[seq, st, st],
        out_shape=[jax.ShapeDtypeStruct((B, Tp, W), F32), jax.ShapeDtypeStruct((B, 1, S), F32),
                   jax.ShapeDtypeStruct((B, 1, S), F32)],
        scratch_shapes=[pltpu.VMEM((tc, S), F32), pltpu.VMEM((tc, S), F32),
                        pltpu.VMEM((1, S), F32), pltpu.VMEM((1, S), F32)],
        compiler_params=_params(("arbitrary", "arbitrary")),
        name="s5",
    )(u, h0r, h0i, sw["ldt"], sw["are"], sw["aim"], sw["bre"], sw["bim"], sw["cre"], sw["cim"],
      sw["d"], sw["gw"], sw["gb"])


def _fcum_kernel(cf_ref, fb_ref, lf_ref, f_ref, *, nb):
    lf = _log_sigmoid(cf_ref[...] + fb_ref[...])
    lf_ref[...] = lf
    cs = _lane_cumsum(lf)
    tot = jnp.broadcast_to(cs[:, LANES - 1:LANES], cs.shape)
    row = lax.broadcasted_iota(jnp.int32, cs.shape, 0) % nb
    inc = tot
    s = 1
    while s < nb:
        inc = inc + jnp.where(row >= s, pltpu.roll(inc, s, axis=0), 0.0)
        s *= 2
    f_ref[...] = (cs + (inc - tot)) * LOG2E


def _fcum(cf_rows, fb_rows, nb):
    B, R, _ = cf_rows.shape
    blk = pl.BlockSpec((None, R, LANES), lambda b: (b, 0, 0))
    return pl.pallas_call(
        functools.partial(_fcum_kernel, nb=nb),
        grid=(B,),
        in_specs=[blk, pl.BlockSpec((R, 1), lambda b: (0, 0))],
        out_specs=[blk, blk],
        out_shape=[jax.ShapeDtypeStruct((B, R, LANES), F32)] * 2,
        compiler_params=_params(("arbitrary",)),
        name="fcum",
    )(cf_rows, fb_rows)


ONES_ROWS = 16
VROWS = ATT_HD + ONES_ROWS


def _init_heads(qT_ref, qm_s, acc_s, m_s):
    qT = qT_ref[...]
    row = lax.broadcasted_iota(jnp.int32, (qT.shape[0], 1), 0)
    for h in range(ATT_HEADS):
        qm_s[h] = jnp.where(row // ATT_HD == h, qT, jnp.zeros_like(qT))
    acc_s[...] = jnp.zeros_like(acc_s)
    m_s[...] = jnp.full_like(m_s, M_INIT)


def _attend_block(kblk, vT_ref, k0, kb, qm_s, acc_s, m_s, s_s, p_s, bias_fn):
    for h in range(ATT_HEADS):
        s_s[h] = _dot(kblk, qm_s[h])
    m_new = []
    for h in range(ATT_HEADS):
        s = bias_fn(h, s_s[h])
        s_s[h] = s
        m_new.append(jnp.maximum(m_s[h], _col_max(s)))
    for h in range(ATT_HEADS):
        p_s[h] = jnp.exp2(s_s[h] - m_new[h]).astype(BF16)
    for h in range(ATT_HEADS):
        rows = slice(h * VROWS, (h + 1) * VROWS)
        pv = _dot(vT_ref[rows, pl.ds(k0, kb)], p_s[h])
        acc_s[rows, :] = jnp.exp2(m_s[h] - m_new[h]) * acc_s[rows, :] + pv
        m_s[h] = m_new[h]


def _finish_heads(o_ref, acc_s):
    for h in range(ATT_HEADS):
        base = h * VROWS
        o_ref[h * ATT_HD:(h + 1) * ATT_HD, :] = (acc_s[base:base + ATT_HD, :]
                                                 / acc_s[base + ATT_HD:base + ATT_HD + 1, :])


def _fox_kernel(qT_ref, k_ref, vT_ref, f2_ref, o_ref, qm_s, acc_s, m_s, s_s, p_s, *, tq, kb):
    q0 = pl.program_id(1) * tq
    _init_heads(qT_ref, qm_s, acc_s, m_s)
    kio = lax.broadcasted_iota(jnp.int32, (kb, tq), 0)
    qpos = q0 + lax.broadcasted_iota(jnp.int32, (kb, tq), 1)

    def block(j, masked):
        k0 = pl.multiple_of(j * kb, kb)
        fk = f2_ref[pl.ds(k0, kb), :]

        def bias(h, s):
            s = s - fk[:, h:h + 1]
            return jnp.where(k0 + kio <= qpos, s, NEG_INF) if masked else s

        _attend_block(k_ref[pl.ds(k0, kb), :], vT_ref, k0, kb, qm_s, acc_s, m_s, s_s, p_s, bias)
        return 0

    n_full = (q0 + 1) // kb
    n_blk = (q0 + tq - 1) // kb + 1
    lax.fori_loop(0, n_full, lambda j, c: block(j, False), 0)
    lax.fori_loop(n_full, n_blk, lambda j, c: block(j, True), 0)
    _finish_heads(o_ref, acc_s)


def _att_scratch(tq, kb):
    return [pltpu.VMEM((ATT_HEADS, GROUP_W, tq), BF16), pltpu.VMEM((ATT_HEADS * VROWS, tq), F32),
            pltpu.VMEM((ATT_HEADS, 1, tq), F32),
            pltpu.VMEM((ATT_HEADS, kb, tq), F32), pltpu.VMEM((ATT_HEADS, kb, tq), BF16)]


def _with_ones_rows(vT):
    B, _, T = vT.shape
    v4 = vT.reshape(B, ATT_HEADS, ATT_HD, T)
    return jnp.concatenate([v4, jnp.ones((B, ATT_HEADS, ONES_ROWS, T), vT.dtype)], axis=2).reshape(
        B, ATT_HEADS * VROWS, T)


def _resident(shape):
    return pl.BlockSpec((None,) + shape, lambda b, i: (b,) + (0,) * len(shape), pipeline_mode=pl.Buffered(1))


def _fox(qT, k, vT, f2col, tq, kb):
    B, W, T = qT.shape
    colblk = pl.BlockSpec((None, W, tq), lambda b, i: (b, 0, i))
    return pl.pallas_call(
        functools.partial(_fox_kernel, tq=tq, kb=kb),
        grid=(B, T // tq),
        in_specs=[colblk, _resident((T, W)), _resident((ATT_HEADS * VROWS, T)), _resident((T, ATT_HEADS))],
        out_specs=colblk,
        out_shape=jax.ShapeDtypeStruct((B, W, T), F32),
        scratch_shapes=_att_scratch(tq, kb),
        compiler_params=_params(("arbitrary", "arbitrary")),
        name="fox",
    )(qT, k, vT, f2col)


def _score_stats_init(C):
    big8, zero8 = jnp.full((SUBLANES, C), BIG, F32), jnp.zeros((SUBLANES, C), F32)
    return -big8, big8, big8, zero8, zero8


def _score_stats(c, x):
    pos = x > 0.0
    return (jnp.maximum(c[0], _fold_rows(x, jnp.max)),
            jnp.minimum(c[1], _fold_rows(jnp.where(x > ADM_CUT, x, BIG), jnp.min)),
            jnp.minimum(c[2], _fold_rows(jnp.where(pos, x, BIG), jnp.min)),
            c[3] + _fold_rows(jnp.where(pos, 1.0, 0.0), jnp.sum),
            c[4] + _fold_rows(jnp.where(x == 0.0, 1.0, 0.0), jnp.sum))


def _topk_to_bias(S, nblk, kb, C, k, nadm, stats=None):
    kf = float(k)

    sb = LANES if kb % LANES == 0 else kb

    def reduce_blocks(fn, init):
        def body(j, c):
            k0 = pl.multiple_of(j * kb, kb)
            for r in range(0, kb, sb):
                c = fn(c, S[pl.ds(k0 + r, sb), :], k0 + r)
            return c
        return lax.fori_loop(0, nblk, body, init)

    fold_sum = lambda x: _fold_rows(x, jnp.sum)
    fold_max = lambda x: _fold_rows(x, jnp.max)
    fold_min = lambda x: _fold_rows(x, jnp.min)

    def count(pred):
        c = reduce_blocks(lambda c, x, k0: c + fold_sum(jnp.where(pred(x, k0), 1.0, 0.0)),
                          jnp.zeros((SUBLANES, C), F32))
        return jnp.sum(c, axis=0, keepdims=True)

    if stats is None:
        stats = reduce_blocks(lambda c, x, k0: _score_stats(c, x), _score_stats_init(C))
    mx, mn, mp, np8, nz8 = stats
    cmax = jnp.max(mx, axis=0, keepdims=True)
    cmin = jnp.min(mn, axis=0, keepdims=True)
    small = nadm <= kf
    cpos = jnp.sum(np8, axis=0, keepdims=True)
    cnn = cpos + jnp.sum(nz8, axis=0, keepdims=True)
    at_zero = (cpos < kf) & (cnn >= kf) & jnp.logical_not(small)
    settled = small | at_zero
    above = cpos >= kf
    lo0 = jnp.where(above, jnp.min(mp, axis=0, keepdims=True), cmin)
    clo0 = jnp.where(above, cpos, nadm)
    hi0 = jnp.where(above, cmax + jnp.maximum(1e-30, 1e-6 * jnp.abs(cmax)), 0.0)
    chi0 = jnp.where(above, 0.0, cnn)

    def any_col(flag):
        return jnp.max(jnp.where(flag, 1.0, 0.0)) > 0.5

    def probe(c, frac):
        lo, hi, clo, chi = c
        mid = lo + (hi - lo) * frac
        cm = count(lambda x, k0: x >= mid)
        ge = cm >= kf
        return jnp.where(ge, mid, lo), jnp.where(ge, hi, mid), jnp.where(ge, cm, clo), jnp.where(ge, chi, cm)

    def unresolved(clo, chi):
        g, r = clo - chi, kf - chi
        return (g > 2.5) & (r > 1.5) & (g - r > 0.5) & jnp.logical_not(settled)

    _, lo, hi, clo, chi = lax.while_loop(
        lambda c: (c[0] < BISECT_ITERS) & any_col(unresolved(c[3], c[4])),
        lambda c: (c[0] + 1,) + probe(c[1:], 0.5),
        (jnp.int32(0), lo0, hi0, clo0, chi0))

    top, bot = reduce_blocks(
        lambda c, x, k0: (jnp.maximum(c[0], fold_max(jnp.where(x < hi, x, -BIG))),
                          jnp.minimum(c[1], fold_min(jnp.where(x >= lo, x, BIG)))),
        (jnp.full((SUBLANES, C), -BIG, F32), jnp.full((SUBLANES, C), BIG, F32)))
    thr = jnp.where(kf - chi < 1.5, jnp.max(top, axis=0, keepdims=True), jnp.min(bot, axis=0, keepdims=True))
    thr = jnp.where(small, cmin, jnp.where(at_zero, 0.0, thr))
    cge = count(lambda x, k0: x >= thr)
    cgt = count(lambda x, k0: x > thr)
    missed = ((cgt >= kf) | (cge < kf)) & jnp.logical_not(settled)

    def slow(_):
        def open_(clo, chi):
            return (clo - chi > 1.5) & jnp.logical_not(settled)

        def search_body(c):
            it, lo, hi, clo, chi = c
            frac = jnp.clip((clo - kf + 0.5) / jnp.maximum(clo - chi, 1.0), 1.0 / 64, 63.0 / 64)
            return (it + 1,) + probe((lo, hi, clo, chi), jnp.where(it % 2 == 0, frac, 0.5))

        _, _, hi2, _, _ = lax.while_loop(lambda c: (c[0] < SEARCH_ITERS) & any_col(open_(c[3], c[4])), search_body,
                                         (jnp.int32(0), lo, hi, clo, chi))

        def unsat(cnt):
            return (cnt < kf) & jnp.logical_not(settled)

        def fix_body(c):
            t, cnt = c
            below = reduce_blocks(lambda m, x, k0: jnp.maximum(m, fold_max(jnp.where(x < t, x, -BIG))),
                                  jnp.full((SUBLANES, C), -BIG, F32))
            nt = jnp.where(unsat(cnt), jnp.max(below, axis=0, keepdims=True), t)
            return nt, count(lambda x, k0: x >= nt)

        t2, cge2 = lax.while_loop(lambda c: any_col(unsat(c[1])), fix_body, (hi2, jnp.zeros((1, C), F32)))
        cgt2 = count(lambda x, k0: x > t2)
        return jnp.where(missed, t2, thr), jnp.where(missed, cge2, cge), jnp.where(missed, cgt2, cgt)

    thr, cge, cgt = lax.cond(any_col(missed), slow, lambda _: (thr, cge, cgt), 0)
    need = kf - cgt
    ties = (cge > kf) & jnp.logical_not(small)

    def write_plain(_):
        def body(j, _):
            k0 = pl.multiple_of(j * kb, kb)
            S[pl.ds(k0, kb), :] = jnp.where(S[pl.ds(k0, kb), :] >= thr, 0.0, NEG_INF)
            return 0
        return lax.fori_loop(0, nblk, body, 0)

    def write_ties(_):
        tri = jnp.where(lax.broadcasted_iota(jnp.int32, (kb, kb), 0) >= lax.broadcasted_iota(jnp.int32, (kb, kb), 1),
                        1.0, 0.0).astype(BF16)

        def body(j, seen):
            k0 = pl.multiple_of(j * kb, kb)
            x = S[pl.ds(k0, kb), :]
            eq = x == thr
            rank = seen + _dot(tri, jnp.where(eq, 1.0, 0.0).astype(BF16))
            S[pl.ds(k0, kb), :] = jnp.where((x > thr) | (eq & (rank <= need)), 0.0, NEG_INF)
            return rank[kb - 1:kb, :]
        lax.fori_loop(0, nblk, body, jnp.zeros((1, C), F32))
        return 0

    lax.cond(any_col(ties), write_ties, write_plain, 0)


def _dsa_kernel(iqT_ref, ik_ref, smT_ref, qT_ref, k_ref, vT_ref, o_ref, S, qm_s, acc_s, m_s, s_s, p_s,
                *, tq, kb, topk):
    q0 = pl.program_id(1) * tq
    n_blk = (q0 + tq - 1) // kb + 1
    kio = lax.broadcasted_iota(jnp.int32, (kb, tq), 0)
    qpos = q0 + lax.broadcasted_iota(jnp.int32, (kb, tq), 1)
    wc = smT_ref[SM_IW:SM_IW + IDX_HEADS, :] * (IDX_W_SCALE * IDX_SCALE)

    def scores(j, stats):
        k0 = pl.multiple_of(j * kb, kb)
        kblk = ik_ref[pl.ds(k0, kb), :]
        sc = jnp.zeros((kb, tq), F32)
        for hh in range(IDX_HEADS):
            d = _dot(kblk, iqT_ref[hh * LANES:(hh + 1) * LANES, :])
            sc = sc + wc[hh:hh + 1, :] * jnp.maximum(d, 0.0)
        sc = jnp.where(k0 + kio <= qpos, sc, NEG_INF)
        S[pl.ds(k0, kb), :] = sc
        return _score_stats(stats, sc)

    stats = lax.fori_loop(0, n_blk, scores, _score_stats_init(tq))

    nadm = (q0 + 1 + lax.broadcasted_iota(jnp.int32, (1, tq), 1)).astype(F32)
    _topk_to_bias(S, n_blk, kb, tq, topk, nadm, stats)

    _init_heads(qT_ref, qm_s, acc_s, m_s)
    kcol =lax.broadcasted_iota(jnp.int32, (kb, 1), 0)

    def attend(j, _):
        k0 = pl.multiple_of(j * kb, kb)
        sel = S[pl.ds(k0, kb), :]
        kpos = (k0 + kcol).astype(F32)

        def bias(h, s):
            return s + (sel + (ALIBI[h] * LOG2E) * kpos)

        _attend_block(k_ref[pl.ds(k0, kb), :], vT_ref, k0, kb, qm_s, acc_s, m_s, s_s, p_s, bias)
        return 0

    lax.fori_loop(0, n_blk, attend, 0)
    _finish_heads(o_ref, acc_s)


def _dsa(iqT, ikp, smT, qT, k, vT, tq, kb, topk):
    B, W, T = qT.shape
    col = lambda r: pl.BlockSpec((None, r, tq), lambda b, i: (b, 0, i))
    return pl.pallas_call(
        functools.partial(_dsa_kernel, tq=tq, kb=kb, topk=topk),
        grid=(B, T // tq),
        in_specs=[col(IDX_HEADS * LANES), _resident((T, LANES)), col(LANES), col(W),
                  _resident((T, W)), _resident((ATT_HEADS * VROWS, T))],
        out_specs=col(W),
        out_shape=jax.ShapeDtypeStruct((B, W, T), F32),
        scratch_shapes=[pltpu.VMEM((T, tq), F32)] + _att_scratch(tq, kb),
        compiler_params=_params(("arbitrary", "arbitrary")),
        name="dsa",
    )(iqT, ikp, smT, qT, k, vT)


def _sa_kernel(pt_ref, q_ref, w_ref, cfn_ref, fb_ref, ikn_ref, *rest, pps, tn):
    ki_refs, lf_refs = rest[:pps], rest[pps:2 * pps]
    s_ref, f_ref, sn_ref, fn_ref, lfn_ref, carry_s = rest[2 * pps:]
    step = pl.program_id(1)

    @pl.when(step == 0)
    def _():
        carry_s[...] = jnp.zeros_like(carry_s)

    qp = q_ref[...]
    q3 = qp[:, 0:3 * IDX_HD]
    w = w_ref[...] * IDX_W_SCALE
    nq = qp.shape[0] // IDX_HEADS

    def scores(kT):
        kh, kl = _split(kT)
        d = _dot(q3, jnp.concatenate([kh, kh, kl], axis=0))
        r = jnp.maximum(d * IDX_SCALE, 0.0) * w
        return jnp.sum(r.reshape(nq, IDX_HEADS, kT.shape[1]), axis=1)

    s_ref[...] = scores(jnp.concatenate([r[...] for r in ki_refs], axis=1))
    cs_all = _lane_cumsum(jnp.concatenate([r[...] for r in lf_refs], axis=0))
    carry = carry_s[...]
    for i in range(pps):
        cs = cs_all[i * ATT_HEADS:(i + 1) * ATT_HEADS, :] + carry
        f_ref[:, i * PAGE:(i + 1) * PAGE] = cs
        carry = cs[:, PAGE - 1:PAGE]
    carry_s[...] = carry

    @pl.when(step == pl.num_programs(1) - 1)
    def _():
        lane = lax.broadcasted_iota(jnp.int32, (ATT_HEADS, PAGE), 1)
        lfn = _log_sigmoid(cfn_ref[...] + fb_ref[...])
        lfn_ref[...] = lfn
        fn_ref[...] = _lane_cumsum(jnp.where(lane < tn, lfn, 0.0)) + carry
        col = lax.broadcasted_iota(jnp.int32, (nq, PAGE), 1)
        rowq = lax.broadcasted_iota(jnp.int32, (nq, PAGE), 0)
        sn_ref[...] = jnp.where((col <= rowq) & (col < tn), scores(ikn_ref[...]), NEG_INF)


def _sa(page_table, l, qrows, wrows, cfn, fb, ikn, kidx_pool, lfT_pool, pps, tn):
    B, n_pages = page_table.shape
    nsteps = n_pages // pps
    P = n_pages * PAGE
    nq = qrows.shape[1] // IDX_HEADS
    per_b = lambda shape: pl.BlockSpec((None,) + shape, lambda b, s, pt: (b,) + (0,) * len(shape))
    ki_specs = [pl.BlockSpec((None, None, IDX_HD, PAGE), lambda b, s, pt, i=i: (l, pt[b, s * pps + i], 0, 0))
                for i in range(pps)]
    lf_specs = [pl.BlockSpec((None, None, ATT_HEADS, PAGE), lambda b, s, pt, i=i: (l, pt[b, s * pps + i], 0, 0))
                for i in range(pps)]
    gs = pltpu.PrefetchScalarGridSpec(
        num_scalar_prefetch=1,
        grid=(B, nsteps),
        in_specs=[per_b((nq * IDX_HEADS, LANES)), per_b((nq * IDX_HEADS, 1)), per_b((ATT_HEADS, PAGE)),
                  pl.BlockSpec((ATT_HEADS, 1), lambda b, s, pt: (0, 0)), per_b((IDX_HD, PAGE))] + ki_specs + lf_specs,
        out_specs=[pl.BlockSpec((None, nq, pps * PAGE), lambda b, s, pt: (b, 0, s)),
                   pl.BlockSpec((None, ATT_HEADS, pps * PAGE), lambda b, s, pt: (b, 0, s)),
                   per_b((nq, PAGE)), per_b((ATT_HEADS, PAGE)), per_b((ATT_HEADS, PAGE))],
        scratch_shapes=[pltpu.VMEM((ATT_HEADS, 1), F32)])
    return pl.pallas_call(
        functools.partial(_sa_kernel, pps=pps, tn=tn),
        grid_spec=gs,
        out_shape=[jax.ShapeDtypeStruct((B, nq, P), F32), jax.ShapeDtypeStruct((B, ATT_HEADS, P), F32),
                   jax.ShapeDtypeStruct((B, nq, PAGE), F32), jax.ShapeDtypeStruct((B, ATT_HEADS, PAGE), F32),
                   jax.ShapeDtypeStruct((B, ATT_HEADS, PAGE), F32)],
        compiler_params=_params(("arbitrary", "arbitrary")),
        name="sample_scores",
    )(page_table, qrows, wrows, cfn, fb, ikn, *([kidx_pool] * pps), *([lfT_pool] * pps))


def _sb_kernel(s_ref, o_ref, *, kb, nblk, topk, past, nq):
    C = s_ref.shape[1]
    o_ref[...] = s_ref[...]
    nadm = (past + 1 + lax.broadcasted_iota(jnp.int32, (1, C), 1) % nq).astype(F32)
    _topk_to_bias(o_ref, nblk, kb, C, topk, nadm)


def _sb(s_allT, topk, past, nq):
    Wt, C = s_allT.shape
    nl = Wt // LANES
    div = max(d for d in range(1, 9) if nl % d == 0)
    kb = div * LANES
    return pl.pallas_call(
        functools.partial(_sb_kernel, kb=kb, nblk=Wt // kb, topk=topk, past=past, nq=nq),
        grid=(1,),
        in_specs=[pl.BlockSpec((Wt, C), lambda i: (0, 0))],
        out_specs=pl.BlockSpec((Wt, C), lambda i: (0, 0)),
        out_shape=jax.ShapeDtypeStruct((Wt, C), F32),
        compiler_params=_params(("arbitrary",)),
        name="sample_topk",
    )(s_allT)


def _sc_kernel(pt_ref, fq_ref, dq_ref, fqc_ref, fp_ref, bp_ref, fn_ref, bn_ref,
               ckn_ref, cvn_ref, dkn_ref, dvn_ref, *rest, pps, past, tn, nq):
    fk_refs, fv_refs = rest[0:pps], rest[pps:2 * pps]
    dk_refs, dv_refs = rest[2 * pps:3 * pps], rest[3 * pps:4 * pps]
    yc_ref, yd_ref, qf_s, qd_s, accf, mf, lf, accd, md, ld = rest[4 * pps:]
    step = pl.program_id(1)
    R = ATT_HEADS * SUBLANES

    def stack_heads(q8):
        return jnp.concatenate([jnp.where(_head_mask(h), q8, jnp.zeros_like(q8)) for h in range(ATT_HEADS)], axis=0)

    @pl.when(step == 0)
    def _():
        qf_s[...] = stack_heads(fq_ref[...])
        qd_s[...] = stack_heads(dq_ref[...])
        for acc, m, l in ((accf, mf, lf), (accd, md, ld)):
            acc[...] = jnp.zeros_like(acc)
            m[...] = jnp.full_like(m, M_INIT)
            l[...] = jnp.zeros_like(l)

    def update(q_s, kT, vT, bias, acc, m, l):
        s = _dot(q_s[...], kT) + bias * LOG2E
        m_prev = m[...]
        m_new = jnp.maximum(m_prev, jnp.max(s, axis=1, keepdims=True))
        p = jnp.exp2(s - m_new)
        alpha = jnp.exp2(m_prev - m_new)
        l[...] = alpha * l[...] + jnp.sum(p, axis=1, keepdims=True)
        m[...] = m_new
        acc[...] = alpha * acc[...] + _nt(p.astype(BF16), vT)

    def per_head_rows(x4):
        return jnp.concatenate([jnp.broadcast_to(x4[h:h + 1, :], (SUBLANES, x4.shape[1]))
                                for h in range(ATT_HEADS)], axis=0)

    def per_query_rows(xq):
        r8 = lax.broadcasted_iota(jnp.int32, (SUBLANES, xq.shape[1]), 0)
        x8 = jnp.zeros((SUBLANES, xq.shape[1]), F32)
        for qq in range(nq):
            x8 = jnp.where(r8 == qq, jnp.broadcast_to(xq[qq:qq + 1, :], x8.shape), x8)
        return jnp.concatenate([x8] * ATT_HEADS, axis=0)

    rowi = lax.broadcasted_iota(jnp.int32, (R, 1), 0)
    qidx = rowi % SUBLANES
    slope = jnp.zeros((R, 1), F32)
    for h in range(ATT_HEADS):
        slope = jnp.where(rowi // SUBLANES == h, ALIBI[h], slope)

    def slab(ref):
        return ref[...].reshape(GROUP_W, ref.shape[-1]).astype(BF16)

    def blocks(refs):
        return jnp.concatenate([slab(r) for r in refs], axis=1)

    kw = pps * PAGE
    col = lax.broadcasted_iota(jnp.int32, (R, kw), 1)
    update(qf_s, blocks(fk_refs), blocks(fv_refs), fqc_ref[...] - per_head_rows(fp_ref[...]), accf, mf, lf)
    dist = (past + qidx - (step * kw + col)).astype(F32)
    update(qd_s, blocks(dk_refs), blocks(dv_refs), per_query_rows(bp_ref[...]) - slope * dist, accd, md, ld)

    @pl.when(step == pl.num_programs(1) - 1)
    def _():
        coln = lax.broadcasted_iota(jnp.int32, (R, PAGE), 1)
        ok = (coln <= qidx) & (coln < tn)
        bias_f = jnp.where(ok, fqc_ref[...] - per_head_rows(fn_ref[...]), NEG_INF)
        update(qf_s, slab(ckn_ref), slab(cvn_ref), bias_f, accf, mf, lf)
        distn = (qidx - coln).astype(F32)
        bias_d = jnp.where(coln < tn, per_query_rows(bn_ref[...]), NEG_INF) - slope * distn
        update(qd_s, slab(dkn_ref), slab(dvn_ref), bias_d, accd, md, ld)
        for acc, l, out in ((accf, lf, yc_ref), (accd, ld, yd_ref)):
            o = acc[...] / l[...]
            y = jnp.zeros((SUBLANES, GROUP_W), F32)
            for h in range(ATT_HEADS):
                y = jnp.where(_head_mask(h), o[h * SUBLANES:(h + 1) * SUBLANES, :], y)
            out[...] = y


def _sc(page_table, l, fq8, dq8, fqc, f_past, b_past, f_new, b_new, ckn, cvn, dkn, dvn,
        fk_pool, fv_pool, dk_pool, dv_pool, pps, tn):
    B, n_pages = page_table.shape
    nsteps = n_pages // pps
    past = n_pages * PAGE
    nq = b_past.shape[1]
    W = GROUP_W
    per_b = lambda shape: pl.BlockSpec((None,) + shape, lambda b, s, pt: (b,) + (0,) * len(shape))
    page = lambda i: pl.BlockSpec((None, None, ATT_HEADS, ATT_HD, PAGE),
                                  lambda b, s, pt, i=i: (l, pt[b, s * pps + i], 0, 0, 0))
    pages = [page(i) for i in range(pps)]
    newpage = per_b((ATT_HEADS, ATT_HD, PAGE))
    R = ATT_HEADS * SUBLANES
    gs = pltpu.PrefetchScalarGridSpec(
        num_scalar_prefetch=1,
        grid=(B, nsteps),
        in_specs=[per_b((SUBLANES, W)), per_b((SUBLANES, W)), per_b((R, 1)),
                  pl.BlockSpec((None, ATT_HEADS, pps * PAGE), lambda b, s, pt: (b, 0, s)),
                  pl.BlockSpec((None, nq, pps * PAGE), lambda b, s, pt: (b, 0, s)),
                  per_b((ATT_HEADS, PAGE)), per_b((nq, PAGE)),
                  newpage, newpage, newpage, newpage] + pages * 4,
        out_specs=[per_b((SUBLANES, W)), per_b((SUBLANES, W))],
        scratch_shapes=[pltpu.VMEM((R, W), BF16), pltpu.VMEM((R, W), BF16),
                        pltpu.VMEM((R, W), F32), pltpu.VMEM((R, 1), F32), pltpu.VMEM((R, 1), F32),
                        pltpu.VMEM((R, W), F32), pltpu.VMEM((R, 1), F32), pltpu.VMEM((R, 1), F32)])
    return pl.pallas_call(
        functools.partial(_sc_kernel, pps=pps, past=past, tn=tn, nq=nq),
        grid_spec=gs,
        out_shape=[jax.ShapeDtypeStruct((B, SUBLANES, W), F32)] * 2,
        compiler_params=_params(("arbitrary", "arbitrary")),
        name="sample_attn",
    )(page_table, fq8, dq8, fqc, f_past, b_past, f_new, b_new, ckn, cvn, dkn, dvn,
      *([fk_pool] * pps), *([fv_pool] * pps), *([dk_pool] * pps), *([dv_pool] * pps))


def _post_kernel(x_ref, ya_ref, yb_ref, yc_ref, yd_ref, g1_ref, sh2_ref, sc2_ref, g2_ref,
                 gn_ref, wo_ref, npost_ref, nfpre_ref, wg_ref, wu_ref, wd_ref, nfpost_ref, o_ref, *, chunks):
    o = None
    for i, r in enumerate((ya_ref, yb_ref, yc_ref, yd_ref)):
        part = _rms(r[...], gn_ref[:, i * GROUP_W:(i + 1) * GROUP_W]).astype(BF16)
        d = _dot(part, wo_ref[i * GROUP_W:(i + 1) * GROUP_W, :])
        o = d if o is None else o + d
    x1 = x_ref[...] + g1_ref[...] * _rms(o, npost_ref[...])
    h2 = (_rms(x1, nfpre_ref[...]) * (1.0 + sc2_ref[...]) + sh2_ref[...]).astype(BF16)
    f = None
    for c0, c1 in chunks:
        gate = _dot(h2, wg_ref[:, c0:c1])
        up = _dot(h2, wu_ref[:, c0:c1])
        d = _dot((gate * _sigmoid(gate) * up).astype(BF16), wd_ref[c0:c1, :])
        f = d if f is None else f + d
    o_ref[...] = x1 + g2_ref[...] * _rms(f, nfpost_ref[...])


def _post(x, ya, yb, yc, yd, mod, tiles_per_mod, pw, tm):
    N, D = x.shape
    H = pw["wg"].shape[1]
    rows_mod = mod.shape[1]
    step = 1024
    chunks = tuple((c, min(c + step, H)) for c in range(0, H, step))
    tok = lambda w: pl.BlockSpec((tm, w), lambda i: (i, 0))
    const = lambda shape: pl.BlockSpec(shape, lambda i: (0,) * len(shape), pipeline_mode=pl.Buffered(1))
    return pl.pallas_call(
        functools.partial(_post_kernel, chunks=chunks),
        grid=(N // tm,),
        in_specs=[tok(D), tok(GROUP_W), tok(GROUP_W), tok(GROUP_W), tok(GROUP_W),
                  _mod_spec(rows_mod, D, tiles_per_mod, 2), _mod_spec(rows_mod, D, tiles_per_mod, 3),
                  _mod_spec(rows_mod, D, tiles_per_mod, 4), _mod_spec(rows_mod, D, tiles_per_mod, 5),
                  const((1, D)), const((D, D)), const((1, D)), const((1, D)),
                  const((D, H)), const((D, H)), const((H, D)), const((1, D))],
        out_specs=tok(D),
        out_shape=jax.ShapeDtypeStruct((N, D), F32),
        compiler_params=_params(("arbitrary",)),
        name="post",
    )(x, ya, yb, yc, yd, mod, mod, mod, mod, pw["gn"], pw["wo"], pw["npost"], pw["nfpre"],
      pw["wg"], pw["wu"], pw["wd"], pw["nfpost"])


def _block_diag(blocks):
    G, r, c = blocks.shape
    eye = jnp.eye(G, dtype=blocks.dtype)
    return (blocks[:, :, None, :] * eye[:, None, :, None]).reshape(G * r, G * c)


def _cat_weight(w):
    D = w.shape[0]
    widths = (GROUP_W, GROUP_W, GROUP_W, GROUP_W, GROUP_W, GROUP_W, ATT_HEADS, GROUP_W, GROUP_W, GROUP_W,
              IDX_HEADS * IDX_HD, IDX_HD, IDX_HEADS)
    offs = [0]
    for wd in widths:
        offs.append(offs[-1] + wd)
    pc = [w[:, offs[i]:offs[i + 1]] for i in range(len(widths))]
    a_x, a_g, b_u, c_q, c_k, c_v, c_f, d_q, d_k, d_v, i_q, i_k, i_w = pc
    iq_rep = jnp.tile(i_q.reshape(D, IDX_HEADS, 1, IDX_HD), (1, 1, LANES // IDX_HD, 1)).reshape(D, IDX_HEADS * LANES)
    ik_rep = jnp.tile(i_k, (1, LANES // IDX_HD))
    small = jnp.concatenate([i_k, c_f, i_w, jnp.zeros((D, LANES - IDX_HD - ATT_HEADS - IDX_HEADS), w.dtype)], axis=1)
    wcat = jnp.concatenate([a_x, a_g, b_u, c_q, c_k, c_v, d_q, d_k, d_v], axis=1).astype(BF16)
    widx = jnp.concatenate([iq_rep, ik_rep, small], axis=1)
    hi = lax.reduce_precision(widx, exponent_bits=8, mantissa_bits=7)
    hi_b = hi.astype(BF16)
    return wcat, jnp.concatenate([hi_b, hi_b, (widx - hi).astype(BF16)], axis=0)


def _layer_weights(l, p):
    row = lambda a: a[l].reshape(1, -1)
    lw = dict(conv_w=p["lru_conv_w"][l], conv_b=row(p["lru_conv_b"]),
              wr=_block_diag(p["lru_wr"][l]).astype(BF16), br=row(p["lru_br"]),
              wi=_block_diag(p["lru_wi"][l]).astype(BF16), bi=row(p["lru_bi"]), lam=row(p["lru_lambda"]))
    sw = dict(ldt=jnp.repeat(p["s5_log_dt"][l], S5_N).reshape(1, S5_STATE),
              are=p["s5_a_re"][l].reshape(1, S5_STATE), aim=p["s5_a_im"][l].reshape(1, S5_STATE),
              bre=_block_diag(jnp.swapaxes(p["s5_b_re"][l], 1, 2)).astype(BF16),
              bim=_block_diag(jnp.swapaxes(p["s5_b_im"][l], 1, 2)).astype(BF16),
              cre=_block_diag(jnp.swapaxes(p["s5_c_re"][l], 1, 2)).astype(BF16),
              cim=_block_diag(jnp.swapaxes(p["s5_c_im"][l], 1, 2)).astype(BF16),
              d=row(p["s5_d"]), gw=p["s5_glu_w"][l].astype(BF16), gb=row(p["s5_glu_b"]))
    pw = dict(gn=row(p["grp_norm"]), wo=p["w_out"][l].astype(BF16), npost=row(p["norm_mix_post"]),
              nfpre=row(p["norm_ffn_pre"]), wg=p["ffn_w_gate"][l].astype(BF16), wu=p["ffn_w_up"][l].astype(BF16),
              wd=p["ffn_w_down"][l].astype(BF16), nfpost=row(p["norm_ffn_post"]))
    return dict(win=_cat_weight(p["w_in"][l]), npre=row(p["norm_mix_pre"]), lru=lw, s5=sw, post=pw,
                fbias=p["fox_f_bias"][l])


def _largest_tile(n, cap):
    t = min(n, cap)
    while n % t:
        t //= 2
    return t


def _recurrent(ax, ag, bu, lru_buf, lru_h0, s5_r0, s5_i0, wts, t_real):
    B, Tp, _ = ax.shape
    tc = _largest_tile(Tp, 256)
    buf8 = jnp.pad(lru_buf, ((0, 0), (SUBLANES - (CONV_W - 1), 0), (0, 0)))
    ya, lru_h = _lru(ax, ag, buf8, lru_h0.reshape(B, 1, GROUP_W), wts["lru"], tc, t_real)
    yb, s5r, s5i = _s5(bu, s5_r0.reshape(B, 1, S5_STATE), s5_i0.reshape(B, 1, S5_STATE), wts["s5"], tc, t_real)
    return ya, yb, lru_h.reshape(B, GROUP_W), s5r.reshape(B, S5_NG, S5_N), s5i.reshape(B, S5_NG, S5_N)


def _prompt_layer(x, mod, wts, B, T):
    N, D = x.shape
    tm = _largest_tile(T, 512)
    (ax, ag, bu, cq, ckf, ckb, cvf, cvb, dq, dkf, dkb, dvf, dvb, iqp, ikp, sm) = _inproj(
        x, mod, T // tm, wts["npre"], *wts["win"], tm)
    seq = lambda a: a.reshape(B, T, a.shape[-1])
    zeros = lambda *s: jnp.zeros(s, F32)
    ax3 = seq(ax)
    ya, yb, lru_h, s5r, s5i = _recurrent(ax3, seq(ag), seq(bu), zeros(B, CONV_W - 1, GROUP_W), zeros(B, GROUP_W),
                                         zeros(B, S5_NG, S5_N), zeros(B, S5_NG, S5_N), wts, T)
    nb = T // LANES
    cf_rows = jnp.swapaxes(seq(sm)[:, :, SM_CF:SM_CF + ATT_HEADS], 1, 2).reshape(B, ATT_HEADS * nb, LANES)
    fb_rows = jnp.repeat(wts["fbias"], nb).reshape(ATT_HEADS * nb, 1)
    lf_rows, f_rows = _fcum(cf_rows, fb_rows, nb)
    f2_col = jnp.swapaxes(f_rows.reshape(B, ATT_HEADS, T), 1, 2)
    logf = jnp.swapaxes(lf_rows.reshape(B, ATT_HEADS, T), 1, 2)
    tr = lambda a: jnp.swapaxes(seq(a), 1, 2)
    tq = kb = _largest_tile(T, 512)
    yc = jnp.swapaxes(_fox(tr(cq), seq(ckb), _with_ones_rows(tr(cvb)), f2_col, tq, kb), 1, 2)
    topk = max(1, min(DSA_TOPK_MAX, T // 4))
    yd = jnp.swapaxes(_dsa(tr(iqp), seq(ikp), tr(sm), tr(dq), seq(dkb), _with_ones_rows(tr(dvb)), tq, kb, topk),
                      1, 2)
    flat = lambda a: a.reshape(N, GROUP_W)
    x2 = _post(x, flat(ya), flat(yb), flat(yc), flat(yd), mod, T // tm, wts["post"], tm)
    heads = lambda a: a.reshape(B, T, ATT_HEADS, ATT_HD)
    state = dict(lru_h=lru_h, lru_conv=ax3[:, T - (CONV_W - 1):, :], s5_re=s5r, s5_im=s5i,
                 fox_k=heads(ckf), fox_v=heads(cvf), fox_logf=logf, dsa_k=heads(dkf), dsa_v=heads(dvf),
                 dsa_kidx=seq(sm)[:, :, SM_IK:SM_IK + IDX_HD])
    return x2, state


def _sample_layer(x, mod, wts, l, B, T, past, caches, page_table):
    N, D = x.shape
    (ax, ag, bu, cq, ckf, ckb, cvf, cvb, dq, dkf, dkb, dvf, dvb, iqp, ikp, sm) = _inproj(
        x, mod, 1, wts["npre"], *wts["win"], N)
    seq = lambda a: a.reshape(B, T, a.shape[-1])
    tp = -(-T // SUBLANES) * SUBLANES
    padt = lambda a, n=tp: jnp.pad(seq(a), ((0, 0), (0, n - T), (0, 0)))
    ax3 = seq(ax)
    ya, yb, lru_h, s5r, s5i = _recurrent(padt(ax), padt(ag), padt(bu), past["lru_conv"], past["lru_h"],
                                         past["s5_re"], past["s5_im"], wts, T)
    sm3 = seq(sm)
    n_pages = page_table.shape[1]
    plen = n_pages * PAGE
    pps = _largest_tile(n_pages, 16)
    iq_rows = iqp.reshape(B, T * IDX_HEADS, LANES)
    wrows = sm3[:, :, SM_IW:SM_IW + IDX_HEADS].reshape(B, T * IDX_HEADS, 1)
    cfn = jnp.pad(jnp.swapaxes(sm3[:, :, SM_CF:SM_CF + ATT_HEADS], 1, 2), ((0, 0), (0, 0), (0, PAGE - T)))
    ikn = jnp.pad(jnp.swapaxes(sm3[:, :, SM_IK:SM_IK + IDX_HD], 1, 2), ((0, 0), (0, 0), (0, PAGE - T)))
    s_past, f_past, s_new, f_new, lf_new = _sa(page_table, l, iq_rows, wrows, cfn,
                                               wts["fbias"].reshape(ATT_HEADS, 1), ikn, caches["kidx"],
                                               caches["logfT"], _largest_tile(n_pages, 32), T)
    topk = max(1, min(DSA_TOPK_MAX, (plen + T) // 4))
    s_allT = jnp.transpose(jnp.concatenate([s_past, s_new], axis=2), (2, 0, 1)).reshape(plen + PAGE, B * T)
    bias_all = jnp.transpose(_sb(s_allT, topk, plen, T).reshape(plen + PAGE, B, T), (1, 2, 0))
    fqc = jnp.pad(f_new[:, :, :T], ((0, 0), (0, 0), (0, SUBLANES - T))).reshape(B, ATT_HEADS * SUBLANES, 1)
    newpage = lambda a: jnp.pad(jnp.transpose(a.reshape(B, T, ATT_HEADS, ATT_HD), (0, 2, 3, 1)),
                                ((0, 0), (0, 0), (0, 0), (0, PAGE - T)))
    yc8, yd8 = _sc(page_table, l, padt(cq, SUBLANES), padt(dq, SUBLANES), fqc, f_past, bias_all[:, :, :plen],
                   f_new, bias_all[:, :, plen:], newpage(ckb), newpage(cvb), newpage(dkb), newpage(dvb),
                   caches["fox_k"], caches["fox_v"], caches["dsa_k"], caches["dsa_v"], pps, T)
    flat = lambda a: a[:, :T, :].reshape(N, GROUP_W)
    x2 = _post(x, flat(ya), flat(yb), flat(yc8), flat(yd8), mod, 1, wts["post"], N)
    heads = lambda a: a.reshape(B, T, ATT_HEADS, ATT_HD)
    state = dict(lru_h=lru_h, lru_conv=ax3[:, T - (CONV_W - 1):, :], s5_re=s5r, s5_im=s5i,
                 fox_k=heads(ckf), fox_v=heads(cvf), fox_logf=jnp.swapaxes(lf_new[:, :, :T], 1, 2),
                 dsa_k=heads(dkf), dsa_v=heads(dvf), dsa_kidx=sm3[:, :, SM_IK:SM_IK + IDX_HD])
    return x2, state


def kernel(x_prompt, x_sample, state_lru_h, state_lru_conv, state_s5_re, state_s5_im, cache_fox_k, cache_fox_v, cache_fox_logf, cache_dsa_k, cache_dsa_v, cache_dsa_kidx, page_table, c_prompt, c_sample, ada_w, ada_b, norm_mix_pre, norm_mix_post, norm_ffn_pre, norm_ffn_post, w_in, lru_conv_w, lru_conv_b, lru_wr, lru_br, lru_wi, lru_bi, lru_lambda, s5_log_dt, s5_a_re, s5_a_im, s5_b_re, s5_b_im, s5_c_re, s5_c_im, s5_d, s5_glu_w, s5_glu_b, fox_f_bias, grp_norm, w_out, ffn_w_gate, ffn_w_up, ffn_w_down):
    p = dict(norm_mix_pre=norm_mix_pre, norm_mix_post=norm_mix_post, norm_ffn_pre=norm_ffn_pre,
             norm_ffn_post=norm_ffn_post, w_in=w_in, lru_conv_w=lru_conv_w, lru_conv_b=lru_conv_b, lru_wr=lru_wr,
             lru_br=lru_br, lru_wi=lru_wi, lru_bi=lru_bi, lru_lambda=lru_lambda, s5_log_dt=s5_log_dt,
             s5_a_re=s5_a_re, s5_a_im=s5_a_im, s5_b_re=s5_b_re, s5_b_im=s5_b_im, s5_c_re=s5_c_re, s5_c_im=s5_c_im,
             s5_d=s5_d, s5_glu_w=s5_glu_w, s5_glu_b=s5_glu_b, fox_f_bias=fox_f_bias, grp_norm=grp_norm,
             w_out=w_out, ffn_w_gate=ffn_w_gate, ffn_w_up=ffn_w_up, ffn_w_down=ffn_w_down)
    B, T, D = x_prompt.shape
    Bs, Ts, _ = x_sample.shape
    L = ada_w.shape[0]
    n_pool = cache_fox_k.shape[1]
    assert T % LANES == 0 and CONV_W - 1 <= Ts <= SUBLANES and cache_fox_k.shape[2] == PAGE

    c_all = jnp.concatenate([c_prompt, c_sample], axis=0)
    c_all = jnp.pad(c_all, ((0, -c_all.shape[0] % SUBLANES), (0, 0)))
    mod_all = _ada(c_all, ada_w, ada_b)

    pool = lambda a: jnp.transpose(a, (0, 1, 3, 4, 2))
    caches = dict(fox_k=pool(cache_fox_k), fox_v=pool(cache_fox_v), dsa_k=pool(cache_dsa_k), dsa_v=pool(cache_dsa_v),
                  kidx=jnp.swapaxes(cache_dsa_kidx, 2, 3), logfT=jnp.swapaxes(cache_fox_logf, 2, 3))

    xp = x_prompt.reshape(B * T, D)
    xs = x_sample.reshape(Bs * Ts, D)
    st_p, st_s = [], []
    for l in range(L):
        wts = _layer_weights(l, p)
        mod_p = mod_all[l, :B].reshape(B, 1, 6 * D)
        mod_s = jnp.repeat(mod_all[l, B:B + Bs], Ts, axis=0).reshape(1, Bs * Ts, 6 * D)
        xp, new_p = _prompt_layer(xp, mod_p, wts, B, T)
        past = dict(lru_h=state_lru_h[l], lru_conv=state_lru_conv[l], s5_re=state_s5_re[l], s5_im=state_s5_im[l])
        xs, new_s = _sample_layer(xs, mod_s, wts, l, Bs, Ts, past, caches, page_table)
        st_p.append(new_p)
        st_s.append(new_s)

    stk = lambda outs, name: jnp.stack([o[name] for o in outs])
    names = ("lru_h", "lru_conv", "s5_re", "s5_im", "fox_k", "fox_v", "fox_logf", "dsa_k", "dsa_v", "dsa_kidx")
    res = [xp.reshape(B, T, D), xs.reshape(Bs, Ts, D)]
    for name in names:
        res += [stk(st_p, name), stk(st_s, name)]
    return tuple(res)
```

```python
import functools
import math

import jax
import jax.numpy as jnp
from jax import lax
from jax.experimental import pallas as pl
from jax.experimental.pallas import tpu as pltpu

F32 = jnp.float32
BF16 = jnp.bfloat16

N_MIXERS = 4
GROUP_W = 256
LRU_HEADS = 4
CONV_W = 4
LRU_C = 8.0
S5_GROUP = 16
S5_NG = 16
S5_N = 64
S5_STATE = S5_NG * S5_N
ATT_HEADS = 4
ATT_HD = 64
IDX_HEADS = 8
IDX_HD = 32
DSA_TOPK_MAX = 256
PAGE = 128
NEG_INF = -1e30
M_INIT = -1e20
BIG = 3e38
ADM_CUT = -1e29
EPS = 1e-6
ATT_SCALE = ATT_HD ** -0.5
IDX_SCALE = IDX_HD ** -0.5
IDX_W_SCALE = IDX_HEADS ** -0.5
ALIBI = tuple(2.0 ** (-8.0 * (h + 1) / ATT_HEADS) for h in range(ATT_HEADS))
LOG2E = math.log2(math.e)
Q_SCALE = ATT_SCALE * LOG2E
BISECT_ITERS = 28
SEARCH_ITERS = 48

LANES = 128
SUBLANES = 8
VMEM_LIMIT = 56 * 1024 * 1024

C_AX, C_AG, C_BU, C_CQ, C_CK, C_CV, C_DQ, C_DK, C_DV = (i * GROUP_W for i in range(9))
P_CAT = 9 * GROUP_W
C_IQ = 0
C_IK = C_IQ + IDX_HEADS * LANES
C_SM = C_IK + LANES
P_IDX = C_SM + LANES
SM_IK, SM_CF, SM_IW = 0, IDX_HD, IDX_HD + ATT_HEADS

NT_DIMS = (((1,), (1,)), ((), ()))


def _nt(a, b):
    return lax.dot_general(a, b, NT_DIMS, preferred_element_type=F32)


def _dot(a, b):
    return jnp.dot(a, b, preferred_element_type=F32)


def _rms(x, g):
    return x * lax.rsqrt(jnp.mean(x * x, axis=-1, keepdims=True) + EPS) * g


def _softplus(z):
    return jnp.maximum(z, 0.0) + jnp.log1p(jnp.exp(-jnp.abs(z)))


def _log_sigmoid(z):
    return -_softplus(-z)


def _expm1(z):
    e = jnp.exp(z)
    one = e == 1.0
    return jnp.where(one, z, (e - 1.0) * z / jnp.where(one, 1.0, jnp.log(e)))


def _sigmoid(z):
    return jax.nn.sigmoid(z)


def _gelu(z):
    return jax.nn.gelu(z)


def _params(sem):
    return pltpu.CompilerParams(dimension_semantics=sem, vmem_limit_bytes=VMEM_LIMIT)


def _head_mask(h, width=GROUP_W):
    lane = lax.broadcasted_iota(jnp.int32, (1, width), 1)
    return (lane >= h * ATT_HD) & (lane < (h + 1) * ATT_HD)


FOLD_CHAINS = 4


def _fold_rows(x, axis_op):
    n, c = x.shape
    g = FOLD_CHAINS if n % (FOLD_CHAINS * SUBLANES) == 0 else 1
    x = axis_op(x.reshape(g, n // (g * SUBLANES), SUBLANES, c), axis=1)
    return axis_op(x, axis=0)


def _col_max(x):
    return jnp.max(_fold_rows(x, jnp.max), axis=0, keepdims=True)


def _lane_cumsum(x):
    lane = lax.broadcasted_iota(jnp.int32, x.shape, 1)
    s = 1
    while s < x.shape[1]:
        x = x + jnp.where(lane >= s, pltpu.roll(x, s, axis=1), 0.0)
        s *= 2
    return x


def _split(x):
    hi = x.astype(BF16)
    return hi, (x - hi.astype(F32)).astype(BF16)


def _ada_kernel(c_ref, w_ref, b_ref, o_ref):
    c = c_ref[...]
    sh, sl = _split(c * _sigmoid(c))
    wh, wl = _split(w_ref[...])
    o_ref[...] = (_dot(sl, wl) + _dot(sl, wh) + _dot(sh, wl)) + _dot(sh, wh) + b_ref[...]


def _ada(c_all, ada_w, ada_b):
    L, D, D6 = ada_w.shape
    Bp = c_all.shape[0]
    tn = 1024
    return pl.pallas_call(
        _ada_kernel,
        grid=(L, D6 // tn),
        in_specs=[pl.BlockSpec((Bp, D), lambda l, j: (0, 0)),
                  pl.BlockSpec((None, D, tn), lambda l, j: (l, 0, j)),
                  pl.BlockSpec((None, 1, tn), lambda l, j: (l, 0, j))],
        out_specs=pl.BlockSpec((None, Bp, tn), lambda l, j: (l, 0, j)),
        out_shape=jax.ShapeDtypeStruct((L, Bp, D6), F32),
        compiler_params=_params(("arbitrary", "arbitrary")),
        name="ada",
    )(c_all, ada_w, ada_b.reshape(L, 1, D6))


def _inproj_kernel(x_ref, sh_ref, sc_ref, g_ref, w_ref, w3_ref,
                   ax_ref, ag_ref, bu_ref, cq_ref, ckf_ref, ckb_ref, cvf_ref, cvb_ref,
                   dq_ref, dkf_ref, dkb_ref, dvf_ref, dvb_ref, iq_ref, ik_ref, sm_ref):
    x = x_ref[...]
    h = _rms(x, g_ref[...]) * (1.0 + sc_ref[...]) + sh_ref[...]
    hb = h.astype(BF16)

    main = _dot(hb, w_ref[...])

    def mm(c0, width):
        return main[:, c0:c0 + width]

    ax_ref[...] = mm(C_AX, GROUP_W)
    ag_ref[...] = mm(C_AG, GROUP_W)
    bu_ref[...] = mm(C_BU, GROUP_W)
    cq_ref[...] = (mm(C_CQ, GROUP_W) * Q_SCALE).astype(BF16)
    ck = mm(C_CK, GROUP_W)
    ckf_ref[...] = ck
    ckb_ref[...] = ck.astype(BF16)
    cv = mm(C_CV, GROUP_W)
    cvf_ref[...] = cv
    cvb_ref[...] = cv.astype(BF16)
    dq_ref[...] = (mm(C_DQ, GROUP_W) * Q_SCALE).astype(BF16)
    dk = mm(C_DK, GROUP_W)
    dkf_ref[...] = dk
    dkb_ref[...] = dk.astype(BF16)
    dv = mm(C_DV, GROUP_W)
    dvf_ref[...] = dv
    dvb_ref[...] = dv.astype(BF16)

    hl = (h - hb.astype(F32)).astype(BF16)
    idx = _dot(jnp.concatenate([hb, hl, hb], axis=1), w3_ref[...])

    def mm3(c0, width):
        return idx[:, c0:c0 + width]

    sm_ref[...] = mm3(C_SM, LANES)

    sub = (lax.broadcasted_iota(jnp.int32, (1, LANES), 1) // IDX_HD)
    for hh in range(IDX_HEADS):
        r = mm3(C_IQ + hh * LANES, LANES)
        hi = r.astype(BF16)
        lo = (r - hi.astype(F32)).astype(BF16)
        zero = jnp.zeros_like(hi)
        iq_ref[:, hh * LANES:(hh + 1) * LANES] = jnp.where(sub == 1, lo, jnp.where(sub == 3, zero, hi))
    r = mm3(C_IK, LANES)
    hi = r.astype(BF16)
    lo = (r - hi.astype(F32)).astype(BF16)
    ik_ref[...] = jnp.where(sub == 2, lo, jnp.where(sub == 3, jnp.zeros_like(hi), hi))


def _mod_spec(rows_mod, D, tiles_per_mod, piece):
    return pl.BlockSpec((None, rows_mod, D), lambda i: (i // tiles_per_mod, 0, piece))


def _inproj(x, mod, tiles_per_mod, g, wcat, widx3, tm):
    N, D = x.shape
    rows_mod = mod.shape[1]
    tok = lambda w: pl.BlockSpec((tm, w), lambda i: (i, 0))
    widths_dtypes = [(GROUP_W, F32)] * 3 + [(GROUP_W, BF16), (GROUP_W, F32), (GROUP_W, BF16), (GROUP_W, F32),
                                            (GROUP_W, BF16), (GROUP_W, BF16), (GROUP_W, F32), (GROUP_W, BF16),
                                            (GROUP_W, F32), (GROUP_W, BF16), (IDX_HEADS * LANES, BF16),
                                            (LANES, BF16), (LANES, F32)]
    return pl.pallas_call(
        _inproj_kernel,
        grid=(N // tm,),
        in_specs=[tok(D), _mod_spec(rows_mod, D, tiles_per_mod, 0), _mod_spec(rows_mod, D, tiles_per_mod, 1),
                  pl.BlockSpec((1, D), lambda i: (0, 0)),
                  pl.BlockSpec((D, P_CAT), lambda i: (0, 0), pipeline_mode=pl.Buffered(1)),
                  pl.BlockSpec((3 * D, P_IDX), lambda i: (0, 0), pipeline_mode=pl.Buffered(1))],
        out_specs=[tok(w) for w, _ in widths_dtypes],
        out_shape=[jax.ShapeDtypeStruct((N, w), dt) for w, dt in widths_dtypes],
        compiler_params=_params(("arbitrary",)),
        name="inproj",
    )(x, mod, mod, g, wcat, widx3)


def _lru_kernel(ax_ref, ag_ref, buf_ref, h0_ref, cw_ref, cb_ref, wr_ref, br_ref, wi_ref, bi_ref, lam_ref,
                ya_ref, hl_ref, xs, hcar, *, tc, last_row):
    c = pl.program_id(1)

    @pl.when(c == 0)
    def _():
        xs[0:SUBLANES, :] = buf_ref[...]
        hcar[...] = h0_ref[...]

    x = ax_ref[...]
    xs[SUBLANES:SUBLANES + tc, :] = x
    w = cw_ref[...]
    y = cb_ref[...] + xs[5:5 + tc, :] * w[0:1] + xs[6:6 + tc, :] * w[1:2] + xs[7:7 + tc, :] * w[2:3] + x * w[3:4]
    xs[0:SUBLANES, :] = xs[tc:tc + SUBLANES, :]

    yb = y.astype(BF16)
    r = _sigmoid(_dot(yb, wr_ref[...]) + br_ref[...])
    i = _sigmoid(_dot(yb, wi_ref[...]) + bi_ref[...])
    log_a = -LRU_C * r * _softplus(-lam_ref[...])
    a = jnp.exp(log_a)
    b = jnp.sqrt(-_expm1(2.0 * log_a)) * (i * y)

    row = lax.broadcasted_iota(jnp.int32, (tc, GROUP_W), 0)
    s = 1
    while s < tc:
        keep = row >= s
        a_sh = jnp.where(keep, pltpu.roll(a, s, axis=0), 1.0)
        b_sh = jnp.where(keep, pltpu.roll(b, s, axis=0), 0.0)
        b = b + a * b_sh
        a = a * a_sh
        s *= 2
    h = b + a * hcar[...]
    hcar[...] = h[tc - 1:tc, :]
    ya_ref[...] = h * _gelu(ag_ref[...])

    @pl.when(c == pl.num_programs(1) - 1)
    def _():
        hl_ref[...] = h[last_row:last_row + 1, :]


def _lru(ax, ag, buf8, h0, lw, tc, t_real):
    B, Tp, W = ax.shape
    seq = pl.BlockSpec((None, tc, W), lambda b, c: (b, c, 0))
    full = lambda shape: pl.BlockSpec(shape, lambda b, c: (0,) * len(shape))
    return pl.pallas_call(
        functools.partial(_lru_kernel, tc=tc, last_row=(t_real - 1) % tc),
        grid=(B, Tp // tc),
        in_specs=[seq, seq,
                  pl.BlockSpec((None, SUBLANES, W), lambda b, c: (b, 0, 0)),
                  pl.BlockSpec((None, 1, W), lambda b, c: (b, 0, 0)),
                  full((CONV_W, W)), full((1, W)), full((W, W)), full((1, W)), full((W, W)), full((1, W)),
                  full((1, W))],
        out_specs=[seq, pl.BlockSpec((None, 1, W), lambda b, c: (b, 0, 0))],
        out_shape=[jax.ShapeDtypeStruct((B, Tp, W), F32), jax.ShapeDtypeStruct((B, 1, W), F32)],
        scratch_shapes=[pltpu.VMEM((tc + SUBLANES, W), F32), pltpu.VMEM((1, W), F32)],
        compiler_params=_params(("arbitrary", "arbitrary")),
        name="lru",
    )(ax, ag, buf8, h0, lw["conv_w"], lw["conv_b"], lw["wr"], lw["br"], lw["wi"], lw["bi"], lw["lam"])


def _s5_kernel(u_ref, h0r_ref, h0i_ref, ldt_ref, are_ref, aim_ref, bre_ref, bim_ref, cre_ref, cim_ref,
               d_ref, gw_ref, gb_ref, y_ref, hlr_ref, hli_ref, hr_s, hi_s, car_r, car_i, *, tc, last_row):
    c = pl.program_id(1)

    @pl.when(c == 0)
    def _():
        car_r[...] = h0r_ref[...]
        car_i[...] = h0i_ref[...]

    dt = jnp.exp(ldt_ref[...])
    ar, ai = are_ref[...], aim_ref[...]
    mag = jnp.exp(dt * ar)
    abr, abi = mag * jnp.cos(dt * ai), mag * jnp.sin(dt * ai)
    den = ar * ar + ai * ai
    nr, ni = abr - 1.0, abi
    cr = (nr * ar + ni * ai) / den
    ci = (ni * ar - nr * ai) / den

    u = u_ref[...]
    ub = u.astype(BF16)
    pre = _dot(ub, bre_ref[...])
    pim = _dot(ub, bim_ref[...])
    hr_s[...] = cr * pre - ci * pim
    hi_s[...] = cr * pim + ci * pre

    def step(t, carry):
        hr, hi = carry
        nhr = abr * hr - abi * hi + hr_s[pl.ds(t, 1), :]
        nhi = abr * hi + abi * hr + hi_s[pl.ds(t, 1), :]
        hr_s[pl.ds(t, 1), :] = nhr
        hi_s[pl.ds(t, 1), :] = nhi
        return nhr, nhi

    hr, hi = lax.fori_loop(0, tc, step, (car_r[...], car_i[...]), unroll=8)
    car_r[...] = hr
    car_i[...] = hi

    hrv, hiv = hr_s[...], hi_s[...]
    y = _dot(hrv.astype(BF16), cre_ref[...]) - _dot(hiv.astype(BF16), cim_ref[...]) + d_ref[...] * u
    g = _gelu(y)
    y_ref[...] = g * _sigmoid(_dot(g.astype(BF16), gw_ref[...]) + gb_ref[...])

    @pl.when(c == pl.num_programs(1) - 1)
    def _():
        hlr_ref[...] = hr_s[last_row:last_row + 1, :]
        hli_ref[...] = hi_s[last_row:last_row + 1, :]


def _s5(u, h0r, h0i, sw, tc, t_real):
    B, Tp, W = u.shape
    S = S5_STATE
    seq = pl.BlockSpec((None, tc, W), lambda b, c: (b, c, 0))
    st = pl.BlockSpec((None, 1, S), lambda b, c: (b, 0, 0))
    full = lambda shape: pl.BlockSpec(shape, lambda b, c: (0,) * len(shape))
    return pl.pallas_call(
        functools.partial(_s5_kernel, tc=tc, last_row=(t_real - 1) % tc),
        grid=(B, Tp // tc),
        in_specs=[seq, st, st, full((1, S)), full((1, S)), full((1, S)), full((W, S)), full((W, S)),
                  full((S, W)), full((S, W)), full((1, W)), full((W, W)), full((1, W))],
        out_specs=[seq, st, st],
        out_shape=[jax.ShapeDtypeStruct((B, Tp, W), F32), jax.ShapeDtypeStruct((B, 1, S), F32),
                   jax.ShapeDtypeStruct((B, 1, S), F32)],
        scratch_shapes=[pltpu.VMEM((tc, S), F32), pltpu.VMEM((tc, S), F32),
                        pltpu.VMEM((1, S), F32), pltpu.VMEM((1, S), F32)],
        compiler_params=_params(("arbitrary", "arbitrary")),
        name="s5",
    )(u, h0r, h0i, sw["ldt"], sw["are"], sw["aim"], sw["bre"], sw["bim"], sw["cre"], sw["cim"],
      sw["d"], sw["gw"], sw["gb"])


def _fcum_kernel(cf_ref, fb_ref, lf_ref, f_ref, *, nb):
    lf = _log_sigmoid(cf_ref[...] + fb_ref[...])
    lf_ref[...] = lf
    cs = _lane_cumsum(lf)
    tot = jnp.broadcast_to(cs[:, LANES - 1:LANES], cs.shape)
    row = lax.broadcasted_iota(jnp.int32, cs.shape, 0) % nb
    inc = tot
    s = 1
    while s < nb:
        inc = inc + jnp.where(row >= s, pltpu.roll(inc, s, axis=0), 0.0)
        s *= 2
    f_ref[...] = (cs + (inc - tot)) * LOG2E


def _fcum(cf_rows, fb_rows, nb):
    B, R, _ = cf_rows.shape
    blk = pl.BlockSpec((None, R, LANES), lambda b: (b, 0, 0))
    return pl.pallas_call(
        functools.partial(_fcum_kernel, nb=nb),
        grid=(B,),
        in_specs=[blk, pl.BlockSpec((R, 1), lambda b: (0, 0))],
        out_specs=[blk, blk],
        out_shape=[jax.ShapeDtypeStruct((B, R, LANES), F32)] * 2,
        compiler_params=_params(("arbitrary",)),
        name="fcum",
    )(cf_rows, fb_rows)


ONES_ROWS = 16
VROWS = ATT_HD + ONES_ROWS


def _init_heads(qT_ref, qm_s, acc_s, m_s):
    qT = qT_ref[...]
    row = lax.broadcasted_iota(jnp.int32, (qT.shape[0], 1), 0)
    for h in range(ATT_HEADS):
        qm_s[h] = jnp.where(row // ATT_HD == h, qT, jnp.zeros_like(qT))
    acc_s[...] = jnp.zeros_like(acc_s)
    m_s[...] = jnp.full_like(m_s, M_INIT)


def _attend_block(kblk, vT_ref, k0, kb, qm_s, acc_s, m_s, s_s, p_s, bias_fn):
    for h in range(ATT_HEADS):
        s_s[h] = _dot(kblk, qm_s[h])
    m_new = []
    for h in range(ATT_HEADS):
        s = bias_fn(h, s_s[h])
        s_s[h] = s
        m_new.append(jnp.maximum(m_s[h], _col_max(s)))
    for h in range(ATT_HEADS):
        p_s[h] = jnp.exp2(s_s[h] - m_new[h]).astype(BF16)
    for h in range(ATT_HEADS):
        rows = slice(h * VROWS, (h + 1) * VROWS)
        pv = _dot(vT_ref[rows, pl.ds(k0, kb)], p_s[h])
        acc_s[rows, :] = jnp.exp2(m_s[h] - m_new[h]) * acc_s[rows, :] + pv
        m_s[h] = m_new[h]


def _finish_heads(o_ref, acc_s):
    for h in range(ATT_HEADS):
        base = h * VROWS
        o_ref[h * ATT_HD:(h + 1) * ATT_HD, :] = (acc_s[base:base + ATT_HD, :]
                                                 / acc_s[base + ATT_HD:base + ATT_HD + 1, :])


def _fox_kernel(qT_ref, k_ref, vT_ref, f2_ref, o_ref, qm_s, acc_s, m_s, s_s, p_s, *, tq, kb):
    q0 = pl.program_id(1) * tq
    _init_heads(qT_ref, qm_s, acc_s, m_s)
    kio = lax.broadcasted_iota(jnp.int32, (kb, tq), 0)
    qpos = q0 + lax.broadcasted_iota(jnp.int32, (kb, tq), 1)

    def block(j, masked):
        k0 = pl.multiple_of(j * kb, kb)
        fk = f2_ref[pl.ds(k0, kb), :]

        def bias(h, s):
            s = s - fk[:, h:h + 1]
            return jnp.where(k0 + kio <= qpos, s, NEG_INF) if masked else s

        _attend_block(k_ref[pl.ds(k0, kb), :], vT_ref, k0, kb, qm_s, acc_s, m_s, s_s, p_s, bias)
        return 0

    n_full = (q0 + 1) // kb
    n_blk = (q0 + tq - 1) // kb + 1
    lax.fori_loop(0, n_full, lambda j, c: block(j, False), 0)
    lax.fori_loop(n_full, n_blk, lambda j, c: block(j, True), 0)
    _finish_heads(o_ref, acc_s)


def _att_scratch(tq, kb):
    return [pltpu.VMEM((ATT_HEADS, GROUP_W, tq), BF16), pltpu.VMEM((ATT_HEADS * VROWS, tq), F32),
            pltpu.VMEM((ATT_HEADS, 1, tq), F32),
            pltpu.VMEM((ATT_HEADS, kb, tq), F32), pltpu.VMEM((ATT_HEADS, kb, tq), BF16)]


def _with_ones_rows(vT):
    B, _, T = vT.shape
    v4 = vT.reshape(B, ATT_HEADS, ATT_HD, T)
    return jnp.concatenate([v4, jnp.ones((B, ATT_HEADS, ONES_ROWS, T), vT.dtype)], axis=2).reshape(
        B, ATT_HEADS * VROWS, T)


def _resident(shape):
    return pl.BlockSpec((None,) + shape, lambda b, i: (b,) + (0,) * len(shape), pipeline_mode=pl.Buffered(1))


def _fox(qT, k, vT, f2col, tq, kb):
    B, W, T = qT.shape
    colblk = pl.BlockSpec((None, W, tq), lambda b, i: (b, 0, i))
    return pl.pallas_call(
        functools.partial(_fox_kernel, tq=tq, kb=kb),
        grid=(B, T // tq),
        in_specs=[colblk, _resident((T, W)), _resident((ATT_HEADS * VROWS, T)), _resident((T, ATT_HEADS))],
        out_specs=colblk,
        out_shape=jax.ShapeDtypeStruct((B, W, T), F32),
        scratch_shapes=_att_scratch(tq, kb),
        compiler_params=_params(("arbitrary", "arbitrary")),
        name="fox",
    )(qT, k, vT, f2col)


def _score_stats_init(C):
    big8, zero8 = jnp.full((SUBLANES, C), BIG, F32), jnp.zeros((SUBLANES, C), F32)
    return -big8, big8, big8, zero8, zero8


def _score_stats(c, x):
    pos = x > 0.0
    return (jnp.maximum(c[0], _fold_rows(x, jnp.max)),
            jnp.minimum(c[1], _fold_rows(jnp.where(x > ADM_CUT, x, BIG), jnp.min)),
            jnp.minimum(c[2], _fold_rows(jnp.where(pos, x, BIG), jnp.min)),
            c[3] + _fold_rows(jnp.where(pos, 1.0, 0.0), jnp.sum),
            c[4] + _fold_rows(jnp.where(x == 0.0, 1.0, 0.0), jnp.sum))


def _topk_to_bias(S, nblk, kb, C, k, nadm, stats=None):
    kf = float(k)

    sb = LANES if kb % LANES == 0 else kb

    def reduce_blocks(fn, init):
        def body(j, c):
            k0 = pl.multiple_of(j * kb, kb)
            for r in range(0, kb, sb):
                c = fn(c, S[pl.ds(k0 + r, sb), :], k0 + r)
            return c
        return lax.fori_loop(0, nblk, body, init)

    fold_sum = lambda x: _fold_rows(x, jnp.sum)
    fold_max = lambda x: _fold_rows(x, jnp.max)
    fold_min = lambda x: _fold_rows(x, jnp.min)

    def count(pred):
        c = reduce_blocks(lambda c, x, k0: c + fold_sum(jnp.where(pred(x, k0), 1.0, 0.0)),
                          jnp.zeros((SUBLANES, C), F32))
        return jnp.sum(c, axis=0, keepdims=True)

    if stats is None:
        stats = reduce_blocks(lambda c, x, k0: _score_stats(c, x), _score_stats_init(C))
    mx, mn, mp, np8, nz8 = stats
    cmax = jnp.max(mx, axis=0, keepdims=True)
    cmin = jnp.min(mn, axis=0, keepdims=True)
    small = nadm <= kf
    cpos = jnp.sum(np8, axis=0, keepdims=True)
    cnn = cpos + jnp.sum(nz8, axis=0, keepdims=True)
    at_zero = (cpos < kf) & (cnn >= kf) & jnp.logical_not(small)
    settled = small | at_zero
    above = cpos >= kf
    lo0 = jnp.where(above, jnp.min(mp, axis=0, keepdims=True), cmin)
    clo0 = jnp.where(above, cpos, nadm)
    hi0 = jnp.where(above, cmax + jnp.maximum(1e-30, 1e-6 * jnp.abs(cmax)), 0.0)
    chi0 = jnp.where(above, 0.0, cnn)

    def any_col(flag):
        return jnp.max(jnp.where(flag, 1.0, 0.0)) > 0.5

    def probe(c, frac):
        lo, hi, clo, chi = c
        mid = lo + (hi - lo) * frac
        cm = count(lambda x, k0: x >= mid)
        ge = cm >= kf
        return jnp.where(ge, mid, lo), jnp.where(ge, hi, mid), jnp.where(ge, cm, clo), jnp.where(ge, chi, cm)

    def unresolved(clo, chi):
        g, r = clo - chi, kf - chi
        return (g > 2.5) & (r > 1.5) & (g - r > 0.5) & jnp.logical_not(settled)

    _, lo, hi, clo, chi = lax.while_loop(
        lambda c: (c[0] < BISECT_ITERS) & any_col(unresolved(c[3], c[4])),
        lambda c: (c[0] + 1,) + probe(c[1:], 0.5),
        (jnp.int32(0), lo0, hi0, clo0, chi0))

    top, bot = reduce_blocks(
        lambda c, x, k0: (jnp.maximum(c[0], fold_max(jnp.where(x < hi, x, -BIG))),
                          jnp.minimum(c[1], fold_min(jnp.where(x >= lo, x, BIG)))),
        (jnp.full((SUBLANES, C), -BIG, F32), jnp.full((SUBLANES, C), BIG, F32)))
    thr = jnp.where(kf - chi < 1.5, jnp.max(top, axis=0, keepdims=True), jnp.min(bot, axis=0, keepdims=True))
    thr = jnp.where(small, cmin, jnp.where(at_zero, 0.0, thr))
    cge = count(lambda x, k0: x >= thr)
    cgt = count(lambda x, k0: x > thr)
    missed = ((cgt >= kf) | (cge < kf)) & jnp.logical_not(settled)

    def slow(_):
        def open_(clo, chi):
            return (clo - chi > 1.5) & jnp.logical_not(settled)

        def search_body(c):
            it, lo, hi, clo, chi = c
            frac = jnp.clip((clo - kf + 0.5) / jnp.maximum(clo - chi, 1.0), 1.0 / 64, 63.0 / 64)
            return (it + 1,) + probe((lo, hi, clo, chi), jnp.where(it % 2 == 0, frac, 0.5))

        _, _, hi2, _, _ = lax.while_loop(lambda c: (c[0] < SEARCH_ITERS) & any_col(open_(c[3], c[4])), search_body,
                                         (jnp.int32(0), lo, hi, clo, chi))

        def unsat(cnt):
            return (cnt < kf) & jnp.logical_not(settled)

        def fix_body(c):
            t, cnt = c
            below = reduce_blocks(lambda m, x, k0: jnp.maximum(m, fold_max(jnp.where(x < t, x, -BIG))),
                                  jnp.full((SUBLANES, C), -BIG, F32))
            nt = jnp.where(unsat(cnt), jnp.max(below, axis=0, keepdims=True), t)
            return nt, count(lambda x, k0: x >= nt)

        t2, cge2 = lax.while_loop(lambda c: any_col(unsat(c[1])), fix_body, (hi2, jnp.zeros((1, C), F32)))
        cgt2 = count(lambda x, k0: x > t2)
        return jnp.where(missed, t2, thr), jnp.where(missed, cge2, cge), jnp.where(missed, cgt2, cgt)

    thr, cge, cgt = lax.cond(any_col(missed), slow, lambda _: (thr, cge, cgt), 0)
    need = kf - cgt
    ties = (cge > kf) & jnp.logical_not(small)

    def write_plain(_):
        def body(j, _):
            k0 = pl.multiple_of(j * kb, kb)
            S[pl.ds(k0, kb), :] = jnp.where(S[pl.ds(k0, kb), :] >= thr, 0.0, NEG_INF)
            return 0
        return lax.fori_loop(0, nblk, body, 0)

    def write_ties(_):
        tri = jnp.where(lax.broadcasted_iota(jnp.int32, (kb, kb), 0) >= lax.broadcasted_iota(jnp.int32, (kb, kb), 1),
                        1.0, 0.0).astype(BF16)

        def body(j, seen):
            k0 = pl.multiple_of(j * kb, kb)
            x = S[pl.ds(k0, kb), :]
            eq = x == thr
            rank = seen + _dot(tri, jnp.where(eq, 1.0, 0.0).astype(BF16))
            S[pl.ds(k0, kb), :] = jnp.where((x > thr) | (eq & (rank <= need)), 0.0, NEG_INF)
            return rank[kb - 1:kb, :]
        lax.fori_loop(0, nblk, body, jnp.zeros((1, C), F32))
        return 0

    lax.cond(any_col(ties), write_ties, write_plain, 0)


def _dsa_kernel(iqT_ref, ik_ref, smT_ref, qT_ref, k_ref, vT_ref, o_ref, S, qm_s, acc_s, m_s, s_s, p_s,
                *, tq, kb, topk):
    q0 = pl.program_id(1) * tq
    n_blk = (q0 + tq - 1) // kb + 1
    kio = lax.broadcasted_iota(jnp.int32, (kb, tq), 0)
    qpos = q0 + lax.broadcasted_iota(jnp.int32, (kb, tq), 1)
    wc = smT_ref[SM_IW:SM_IW + IDX_HEADS, :] * (IDX_W_SCALE * IDX_SCALE)

    def scores(j, stats):
        k0 = pl.multiple_of(j * kb, kb)
        kblk = ik_ref[pl.ds(k0, kb), :]
        sc = jnp.zeros((kb, tq), F32)
        for hh in range(IDX_HEADS):
            d = _dot(kblk, iqT_ref[hh * LANES:(hh + 1) * LANES, :])
            sc = sc + wc[hh:hh + 1, :] * jnp.maximum(d, 0.0)
        sc = jnp.where(k0 + kio <= qpos, sc, NEG_INF)
        S[pl.ds(k0, kb), :] = sc
        return _score_stats(stats, sc)

    stats = lax.fori_loop(0, n_blk, scores, _score_stats_init(tq))

    nadm = (q0 + 1 + lax.broadcasted_iota(jnp.int32, (1, tq), 1)).astype(F32)
    _topk_to_bias(S, n_blk, kb, tq, topk, nadm, stats)

    _init_heads(qT_ref, qm_s, acc_s, m_s)
    kcol =lax.broadcasted_iota(jnp.int32, (kb, 1), 0)

    def attend(j, _):
        k0 = pl.multiple_of(j * kb, kb)
        sel = S[pl.ds(k0, kb), :]
        kpos = (k0 + kcol).astype(F32)

        def bias(h, s):
            return s + (sel + (ALIBI[h] * LOG2E) * kpos)

        _attend_block(k_ref[pl.ds(k0, kb), :], vT_ref, k0, kb, qm_s, acc_s, m_s, s_s, p_s, bias)
        return 0

    lax.fori_loop(0, n_blk, attend, 0)
    _finish_heads(o_ref, acc_s)


def _dsa(iqT, ikp, smT, qT, k, vT, tq, kb, topk):
    B, W, T = qT.shape
    col = lambda r: pl.BlockSpec((None, r, tq), lambda b, i: (b, 0, i))
    return pl.pallas_call(
        functools.partial(_dsa_kernel, tq=tq, kb=kb, topk=topk),
        grid=(B, T // tq),
        in_specs=[col(IDX_HEADS * LANES), _resident((T, LANES)), col(LANES), col(W),
                  _resident((T, W)), _resident((ATT_HEADS * VROWS, T))],
        out_specs=col(W),
        out_shape=jax.ShapeDtypeStruct((B, W, T), F32),
        scratch_shapes=[pltpu.VMEM((T, tq), F32)] + _att_scratch(tq, kb),
        compiler_params=_params(("arbitrary", "arbitrary")),
        name="dsa",
    )(iqT, ikp, smT, qT, k, vT)


def _sa_kernel(pt_ref, q_ref, w_ref, cfn_ref, fb_ref, ikn_ref, *rest, pps, tn):
    pg_refs = rest[:pps]
    s_ref, f_ref, sn_ref, fn_ref, lfn_ref, carry_s = rest[pps:]
    step = pl.program_id(1)

    @pl.when(step == 0)
    def _():
        carry_s[...] = jnp.zeros_like(carry_s)

    qp = q_ref[...]
    q3 = qp[:, 0:3 * IDX_HD]
    w = w_ref[...] * IDX_W_SCALE
    nq = qp.shape[0] // IDX_HEADS

    def scores(kT):
        kh, kl = _split(kT)
        d = _dot(q3, jnp.concatenate([kh, kh, kl], axis=0))
        r = jnp.maximum(d * IDX_SCALE, 0.0) * w
        return jnp.sum(r.reshape(nq, IDX_HEADS, kT.shape[1]), axis=1)

    s_ref[...] = scores(jnp.concatenate([r[0:IDX_HD, :] for r in pg_refs], axis=1))
    cs_all = _lane_cumsum(jnp.concatenate([r[IDX_HD:IDX_HD + ATT_HEADS, :] for r in pg_refs], axis=0))
    carry = carry_s[...]
    for i in range(pps):
        cs = cs_all[i * ATT_HEADS:(i + 1) * ATT_HEADS, :] + carry
        f_ref[:, i * PAGE:(i + 1) * PAGE] = cs
        carry = cs[:, PAGE - 1:PAGE]
    carry_s[...] = carry

    @pl.when(step == pl.num_programs(1) - 1)
    def _():
        lane = lax.broadcasted_iota(jnp.int32, (ATT_HEADS, PAGE), 1)
        lfn = _log_sigmoid(cfn_ref[...] + fb_ref[...])
        lfn_ref[...] = lfn
        fn_ref[...] = _lane_cumsum(jnp.where(lane < tn, lfn, 0.0)) + carry
        col = lax.broadcasted_iota(jnp.int32, (nq, PAGE), 1)
        rowq = lax.broadcasted_iota(jnp.int32, (nq, PAGE), 0)
        sn_ref[...] = jnp.where((col <= rowq) & (col < tn), scores(ikn_ref[...]), NEG_INF)


def _sa(page_table, l, qrows, wrows, cfn, fb, ikn, idx_pool, pps, tn):
    B, n_pages = page_table.shape
    nsteps = n_pages // pps
    P = n_pages * PAGE
    nq = qrows.shape[1] // IDX_HEADS
    per_b = lambda shape: pl.BlockSpec((None,) + shape, lambda b, s, pt: (b,) + (0,) * len(shape))
    pg_specs = [pl.BlockSpec((None, None, IDX_HD + ATT_HEADS, PAGE),
                             lambda b, s, pt, i=i: (l, pt[b, s * pps + i], 0, 0)) for i in range(pps)]
    gs = pltpu.PrefetchScalarGridSpec(
        num_scalar_prefetch=1,
        grid=(B, nsteps),
        in_specs=[per_b((nq * IDX_HEADS, LANES)), per_b((nq * IDX_HEADS, 1)), per_b((ATT_HEADS, PAGE)),
                  pl.BlockSpec((ATT_HEADS, 1), lambda b, s, pt: (0, 0)), per_b((IDX_HD, PAGE))] + pg_specs,
        out_specs=[pl.BlockSpec((None, nq, pps * PAGE), lambda b, s, pt: (b, 0, s)),
                   pl.BlockSpec((None, ATT_HEADS, pps * PAGE), lambda b, s, pt: (b, 0, s)),
                   per_b((nq, PAGE)), per_b((ATT_HEADS, PAGE)), per_b((ATT_HEADS, PAGE))],
        scratch_shapes=[pltpu.VMEM((ATT_HEADS, 1), F32)])
    return pl.pallas_call(
        functools.partial(_sa_kernel, pps=pps, tn=tn),
        grid_spec=gs,
        out_shape=[jax.ShapeDtypeStruct((B, nq, P), F32), jax.ShapeDtypeStruct((B, ATT_HEADS, P), F32),
                   jax.ShapeDtypeStruct((B, nq, PAGE), F32), jax.ShapeDtypeStruct((B, ATT_HEADS, PAGE), F32),
                   jax.ShapeDtypeStruct((B, ATT_HEADS, PAGE), F32)],
        compiler_params=_params(("arbitrary", "arbitrary")),
        name="sample_scores",
    )(page_table, qrows, wrows, cfn, fb, ikn, *([idx_pool] * pps))


def _sb_kernel(s_ref, o_ref, *, kb, nblk, topk, past, nq):
    C = s_ref.shape[1]
    o_ref[...] = s_ref[...]
    nadm = (past + 1 + lax.broadcasted_iota(jnp.int32, (1, C), 1) % nq).astype(F32)
    _topk_to_bias(o_ref, nblk, kb, C, topk, nadm)


def _sb(s_allT, topk, past, nq):
    Wt, C = s_allT.shape
    nl = Wt // LANES
    div = max(d for d in range(1, 9) if nl % d == 0)
    kb = div * LANES
    return pl.pallas_call(
        functools.partial(_sb_kernel, kb=kb, nblk=Wt // kb, topk=topk, past=past, nq=nq),
        grid=(1,),
        in_specs=[pl.BlockSpec((Wt, C), lambda i: (0, 0))],
        out_specs=pl.BlockSpec((Wt, C), lambda i: (0, 0)),
        out_shape=jax.ShapeDtypeStruct((Wt, C), F32),
        compiler_params=_params(("arbitrary",)),
        name="sample_topk",
    )(s_allT)


def _sc_kernel(pt_ref, fq_ref, dq_ref, fqc_ref, fp_ref, bp_ref, fn_ref, bn_ref,
               ckn_ref, cvn_ref, dkn_ref, dvn_ref, *rest, pps, past, tn, nq):
    fk_refs, fv_refs = rest[0:pps], rest[pps:2 * pps]
    dk_refs, dv_refs = rest[2 * pps:3 * pps], rest[3 * pps:4 * pps]
    yc_ref, yd_ref, qf_s, qd_s, accf, mf, lf, accd, md, ld = rest[4 * pps:]
    step = pl.program_id(1)
    R = ATT_HEADS * SUBLANES

    def stack_heads(q8):
        return jnp.concatenate([jnp.where(_head_mask(h), q8, jnp.zeros_like(q8)) for h in range(ATT_HEADS)], axis=0)

    @pl.when(step == 0)
    def _():
        qf_s[...] = stack_heads(fq_ref[...])
        qd_s[...] = stack_heads(dq_ref[...])
        for acc, m, l in ((accf, mf, lf), (accd, md, ld)):
            acc[...] = jnp.zeros_like(acc)
            m[...] = jnp.full_like(m, M_INIT)
            l[...] = jnp.zeros_like(l)

    def update(q_s, kT, vT, bias, acc, m, l):
        s = _dot(q_s[...], kT) + bias * LOG2E
        m_prev = m[...]
        m_new = jnp.maximum(m_prev, jnp.max(s, axis=1, keepdims=True))
        p = jnp.exp2(s - m_new)
        alpha = jnp.exp2(m_prev - m_new)
        l[...] = alpha * l[...] + jnp.sum(p, axis=1, keepdims=True)
        m[...] = m_new
        acc[...] = alpha * acc[...] + _nt(p.astype(BF16), vT)

    def per_head_rows(x4):
        return jnp.concatenate([jnp.broadcast_to(x4[h:h + 1, :], (SUBLANES, x4.shape[1]))
                                for h in range(ATT_HEADS)], axis=0)

    def per_query_rows(xq):
        r8 = lax.broadcasted_iota(jnp.int32, (SUBLANES, xq.shape[1]), 0)
        x8 = jnp.zeros((SUBLANES, xq.shape[1]), F32)
        for qq in range(nq):
            x8 = jnp.where(r8 == qq, jnp.broadcast_to(xq[qq:qq + 1, :], x8.shape), x8)
        return jnp.concatenate([x8] * ATT_HEADS, axis=0)

    rowi = lax.broadcasted_iota(jnp.int32, (R, 1), 0)
    qidx = rowi % SUBLANES
    slope = jnp.zeros((R, 1), F32)
    for h in range(ATT_HEADS):
        slope = jnp.where(rowi // SUBLANES == h, ALIBI[h], slope)

    def slab(ref):
        return ref[...].reshape(GROUP_W, ref.shape[-1]).astype(BF16)

    def blocks(refs):
        return jnp.concatenate([slab(r) for r in refs], axis=1)

    kw = pps * PAGE
    col = lax.broadcasted_iota(jnp.int32, (R, kw), 1)
    update(qf_s, blocks(fk_refs), blocks(fv_refs), fqc_ref[...] - per_head_rows(fp_ref[...]), accf, mf, lf)
    dist = (past + qidx - (step * kw + col)).astype(F32)
    update(qd_s, blocks(dk_refs), blocks(dv_refs), per_query_rows(bp_ref[...]) - slope * dist, accd, md, ld)

    @pl.when(step == pl.num_programs(1) - 1)
    def _():
        coln = lax.broadcasted_iota(jnp.int32, (R, PAGE), 1)
        ok = (coln <= qidx) & (coln < tn)
        bias_f = jnp.where(ok, fqc_ref[...] - per_head_rows(fn_ref[...]), NEG_INF)
        update(qf_s, slab(ckn_ref), slab(cvn_ref), bias_f, accf, mf, lf)
        distn = (qidx - coln).astype(F32)
        bias_d = jnp.where(coln < tn, per_query_rows(bn_ref[...]), NEG_INF) - slope * distn
        update(qd_s, slab(dkn_ref), slab(dvn_ref), bias_d, accd, md, ld)
        for acc, l, out in ((accf, lf, yc_ref), (accd, ld, yd_ref)):
            o = acc[...] / l[...]
            y = jnp.zeros((SUBLANES, GROUP_W), F32)
            for h in range(ATT_HEADS):
                y = jnp.where(_head_mask(h), o[h * SUBLANES:(h + 1) * SUBLANES, :], y)
            out[...] = y


def _sc(page_table, l, fq8, dq8, fqc, f_past, b_past, f_new, b_new, ckn, cvn, dkn, dvn,
        fk_pool, fv_pool, dk_pool, dv_pool, pps, tn):
    B, n_pages = page_table.shape
    nsteps = n_pages // pps
    past = n_pages * PAGE
    nq = b_past.shape[1]
    W = GROUP_W
    per_b = lambda shape: pl.BlockSpec((None,) + shape, lambda b, s, pt: (b,) + (0,) * len(shape))
    page = lambda i: pl.BlockSpec((None, None, ATT_HEADS, ATT_HD, PAGE),
                                  lambda b, s, pt, i=i: (l, pt[b, s * pps + i], 0, 0, 0))
    pages = [page(i) for i in range(pps)]
    newpage = per_b((ATT_HEADS, ATT_HD, PAGE))
    R = ATT_HEADS * SUBLANES
    gs = pltpu.PrefetchScalarGridSpec(
        num_scalar_prefetch=1,
        grid=(B, nsteps),
        in_specs=[per_b((SUBLANES, W)), per_b((SUBLANES, W)), per_b((R, 1)),
                  pl.BlockSpec((None, ATT_HEADS, pps * PAGE), lambda b, s, pt: (b, 0, s)),
                  pl.BlockSpec((None, nq, pps * PAGE), lambda b, s, pt: (b, 0, s)),
                  per_b((ATT_HEADS, PAGE)), per_b((nq, PAGE)),
                  newpage, newpage, newpage, newpage] + pages * 4,
        out_specs=[per_b((SUBLANES, W)), per_b((SUBLANES, W))],
        scratch_shapes=[pltpu.VMEM((R, W), BF16), pltpu.VMEM((R, W), BF16),
                        pltpu.VMEM((R, W), F32), pltpu.VMEM((R, 1), F32), pltpu.VMEM((R, 1), F32),
                        pltpu.VMEM((R, W), F32), pltpu.VMEM((R, 1), F32), pltpu.VMEM((R, 1), F32)])
    return pl.pallas_call(
        functools.partial(_sc_kernel, pps=pps, past=past, tn=tn, nq=nq),
        grid_spec=gs,
        out_shape=[jax.ShapeDtypeStruct((B, SUBLANES, W), F32)] * 2,
        compiler_params=_params(("arbitrary", "arbitrary")),
        name="sample_attn",
    )(page_table, fq8, dq8, fqc, f_past, b_past, f_new, b_new, ckn, cvn, dkn, dvn,
      *([fk_pool] * pps), *([fv_pool] * pps), *([dk_pool] * pps), *([dv_pool] * pps))


def _post_kernel(x_ref, ya_ref, yb_ref, yc_ref, yd_ref, g1_ref, sh2_ref, sc2_ref, g2_ref,
                 gn_ref, wo_ref, npost_ref, nfpre_ref, wg_ref, wu_ref, wd_ref, nfpost_ref, o_ref, *, chunks):
    o = None
    for i, r in enumerate((ya_ref, yb_ref, yc_ref, yd_ref)):
        part = _rms(r[...], gn_ref[:, i * GROUP_W:(i + 1) * GROUP_W]).astype(BF16)
        d = _dot(part, wo_ref[i * GROUP_W:(i + 1) * GROUP_W, :])
        o = d if o is None else o + d
    x1 = x_ref[...] + g1_ref[...] * _rms(o, npost_ref[...])
    h2 = (_rms(x1, nfpre_ref[...]) * (1.0 + sc2_ref[...]) + sh2_ref[...]).astype(BF16)
    f = None
    for c0, c1 in chunks:
        gate = _dot(h2, wg_ref[:, c0:c1])
        up = _dot(h2, wu_ref[:, c0:c1])
        d = _dot((gate * _sigmoid(gate) * up).astype(BF16), wd_ref[c0:c1, :])
        f = d if f is None else f + d
    o_ref[...] = x1 + g2_ref[...] * _rms(f, nfpost_ref[...])


def _post(x, ya, yb, yc, yd, mod, tiles_per_mod, pw, tm):
    N, D = x.shape
    H = pw["wg"].shape[1]
    rows_mod = mod.shape[1]
    step = 1024
    chunks = tuple((c, min(c + step, H)) for c in range(0, H, step))
    tok = lambda w: pl.BlockSpec((tm, w), lambda i: (i, 0))
    const = lambda shape: pl.BlockSpec(shape, lambda i: (0,) * len(shape), pipeline_mode=pl.Buffered(1))
    return pl.pallas_call(
        functools.partial(_post_kernel, chunks=chunks),
        grid=(N // tm,),
        in_specs=[tok(D), tok(GROUP_W), tok(GROUP_W), tok(GROUP_W), tok(GROUP_W),
                  _mod_spec(rows_mod, D, tiles_per_mod, 2), _mod_spec(rows_mod, D, tiles_per_mod, 3),
                  _mod_spec(rows_mod, D, tiles_per_mod, 4), _mod_spec(rows_mod, D, tiles_per_mod, 5),
                  const((1, D)), const((D, D)), const((1, D)), const((1, D)),
                  const((D, H)), const((D, H)), const((H, D)), const((1, D))],
        out_specs=tok(D),
        out_shape=jax.ShapeDtypeStruct((N, D), F32),
        compiler_params=_params(("arbitrary",)),
        name="post",
    )(x, ya, yb, yc, yd, mod, mod, mod, mod, pw["gn"], pw["wo"], pw["npost"], pw["nfpre"],
      pw["wg"], pw["wu"], pw["wd"], pw["nfpost"])


def _block_diag(blocks):
    G, r, c = blocks.shape
    eye = jnp.eye(G, dtype=blocks.dtype)
    return (blocks[:, :, None, :] * eye[:, None, :, None]).reshape(G * r, G * c)


def _cat_weight(w):
    D = w.shape[0]
    widths = (GROUP_W, GROUP_W, GROUP_W, GROUP_W, GROUP_W, GROUP_W, ATT_HEADS, GROUP_W, GROUP_W, GROUP_W,
              IDX_HEADS * IDX_HD, IDX_HD, IDX_HEADS)
    offs = [0]
    for wd in widths:
        offs.append(offs[-1] + wd)
    pc = [w[:, offs[i]:offs[i + 1]] for i in range(len(widths))]
    a_x, a_g, b_u, c_q, c_k, c_v, c_f, d_q, d_k, d_v, i_q, i_k, i_w = pc
    iq_rep = jnp.tile(i_q.reshape(D, IDX_HEADS, 1, IDX_HD), (1, 1, LANES // IDX_HD, 1)).reshape(D, IDX_HEADS * LANES)
    ik_rep = jnp.tile(i_k, (1, LANES // IDX_HD))
    small = jnp.concatenate([i_k, c_f, i_w, jnp.zeros((D, LANES - IDX_HD - ATT_HEADS - IDX_HEADS), w.dtype)], axis=1)
    wcat = jnp.concatenate([a_x, a_g, b_u, c_q, c_k, c_v, d_q, d_k, d_v], axis=1).astype(BF16)
    widx = jnp.concatenate([iq_rep, ik_rep, small], axis=1)
    hi = lax.reduce_precision(widx, exponent_bits=8, mantissa_bits=7)
    hi_b = hi.astype(BF16)
    return wcat, jnp.concatenate([hi_b, hi_b, (widx - hi).astype(BF16)], axis=0)


def _layer_weights(l, p):
    row = lambda a: a[l].reshape(1, -1)
    lw = dict(conv_w=p["lru_conv_w"][l], conv_b=row(p["lru_conv_b"]),
              wr=_block_diag(p["lru_wr"][l]).astype(BF16), br=row(p["lru_br"]),
              wi=_block_diag(p["lru_wi"][l]).astype(BF16), bi=row(p["lru_bi"]), lam=row(p["lru_lambda"]))
    sw = dict(ldt=jnp.repeat(p["s5_log_dt"][l], S5_N).reshape(1, S5_STATE),
              are=p["s5_a_re"][l].reshape(1, S5_STATE), aim=p["s5_a_im"][l].reshape(1, S5_STATE),
              bre=_block_diag(jnp.swapaxes(p["s5_b_re"][l], 1, 2)).astype(BF16),
              bim=_block_diag(jnp.swapaxes(p["s5_b_im"][l], 1, 2)).astype(BF16),
              cre=_block_diag(jnp.swapaxes(p["s5_c_re"][l], 1, 2)).astype(BF16),
              cim=_block_diag(jnp.swapaxes(p["s5_c_im"][l], 1, 2)).astype(BF16),
              d=row(p["s5_d"]), gw=p["s5_glu_w"][l].astype(BF16), gb=row(p["s5_glu_b"]))
    pw = dict(gn=row(p["grp_norm"]), wo=p["w_out"][l].astype(BF16), npost=row(p["norm_mix_post"]),
              nfpre=row(p["norm_ffn_pre"]), wg=p["ffn_w_gate"][l].astype(BF16), wu=p["ffn_w_up"][l].astype(BF16),
              wd=p["ffn_w_down"][l].astype(BF16), nfpost=row(p["norm_ffn_post"]))
    return dict(win=_cat_weight(p["w_in"][l]), npre=row(p["norm_mix_pre"]), lru=lw, s5=sw, post=pw,
                fbias=p["fox_f_bias"][l])


def _largest_tile(n, cap):
    t = min(n, cap)
    while n % t:
        t //= 2
    return t


def _recurrent(ax, ag, bu, lru_buf, lru_h0, s5_r0, s5_i0, wts, t_real):
    B, Tp, _ = ax.shape
    tc = _largest_tile(Tp, 256)
    buf8 = jnp.pad(lru_buf, ((0, 0), (SUBLANES - (CONV_W - 1), 0), (0, 0)))
    ya, lru_h = _lru(ax, ag, buf8, lru_h0.reshape(B, 1, GROUP_W), wts["lru"], tc, t_real)
    yb, s5r, s5i = _s5(bu, s5_r0.reshape(B, 1, S5_STATE), s5_i0.reshape(B, 1, S5_STATE), wts["s5"], tc, t_real)
    return ya, yb, lru_h.reshape(B, GROUP_W), s5r.reshape(B, S5_NG, S5_N), s5i.reshape(B, S5_NG, S5_N)


def _prompt_layer(x, mod, wts, B, T):
    N, D = x.shape
    tm = _largest_tile(T, 512)
    (ax, ag, bu, cq, ckf, ckb, cvf, cvb, dq, dkf, dkb, dvf, dvb, iqp, ikp, sm) = _inproj(
        x, mod, T // tm, wts["npre"], *wts["win"], tm)
    seq = lambda a: a.reshape(B, T, a.shape[-1])
    zeros = lambda *s: jnp.zeros(s, F32)
    ax3 = seq(ax)
    ya, yb, lru_h, s5r, s5i = _recurrent(ax3, seq(ag), seq(bu), zeros(B, CONV_W - 1, GROUP_W), zeros(B, GROUP_W),
                                         zeros(B, S5_NG, S5_N), zeros(B, S5_NG, S5_N), wts, T)
    nb = T // LANES
    cf_rows = jnp.swapaxes(seq(sm)[:, :, SM_CF:SM_CF + ATT_HEADS], 1, 2).reshape(B, ATT_HEADS * nb, LANES)
    fb_rows = jnp.repeat(wts["fbias"], nb).reshape(ATT_HEADS * nb, 1)
    lf_rows, f_rows = _fcum(cf_rows, fb_rows, nb)
    f2_col = jnp.swapaxes(f_rows.reshape(B, ATT_HEADS, T), 1, 2)
    logf = jnp.swapaxes(lf_rows.reshape(B, ATT_HEADS, T), 1, 2)
    tr = lambda a: jnp.swapaxes(seq(a), 1, 2)
    tq = kb = _largest_tile(T, 512)
    yc = jnp.swapaxes(_fox(tr(cq), seq(ckb), _with_ones_rows(tr(cvb)), f2_col, tq, kb), 1, 2)
    topk = max(1, min(DSA_TOPK_MAX, T // 4))
    yd = jnp.swapaxes(_dsa(tr(iqp), seq(ikp), tr(sm), tr(dq), seq(dkb), _with_ones_rows(tr(dvb)), tq, kb, topk),
                      1, 2)
    flat = lambda a: a.reshape(N, GROUP_W)
    x2 = _post(x, flat(ya), flat(yb), flat(yc), flat(yd), mod, T // tm, wts["post"], tm)
    heads = lambda a: a.reshape(B, T, ATT_HEADS, ATT_HD)
    state = dict(lru_h=lru_h, lru_conv=ax3[:, T - (CONV_W - 1):, :], s5_re=s5r, s5_im=s5i,
                 fox_k=heads(ckf), fox_v=heads(cvf), fox_logf=logf, dsa_k=heads(dkf), dsa_v=heads(dvf),
                 dsa_kidx=seq(sm)[:, :, SM_IK:SM_IK + IDX_HD])
    return x2, state


def _sample_layer(x, mod, wts, l, B, T, past, caches, page_table):
    N, D = x.shape
    (ax, ag, bu, cq, ckf, ckb, cvf, cvb, dq, dkf, dkb, dvf, dvb, iqp, ikp, sm) = _inproj(
        x, mod, 1, wts["npre"], *wts["win"], N)
    seq = lambda a: a.reshape(B, T, a.shape[-1])
    tp = -(-T // SUBLANES) * SUBLANES
    padt = lambda a, n=tp: jnp.pad(seq(a), ((0, 0), (0, n - T), (0, 0)))
    ax3 = seq(ax)
    ya, yb, lru_h, s5r, s5i = _recurrent(padt(ax), padt(ag), padt(bu), past["lru_conv"], past["lru_h"],
                                         past["s5_re"], past["s5_im"], wts, T)
    sm3 = seq(sm)
    n_pages = page_table.shape[1]
    plen = n_pages * PAGE
    pps = _largest_tile(n_pages, 16)
    iq_rows = iqp.reshape(B, T * IDX_HEADS, LANES)
    wrows = sm3[:, :, SM_IW:SM_IW + IDX_HEADS].reshape(B, T * IDX_HEADS, 1)
    cfn = jnp.pad(jnp.swapaxes(sm3[:, :, SM_CF:SM_CF + ATT_HEADS], 1, 2), ((0, 0), (0, 0), (0, PAGE - T)))
    ikn = jnp.pad(jnp.swapaxes(sm3[:, :, SM_IK:SM_IK + IDX_HD], 1, 2), ((0, 0), (0, 0), (0, PAGE - T)))
    s_past, f_past, s_new, f_new, lf_new = _sa(page_table, l, iq_rows, wrows, cfn,
                                               wts["fbias"].reshape(ATT_HEADS, 1), ikn, caches["idx"],
                                               _largest_tile(n_pages, 64), T)
    topk = max(1, min(DSA_TOPK_MAX, (plen + T) // 4))
    s_allT = jnp.transpose(jnp.concatenate([s_past, s_new], axis=2), (2, 0, 1)).reshape(plen + PAGE, B * T)
    bias_all = jnp.transpose(_sb(s_allT, topk, plen, T).reshape(plen + PAGE, B, T), (1, 2, 0))
    fqc = jnp.pad(f_new[:, :, :T], ((0, 0), (0, 0), (0, SUBLANES - T))).reshape(B, ATT_HEADS * SUBLANES, 1)
    newpage = lambda a: jnp.pad(jnp.transpose(a.reshape(B, T, ATT_HEADS, ATT_HD), (0, 2, 3, 1)),
                                ((0, 0), (0, 0), (0, 0), (0, PAGE - T)))
    yc8, yd8 = _sc(page_table, l, padt(cq, SUBLANES), padt(dq, SUBLANES), fqc, f_past, bias_all[:, :, :plen],
                   f_new, bias_all[:, :, plen:], newpage(ckb), newpage(cvb), newpage(dkb), newpage(dvb),
                   caches["fox_k"], caches["fox_v"], caches["dsa_k"], caches["dsa_v"], pps, T)
    flat = lambda a: a[:, :T, :].reshape(N, GROUP_W)
    x2 = _post(x, flat(ya), flat(yb), flat(yc8), flat(yd8), mod, 1, wts["post"], N)
    heads = lambda a: a.reshape(B, T, ATT_HEADS, ATT_HD)
    state = dict(lru_h=lru_h, lru_conv=ax3[:, T - (CONV_W - 1):, :], s5_re=s5r, s5_im=s5i,
                 fox_k=heads(ckf), fox_v=heads(cvf), fox_logf=jnp.swapaxes(lf_new[:, :, :T], 1, 2),
                 dsa_k=heads(dkf), dsa_v=heads(dvf), dsa_kidx=sm3[:, :, SM_IK:SM_IK + IDX_HD])
    return x2, state


def kernel(x_prompt, x_sample, state_lru_h, state_lru_conv, state_s5_re, state_s5_im, cache_fox_k, cache_fox_v, cache_fox_logf, cache_dsa_k, cache_dsa_v, cache_dsa_kidx, page_table, c_prompt, c_sample, ada_w, ada_b, norm_mix_pre, norm_mix_post, norm_ffn_pre, norm_ffn_post, w_in, lru_conv_w, lru_conv_b, lru_wr, lru_br, lru_wi, lru_bi, lru_lambda, s5_log_dt, s5_a_re, s5_a_im, s5_b_re, s5_b_im, s5_c_re, s5_c_im, s5_d, s5_glu_w, s5_glu_b, fox_f_bias, grp_norm, w_out, ffn_w_gate, ffn_w_up, ffn_w_down):
    p = dict(norm_mix_pre=norm_mix_pre, norm_mix_post=norm_mix_post, norm_ffn_pre=norm_ffn_pre,
             norm_ffn_post=norm_ffn_post, w_in=w_in, lru_conv_w=lru_conv_w, lru_conv_b=lru_conv_b, lru_wr=lru_wr,
             lru_br=lru_br, lru_wi=lru_wi, lru_bi=lru_bi, lru_lambda=lru_lambda, s5_log_dt=s5_log_dt,
             s5_a_re=s5_a_re, s5_a_im=s5_a_im, s5_b_re=s5_b_re, s5_b_im=s5_b_im, s5_c_re=s5_c_re, s5_c_im=s5_c_im,
             s5_d=s5_d, s5_glu_w=s5_glu_w, s5_glu_b=s5_glu_b, fox_f_bias=fox_f_bias, grp_norm=grp_norm,
             w_out=w_out, ffn_w_gate=ffn_w_gate, ffn_w_up=ffn_w_up, ffn_w_down=ffn_w_down)
    B, T, D = x_prompt.shape
    Bs, Ts, _ = x_sample.shape
    L = ada_w.shape[0]
    n_pool = cache_fox_k.shape[1]
    assert T % LANES == 0 and CONV_W - 1 <= Ts <= SUBLANES and cache_fox_k.shape[2] == PAGE

    c_all = jnp.concatenate([c_prompt, c_sample], axis=0)
    c_all = jnp.pad(c_all, ((0, -c_all.shape[0] % SUBLANES), (0, 0)))
    mod_all = _ada(c_all, ada_w, ada_b)

    pool = lambda a: jnp.transpose(a, (0, 1, 3, 4, 2))
    caches = dict(fox_k=pool(cache_fox_k), fox_v=pool(cache_fox_v), dsa_k=pool(cache_dsa_k), dsa_v=pool(cache_dsa_v),
                  idx=jnp.concatenate([jnp.swapaxes(cache_dsa_kidx, 2, 3), jnp.swapaxes(cache_fox_logf, 2, 3)], axis=2))

    xp = x_prompt.reshape(B * T, D)
    xs = x_sample.reshape(Bs * Ts, D)
    st_p, st_s = [], []
    for l in range(L):
        wts = _layer_weights(l, p)
        mod_p = mod_all[l, :B].reshape(B, 1, 6 * D)
        mod_s = jnp.repeat(mod_all[l, B:B + Bs], Ts, axis=0).reshape(1, Bs * Ts, 6 * D)
        xp, new_p = _prompt_layer(xp, mod_p, wts, B, T)
        past = dict(lru_h=state_lru_h[l], lru_conv=state_lru_conv[l], s5_re=state_s5_re[l], s5_im=state_s5_im[l])
        xs, new_s = _sample_layer(xs, mod_s, wts, l, Bs, Ts, past, caches, page_table)
        st_p.append(new_p)
        st_s.append(new_s)

    stk = lambda outs, name: jnp.stack([o[name] for o in outs])
    names = ("lru_h", "lru_conv", "s5_re", "s5_im", "fox_k", "fox_v", "fox_logf", "dsa_k", "dsa_v", "dsa_kidx")
    res = [xp.reshape(B, T, D), xs.reshape(Bs, Ts, D)]
    for name in names:
        res += [stk(st_p, name), stk(st_s, name)]
    return tuple(res)
```

```python
import functools
import math

import jax
import jax.numpy as jnp
from jax import lax
from jax.experimental import pallas as pl
from jax.experimental.pallas import tpu as pltpu

F32 = jnp.float32
BF16 = jnp.bfloat16

GROUP_W = 256
CONV_W = 4
LRU_C = 8.0
S5_NG = 16
S5_N = 64
S5_STATE = S5_NG * S5_N
ATT_HEADS = 4
ATT_HD = 64
IDX_HEADS = 8
IDX_HD = 32
DSA_TOPK_MAX = 256
PAGE = 128
NEG_INF = -1e30
M_INIT = -1e20
BIG = 3e38
ADM_CUT = -1e29
EPS = 1e-6
ATT_SCALE = ATT_HD ** -0.5
IDX_SCALE = IDX_HD ** -0.5
IDX_W_SCALE = IDX_HEADS ** -0.5
ALIBI = tuple(2.0 ** (-8.0 * (h + 1) / ATT_HEADS) for h in range(ATT_HEADS))
LOG2E = math.log2(math.e)
Q_SCALE = ATT_SCALE * LOG2E
BISECT_ITERS = 28
SEARCH_ITERS = 48

LANES = 128
SUBLANES = 8
VMEM_LIMIT = 56 * 1024 * 1024

C_AX, C_AG, C_BU, C_CQ, C_CK, C_CV, C_DQ, C_DK, C_DV = (i * GROUP_W for i in range(9))
P_CAT = 9 * GROUP_W
C_IQ = 0
C_IK = C_IQ + IDX_HEADS * LANES
C_SM = C_IK + LANES
P_IDX = C_SM + LANES
SM_IK, SM_CF, SM_IW = 0, IDX_HD, IDX_HD + ATT_HEADS

NT_DIMS = (((1,), (1,)), ((), ()))


def _nt(a, b):
    return lax.dot_general(a, b, NT_DIMS, preferred_element_type=F32)


def _dot(a, b):
    return jnp.dot(a, b, preferred_element_type=F32)


def _rms(x, g):
    return x * lax.rsqrt(jnp.mean(x * x, axis=-1, keepdims=True) + EPS) * g


def _softplus(z):
    return jnp.maximum(z, 0.0) + jnp.log1p(jnp.exp(-jnp.abs(z)))


def _log_sigmoid(z):
    return -_softplus(-z)


def _expm1(z):
    e = jnp.exp(z)
    one = e == 1.0
    return jnp.where(one, z, (e - 1.0) * z / jnp.where(one, 1.0, jnp.log(e)))


def _sigmoid(z):
    return jax.nn.sigmoid(z)


def _gelu(z):
    return jax.nn.gelu(z)


def _params(sem):
    return pltpu.CompilerParams(dimension_semantics=sem, vmem_limit_bytes=VMEM_LIMIT)


def _head_mask(h, width=GROUP_W):
    lane = lax.broadcasted_iota(jnp.int32, (1, width), 1)
    return (lane >= h * ATT_HD) & (lane < (h + 1) * ATT_HD)


FOLD_CHAINS = 4


def _fold_rows(x, axis_op):
    n, c = x.shape
    g = FOLD_CHAINS if n % (FOLD_CHAINS * SUBLANES) == 0 else 1
    x = axis_op(x.reshape(g, n // (g * SUBLANES), SUBLANES, c), axis=1)
    return axis_op(x, axis=0)


def _col_max(x):
    return jnp.max(_fold_rows(x, jnp.max), axis=0, keepdims=True)


def _lane_cumsum(x):
    lane = lax.broadcasted_iota(jnp.int32, x.shape, 1)
    s = 1
    while s < x.shape[1]:
        x = x + jnp.where(lane >= s, pltpu.roll(x, s, axis=1), 0.0)
        s *= 2
    return x


def _split(x):
    hi = x.astype(BF16)
    return hi, (x - hi.astype(F32)).astype(BF16)


def _ada_kernel(c_ref, w_ref, b_ref, o_ref):
    c = c_ref[...]
    sh, sl = _split(c * _sigmoid(c))
    wh, wl = _split(w_ref[...])
    o_ref[...] = (_dot(sl, wl) + _dot(sl, wh) + _dot(sh, wl)) + _dot(sh, wh) + b_ref[...]


def _ada(c_all, ada_w, ada_b):
    L, D, D6 = ada_w.shape
    Bp = c_all.shape[0]
    tn = 1024
    return pl.pallas_call(
        _ada_kernel,
        grid=(L, D6 // tn),
        in_specs=[pl.BlockSpec((Bp, D), lambda l, j: (0, 0)),
                  pl.BlockSpec((None, D, tn), lambda l, j: (l, 0, j)),
                  pl.BlockSpec((None, 1, tn), lambda l, j: (l, 0, j))],
        out_specs=pl.BlockSpec((None, Bp, tn), lambda l, j: (l, 0, j)),
        out_shape=jax.ShapeDtypeStruct((L, Bp, D6), F32),
        compiler_params=_params(("arbitrary", "arbitrary")),
        name="ada",
    )(c_all, ada_w, ada_b.reshape(L, 1, D6))


def _inproj_kernel(x_ref, sh_ref, sc_ref, g_ref, w_ref, w3_ref,
                   ax_ref, ag_ref, bu_ref, cq_ref, ckf_ref, ckb_ref, cvf_ref, cvb_ref,
                   dq_ref, dkf_ref, dkb_ref, dvf_ref, dvb_ref, iq_ref, ik_ref, sm_ref):
    x = x_ref[...]
    h = _rms(x, g_ref[...]) * (1.0 + sc_ref[...]) + sh_ref[...]
    hb = h.astype(BF16)

    main = _dot(hb, w_ref[...])

    def mm(c0, width):
        return main[:, c0:c0 + width]

    ax_ref[...] = mm(C_AX, GROUP_W)
    ag_ref[...] = mm(C_AG, GROUP_W)
    bu_ref[...] = mm(C_BU, GROUP_W)
    cq_ref[...] = (mm(C_CQ, GROUP_W) * Q_SCALE).astype(BF16)
    ck = mm(C_CK, GROUP_W)
    ckf_ref[...] = ck
    ckb_ref[...] = ck.astype(BF16)
    cv = mm(C_CV, GROUP_W)
    cvf_ref[...] = cv
    cvb_ref[...] = cv.astype(BF16)
    dq_ref[...] = (mm(C_DQ, GROUP_W) * Q_SCALE).astype(BF16)
    dk = mm(C_DK, GROUP_W)
    dkf_ref[...] = dk
    dkb_ref[...] = dk.astype(BF16)
    dv = mm(C_DV, GROUP_W)
    dvf_ref[...] = dv
    dvb_ref[...] = dv.astype(BF16)

    hl = (h - hb.astype(F32)).astype(BF16)
    idx = _dot(jnp.concatenate([hb, hl, hb], axis=1), w3_ref[...])

    def mm3(c0, width):
        return idx[:, c0:c0 + width]

    sm_ref[...] = mm3(C_SM, LANES)

    sub = (lax.broadcasted_iota(jnp.int32, (1, LANES), 1) // IDX_HD)
    for hh in range(IDX_HEADS):
        r = mm3(C_IQ + hh * LANES, LANES)
        hi = r.astype(BF16)
        lo = (r - hi.astype(F32)).astype(BF16)
        zero = jnp.zeros_like(hi)
        iq_ref[:, hh * LANES:(hh + 1) * LANES] = jnp.where(sub == 1, lo, jnp.where(sub == 3, zero, hi))
    r = mm3(C_IK, LANES)
    hi = r.astype(BF16)
    lo = (r - hi.astype(F32)).astype(BF16)
    ik_ref[...] = jnp.where(sub == 2, lo, jnp.where(sub == 3, jnp.zeros_like(hi), hi))


def _mod_spec(rows_mod, D, tiles_per_mod, piece):
    return pl.BlockSpec((None, rows_mod, D), lambda i: (i // tiles_per_mod, 0, piece))


def _inproj(x, mod, tiles_per_mod, g, wcat, widx3, tm):
    N, D = x.shape
    rows_mod = mod.shape[1]
    tok = lambda w: pl.BlockSpec((tm, w), lambda i: (i, 0))
    widths_dtypes = [(GROUP_W, F32)] * 3 + [(GROUP_W, BF16), (GROUP_W, F32), (GROUP_W, BF16), (GROUP_W, F32),
                                            (GROUP_W, BF16), (GROUP_W, BF16), (GROUP_W, F32), (GROUP_W, BF16),
                                            (GROUP_W, F32), (GROUP_W, BF16), (IDX_HEADS * LANES, BF16),
                                            (LANES, BF16), (LANES, F32)]
    return pl.pallas_call(
        _inproj_kernel,
        grid=(N // tm,),
        in_specs=[tok(D), _mod_spec(rows_mod, D, tiles_per_mod, 0), _mod_spec(rows_mod, D, tiles_per_mod, 1),
                  pl.BlockSpec((1, D), lambda i: (0, 0)),
                  pl.BlockSpec((D, P_CAT), lambda i: (0, 0), pipeline_mode=pl.Buffered(1)),
                  pl.BlockSpec((3 * D, P_IDX), lambda i: (0, 0), pipeline_mode=pl.Buffered(1))],
        out_specs=[tok(w) for w, _ in widths_dtypes],
        out_shape=[jax.ShapeDtypeStruct((N, w), dt) for w, dt in widths_dtypes],
        compiler_params=_params(("arbitrary",)),
        name="inproj",
    )(x, mod, mod, g, wcat, widx3)


def _lru_kernel(ax_ref, ag_ref, buf_ref, h0_ref, cw_ref, cb_ref, wr_ref, br_ref, wi_ref, bi_ref, lam_ref,
                ya_ref, hl_ref, xs, hcar, *, tc, last_row):
    c = pl.program_id(1)

    @pl.when(c == 0)
    def _():
        xs[0:SUBLANES, :] = buf_ref[...]
        hcar[...] = h0_ref[...]

    x = ax_ref[...]
    xs[SUBLANES:SUBLANES + tc, :] = x
    w = cw_ref[...]
    y = cb_ref[...] + xs[5:5 + tc, :] * w[0:1] + xs[6:6 + tc, :] * w[1:2] + xs[7:7 + tc, :] * w[2:3] + x * w[3:4]
    xs[0:SUBLANES, :] = xs[tc:tc + SUBLANES, :]

    yb = y.astype(BF16)
    r = _sigmoid(_dot(yb, wr_ref[...]) + br_ref[...])
    i = _sigmoid(_dot(yb, wi_ref[...]) + bi_ref[...])
    log_a = -LRU_C * r * _softplus(-lam_ref[...])
    a = jnp.exp(log_a)
    b = jnp.sqrt(-_expm1(2.0 * log_a)) * (i * y)

    row = lax.broadcasted_iota(jnp.int32, (tc, GROUP_W), 0)
    s = 1
    while s < tc:
        keep = row >= s
        a_sh = jnp.where(keep, pltpu.roll(a, s, axis=0), 1.0)
        b_sh = jnp.where(keep, pltpu.roll(b, s, axis=0), 0.0)
        b = b + a * b_sh
        a = a * a_sh
        s *= 2
    h = b + a * hcar[...]
    hcar[...] = h[tc - 1:tc, :]
    ya_ref[...] = h * _gelu(ag_ref[...])

    @pl.when(c == pl.num_programs(1) - 1)
    def _():
        hl_ref[...] = h[last_row:last_row + 1, :]


def _lru(ax, ag, buf8, h0, lw, tc, t_real):
    B, Tp, W = ax.shape
    seq = pl.BlockSpec((None, tc, W), lambda b, c: (b, c, 0))
    full = lambda shape: pl.BlockSpec(shape, lambda b, c: (0,) * len(shape))
    return pl.pallas_call(
        functools.partial(_lru_kernel, tc=tc, last_row=(t_real - 1) % tc),
        grid=(B, Tp // tc),
        in_specs=[seq, seq,
                  pl.BlockSpec((None, SUBLANES, W), lambda b, c: (b, 0, 0)),
                  pl.BlockSpec((None, 1, W), lambda b, c: (b, 0, 0)),
                  full((CONV_W, W)), full((1, W)), full((W, W)), full((1, W)), full((W, W)), full((1, W)),
                  full((1, W))],
        out_specs=[seq, pl.BlockSpec((None, 1, W), lambda b, c: (b, 0, 0))],
        out_shape=[jax.ShapeDtypeStruct((B, Tp, W), F32), jax.ShapeDtypeStruct((B, 1, W), F32)],
        scratch_shapes=[pltpu.VMEM((tc + SUBLANES, W), F32), pltpu.VMEM((1, W), F32)],
        compiler_params=_params(("arbitrary", "arbitrary")),
        name="lru",
    )(ax, ag, buf8, h0, lw["conv_w"], lw["conv_b"], lw["wr"], lw["br"], lw["wi"], lw["bi"], lw["lam"])


def _s5_kernel(u_ref, h0r_ref, h0i_ref, ldt_ref, are_ref, aim_ref, bre_ref, bim_ref, cre_ref, cim_ref,
               d_ref, gw_ref, gb_ref, y_ref, hlr_ref, hli_ref, hr_s, hi_s, car_r, car_i, *, tc, last_row):
    c = pl.program_id(1)

    @pl.when(c == 0)
    def _():
        car_r[...] = h0r_ref[...]
        car_i[...] = h0i_ref[...]

    dt = jnp.exp(ldt_ref[...])
    ar, ai = are_ref[...], aim_ref[...]
    mag = jnp.exp(dt * ar)
    abr, abi = mag * jnp.cos(dt * ai), mag * jnp.sin(dt * ai)
    den = ar * ar + ai * ai
    nr, ni = abr - 1.0, abi
    cr = (nr * ar + ni * ai) / den
    ci = (ni * ar - nr * ai) / den

    u = u_ref[...]
    ub = u.astype(BF16)
    pre = _dot(ub, bre_ref[...])
    pim = _dot(ub, bim_ref[...])
    hr_s[...] = cr * pre - ci * pim
    hi_s[...] = cr * pim + ci * pre

    def step(t, carry):
        hr, hi = carry
        nhr = abr * hr - abi * hi + hr_s[pl.ds(t, 1), :]
        nhi = abr * hi + abi * hr + hi_s[pl.ds(t, 1), :]
        hr_s[pl.ds(t, 1), :] = nhr
        hi_s[pl.ds(t, 1), :] = nhi
        return nhr, nhi

    hr, hi = lax.fori_loop(0, tc, step, (car_r[...], car_i[...]), unroll=8)
    car_r[...] = hr
    car_i[...] = hi

    hrv, hiv = hr_s[...], hi_s[...]
    y = _dot(hrv.astype(BF16), cre_ref[...]) - _dot(hiv.astype(BF16), cim_ref[...]) + d_ref[...] * u
    g = _gelu(y)
    y_ref[...] = g * _sigmoid(_dot(g.astype(BF16), gw_ref[...]) + gb_ref[...])

    @pl.when(c == pl.num_programs(1) - 1)
    def _():
        hlr_ref[...] = hr_s[last_row:last_row + 1, :]
        hli_ref[...] = hi_s[last_row:last_row + 1, :]


def _s5(u, h0r, h0i, sw, tc, t_real):
    B, Tp, W = u.shape
    S = S5_STATE
    seq = pl.BlockSpec((None, tc, W), lambda b, c: (b, c, 0))
    st = pl.BlockSpec((None, 1, S), lambda b, c: (b, 0, 0))
    full = lambda shape: pl.BlockSpec(shape, lambda b, c: (0,) * len(shape))
    return pl.pallas_call(
        functools.partial(_s5_kernel, tc=tc, last_row=(t_real - 1) % tc),
        grid=(B, Tp // tc),
        in_specs=[seq, st, st, full((1, S)), full((1, S)), full((1, S)), full((W, S)), full((W, S)),
                  full((S, W)), full((S, W)), full((1, W)), full((W, W)), full((1, W))],
        out_specs=[seq, st, st],
        out_shape=[jax.ShapeDtypeStruct((B, Tp, W), F32), jax.ShapeDtypeStruct((B, 1, S), F32),
                   jax.ShapeDtypeStruct((B, 1, S), F32)],
        scratch_shapes=[pltpu.VMEM((tc, S), F32), pltpu.VMEM((tc, S), F32),
                        pltpu.VMEM((1, S), F32), pltpu.VMEM((1, S), F32)],
        compiler_params=_params(("arbitrary", "arbitrary")),
        name="s5",
    )(u, h0r, h0i, sw["ldt"], sw["are"], sw["aim"], sw["bre"], sw["bim"], sw["cre"], sw["cim"],
      sw["d"], sw["gw"], sw["gb"])


def _fcum_kernel(cf_ref, fb_ref, lf_ref, f_ref, *, nb):
    lf = _log_sigmoid(cf_ref[...] + fb_ref[...])
    lf_ref[...] = lf
    cs = _lane_cumsum(lf)
    tot = jnp.broadcast_to(cs[:, LANES - 1:LANES], cs.shape)
    row = lax.broadcasted_iota(jnp.int32, cs.shape, 0) % nb
    inc = tot
    s = 1
    while s < nb:
        inc = inc + jnp.where(row >= s, pltpu.roll(inc, s, axis=0), 0.0)
        s *= 2
    f_ref[...] = (cs + (inc - tot)) * LOG2E


def _fcum(cf_rows, fb_rows, nb):
    B, R, _ = cf_rows.shape
    blk = pl.BlockSpec((None, R, LANES), lambda b: (b, 0, 0))
    return pl.pallas_call(
        functools.partial(_fcum_kernel, nb=nb),
        grid=(B,),
        in_specs=[blk, pl.BlockSpec((R, 1), lambda b: (0, 0))],
        out_specs=[blk, blk],
        out_shape=[jax.ShapeDtypeStruct((B, R, LANES), F32)] * 2,
        compiler_params=_params(("arbitrary",)),
        name="fcum",
    )(cf_rows, fb_rows)


ONES_ROWS = 16
VROWS = ATT_HD + ONES_ROWS


def _init_heads(qT_ref, qm_s, acc_s, m_s):
    qT = qT_ref[...]
    row = lax.broadcasted_iota(jnp.int32, (qT.shape[0], 1), 0)
    for h in range(ATT_HEADS):
        qm_s[h] = jnp.where(row // ATT_HD == h, qT, jnp.zeros_like(qT))
    acc_s[...] = jnp.zeros_like(acc_s)
    m_s[...] = jnp.full_like(m_s, M_INIT)


def _attend_block(kblk, vT_ref, k0, kb, qm_s, acc_s, m_s, s_s, p_s, bias_fn):
    for h in range(ATT_HEADS):
        s_s[h] = _dot(kblk, qm_s[h])
    m_new = []
    for h in range(ATT_HEADS):
        s = bias_fn(h, s_s[h])
        s_s[h] = s
        m_new.append(jnp.maximum(m_s[h], _col_max(s)))
    for h in range(ATT_HEADS):
        p_s[h] = jnp.exp2(s_s[h] - m_new[h]).astype(BF16)
    for h in range(ATT_HEADS):
        rows = slice(h * VROWS, (h + 1) * VROWS)
        pv = _dot(vT_ref[rows, pl.ds(k0, kb)], p_s[h])
        acc_s[rows, :] = jnp.exp2(m_s[h] - m_new[h]) * acc_s[rows, :] + pv
        m_s[h] = m_new[h]


def _finish_heads(o_ref, acc_s):
    for h in range(ATT_HEADS):
        base = h * VROWS
        o_ref[h * ATT_HD:(h + 1) * ATT_HD, :] = (acc_s[base:base + ATT_HD, :]
                                                 / acc_s[base + ATT_HD:base + ATT_HD + 1, :])


def _fox_kernel(qT_ref, k_ref, vT_ref, f2_ref, o_ref, qm_s, acc_s, m_s, s_s, p_s, *, tq, kb):
    q0 = pl.program_id(1) * tq
    _init_heads(qT_ref, qm_s, acc_s, m_s)
    kio = lax.broadcasted_iota(jnp.int32, (kb, tq), 0)
    qpos = q0 + lax.broadcasted_iota(jnp.int32, (kb, tq), 1)

    def block(j, masked):
        k0 = pl.multiple_of(j * kb, kb)
        fk = f2_ref[pl.ds(k0, kb), :]

        def bias(h, s):
            s = s - fk[:, h:h + 1]
            return jnp.where(k0 + kio <= qpos, s, NEG_INF) if masked else s

        _attend_block(k_ref[pl.ds(k0, kb), :], vT_ref, k0, kb, qm_s, acc_s, m_s, s_s, p_s, bias)
        return 0

    n_full = (q0 + 1) // kb
    n_blk = (q0 + tq - 1) // kb + 1
    lax.fori_loop(0, n_full, lambda j, c: block(j, False), 0)
    lax.fori_loop(n_full, n_blk, lambda j, c: block(j, True), 0)
    _finish_heads(o_ref, acc_s)


def _att_scratch(tq, kb):
    return [pltpu.VMEM((ATT_HEADS, GROUP_W, tq), BF16), pltpu.VMEM((ATT_HEADS * VROWS, tq), F32),
            pltpu.VMEM((ATT_HEADS, 1, tq), F32),
            pltpu.VMEM((ATT_HEADS, kb, tq), F32), pltpu.VMEM((ATT_HEADS, kb, tq), BF16)]


def _with_ones_rows(vT):
    B, _, T = vT.shape
    v4 = vT.reshape(B, ATT_HEADS, ATT_HD, T)
    return jnp.concatenate([v4, jnp.ones((B, ATT_HEADS, ONES_ROWS, T), vT.dtype)], axis=2).reshape(
        B, ATT_HEADS * VROWS, T)


def _resident(shape):
    return pl.BlockSpec((None,) + shape, lambda b, i: (b,) + (0,) * len(shape), pipeline_mode=pl.Buffered(1))


def _fox(qT, k, vT, f2col, tq, kb):
    B, W, T = qT.shape
    colblk = pl.BlockSpec((None, W, tq), lambda b, i: (b, 0, i))
    return pl.pallas_call(
        functools.partial(_fox_kernel, tq=tq, kb=kb),
        grid=(B, T // tq),
        in_specs=[colblk, _resident((T, W)), _resident((ATT_HEADS * VROWS, T)), _resident((T, ATT_HEADS))],
        out_specs=colblk,
        out_shape=jax.ShapeDtypeStruct((B, W, T), F32),
        scratch_shapes=_att_scratch(tq, kb),
        compiler_params=_params(("arbitrary", "arbitrary")),
        name="fox",
    )(qT, k, vT, f2col)


def _score_stats_init(C):
    big8, zero8 = jnp.full((SUBLANES, C), BIG, F32), jnp.zeros((SUBLANES, C), F32)
    return -big8, big8, big8, zero8, zero8


def _score_stats(c, x):
    pos = x > 0.0
    return (jnp.maximum(c[0], _fold_rows(x, jnp.max)),
            jnp.minimum(c[1], _fold_rows(jnp.where(x > ADM_CUT, x, BIG), jnp.min)),
            jnp.minimum(c[2], _fold_rows(jnp.where(pos, x, BIG), jnp.min)),
            c[3] + _fold_rows(jnp.where(pos, 1.0, 0.0), jnp.sum),
            c[4] + _fold_rows(jnp.where(x == 0.0, 1.0, 0.0), jnp.sum))


def _topk_to_bias(S, nblk, kb, C, k, nadm, stats=None):
    kf = float(k)

    sb = LANES if kb % LANES == 0 else kb

    def reduce_blocks(fn, init):
        def body(j, c):
            k0 = pl.multiple_of(j * kb, kb)
            for r in range(0, kb, sb):
                c = fn(c, S[pl.ds(k0 + r, sb), :], k0 + r)
            return c
        return lax.fori_loop(0, nblk, body, init)

    fold_sum = lambda x: _fold_rows(x, jnp.sum)
    fold_max = lambda x: _fold_rows(x, jnp.max)
    fold_min = lambda x: _fold_rows(x, jnp.min)

    def count(pred):
        c = reduce_blocks(lambda c, x, k0: c + fold_sum(jnp.where(pred(x, k0), 1.0, 0.0)),
                          jnp.zeros((SUBLANES, C), F32))
        return jnp.sum(c, axis=0, keepdims=True)

    if stats is None:
        stats = reduce_blocks(lambda c, x, k0: _score_stats(c, x), _score_stats_init(C))
    mx, mn, mp, np8, nz8 = stats
    cmax = jnp.max(mx, axis=0, keepdims=True)
    cmin = jnp.min(mn, axis=0, keepdims=True)
    small = nadm <= kf
    cpos = jnp.sum(np8, axis=0, keepdims=True)
    cnn = cpos + jnp.sum(nz8, axis=0, keepdims=True)
    at_zero = (cpos < kf) & (cnn >= kf) & jnp.logical_not(small)
    settled = small | at_zero
    above = cpos >= kf
    lo0 = jnp.where(above, jnp.min(mp, axis=0, keepdims=True), cmin)
    clo0 = jnp.where(above, cpos, nadm)
    hi0 = jnp.where(above, cmax + jnp.maximum(1e-30, 1e-6 * jnp.abs(cmax)), 0.0)
    chi0 = jnp.where(above, 0.0, cnn)

    def any_col(flag):
        return jnp.max(jnp.where(flag, 1.0, 0.0)) > 0.5

    def probe(c, frac):
        lo, hi, clo, chi = c
        mid = lo + (hi - lo) * frac
        cm = count(lambda x, k0: x >= mid)
        ge = cm >= kf
        return jnp.where(ge, mid, lo), jnp.where(ge, hi, mid), jnp.where(ge, cm, clo), jnp.where(ge, chi, cm)

    def unresolved(clo, chi):
        g, r = clo - chi, kf - chi
        return (g > 2.5) & (r > 1.5) & (g - r > 0.5) & jnp.logical_not(settled)

    _, lo, hi, clo, chi = lax.while_loop(
        lambda c: (c[0] < BISECT_ITERS) & any_col(unresolved(c[3], c[4])),
        lambda c: (c[0] + 1,) + probe(c[1:], 0.5),
        (jnp.int32(0), lo0, hi0, clo0, chi0))

    top, bot = reduce_blocks(
        lambda c, x, k0: (jnp.maximum(c[0], fold_max(jnp.where(x < hi, x, -BIG))),
                          jnp.minimum(c[1], fold_min(jnp.where(x >= lo, x, BIG)))),
        (jnp.full((SUBLANES, C), -BIG, F32), jnp.full((SUBLANES, C), BIG, F32)))
    thr = jnp.where(kf - chi < 1.5, jnp.max(top, axis=0, keepdims=True), jnp.min(bot, axis=0, keepdims=True))
    thr = jnp.where(small, cmin, jnp.where(at_zero, 0.0, thr))
    cge = count(lambda x, k0: x >= thr)
    cgt = count(lambda x, k0: x > thr)
    missed = ((cgt >= kf) | (cge < kf)) & jnp.logical_not(settled)

    def slow(_):
        def open_(clo, chi):
            return (clo - chi > 1.5) & jnp.logical_not(settled)

        def search_body(c):
            it, lo, hi, clo, chi = c
            frac = jnp.clip((clo - kf + 0.5) / jnp.maximum(clo - chi, 1.0), 1.0 / 64, 63.0 / 64)
            return (it + 1,) + probe((lo, hi, clo, chi), jnp.where(it % 2 == 0, frac, 0.5))

        _, _, hi2, _, _ = lax.while_loop(lambda c: (c[0] < SEARCH_ITERS) & any_col(open_(c[3], c[4])), search_body,
                                         (jnp.int32(0), lo, hi, clo, chi))

        def unsat(cnt):
            return (cnt < kf) & jnp.logical_not(settled)

        def fix_body(c):
            t, cnt = c
            below = reduce_blocks(lambda m, x, k0: jnp.maximum(m, fold_max(jnp.where(x < t, x, -BIG))),
                                  jnp.full((SUBLANES, C), -BIG, F32))
            nt = jnp.where(unsat(cnt), jnp.max(below, axis=0, keepdims=True), t)
            return nt, count(lambda x, k0: x >= nt)

        t2, cge2 = lax.while_loop(lambda c: any_col(unsat(c[1])), fix_body, (hi2, jnp.zeros((1, C), F32)))
        cgt2 = count(lambda x, k0: x > t2)
        return jnp.where(missed, t2, thr), jnp.where(missed, cge2, cge), jnp.where(missed, cgt2, cgt)

    thr, cge, cgt = lax.cond(any_col(missed), slow, lambda _: (thr, cge, cgt), 0)
    need = kf - cgt
    ties = (cge > kf) & jnp.logical_not(small)

    def write_plain(_):
        def body(j, _):
            k0 = pl.multiple_of(j * kb, kb)
            S[pl.ds(k0, kb), :] = jnp.where(S[pl.ds(k0, kb), :] >= thr, 0.0, NEG_INF)
            return 0
        return lax.fori_loop(0, nblk, body, 0)

    def write_ties(_):
        tri = jnp.where(lax.broadcasted_iota(jnp.int32, (kb, kb), 0) >= lax.broadcasted_iota(jnp.int32, (kb, kb), 1),
                        1.0, 0.0).astype(BF16)

        def body(j, seen):
            k0 = pl.multiple_of(j * kb, kb)
            x = S[pl.ds(k0, kb), :]
            eq = x == thr
            rank = seen + _dot(tri, jnp.where(eq, 1.0, 0.0).astype(BF16))
            S[pl.ds(k0, kb), :] = jnp.where((x > thr) | (eq & (rank <= need)), 0.0, NEG_INF)
            return rank[kb - 1:kb, :]
        lax.fori_loop(0, nblk, body, jnp.zeros((1, C), F32))
        return 0

    lax.cond(any_col(ties), write_ties, write_plain, 0)


def _dsa_kernel(iqT_ref, ik_ref, smT_ref, qT_ref, k_ref, vT_ref, o_ref, S, qm_s, acc_s, m_s, s_s, p_s,
                *, tq, kb, topk):
    q0 = pl.program_id(1) * tq
    n_blk = (q0 + tq - 1) // kb + 1
    kio = lax.broadcasted_iota(jnp.int32, (kb, tq), 0)
    qpos = q0 + lax.broadcasted_iota(jnp.int32, (kb, tq), 1)
    wc = smT_ref[SM_IW:SM_IW + IDX_HEADS, :] * (IDX_W_SCALE * IDX_SCALE)

    def scores(j, stats):
        k0 = pl.multiple_of(j * kb, kb)
        kblk = ik_ref[pl.ds(k0, kb), :]
        sc = jnp.zeros((kb, tq), F32)
        for hh in range(IDX_HEADS):
            d = _dot(kblk, iqT_ref[hh * LANES:(hh + 1) * LANES, :])
            sc = sc + wc[hh:hh + 1, :] * jnp.maximum(d, 0.0)
        sc = jnp.where(k0 + kio <= qpos, sc, NEG_INF)
        S[pl.ds(k0, kb), :] = sc
        return _score_stats(stats, sc)

    stats = lax.fori_loop(0, n_blk, scores, _score_stats_init(tq))

    nadm = (q0 + 1 + lax.broadcasted_iota(jnp.int32, (1, tq), 1)).astype(F32)
    _topk_to_bias(S, n_blk, kb, tq, topk, nadm, stats)

    _init_heads(qT_ref, qm_s, acc_s, m_s)
    kcol =lax.broadcasted_iota(jnp.int32, (kb, 1), 0)

    def attend(j, _):
        k0 = pl.multiple_of(j * kb, kb)
        sel = S[pl.ds(k0, kb), :]
        kpos = (k0 + kcol).astype(F32)

        def bias(h, s):
            return s + (sel + (ALIBI[h] * LOG2E) * kpos)

        _attend_block(k_ref[pl.ds(k0, kb), :], vT_ref, k0, kb, qm_s, acc_s, m_s, s_s, p_s, bias)
        return 0

    lax.fori_loop(0, n_blk, attend, 0)
    _finish_heads(o_ref, acc_s)


def _dsa(iqT, ikp, smT, qT, k, vT, tq, kb, topk):
    B, W, T = qT.shape
    col = lambda r: pl.BlockSpec((None, r, tq), lambda b, i: (b, 0, i))
    return pl.pallas_call(
        functools.partial(_dsa_kernel, tq=tq, kb=kb, topk=topk),
        grid=(B, T // tq),
        in_specs=[col(IDX_HEADS * LANES), _resident((T, LANES)), col(LANES), col(W),
                  _resident((T, W)), _resident((ATT_HEADS * VROWS, T))],
        out_specs=col(W),
        out_shape=jax.ShapeDtypeStruct((B, W, T), F32),
        scratch_shapes=[pltpu.VMEM((T, tq), F32)] + _att_scratch(tq, kb),
        compiler_params=_params(("arbitrary", "arbitrary")),
        name="dsa",
    )(iqT, ikp, smT, qT, k, vT)


def _sa_kernel(pt_ref, q_ref, w_ref, cfn_ref, fb_ref, ikn_ref, *rest, pps, tn):
    pg_refs = rest[:pps]
    s_ref, f_ref, sn_ref, fn_ref, lfn_ref, carry_s = rest[pps:]
    step = pl.program_id(1)

    @pl.when(step == 0)
    def _():
        carry_s[...] = jnp.zeros_like(carry_s)

    qp = q_ref[...]
    q3 = qp[:, 0:3 * IDX_HD]
    w = w_ref[...] * IDX_W_SCALE
    nq = qp.shape[0] // IDX_HEADS

    def scores(kT):
        kh, kl = _split(kT)
        d = _dot(q3, jnp.concatenate([kh, kh, kl], axis=0))
        r = jnp.maximum(d * IDX_SCALE, 0.0) * w
        return jnp.sum(r.reshape(nq, IDX_HEADS, kT.shape[1]), axis=1)

    s_ref[...] = scores(jnp.concatenate([r[0:IDX_HD, :] for r in pg_refs], axis=1))
    cs_all = _lane_cumsum(jnp.concatenate([r[IDX_HD:IDX_HD + ATT_HEADS, :] for r in pg_refs], axis=0))
    carry = carry_s[...]
    for i in range(pps):
        cs = cs_all[i * ATT_HEADS:(i + 1) * ATT_HEADS, :] + carry
        f_ref[:, i * PAGE:(i + 1) * PAGE] = cs
        carry = cs[:, PAGE - 1:PAGE]
    carry_s[...] = carry

    @pl.when(step == pl.num_programs(1) - 1)
    def _():
        lane = lax.broadcasted_iota(jnp.int32, (ATT_HEADS, PAGE), 1)
        lfn = _log_sigmoid(cfn_ref[...] + fb_ref[...])
        lfn_ref[...] = lfn
        fn_ref[...] = _lane_cumsum(jnp.where(lane < tn, lfn, 0.0)) + carry
        col = lax.broadcasted_iota(jnp.int32, (nq, PAGE), 1)
        rowq = lax.broadcasted_iota(jnp.int32, (nq, PAGE), 0)
        sn_ref[...] = jnp.where((col <= rowq) & (col < tn), scores(ikn_ref[...]), NEG_INF)


def _sa(page_table, l, qrows, wrows, cfn, fb, ikn, idx_pool, pps, tn):
    B, n_pages = page_table.shape
    nsteps = n_pages // pps
    P = n_pages * PAGE
    nq = qrows.shape[1] // IDX_HEADS
    per_b = lambda shape: pl.BlockSpec((None,) + shape, lambda b, s, pt: (b,) + (0,) * len(shape))
    pg_specs = [pl.BlockSpec((None, None, IDX_HD + ATT_HEADS, PAGE),
                             lambda b, s, pt, i=i: (l, pt[b, s * pps + i], 0, 0)) for i in range(pps)]
    gs = pltpu.PrefetchScalarGridSpec(
        num_scalar_prefetch=1,
        grid=(B, nsteps),
        in_specs=[per_b((nq * IDX_HEADS, LANES)), per_b((nq * IDX_HEADS, 1)), per_b((ATT_HEADS, PAGE)),
                  pl.BlockSpec((ATT_HEADS, 1), lambda b, s, pt: (0, 0)), per_b((IDX_HD, PAGE))] + pg_specs,
        out_specs=[pl.BlockSpec((None, nq, pps * PAGE), lambda b, s, pt: (b, 0, s)),
                   pl.BlockSpec((None, ATT_HEADS, pps * PAGE), lambda b, s, pt: (b, 0, s)),
                   per_b((nq, PAGE)), per_b((ATT_HEADS, PAGE)), per_b((ATT_HEADS, PAGE))],
        scratch_shapes=[pltpu.VMEM((ATT_HEADS, 1), F32)])
    return pl.pallas_call(
        functools.partial(_sa_kernel, pps=pps, tn=tn),
        grid_spec=gs,
        out_shape=[jax.ShapeDtypeStruct((B, nq, P), F32), jax.ShapeDtypeStruct((B, ATT_HEADS, P), F32),
                   jax.ShapeDtypeStruct((B, nq, PAGE), F32), jax.ShapeDtypeStruct((B, ATT_HEADS, PAGE), F32),
                   jax.ShapeDtypeStruct((B, ATT_HEADS, PAGE), F32)],
        compiler_params=_params(("arbitrary", "arbitrary")),
        name="sample_scores",
    )(page_table, qrows, wrows, cfn, fb, ikn, *([idx_pool] * pps))


def _sb_kernel(s_ref, o_ref, *, kb, nblk, topk, past, nq):
    C = s_ref.shape[1]
    o_ref[...] = s_ref[...]
    nadm = (past + 1 + lax.broadcasted_iota(jnp.int32, (1, C), 1) % nq).astype(F32)
    _topk_to_bias(o_ref, nblk, kb, C, topk, nadm)


def _sb(s_allT, topk, past, nq):
    Wt, C = s_allT.shape
    nl = Wt // LANES
    div = max(d for d in range(1, 9) if nl % d == 0)
    kb = div * LANES
    return pl.pallas_call(
        functools.partial(_sb_kernel, kb=kb, nblk=Wt // kb, topk=topk, past=past, nq=nq),
        grid=(1,),
        in_specs=[pl.BlockSpec((Wt, C), lambda i: (0, 0))],
        out_specs=pl.BlockSpec((Wt, C), lambda i: (0, 0)),
        out_shape=jax.ShapeDtypeStruct((Wt, C), F32),
        compiler_params=_params(("arbitrary",)),
        name="sample_topk",
    )(s_allT)


def _sc_kernel(pt_ref, fq_ref, dq_ref, fqc_ref, fp_ref, bp_ref, fn_ref, bn_ref,
               ckn_ref, cvn_ref, dkn_ref, dvn_ref, *rest, pps, past, tn, nq):
    fk_refs, fv_refs = rest[0:pps], rest[pps:2 * pps]
    dk_refs, dv_refs = rest[2 * pps:3 * pps], rest[3 * pps:4 * pps]
    yc_ref, yd_ref, qf_s, qd_s, accf, mf, lf, accd, md, ld = rest[4 * pps:]
    step = pl.program_id(1)
    R = ATT_HEADS * SUBLANES

    def stack_heads(q8):
        return jnp.concatenate([jnp.where(_head_mask(h), q8, jnp.zeros_like(q8)) for h in range(ATT_HEADS)], axis=0)

    @pl.when(step == 0)
    def _():
        qf_s[...] = stack_heads(fq_ref[...])
        qd_s[...] = stack_heads(dq_ref[...])
        for acc, m, l in ((accf, mf, lf), (accd, md, ld)):
            acc[...] = jnp.zeros_like(acc)
            m[...] = jnp.full_like(m, M_INIT)
            l[...] = jnp.zeros_like(l)

    def update(q_s, kT, vT, bias, acc, m, l):
        s = _dot(q_s[...], kT) + bias * LOG2E
        m_prev = m[...]
        m_new = jnp.maximum(m_prev, jnp.max(s, axis=1, keepdims=True))
        p = jnp.exp2(s - m_new)
        alpha = jnp.exp2(m_prev - m_new)
        l[...] = alpha * l[...] + jnp.sum(p, axis=1, keepdims=True)
        m[...] = m_new
        acc[...] = alpha * acc[...] + _nt(p.astype(BF16), vT)

    def per_head_rows(x4):
        return jnp.concatenate([jnp.broadcast_to(x4[h:h + 1, :], (SUBLANES, x4.shape[1]))
                                for h in range(ATT_HEADS)], axis=0)

    def per_query_rows(xq):
        r8 = lax.broadcasted_iota(jnp.int32, (SUBLANES, xq.shape[1]), 0)
        x8 = jnp.zeros((SUBLANES, xq.shape[1]), F32)
        for qq in range(nq):
            x8 = jnp.where(r8 == qq, jnp.broadcast_to(xq[qq:qq + 1, :], x8.shape), x8)
        return jnp.concatenate([x8] * ATT_HEADS, axis=0)

    rowi = lax.broadcasted_iota(jnp.int32, (R, 1), 0)
    qidx = rowi % SUBLANES
    slope = jnp.zeros((R, 1), F32)
    for h in range(ATT_HEADS):
        slope = jnp.where(rowi // SUBLANES == h, ALIBI[h], slope)

    def slab(ref):
        return ref[...].reshape(GROUP_W, ref.shape[-1]).astype(BF16)

    def blocks(refs):
        return jnp.concatenate([slab(r) for r in refs], axis=1)

    kw = pps * PAGE
    col = lax.broadcasted_iota(jnp.int32, (R, kw), 1)
    update(qf_s, blocks(fk_refs), blocks(fv_refs), fqc_ref[...] - per_head_rows(fp_ref[...]), accf, mf, lf)
    dist = (past + qidx - (step * kw + col)).astype(F32)
    update(qd_s, blocks(dk_refs), blocks(dv_refs), per_query_rows(bp_ref[...]) - slope * dist, accd, md, ld)

    @pl.when(step == pl.num_programs(1) - 1)
    def _():
        coln = lax.broadcasted_iota(jnp.int32, (R, PAGE), 1)
        ok = (coln <= qidx) & (coln < tn)
        bias_f = jnp.where(ok, fqc_ref[...] - per_head_rows(fn_ref[...]), NEG_INF)
        update(qf_s, slab(ckn_ref), slab(cvn_ref), bias_f, accf, mf, lf)
        distn = (qidx - coln).astype(F32)
        bias_d = jnp.where(coln < tn, per_query_rows(bn_ref[...]), NEG_INF) - slope * distn
        update(qd_s, slab(dkn_ref), slab(dvn_ref), bias_d, accd, md, ld)
        for acc, l, out in ((accf, lf, yc_ref), (accd, ld, yd_ref)):
            o = acc[...] / l[...]
            y = jnp.zeros((SUBLANES, GROUP_W), F32)
            for h in range(ATT_HEADS):
                y = jnp.where(_head_mask(h), o[h * SUBLANES:(h + 1) * SUBLANES, :], y)
            out[...] = y


def _sc(page_table, l, fq8, dq8, fqc, f_past, b_past, f_new, b_new, ckn, cvn, dkn, dvn,
        fk_pool, fv_pool, dk_pool, dv_pool, pps, tn):
    B, n_pages = page_table.shape
    nsteps = n_pages // pps
    past = n_pages * PAGE
    nq = b_past.shape[1]
    W = GROUP_W
    per_b = lambda shape: pl.BlockSpec((None,) + shape, lambda b, s, pt: (b,) + (0,) * len(shape))
    page = lambda i: pl.BlockSpec((None, None, ATT_HEADS, ATT_HD, PAGE),
                                  lambda b, s, pt, i=i: (l, pt[b, s * pps + i], 0, 0, 0))
    pages = [page(i) for i in range(pps)]
    newpage = per_b((ATT_HEADS, ATT_HD, PAGE))
    R = ATT_HEADS * SUBLANES
    gs = pltpu.PrefetchScalarGridSpec(
        num_scalar_prefetch=1,
        grid=(B, nsteps),
        in_specs=[per_b((SUBLANES, W)), per_b((SUBLANES, W)), per_b((R, 1)),
                  pl.BlockSpec((None, ATT_HEADS, pps * PAGE), lambda b, s, pt: (b, 0, s)),
                  pl.BlockSpec((None, nq, pps * PAGE), lambda b, s, pt: (b, 0, s)),
                  per_b((ATT_HEADS, PAGE)), per_b((nq, PAGE)),
                  newpage, newpage, newpage, newpage] + pages * 4,
        out_specs=[per_b((SUBLANES, W)), per_b((SUBLANES, W))],
        scratch_shapes=[pltpu.VMEM((R, W), BF16), pltpu.VMEM((R, W), BF16),
                        pltpu.VMEM((R, W), F32), pltpu.VMEM((R, 1), F32), pltpu.VMEM((R, 1), F32),
                        pltpu.VMEM((R, W), F32), pltpu.VMEM((R, 1), F32), pltpu.VMEM((R, 1), F32)])
    return pl.pallas_call(
        functools.partial(_sc_kernel, pps=pps, past=past, tn=tn, nq=nq),
        grid_spec=gs,
        out_shape=[jax.ShapeDtypeStruct((B, SUBLANES, W), F32)] * 2,
        compiler_params=_params(("arbitrary", "arbitrary")),
        name="sample_attn",
    )(page_table, fq8, dq8, fqc, f_past, b_past, f_new, b_new, ckn, cvn, dkn, dvn,
      *([fk_pool] * pps), *([fv_pool] * pps), *([dk_pool] * pps), *([dv_pool] * pps))


def _post_kernel(x_ref, ya_ref, yb_ref, yc_ref, yd_ref, g1_ref, sh2_ref, sc2_ref, g2_ref,
                 gn_ref, wo_ref, npost_ref, nfpre_ref, wg_ref, wu_ref, wd_ref, nfpost_ref, o_ref, *, chunks):
    o = None
    for i, r in enumerate((ya_ref, yb_ref, yc_ref, yd_ref)):
        part = _rms(r[...], gn_ref[:, i * GROUP_W:(i + 1) * GROUP_W]).astype(BF16)
        d = _dot(part, wo_ref[i * GROUP_W:(i + 1) * GROUP_W, :])
        o = d if o is None else o + d
    x1 = x_ref[...] + g1_ref[...] * _rms(o, npost_ref[...])
    h2 = (_rms(x1, nfpre_ref[...]) * (1.0 + sc2_ref[...]) + sh2_ref[...]).astype(BF16)
    f = None
    for c0, c1 in chunks:
        gate = _dot(h2, wg_ref[:, c0:c1])
        up = _dot(h2, wu_ref[:, c0:c1])
        d = _dot((gate * _sigmoid(gate) * up).astype(BF16), wd_ref[c0:c1, :])
        f = d if f is None else f + d
    o_ref[...] = x1 + g2_ref[...] * _rms(f, nfpost_ref[...])


def _post(x, ya, yb, yc, yd, mod, tiles_per_mod, pw, tm):
    N, D = x.shape
    H = pw["wg"].shape[1]
    rows_mod = mod.shape[1]
    step = 1024
    chunks = tuple((c, min(c + step, H)) for c in range(0, H, step))
    tok = lambda w: pl.BlockSpec((tm, w), lambda i: (i, 0))
    const = lambda shape: pl.BlockSpec(shape, lambda i: (0,) * len(shape), pipeline_mode=pl.Buffered(1))
    return pl.pallas_call(
        functools.partial(_post_kernel, chunks=chunks),
        grid=(N // tm,),
        in_specs=[tok(D), tok(GROUP_W), tok(GROUP_W), tok(GROUP_W), tok(GROUP_W),
                  _mod_spec(rows_mod, D, tiles_per_mod, 2), _mod_spec(rows_mod, D, tiles_per_mod, 3),
                  _mod_spec(rows_mod, D, tiles_per_mod, 4), _mod_spec(rows_mod, D, tiles_per_mod, 5),
                  const((1, D)), const((D, D)), const((1, D)), const((1, D)),
                  const((D, H)), const((D, H)), const((H, D)), const((1, D))],
        out_specs=tok(D),
        out_shape=jax.ShapeDtypeStruct((N, D), F32),
        compiler_params=_params(("arbitrary",)),
        name="post",
    )(x, ya, yb, yc, yd, mod, mod, mod, mod, pw["gn"], pw["wo"], pw["npost"], pw["nfpre"],
      pw["wg"], pw["wu"], pw["wd"], pw["nfpost"])


def _block_diag(blocks):
    G, r, c = blocks.shape
    eye = jnp.eye(G, dtype=blocks.dtype)
    return (blocks[:, :, None, :] * eye[:, None, :, None]).reshape(G * r, G * c)


def _cat_weight(w):
    D = w.shape[0]
    widths = (GROUP_W, GROUP_W, GROUP_W, GROUP_W, GROUP_W, GROUP_W, ATT_HEADS, GROUP_W, GROUP_W, GROUP_W,
              IDX_HEADS * IDX_HD, IDX_HD, IDX_HEADS)
    offs = [0]
    for wd in widths:
        offs.append(offs[-1] + wd)
    pc = [w[:, offs[i]:offs[i + 1]] for i in range(len(widths))]
    a_x, a_g, b_u, c_q, c_k, c_v, c_f, d_q, d_k, d_v, i_q, i_k, i_w = pc
    iq_rep = jnp.tile(i_q.reshape(D, IDX_HEADS, 1, IDX_HD), (1, 1, LANES // IDX_HD, 1)).reshape(D, IDX_HEADS * LANES)
    ik_rep = jnp.tile(i_k, (1, LANES // IDX_HD))
    small = jnp.concatenate([i_k, c_f, i_w, jnp.zeros((D, LANES - IDX_HD - ATT_HEADS - IDX_HEADS), w.dtype)], axis=1)
    wcat = jnp.concatenate([a_x, a_g, b_u, c_q, c_k, c_v, d_q, d_k, d_v], axis=1).astype(BF16)
    widx = jnp.concatenate([iq_rep, ik_rep, small], axis=1)
    hi = lax.reduce_precision(widx, exponent_bits=8, mantissa_bits=7)
    hi_b = hi.astype(BF16)
    return wcat, jnp.concatenate([hi_b, hi_b, (widx - hi).astype(BF16)], axis=0)


def _layer_weights(l, p):
    row = lambda a: a[l].reshape(1, -1)
    lw = dict(conv_w=p["lru_conv_w"][l], conv_b=row(p["lru_conv_b"]),
              wr=_block_diag(p["lru_wr"][l]).astype(BF16), br=row(p["lru_br"]),
              wi=_block_diag(p["lru_wi"][l]).astype(BF16), bi=row(p["lru_bi"]), lam=row(p["lru_lambda"]))
    sw = dict(ldt=jnp.repeat(p["s5_log_dt"][l], S5_N).reshape(1, S5_STATE),
              are=p["s5_a_re"][l].reshape(1, S5_STATE), aim=p["s5_a_im"][l].reshape(1, S5_STATE),
              bre=_block_diag(jnp.swapaxes(p["s5_b_re"][l], 1, 2)).astype(BF16),
              bim=_block_diag(jnp.swapaxes(p["s5_b_im"][l], 1, 2)).astype(BF16),
              cre=_block_diag(jnp.swapaxes(p["s5_c_re"][l], 1, 2)).astype(BF16),
              cim=_block_diag(jnp.swapaxes(p["s5_c_im"][l], 1, 2)).astype(BF16),
              d=row(p["s5_d"]), gw=p["s5_glu_w"][l].astype(BF16), gb=row(p["s5_glu_b"]))
    pw = dict(gn=row(p["grp_norm"]), wo=p["w_out"][l].astype(BF16), npost=row(p["norm_mix_post"]),
              nfpre=row(p["norm_ffn_pre"]), wg=p["ffn_w_gate"][l].astype(BF16), wu=p["ffn_w_up"][l].astype(BF16),
              wd=p["ffn_w_down"][l].astype(BF16), nfpost=row(p["norm_ffn_post"]))
    return dict(win=_cat_weight(p["w_in"][l]), npre=row(p["norm_mix_pre"]), lru=lw, s5=sw, post=pw,
                fbias=p["fox_f_bias"][l])


def _largest_tile(n, cap):
    t = min(n, cap)
    while n % t:
        t //= 2
    return t


def _recurrent(ax, ag, bu, lru_buf, lru_h0, s5_r0, s5_i0, wts, t_real):
    B, Tp, _ = ax.shape
    tc = _largest_tile(Tp, 256)
    buf8 = jnp.pad(lru_buf, ((0, 0), (SUBLANES - (CONV_W - 1), 0), (0, 0)))
    ya, lru_h = _lru(ax, ag, buf8, lru_h0.reshape(B, 1, GROUP_W), wts["lru"], tc, t_real)
    yb, s5r, s5i = _s5(bu, s5_r0.reshape(B, 1, S5_STATE), s5_i0.reshape(B, 1, S5_STATE), wts["s5"], tc, t_real)
    return ya, yb, lru_h.reshape(B, GROUP_W), s5r.reshape(B, S5_NG, S5_N), s5i.reshape(B, S5_NG, S5_N)


def _prompt_layer(x, mod, wts, B, T):
    N, D = x.shape
    tm = _largest_tile(T, 512)
    (ax, ag, bu, cq, ckf, ckb, cvf, cvb, dq, dkf, dkb, dvf, dvb, iqp, ikp, sm) = _inproj(
        x, mod, T // tm, wts["npre"], *wts["win"], tm)
    seq = lambda a: a.reshape(B, T, a.shape[-1])
    zeros = lambda *s: jnp.zeros(s, F32)
    ax3 = seq(ax)
    ya, yb, lru_h, s5r, s5i = _recurrent(ax3, seq(ag), seq(bu), zeros(B, CONV_W - 1, GROUP_W), zeros(B, GROUP_W),
                                         zeros(B, S5_NG, S5_N), zeros(B, S5_NG, S5_N), wts, T)
    nb = T // LANES
    cf_rows = jnp.swapaxes(seq(sm)[:, :, SM_CF:SM_CF + ATT_HEADS], 1, 2).reshape(B, ATT_HEADS * nb, LANES)
    fb_rows = jnp.repeat(wts["fbias"], nb).reshape(ATT_HEADS * nb, 1)
    lf_rows, f_rows = _fcum(cf_rows, fb_rows, nb)
    f2_col = jnp.swapaxes(f_rows.reshape(B, ATT_HEADS, T), 1, 2)
    logf = jnp.swapaxes(lf_rows.reshape(B, ATT_HEADS, T), 1, 2)
    tr = lambda a: jnp.swapaxes(seq(a), 1, 2)
    tq = kb = _largest_tile(T, 512)
    yc = jnp.swapaxes(_fox(tr(cq), seq(ckb), _with_ones_rows(tr(cvb)), f2_col, tq, kb), 1, 2)
    topk = max(1, min(DSA_TOPK_MAX, T // 4))
    yd = jnp.swapaxes(_dsa(tr(iqp), seq(ikp), tr(sm), tr(dq), seq(dkb), _with_ones_rows(tr(dvb)), tq, kb, topk),
                      1, 2)
    flat = lambda a: a.reshape(N, GROUP_W)
    x2 = _post(x, flat(ya), flat(yb), flat(yc), flat(yd), mod, T // tm, wts["post"], tm)
    heads = lambda a: a.reshape(B, T, ATT_HEADS, ATT_HD)
    state = dict(lru_h=lru_h, lru_conv=ax3[:, T - (CONV_W - 1):, :], s5_re=s5r, s5_im=s5i,
                 fox_k=heads(ckf), fox_v=heads(cvf), fox_logf=logf, dsa_k=heads(dkf), dsa_v=heads(dvf),
                 dsa_kidx=seq(sm)[:, :, SM_IK:SM_IK + IDX_HD])
    return x2, state


def _sample_layer(x, mod, wts, l, B, T, past, caches, page_table):
    N, D = x.shape
    (ax, ag, bu, cq, ckf, ckb, cvf, cvb, dq, dkf, dkb, dvf, dvb, iqp, ikp, sm) = _inproj(
        x, mod, 1, wts["npre"], *wts["win"], N)
    seq = lambda a: a.reshape(B, T, a.shape[-1])
    tp = -(-T // SUBLANES) * SUBLANES
    padt = lambda a, n=tp: jnp.pad(seq(a), ((0, 0), (0, n - T), (0, 0)))
    ax3 = seq(ax)
    ya, yb, lru_h, s5r, s5i = _recurrent(padt(ax), padt(ag), padt(bu), past["lru_conv"], past["lru_h"],
                                         past["s5_re"], past["s5_im"], wts, T)
    sm3 = seq(sm)
    n_pages = page_table.shape[1]
    plen = n_pages * PAGE
    pps = _largest_tile(n_pages, 16)
    iq_rows = iqp.reshape(B, T * IDX_HEADS, LANES)
    wrows = sm3[:, :, SM_IW:SM_IW + IDX_HEADS].reshape(B, T * IDX_HEADS, 1)
    cfn = jnp.pad(jnp.swapaxes(sm3[:, :, SM_CF:SM_CF + ATT_HEADS], 1, 2), ((0, 0), (0, 0), (0, PAGE - T)))
    ikn = jnp.pad(jnp.swapaxes(sm3[:, :, SM_IK:SM_IK + IDX_HD], 1, 2), ((0, 0), (0, 0), (0, PAGE - T)))
    s_past, f_past, s_new, f_new, lf_new = _sa(page_table, l, iq_rows, wrows, cfn,
                                               wts["fbias"].reshape(ATT_HEADS, 1), ikn, caches["idx"],
                                               _largest_tile(n_pages, 64), T)
    topk = max(1, min(DSA_TOPK_MAX, (plen + T) // 4))
    s_allT = jnp.transpose(jnp.concatenate([s_past, s_new], axis=2), (2, 0, 1)).reshape(plen + PAGE, B * T)
    bias_all = jnp.transpose(_sb(s_allT, topk, plen, T).reshape(plen + PAGE, B, T), (1, 2, 0))
    fqc = jnp.pad(f_new[:, :, :T], ((0, 0), (0, 0), (0, SUBLANES - T))).reshape(B, ATT_HEADS * SUBLANES, 1)
    newpage = lambda a: jnp.pad(jnp.transpose(a.reshape(B, T, ATT_HEADS, ATT_HD), (0, 2, 3, 1)),
                                ((0, 0), (0, 0), (0, 0), (0, PAGE - T)))
    yc8, yd8 = _sc(page_table, l, padt(cq, SUBLANES), padt(dq, SUBLANES), fqc, f_past, bias_all[:, :, :plen],
                   f_new, bias_all[:, :, plen:], newpage(ckb), newpage(cvb), newpage(dkb), newpage(dvb),
                   caches["fox_k"], caches["fox_v"], caches["dsa_k"], caches["dsa_v"], pps, T)
    flat = lambda a: a[:, :T, :].reshape(N, GROUP_W)
    x2 = _post(x, flat(ya), flat(yb), flat(yc8), flat(yd8), mod, 1, wts["post"], N)
    heads = lambda a: a.reshape(B, T, ATT_HEADS, ATT_HD)
    state = dict(lru_h=lru_h, lru_conv=ax3[:, T - (CONV_W - 1):, :], s5_re=s5r, s5_im=s5i,
                 fox_k=heads(ckf), fox_v=heads(cvf), fox_logf=jnp.swapaxes(lf_new[:, :, :T], 1, 2),
                 dsa_k=heads(dkf), dsa_v=heads(dvf), dsa_kidx=sm3[:, :, SM_IK:SM_IK + IDX_HD])
    return x2, state


def kernel(x_prompt, x_sample, state_lru_h, state_lru_conv, state_s5_re, state_s5_im, cache_fox_k, cache_fox_v, cache_fox_logf, cache_dsa_k, cache_dsa_v, cache_dsa_kidx, page_table, c_prompt, c_sample, ada_w, ada_b, norm_mix_pre, norm_mix_post, norm_ffn_pre, norm_ffn_post, w_in, lru_conv_w, lru_conv_b, lru_wr, lru_br, lru_wi, lru_bi, lru_lambda, s5_log_dt, s5_a_re, s5_a_im, s5_b_re, s5_b_im, s5_c_re, s5_c_im, s5_d, s5_glu_w, s5_glu_b, fox_f_bias, grp_norm, w_out, ffn_w_gate, ffn_w_up, ffn_w_down):
    p = dict(norm_mix_pre=norm_mix_pre, norm_mix_post=norm_mix_post, norm_ffn_pre=norm_ffn_pre,
             norm_ffn_post=norm_ffn_post, w_in=w_in, lru_conv_w=lru_conv_w, lru_conv_b=lru_conv_b, lru_wr=lru_wr,
             lru_br=lru_br, lru_wi=lru_wi, lru_bi=lru_bi, lru_lambda=lru_lambda, s5_log_dt=s5_log_dt,
             s5_a_re=s5_a_re, s5_a_im=s5_a_im, s5_b_re=s5_b_re, s5_b_im=s5_b_im, s5_c_re=s5_c_re, s5_c_im=s5_c_im,
             s5_d=s5_d, s5_glu_w=s5_glu_w, s5_glu_b=s5_glu_b, fox_f_bias=fox_f_bias, grp_norm=grp_norm,
             w_out=w_out, ffn_w_gate=ffn_w_gate, ffn_w_up=ffn_w_up, ffn_w_down=ffn_w_down)
    B, T, D = x_prompt.shape
    Bs, Ts, _ = x_sample.shape
    L = ada_w.shape[0]
    n_pool = cache_fox_k.shape[1]
    assert T % LANES == 0 and CONV_W - 1 <= Ts <= SUBLANES and cache_fox_k.shape[2] == PAGE

    c_all = jnp.concatenate([c_prompt, c_sample], axis=0)
    c_all = jnp.pad(c_all, ((0, -c_all.shape[0] % SUBLANES), (0, 0)))
    mod_all = _ada(c_all, ada_w, ada_b)

    pool = lambda a: jnp.transpose(a, (0, 1, 3, 4, 2))
    caches = dict(fox_k=pool(cache_fox_k), fox_v=pool(cache_fox_v), dsa_k=pool(cache_dsa_k), dsa_v=pool(cache_dsa_v),
                  idx=jnp.concatenate([jnp.swapaxes(cache_dsa_kidx, 2, 3), jnp.swapaxes(cache_fox_logf, 2, 3)], axis=2))

    xp = x_prompt.reshape(B * T, D)
    xs = x_sample.reshape(Bs * Ts, D)
    st_p, st_s = [], []
    for l in range(L):
        wts = _layer_weights(l, p)
        mod_p = mod_all[l, :B].reshape(B, 1, 6 * D)
        mod_s = jnp.repeat(mod_all[l, B:B + Bs], Ts, axis=0).reshape(1, Bs * Ts, 6 * D)
        xp, new_p = _prompt_layer(xp, mod_p, wts, B, T)
        past = dict(lru_h=state_lru_h[l], lru_conv=state_lru_conv[l], s5_re=state_s5_re[l], s5_im=state_s5_im[l])
        xs, new_s = _sample_layer(xs, mod_s, wts, l, Bs, Ts, past, caches, page_table)
        st_p.append(new_p)
        st_s.append(new_s)

    stk = lambda outs, name: jnp.stack([o[name] for o in outs])
    names = ("lru_h", "lru_conv", "s5_re", "s5_im", "fox_k", "fox_v", "fox_logf", "dsa_k", "dsa_v", "dsa_kidx")
    res = [xp.reshape(B, T, D), xs.reshape(Bs, Ts, D)]
    for name in names:
        res += [stk(st_p, name), stk(st_s, name)]
    return tuple(res)
```

```python
import functools
import math

import jax
import jax.numpy as jnp
from jax import lax
from jax.experimental import pallas as pl
from jax.experimental.pallas import tpu as pltpu

F32 = jnp.float32
BF16 = jnp.bfloat16

GROUP_W = 256
CONV_W = 4
LRU_C = 8.0
S5_NG = 16
S5_N = 64
S5_STATE = S5_NG * S5_N
ATT_HEADS = 4
ATT_HD = 64
IDX_HEADS = 8
IDX_HD = 32
DSA_TOPK_MAX = 256
PAGE = 128
NEG_INF = -1e30
M_INIT = -1e20
BIG = 3e38
ADM_CUT = -1e29
EPS = 1e-6
ATT_SCALE = ATT_HD ** -0.5
IDX_SCALE = IDX_HD ** -0.5
IDX_W_SCALE = IDX_HEADS ** -0.5
ALIBI = tuple(2.0 ** (-8.0 * (h + 1) / ATT_HEADS) for h in range(ATT_HEADS))
LOG2E = math.log2(math.e)
Q_SCALE = ATT_SCALE * LOG2E
BISECT_ITERS = 28
SEARCH_ITERS = 48

LANES = 128
SUBLANES = 8
VMEM_LIMIT = 56 * 1024 * 1024

C_AX, C_AG, C_BU, C_CQ, C_CK, C_CV, C_DQ, C_DK, C_DV = (i * GROUP_W for i in range(9))
P_CAT = 9 * GROUP_W
C_IQ = 0
C_IK = C_IQ + IDX_HEADS * LANES
C_SM = C_IK + LANES
P_IDX = C_SM + LANES
SM_IK, SM_CF, SM_IW = 0, IDX_HD, IDX_HD + ATT_HEADS

NT_DIMS = (((1,), (1,)), ((), ()))


def _nt(a, b):
    return lax.dot_general(a, b, NT_DIMS, preferred_element_type=F32)


def _dot(a, b):
    return jnp.dot(a, b, preferred_element_type=F32)


def _rms(x, g):
    return x * lax.rsqrt(jnp.mean(x * x, axis=-1, keepdims=True) + EPS) * g


def _softplus(z):
    return jnp.maximum(z, 0.0) + jnp.log1p(jnp.exp(-jnp.abs(z)))


def _log_sigmoid(z):
    return -_softplus(-z)


def _expm1(z):
    e = jnp.exp(z)
    one = e == 1.0
    return jnp.where(one, z, (e - 1.0) * z / jnp.where(one, 1.0, jnp.log(e)))


def _sigmoid(z):
    return jax.nn.sigmoid(z)


def _gelu(z):
    return jax.nn.gelu(z)


def _params(sem):
    return pltpu.CompilerParams(dimension_semantics=sem, vmem_limit_bytes=VMEM_LIMIT)


def _head_mask(h, width=GROUP_W):
    lane = lax.broadcasted_iota(jnp.int32, (1, width), 1)
    return (lane >= h * ATT_HD) & (lane < (h + 1) * ATT_HD)


FOLD_CHAINS = 4


def _fold_rows(x, axis_op):
    n, c = x.shape
    g = FOLD_CHAINS if n % (FOLD_CHAINS * SUBLANES) == 0 else 1
    x = axis_op(x.reshape(g, n // (g * SUBLANES), SUBLANES, c), axis=1)
    return axis_op(x, axis=0)


def _col_max(x):
    return jnp.max(_fold_rows(x, jnp.max), axis=0, keepdims=True)


def _lane_cumsum(x):
    lane = lax.broadcasted_iota(jnp.int32, x.shape, 1)
    s = 1
    while s < x.shape[1]:
        x = x + jnp.where(lane >= s, pltpu.roll(x, s, axis=1), 0.0)
        s *= 2
    return x


def _split(x):
    hi = x.astype(BF16)
    return hi, (x - hi.astype(F32)).astype(BF16)


def _ada_kernel(c_ref, w_ref, b_ref, o_ref):
    c = c_ref[...]
    sh, sl = _split(c * _sigmoid(c))
    wh, wl = _split(w_ref[...])
    o_ref[...] = (_dot(sl, wl) + _dot(sl, wh) + _dot(sh, wl)) + _dot(sh, wh) + b_ref[...]


def _ada(c_all, ada_w, ada_b):
    L, D, D6 = ada_w.shape
    Bp = c_all.shape[0]
    tn = 1024
    return pl.pallas_call(
        _ada_kernel,
        grid=(L, D6 // tn),
        in_specs=[pl.BlockSpec((Bp, D), lambda l, j: (0, 0)),
                  pl.BlockSpec((None, D, tn), lambda l, j: (l, 0, j)),
                  pl.BlockSpec((None, 1, tn), lambda l, j: (l, 0, j))],
        out_specs=pl.BlockSpec((None, Bp, tn), lambda l, j: (l, 0, j)),
        out_shape=jax.ShapeDtypeStruct((L, Bp, D6), F32),
        compiler_params=_params(("arbitrary", "arbitrary")),
        name="ada",
    )(c_all, ada_w, ada_b.reshape(L, 1, D6))


def _inproj_kernel(x_ref, sh_ref, sc_ref, g_ref, w_ref, w3_ref,
                   ax_ref, ag_ref, bu_ref, cq_ref, ckf_ref, ckb_ref, cvf_ref, cvb_ref,
                   dq_ref, dkf_ref, dkb_ref, dvf_ref, dvb_ref, iq_ref, ik_ref, sm_ref):
    x = x_ref[...]
    h = _rms(x, g_ref[...]) * (1.0 + sc_ref[...]) + sh_ref[...]
    hb = h.astype(BF16)

    main = _dot(hb, w_ref[...])

    def mm(c0, width):
        return main[:, c0:c0 + width]

    ax_ref[...] = mm(C_AX, GROUP_W)
    ag_ref[...] = mm(C_AG, GROUP_W)
    bu_ref[...] = mm(C_BU, GROUP_W)
    cq_ref[...] = (mm(C_CQ, GROUP_W) * Q_SCALE).astype(BF16)
    ck = mm(C_CK, GROUP_W)
    ckf_ref[...] = ck
    ckb_ref[...] = ck.astype(BF16)
    cv = mm(C_CV, GROUP_W)
    cvf_ref[...] = cv
    cvb_ref[...] = cv.astype(BF16)
    dq_ref[...] = (mm(C_DQ, GROUP_W) * Q_SCALE).astype(BF16)
    dk = mm(C_DK, GROUP_W)
    dkf_ref[...] = dk
    dkb_ref[...] = dk.astype(BF16)
    dv = mm(C_DV, GROUP_W)
    dvf_ref[...] = dv
    dvb_ref[...] = dv.astype(BF16)

    hl = (h - hb.astype(F32)).astype(BF16)
    idx = _dot(jnp.concatenate([hb, hl, hb], axis=1), w3_ref[...])

    def mm3(c0, width):
        return idx[:, c0:c0 + width]

    sm_ref[...] = mm3(C_SM, LANES)

    sub = (lax.broadcasted_iota(jnp.int32, (1, LANES), 1) // IDX_HD)
    for hh in range(IDX_HEADS):
        r = mm3(C_IQ + hh * LANES, LANES)
        hi = r.astype(BF16)
        lo = (r - hi.astype(F32)).astype(BF16)
        zero = jnp.zeros_like(hi)
        iq_ref[:, hh * LANES:(hh + 1) * LANES] = jnp.where(sub == 1, lo, jnp.where(sub == 3, zero, hi))
    r = mm3(C_IK, LANES)
    hi = r.astype(BF16)
    lo = (r - hi.astype(F32)).astype(BF16)
    ik_ref[...] = jnp.where(sub == 2, lo, jnp.where(sub == 3, jnp.zeros_like(hi), hi))


def _mod_spec(rows_mod, D, tiles_per_mod, piece):
    return pl.BlockSpec((None, rows_mod, D), lambda i: (i // tiles_per_mod, 0, piece))


def _inproj(x, mod, tiles_per_mod, g, wcat, widx3, tm):
    N, D = x.shape
    rows_mod = mod.shape[1]
    tok = lambda w: pl.BlockSpec((tm, w), lambda i: (i, 0))
    widths_dtypes = [(GROUP_W, F32)] * 3 + [(GROUP_W, BF16), (GROUP_W, F32), (GROUP_W, BF16), (GROUP_W, F32),
                                            (GROUP_W, BF16), (GROUP_W, BF16), (GROUP_W, F32), (GROUP_W, BF16),
                                            (GROUP_W, F32), (GROUP_W, BF16), (IDX_HEADS * LANES, BF16),
                                            (LANES, BF16), (LANES, F32)]
    return pl.pallas_call(
        _inproj_kernel,
        grid=(N // tm,),
        in_specs=[tok(D), _mod_spec(rows_mod, D, tiles_per_mod, 0), _mod_spec(rows_mod, D, tiles_per_mod, 1),
                  pl.BlockSpec((1, D), lambda i: (0, 0)),
                  pl.BlockSpec((D, P_CAT), lambda i: (0, 0), pipeline_mode=pl.Buffered(1)),
                  pl.BlockSpec((3 * D, P_IDX), lambda i: (0, 0), pipeline_mode=pl.Buffered(1))],
        out_specs=[tok(w) for w, _ in widths_dtypes],
        out_shape=[jax.ShapeDtypeStruct((N, w), dt) for w, dt in widths_dtypes],
        compiler_params=_params(("arbitrary",)),
        name="inproj",
    )(x, mod, mod, g, wcat, widx3)


def _lru_kernel(ax_ref, ag_ref, buf_ref, h0_ref, cw_ref, cb_ref, wr_ref, br_ref, wi_ref, bi_ref, lam_ref,
                ya_ref, hl_ref, xs, hcar, *, tc, last_row):
    c = pl.program_id(1)

    @pl.when(c == 0)
    def _():
        xs[0:SUBLANES, :] = buf_ref[...]
        hcar[...] = h0_ref[...]

    x = ax_ref[...]
    xs[SUBLANES:SUBLANES + tc, :] = x
    w = cw_ref[...]
    y = cb_ref[...] + xs[5:5 + tc, :] * w[0:1] + xs[6:6 + tc, :] * w[1:2] + xs[7:7 + tc, :] * w[2:3] + x * w[3:4]
    xs[0:SUBLANES, :] = xs[tc:tc + SUBLANES, :]

    yb = y.astype(BF16)
    r = _sigmoid(_dot(yb, wr_ref[...]) + br_ref[...])
    i = _sigmoid(_dot(yb, wi_ref[...]) + bi_ref[...])
    log_a = -LRU_C * r * _softplus(-lam_ref[...])
    a = jnp.exp(log_a)
    b = jnp.sqrt(-_expm1(2.0 * log_a)) * (i * y)

    row = lax.broadcasted_iota(jnp.int32, (tc, GROUP_W), 0)
    s = 1
    while s < tc:
        keep = row >= s
        a_sh = jnp.where(keep, pltpu.roll(a, s, axis=0), 1.0)
        b_sh = jnp.where(keep, pltpu.roll(b, s, axis=0), 0.0)
        b = b + a * b_sh
        a = a * a_sh
        s *= 2
    h = b + a * hcar[...]
    hcar[...] = h[tc - 1:tc, :]
    ya_ref[...] = h * _gelu(ag_ref[...])

    @pl.when(c == pl.num_programs(1) - 1)
    def _():
        hl_ref[...] = h[last_row:last_row + 1, :]


def _lru(ax, ag, buf8, h0, lw, tc, t_real):
    B, Tp, W = ax.shape
    seq = pl.BlockSpec((None, tc, W), lambda b, c: (b, c, 0))
    full = lambda shape: pl.BlockSpec(shape, lambda b, c: (0,) * len(shape))
    return pl.pallas_call(
        functools.partial(_lru_kernel, tc=tc, last_row=(t_real - 1) % tc),
        grid=(B, Tp // tc),
        in_specs=[seq, seq,
                  pl.BlockSpec((None, SUBLANES, W), lambda b, c: (b, 0, 0)),
                  pl.BlockSpec((None, 1, W), lambda b, c: (b, 0, 0)),
                  full((CONV_W, W)), full((1, W)), full((W, W)), full((1, W)), full((W, W)), full((1, W)),
                  full((1, W))],
        out_specs=[seq, pl.BlockSpec((None, 1, W), lambda b, c: (b, 0, 0))],
        out_shape=[jax.ShapeDtypeStruct((B, Tp, W), F32), jax.ShapeDtypeStruct((B, 1, W), F32)],
        scratch_shapes=[pltpu.VMEM((tc + SUBLANES, W), F32), pltpu.VMEM((1, W), F32)],
        compiler_params=_params(("arbitrary", "arbitrary")),
        name="lru",
    )(ax, ag, buf8, h0, lw["conv_w"], lw["conv_b"], lw["wr"], lw["br"], lw["wi"], lw["bi"], lw["lam"])


def _s5_kernel(u_ref, h0r_ref, h0i_ref, ldt_ref, are_ref, aim_ref, bre_ref, bim_ref, cre_ref, cim_ref,
               d_ref, gw_ref, gb_ref, y_ref, hlr_ref, hli_ref, hr_s, hi_s, car_r, car_i, *, tc, last_row):
    c = pl.program_id(1)

    @pl.when(c == 0)
    def _():
        car_r[...] = h0r_ref[...]
        car_i[...] = h0i_ref[...]

    dt = jnp.exp(ldt_ref[...])
    ar, ai = are_ref[...], aim_ref[...]
    mag = jnp.exp(dt * ar)
    abr, abi = mag * jnp.cos(dt * ai), mag * jnp.sin(dt * ai)
    den = ar * ar + ai * ai
    nr, ni = abr - 1.0, abi
    cr = (nr * ar + ni * ai) / den
    ci = (ni * ar - nr * ai) / den

    u = u_ref[...]
    ub = u.astype(BF16)
    pre = _dot(ub, bre_ref[...])
    pim = _dot(ub, bim_ref[...])
    hr_s[...] = cr * pre - ci * pim
    hi_s[...] = cr * pim + ci * pre

    def step(t, carry):
        hr, hi = carry
        nhr = abr * hr - abi * hi + hr_s[pl.ds(t, 1), :]
        nhi = abr * hi + abi * hr + hi_s[pl.ds(t, 1), :]
        hr_s[pl.ds(t, 1), :] = nhr
        hi_s[pl.ds(t, 1), :] = nhi
        return nhr, nhi

    hr, hi = lax.fori_loop(0, tc, step, (car_r[...], car_i[...]), unroll=8)
    car_r[...] = hr
    car_i[...] = hi

    hrv, hiv = hr_s[...], hi_s[...]
    y = _dot(hrv.astype(BF16), cre_ref[...]) - _dot(hiv.astype(BF16), cim_ref[...]) + d_ref[...] * u
    g = _gelu(y)
    y_ref[...] = g * _sigmoid(_dot(g.astype(BF16), gw_ref[...]) + gb_ref[...])

    @pl.when(c == pl.num_programs(1) - 1)
    def _():
        hlr_ref[...] = hr_s[last_row:last_row + 1, :]
        hli_ref[...] = hi_s[last_row:last_row + 1, :]


def _s5(u, h0r, h0i, sw, tc, t_real):
    B, Tp, W = u.shape
    S = S5_STATE
    seq = pl.BlockSpec((None, tc, W), lambda b, c: (b, c, 0))
    st = pl.BlockSpec((None, 1, S), lambda b, c: (b, 0, 0))
    full = lambda shape: pl.BlockSpec(shape, lambda b, c: (0,) * len(shape))
    return pl.pallas_call(
        functools.partial(_s5_kernel, tc=tc, last_row=(t_real - 1) % tc),
        grid=(B, Tp // tc),
        in_specs=[seq, st, st, full((1, S)), full((1, S)), full((1, S)), full((W, S)), full((W, S)),
                  full((S, W)), full((S, W)), full((1, W)), full((W, W)), full((1, W))],
        out_specs=[seq, st, st],
        out_shape=[jax.ShapeDtypeStruct((B, Tp, W), F32), jax.ShapeDtypeStruct((B, 1, S), F32),
                   jax.ShapeDtypeStruct((B, 1, S), F32)],
        scratch_shapes=[pltpu.VMEM((tc, S), F32), pltpu.VMEM((tc, S), F32),
                        pltpu.VMEM((1, S), F32), pltpu.VMEM((1, S), F32)],
        compiler_params=_params(("arbitrary", "arbitrary")),
        name="s5",
    )(u, h0r, h0i, sw["ldt"], sw["are"], sw["aim"], sw["bre"], sw["bim"], sw["cre"], sw["cim"],
      sw["d"], sw["gw"], sw["gb"])


def _fcum_kernel(cf_ref, fb_ref, lf_ref, f_ref, *, nb):
    lf = _log_sigmoid(cf_ref[...] + fb_ref[...])
    lf_ref[...] = lf
    cs = _lane_cumsum(lf)
    tot = jnp.broadcast_to(cs[:, LANES - 1:LANES], cs.shape)
    row = lax.broadcasted_iota(jnp.int32, cs.shape, 0) % nb
    inc = tot
    s = 1
    while s < nb:
        inc = inc + jnp.where(row >= s, pltpu.roll(inc, s, axis=0), 0.0)
        s *= 2
    f_ref[...] = (cs + (inc - tot)) * LOG2E


def _fcum(cf_rows, fb_rows, nb):
    B, R, _ = cf_rows.shape
    blk = pl.BlockSpec((None, R, LANES), lambda b: (b, 0, 0))
    return pl.pallas_call(
        functools.partial(_fcum_kernel, nb=nb),
        grid=(B,),
        in_specs=[blk, pl.BlockSpec((R, 1), lambda b: (0, 0))],
        out_specs=[blk, blk],
        out_shape=[jax.ShapeDtypeStruct((B, R, LANES), F32)] * 2,
        compiler_params=_params(("arbitrary",)),
        name="fcum",
    )(cf_rows, fb_rows)


ONES_ROWS = 16
VROWS = ATT_HD + ONES_ROWS


def _init_heads(qT_ref, qm_s, acc_s, m_s):
    qT = qT_ref[...]
    row = lax.broadcasted_iota(jnp.int32, (qT.shape[0], 1), 0)
    for h in range(ATT_HEADS):
        qm_s[h] = jnp.where(row // ATT_HD == h, qT, jnp.zeros_like(qT))
    acc_s[...] = jnp.zeros_like(acc_s)
    m_s[...] = jnp.full_like(m_s, M_INIT)


def _attend_block(kblk, vT_ref, k0, kb, qm_s, acc_s, m_s, s_s, p_s, bias_fn):
    for h in range(ATT_HEADS):
        s_s[h] = _dot(kblk, qm_s[h])
    m_new = []
    for h in range(ATT_HEADS):
        s = bias_fn(h, s_s[h])
        s_s[h] = s
        m_new.append(jnp.maximum(m_s[h], _col_max(s)))
    for h in range(ATT_HEADS):
        p_s[h] = jnp.exp2(s_s[h] - m_new[h]).astype(BF16)
    for h in range(ATT_HEADS):
        rows = slice(h * VROWS, (h + 1) * VROWS)
        pv = _dot(vT_ref[rows, pl.ds(k0, kb)], p_s[h])
        acc_s[rows, :] = jnp.exp2(m_s[h] - m_new[h]) * acc_s[rows, :] + pv
        m_s[h] = m_new[h]


def _finish_heads(o_ref, acc_s):
    for h in range(ATT_HEADS):
        base = h * VROWS
        o_ref[h * ATT_HD:(h + 1) * ATT_HD, :] = (acc_s[base:base + ATT_HD, :]
                                                 / acc_s[base + ATT_HD:base + ATT_HD + 1, :])


def _fox_kernel(qT_ref, k_ref, vT_ref, f2_ref, o_ref, qm_s, acc_s, m_s, s_s, p_s, *, tq, kb):
    q0 = pl.program_id(1) * tq
    _init_heads(qT_ref, qm_s, acc_s, m_s)
    kio = lax.broadcasted_iota(jnp.int32, (kb, tq), 0)
    qpos = q0 + lax.broadcasted_iota(jnp.int32, (kb, tq), 1)

    def block(j, masked):
        k0 = pl.multiple_of(j * kb, kb)
        fk = f2_ref[pl.ds(k0, kb), :]

        def bias(h, s):
            s = s - fk[:, h:h + 1]
            return jnp.where(k0 + kio <= qpos, s, NEG_INF) if masked else s

        _attend_block(k_ref[pl.ds(k0, kb), :], vT_ref, k0, kb, qm_s, acc_s, m_s, s_s, p_s, bias)
        return 0

    n_full = (q0 + 1) // kb
    n_blk = (q0 + tq - 1) // kb + 1
    lax.fori_loop(0, n_full, lambda j, c: block(j, False), 0)
    lax.fori_loop(n_full, n_blk, lambda j, c: block(j, True), 0)
    _finish_heads(o_ref, acc_s)


def _att_scratch(tq, kb):
    return [pltpu.VMEM((ATT_HEADS, GROUP_W, tq), BF16), pltpu.VMEM((ATT_HEADS * VROWS, tq), F32),
            pltpu.VMEM((ATT_HEADS, 1, tq), F32),
            pltpu.VMEM((ATT_HEADS, kb, tq), F32), pltpu.VMEM((ATT_HEADS, kb, tq), BF16)]


def _with_ones_rows(vT):
    B, _, T = vT.shape
    v4 = vT.reshape(B, ATT_HEADS, ATT_HD, T)
    return jnp.concatenate([v4, jnp.ones((B, ATT_HEADS, ONES_ROWS, T), vT.dtype)], axis=2).reshape(
        B, ATT_HEADS * VROWS, T)


def _resident(shape):
    return pl.BlockSpec((None,) + shape, lambda b, i: (b,) + (0,) * len(shape), pipeline_mode=pl.Buffered(1))


def _fox(qT, k, vT, f2col, tq, kb):
    B, W, T = qT.shape
    colblk = pl.BlockSpec((None, W, tq), lambda b, i: (b, 0, i))
    return pl.pallas_call(
        functools.partial(_fox_kernel, tq=tq, kb=kb),
        grid=(B, T // tq),
        in_specs=[colblk, _resident((T, W)), _resident((ATT_HEADS * VROWS, T)), _resident((T, ATT_HEADS))],
        out_specs=colblk,
        out_shape=jax.ShapeDtypeStruct((B, W, T), F32),
        scratch_shapes=_att_scratch(tq, kb),
        compiler_params=_params(("arbitrary", "arbitrary")),
        name="fox",
    )(qT, k, vT, f2col)


def _score_stats_init(C):
    big8, zero8 = jnp.full((SUBLANES, C), BIG, F32), jnp.zeros((SUBLANES, C), F32)
    return -big8, big8, big8, zero8, zero8


def _score_stats(c, x):
    pos = x > 0.0
    return (jnp.maximum(c[0], _fold_rows(x, jnp.max)),
            jnp.minimum(c[1], _fold_rows(jnp.where(x > ADM_CUT, x, BIG), jnp.min)),
            jnp.minimum(c[2], _fold_rows(jnp.where(pos, x, BIG), jnp.min)),
            c[3] + _fold_rows(jnp.where(pos, 1.0, 0.0), jnp.sum),
            c[4] + _fold_rows(jnp.where(x == 0.0, 1.0, 0.0), jnp.sum))


def _topk_to_bias(S, nblk, kb, C, k, nadm, stats=None):
    kf = float(k)

    sb = LANES if kb % LANES == 0 else kb

    def reduce_blocks(fn, init):
        def body(j, c):
            k0 = pl.multiple_of(j * kb, kb)
            for r in range(0, kb, sb):
                c = fn(c, S[pl.ds(k0 + r, sb), :], k0 + r)
            return c
        return lax.fori_loop(0, nblk, body, init)

    fold_sum = lambda x: _fold_rows(x, jnp.sum)
    fold_max = lambda x: _fold_rows(x, jnp.max)
    fold_min = lambda x: _fold_rows(x, jnp.min)

    def count(pred):
        c = reduce_blocks(lambda c, x, k0: c + fold_sum(jnp.where(pred(x, k0), 1.0, 0.0)),
                          jnp.zeros((SUBLANES, C), F32))
        return jnp.sum(c, axis=0, keepdims=True)

    if stats is None:
        stats = reduce_blocks(lambda c, x, k0: _score_stats(c, x), _score_stats_init(C))
    mx, mn, mp, np8, nz8 = stats
    cmax = jnp.max(mx, axis=0, keepdims=True)
    cmin = jnp.min(mn, axis=0, keepdims=True)
    small = nadm <= kf
    cpos = jnp.sum(np8, axis=0, keepdims=True)
    cnn = cpos + jnp.sum(nz8, axis=0, keepdims=True)
    at_zero = (cpos < kf) & (cnn >= kf) & jnp.logical_not(small)
    settled = small | at_zero
    above = cpos >= kf
    lo0 = jnp.where(above, jnp.min(mp, axis=0, keepdims=True), cmin)
    clo0 = jnp.where(above, cpos, nadm)
    hi0 = jnp.where(above, cmax + jnp.maximum(1e-30, 1e-6 * jnp.abs(cmax)), 0.0)
    chi0 = jnp.where(above, 0.0, cnn)

    def any_col(flag):
        return jnp.max(jnp.where(flag, 1.0, 0.0)) > 0.5

    def probe(c, frac):
        lo, hi, clo, chi = c
        mid = lo + (hi - lo) * frac
        cm = count(lambda x, k0: x >= mid)
        ge = cm >= kf
        return jnp.where(ge, mid, lo), jnp.where(ge, hi, mid), jnp.where(ge, cm, clo), jnp.where(ge, chi, cm)

    def unresolved(clo, chi):
        g, r = clo - chi, kf - chi
        return (g > 2.5) & (r > 1.5) & (g - r > 0.5) & jnp.logical_not(settled)

    _, lo, hi, clo, chi = lax.while_loop(
        lambda c: (c[0] < BISECT_ITERS) & any_col(unresolved(c[3], c[4])),
        lambda c: (c[0] + 1,) + probe(c[1:], 0.5),
        (jnp.int32(0), lo0, hi0, clo0, chi0))

    top, bot = reduce_blocks(
        lambda c, x, k0: (jnp.maximum(c[0], fold_max(jnp.where(x < hi, x, -BIG))),
                          jnp.minimum(c[1], fold_min(jnp.where(x >= lo, x, BIG)))),
        (jnp.full((SUBLANES, C), -BIG, F32), jnp.full((SUBLANES, C), BIG, F32)))
    thr = jnp.where(kf - chi < 1.5, jnp.max(top, axis=0, keepdims=True), jnp.min(bot, axis=0, keepdims=True))
    thr = jnp.where(small, cmin, jnp.where(at_zero, 0.0, thr))
    cge = count(lambda x, k0: x >= thr)
    cgt = count(lambda x, k0: x > thr)
    missed = ((cgt >= kf) | (cge < kf)) & jnp.logical_not(settled)

    def slow(_):
        def open_(clo, chi):
            return (clo - chi > 1.5) & jnp.logical_not(settled)

        def search_body(c):
            it, lo, hi, clo, chi = c
            frac = jnp.clip((clo - kf + 0.5) / jnp.maximum(clo - chi, 1.0), 1.0 / 64, 63.0 / 64)
            return (it + 1,) + probe((lo, hi, clo, chi), jnp.where(it % 2 == 0, frac, 0.5))

        _, _, hi2, _, _ = lax.while_loop(lambda c: (c[0] < SEARCH_ITERS) & any_col(open_(c[3], c[4])), search_body,
                                         (jnp.int32(0), lo, hi, clo, chi))

        def unsat(cnt):
            return (cnt < kf) & jnp.logical_not(settled)

        def fix_body(c):
            t, cnt = c
            below = reduce_blocks(lambda m, x, k0: jnp.maximum(m, fold_max(jnp.where(x < t, x, -BIG))),
                                  jnp.full((SUBLANES, C), -BIG, F32))
            nt = jnp.where(unsat(cnt), jnp.max(below, axis=0, keepdims=True), t)
            return nt, count(lambda x, k0: x >= nt)

        t2, cge2 = lax.while_loop(lambda c: any_col(unsat(c[1])), fix_body, (hi2, jnp.zeros((1, C), F32)))
        cgt2 = count(lambda x, k0: x > t2)
        return jnp.where(missed, t2, thr), jnp.where(missed, cge2, cge), jnp.where(missed, cgt2, cgt)

    thr, cge, cgt = lax.cond(any_col(missed), slow, lambda _: (thr, cge, cgt), 0)
    need = kf - cgt
    ties = (cge > kf) & jnp.logical_not(small)

    def write_plain(_):
        def body(j, _):
            k0 = pl.multiple_of(j * kb, kb)
            S[pl.ds(k0, kb), :] = jnp.where(S[pl.ds(k0, kb), :] >= thr, 0.0, NEG_INF)
            return 0
        return lax.fori_loop(0, nblk, body, 0)

    def write_ties(_):
        tri = jnp.where(lax.broadcasted_iota(jnp.int32, (kb, kb), 0) >= lax.broadcasted_iota(jnp.int32, (kb, kb), 1),
                        1.0, 0.0).astype(BF16)

        def body(j, seen):
            k0 = pl.multiple_of(j * kb, kb)
            x = S[pl.ds(k0, kb), :]
            eq = x == thr
            rank = seen + _dot(tri, jnp.where(eq, 1.0, 0.0).astype(BF16))
            S[pl.ds(k0, kb), :] = jnp.where((x > thr) | (eq & (rank <= need)), 0.0, NEG_INF)
            return rank[kb - 1:kb, :]
        lax.fori_loop(0, nblk, body, jnp.zeros((1, C), F32))
        return 0

    lax.cond(any_col(ties), write_ties, write_plain, 0)


def _dsa_kernel(iqT_ref, ik_ref, smT_ref, qT_ref, k_ref, vT_ref, o_ref, S, qm_s, acc_s, m_s, s_s, p_s,
                *, tq, kb, topk):
    q0 = pl.program_id(1) * tq
    n_blk = (q0 + tq - 1) // kb + 1
    kio = lax.broadcasted_iota(jnp.int32, (kb, tq), 0)
    qpos = q0 + lax.broadcasted_iota(jnp.int32, (kb, tq), 1)
    wc = smT_ref[SM_IW:SM_IW + IDX_HEADS, :] * (IDX_W_SCALE * IDX_SCALE)

    def scores(j, stats):
        k0 = pl.multiple_of(j * kb, kb)
        kblk = ik_ref[pl.ds(k0, kb), :]
        sc = jnp.zeros((kb, tq), F32)
        for hh in range(IDX_HEADS):
            d = _dot(kblk, iqT_ref[hh * LANES:(hh + 1) * LANES, :])
            sc = sc + wc[hh:hh + 1, :] * jnp.maximum(d, 0.0)
        sc = jnp.where(k0 + kio <= qpos, sc, NEG_INF)
        S[pl.ds(k0, kb), :] = sc
        return _score_stats(stats, sc)

    stats = lax.fori_loop(0, n_blk, scores, _score_stats_init(tq))

    nadm = (q0 + 1 + lax.broadcasted_iota(jnp.int32, (1, tq), 1)).astype(F32)
    _topk_to_bias(S, n_blk, kb, tq, topk, nadm, stats)

    _init_heads(qT_ref, qm_s, acc_s, m_s)
    kcol =lax.broadcasted_iota(jnp.int32, (kb, 1), 0)

    def attend(j, _):
        k0 = pl.multiple_of(j * kb, kb)
        sel = S[pl.ds(k0, kb), :]
        kpos = (k0 + kcol).astype(F32)

        def bias(h, s):
            return s + (sel + (ALIBI[h] * LOG2E) * kpos)

        _attend_block(k_ref[pl.ds(k0, kb), :], vT_ref, k0, kb, qm_s, acc_s, m_s, s_s, p_s, bias)
        return 0

    lax.fori_loop(0, n_blk, attend, 0)
    _finish_heads(o_ref, acc_s)


def _dsa(iqT, ikp, smT, qT, k, vT, tq, kb, topk):
    B, W, T = qT.shape
    col = lambda r: pl.BlockSpec((None, r, tq), lambda b, i: (b, 0, i))
    return pl.pallas_call(
        functools.partial(_dsa_kernel, tq=tq, kb=kb, topk=topk),
        grid=(B, T // tq),
        in_specs=[col(IDX_HEADS * LANES), _resident((T, LANES)), col(LANES), col(W),
                  _resident((T, W)), _resident((ATT_HEADS * VROWS, T))],
        out_specs=col(W),
        out_shape=jax.ShapeDtypeStruct((B, W, T), F32),
        scratch_shapes=[pltpu.VMEM((T, tq), F32)] + _att_scratch(tq, kb),
        compiler_params=_params(("arbitrary", "arbitrary")),
        name="dsa",
    )(iqT, ikp, smT, qT, k, vT)


def _sa_kernel(pt_ref, q_ref, w_ref, cfn_ref, fb_ref, ikn_ref, *rest, pps, tn):
    pg_refs = rest[:pps]
    s_ref, f_ref, sn_ref, fn_ref, lfn_ref, carry_s = rest[pps:]
    step = pl.program_id(1)

    @pl.when(step == 0)
    def _():
        carry_s[...] = jnp.zeros_like(carry_s)

    qp = q_ref[...]
    q3 = qp[:, 0:3 * IDX_HD]
    w = w_ref[...] * IDX_W_SCALE
    nq = qp.shape[0] // IDX_HEADS

    def scores(kT):
        kh, kl = _split(kT)
        d = _dot(q3, jnp.concatenate([kh, kh, kl], axis=0))
        r = jnp.maximum(d * IDX_SCALE, 0.0) * w
        return jnp.sum(r.reshape(nq, IDX_HEADS, kT.shape[1]), axis=1)

    s_ref[...] = scores(jnp.concatenate([r[0:IDX_HD, :] for r in pg_refs], axis=1))
    cs_all = _lane_cumsum(jnp.concatenate([r[IDX_HD:IDX_HD + ATT_HEADS, :] for r in pg_refs], axis=0))
    carry = carry_s[...]
    for i in range(pps):
        cs = cs_all[i * ATT_HEADS:(i + 1) * ATT_HEADS, :] + carry
        f_ref[:, i * PAGE:(i + 1) * PAGE] = cs
        carry = cs[:, PAGE - 1:PAGE]
    carry_s[...] = carry

    @pl.when(step == pl.num_programs(1) - 1)
    def _():
        lane = lax.broadcasted_iota(jnp.int32, (ATT_HEADS, PAGE), 1)
        lfn = _log_sigmoid(cfn_ref[...] + fb_ref[...])
        lfn_ref[...] = lfn
        fn_ref[...] = _lane_cumsum(jnp.where(lane < tn, lfn, 0.0)) + carry
        col = lax.broadcasted_iota(jnp.int32, (nq, PAGE), 1)
        rowq = lax.broadcasted_iota(jnp.int32, (nq, PAGE), 0)
        sn_ref[...] = jnp.where((col <= rowq) & (col < tn), scores(ikn_ref[...]), NEG_INF)


def _sa(page_table, l, qrows, wrows, cfn, fb, ikn, idx_pool, pps, tn):
    B, n_pages = page_table.shape
    nsteps = n_pages // pps
    P = n_pages * PAGE
    nq = qrows.shape[1] // IDX_HEADS
    per_b = lambda shape: pl.BlockSpec((None,) + shape, lambda b, s, pt: (b,) + (0,) * len(shape))
    pg_specs = [pl.BlockSpec((None, None, IDX_HD + ATT_HEADS, PAGE),
                             lambda b, s, pt, i=i: (l, pt[b, s * pps + i], 0, 0)) for i in range(pps)]
    gs = pltpu.PrefetchScalarGridSpec(
        num_scalar_prefetch=1,
        grid=(B, nsteps),
        in_specs=[per_b((nq * IDX_HEADS, LANES)), per_b((nq * IDX_HEADS, 1)), per_b((ATT_HEADS, PAGE)),
                  pl.BlockSpec((ATT_HEADS, 1), lambda b, s, pt: (0, 0)), per_b((IDX_HD, PAGE))] + pg_specs,
        out_specs=[pl.BlockSpec((None, nq, pps * PAGE), lambda b, s, pt: (b, 0, s)),
                   pl.BlockSpec((None, ATT_HEADS, pps * PAGE), lambda b, s, pt: (b, 0, s)),
                   per_b((nq, PAGE)), per_b((ATT_HEADS, PAGE)), per_b((ATT_HEADS, PAGE))],
        scratch_shapes=[pltpu.VMEM((ATT_HEADS, 1), F32)])
    return pl.pallas_call(
        functools.partial(_sa_kernel, pps=pps, tn=tn),
        grid_spec=gs,
        out_shape=[jax.ShapeDtypeStruct((B, nq, P), F32), jax.ShapeDtypeStruct((B, ATT_HEADS, P), F32),
                   jax.ShapeDtypeStruct((B, nq, PAGE), F32), jax.ShapeDtypeStruct((B, ATT_HEADS, PAGE), F32),
                   jax.ShapeDtypeStruct((B, ATT_HEADS, PAGE), F32)],
        compiler_params=_params(("arbitrary", "arbitrary")),
        name="sample_scores",
    )(page_table, qrows, wrows, cfn, fb, ikn, *([idx_pool] * pps))


def _sb_kernel(s_ref, o_ref, *, kb, nblk, topk, past, nq):
    C = s_ref.shape[1]
    o_ref[...] = s_ref[...]
    nadm = (past + 1 + lax.broadcasted_iota(jnp.int32, (1, C), 1) % nq).astype(F32)
    _topk_to_bias(o_ref, nblk, kb, C, topk, nadm)


def _sb(s_allT, topk, past, nq):
    Wt, C = s_allT.shape
    nl = Wt // LANES
    div = max(d for d in range(1, 9) if nl % d == 0)
    kb = div * LANES
    return pl.pallas_call(
        functools.partial(_sb_kernel, kb=kb, nblk=Wt // kb, topk=topk, past=past, nq=nq),
        grid=(1,),
        in_specs=[pl.BlockSpec((Wt, C), lambda i: (0, 0))],
        out_specs=pl.BlockSpec((Wt, C), lambda i: (0, 0)),
        out_shape=jax.ShapeDtypeStruct((Wt, C), F32),
        compiler_params=_params(("arbitrary",)),
        name="sample_topk",
    )(s_allT)


def _sc_kernel(pt_ref, fq_ref, dq_ref, fqc_ref, fp_ref, bp_ref, fn_ref, bn_ref,
               ckn_ref, cvn_ref, dkn_ref, dvn_ref, *rest, pps, past, tn, nq):
    fk_refs, fv_refs = rest[0:pps], rest[pps:2 * pps]
    dk_refs, dv_refs = rest[2 * pps:3 * pps], rest[3 * pps:4 * pps]
    yc_ref, yd_ref, qf_s, qd_s, accf, mf, lf, accd, md, ld = rest[4 * pps:]
    step = pl.program_id(1)
    R = ATT_HEADS * SUBLANES

    def stack_heads(q8):
        return jnp.concatenate([jnp.where(_head_mask(h), q8, jnp.zeros_like(q8)) for h in range(ATT_HEADS)], axis=0)

    @pl.when(step == 0)
    def _():
        qf_s[...] = stack_heads(fq_ref[...])
        qd_s[...] = stack_heads(dq_ref[...])
        for acc, m, l in ((accf, mf, lf), (accd, md, ld)):
            acc[...] = jnp.zeros_like(acc)
            m[...] = jnp.full_like(m, M_INIT)
            l[...] = jnp.zeros_like(l)

    def update(q_s, kT, vT, bias, acc, m, l):
        s = _dot(q_s[...], kT) + bias * LOG2E
        m_prev = m[...]
        m_new = jnp.maximum(m_prev, jnp.max(s, axis=1, keepdims=True))
        p = jnp.exp2(s - m_new)
        alpha = jnp.exp2(m_prev - m_new)
        l[...] = alpha * l[...] + jnp.sum(p, axis=1, keepdims=True)
        m[...] = m_new
        acc[...] = alpha * acc[...] + _nt(p.astype(BF16), vT)

    def per_head_rows(x4):
        return jnp.concatenate([jnp.broadcast_to(x4[h:h + 1, :], (SUBLANES, x4.shape[1]))
                                for h in range(ATT_HEADS)], axis=0)

    def per_query_rows(xq):
        r8 = lax.broadcasted_iota(jnp.int32, (SUBLANES, xq.shape[1]), 0)
        x8 = jnp.zeros((SUBLANES, xq.shape[1]), F32)
        for qq in range(nq):
            x8 = jnp.where(r8 == qq, jnp.broadcast_to(xq[qq:qq + 1, :], x8.shape), x8)
        return jnp.concatenate([x8] * ATT_HEADS, axis=0)

    rowi = lax.broadcasted_iota(jnp.int32, (R, 1), 0)
    qidx = rowi % SUBLANES
    slope = jnp.zeros((R, 1), F32)
    for h in range(ATT_HEADS):
        slope = jnp.where(rowi // SUBLANES == h, ALIBI[h], slope)

    def slab(ref):
        return ref[...].reshape(GROUP_W, ref.shape[-1]).astype(BF16)

    def blocks(refs):
        return jnp.concatenate([slab(r) for r in refs], axis=1)

    kw = pps * PAGE
    col = lax.broadcasted_iota(jnp.int32, (R, kw), 1)
    update(qf_s, blocks(fk_refs), blocks(fv_refs), fqc_ref[...] - per_head_rows(fp_ref[...]), accf, mf, lf)
    dist = (past + qidx - (step * kw + col)).astype(F32)
    update(qd_s, blocks(dk_refs), blocks(dv_refs), per_query_rows(bp_ref[...]) - slope * dist, accd, md, ld)

    @pl.when(step == pl.num_programs(1) - 1)
    def _():
        coln = lax.broadcasted_iota(jnp.int32, (R, PAGE), 1)
        ok = (coln <= qidx) & (coln < tn)
        bias_f = jnp.where(ok, fqc_ref[...] - per_head_rows(fn_ref[...]), NEG_INF)
        update(qf_s, slab(ckn_ref), slab(cvn_ref), bias_f, accf, mf, lf)
        distn = (qidx - coln).astype(F32)
        bias_d = jnp.where(coln < tn, per_query_rows(bn_ref[...]), NEG_INF) - slope * distn
        update(qd_s, slab(dkn_ref), slab(dvn_ref), bias_d, accd, md, ld)
        for acc, l, out in ((accf, lf, yc_ref), (accd, ld, yd_ref)):
            o = acc[...] / l[...]
            y = jnp.zeros((SUBLANES, GROUP_W), F32)
            for h in range(ATT_HEADS):
                y = jnp.where(_head_mask(h), o[h * SUBLANES:(h + 1) * SUBLANES, :], y)
            out[...] = y


def _sc(page_table, l, fq8, dq8, fqc, f_past, b_past, f_new, b_new, ckn, cvn, dkn, dvn,
        fk_pool, fv_pool, dk_pool, dv_pool, pps, tn):
    B, n_pages = page_table.shape
    nsteps = n_pages // pps
    past = n_pages * PAGE
    nq = b_past.shape[1]
    W = GROUP_W
    per_b = lambda shape: pl.BlockSpec((None,) + shape, lambda b, s, pt: (b,) + (0,) * len(shape))
    page = lambda i: pl.BlockSpec((None, None, ATT_HEADS, ATT_HD, PAGE),
                                  lambda b, s, pt, i=i: (l, pt[b, s * pps + i], 0, 0, 0))
    pages = [page(i) for i in range(pps)]
    newpage = per_b((ATT_HEADS, ATT_HD, PAGE))
    R = ATT_HEADS * SUBLANES
    gs = pltpu.PrefetchScalarGridSpec(
        num_scalar_prefetch=1,
        grid=(B, nsteps),
        in_specs=[per_b((SUBLANES, W)), per_b((SUBLANES, W)), per_b((R, 1)),
                  pl.BlockSpec((None, ATT_HEADS, pps * PAGE), lambda b, s, pt: (b, 0, s)),
                  pl.BlockSpec((None, nq, pps * PAGE), lambda b, s, pt: (b, 0, s)),
                  per_b((ATT_HEADS, PAGE)), per_b((nq, PAGE)),
                  newpage, newpage, newpage, newpage] + pages * 4,
        out_specs=[per_b((SUBLANES, W)), per_b((SUBLANES, W))],
        scratch_shapes=[pltpu.VMEM((R, W), BF16), pltpu.VMEM((R, W), BF16),
                        pltpu.VMEM((R, W), F32), pltpu.VMEM((R, 1), F32), pltpu.VMEM((R, 1), F32),
                        pltpu.VMEM((R, W), F32), pltpu.VMEM((R, 1), F32), pltpu.VMEM((R, 1), F32)])
    return pl.pallas_call(
        functools.partial(_sc_kernel, pps=pps, past=past, tn=tn, nq=nq),
        grid_spec=gs,
        out_shape=[jax.ShapeDtypeStruct((B, SUBLANES, W), F32)] * 2,
        compiler_params=_params(("arbitrary", "arbitrary")),
        name="sample_attn",
    )(page_table, fq8, dq8, fqc, f_past, b_past, f_new, b_new, ckn, cvn, dkn, dvn,
      *([fk_pool] * pps), *([fv_pool] * pps), *([dk_pool] * pps), *([dv_pool] * pps))


def _post_kernel(x_ref, ya_ref, yb_ref, yc_ref, yd_ref, g1_ref, sh2_ref, sc2_ref, g2_ref,
                 gn_ref, wo_ref, npost_ref, nfpre_ref, wg_ref, wu_ref, wd_ref, nfpost_ref, o_ref, *, chunks):
    o = None
    for i, r in enumerate((ya_ref, yb_ref, yc_ref, yd_ref)):
        part = _rms(r[...], gn_ref[:, i * GROUP_W:(i + 1) * GROUP_W]).astype(BF16)
        d = _dot(part, wo_ref[i * GROUP_W:(i + 1) * GROUP_W, :])
        o = d if o is None else o + d
    x1 = x_ref[...] + g1_ref[...] * _rms(o, npost_ref[...])
    h2 = (_rms(x1, nfpre_ref[...]) * (1.0 + sc2_ref[...]) + sh2_ref[...]).astype(BF16)
    f = None
    for c0, c1 in chunks:
        gate = _dot(h2, wg_ref[:, c0:c1])
        up = _dot(h2, wu_ref[:, c0:c1])
        d = _dot((gate * _sigmoid(gate) * up).astype(BF16), wd_ref[c0:c1, :])
        f = d if f is None else f + d
    o_ref[...] = x1 + g2_ref[...] * _rms(f, nfpost_ref[...])


def _post(x, ya, yb, yc, yd, mod, tiles_per_mod, pw, tm):
    N, D = x.shape
    H = pw["wg"].shape[1]
    rows_mod = mod.shape[1]
    step = 1024
    chunks = tuple((c, min(c + step, H)) for c in range(0, H, step))
    tok = lambda w: pl.BlockSpec((tm, w), lambda i: (i, 0))
    const = lambda shape: pl.BlockSpec(shape, lambda i: (0,) * len(shape), pipeline_mode=pl.Buffered(1))
    return pl.pallas_call(
        functools.partial(_post_kernel, chunks=chunks),
        grid=(N // tm,),
        in_specs=[tok(D), tok(GROUP_W), tok(GROUP_W), tok(GROUP_W), tok(GROUP_W),
                  _mod_spec(rows_mod, D, tiles_per_mod, 2), _mod_spec(rows_mod, D, tiles_per_mod, 3),
                  _mod_spec(rows_mod, D, tiles_per_mod, 4), _mod_spec(rows_mod, D, tiles_per_mod, 5),
                  const((1, D)), const((D, D)), const((1, D)), const((1, D)),
                  const((D, H)), const((D, H)), const((H, D)), const((1, D))],
        out_specs=tok(D),
        out_shape=jax.ShapeDtypeStruct((N, D), F32),
        compiler_params=_params(("arbitrary",)),
        name="post",
    )(x, ya, yb, yc, yd, mod, mod, mod, mod, pw["gn"], pw["wo"], pw["npost"], pw["nfpre"],
      pw["wg"], pw["wu"], pw["wd"], pw["nfpost"])


def _block_diag(blocks):
    G, r, c = blocks.shape
    eye = jnp.eye(G, dtype=blocks.dtype)
    return (blocks[:, :, None, :] * eye[:, None, :, None]).reshape(G * r, G * c)


def _cat_weight(w):
    D = w.shape[0]
    widths = (GROUP_W, GROUP_W, GROUP_W, GROUP_W, GROUP_W, GROUP_W, ATT_HEADS, GROUP_W, GROUP_W, GROUP_W,
              IDX_HEADS * IDX_HD, IDX_HD, IDX_HEADS)
    offs = [0]
    for wd in widths:
        offs.append(offs[-1] + wd)
    pc = [w[:, offs[i]:offs[i + 1]] for i in range(len(widths))]
    a_x, a_g, b_u, c_q, c_k, c_v, c_f, d_q, d_k, d_v, i_q, i_k, i_w = pc
    iq_rep = jnp.tile(i_q.reshape(D, IDX_HEADS, 1, IDX_HD), (1, 1, LANES // IDX_HD, 1)).reshape(D, IDX_HEADS * LANES)
    ik_rep = jnp.tile(i_k, (1, LANES // IDX_HD))
    small = jnp.concatenate([i_k, c_f, i_w, jnp.zeros((D, LANES - IDX_HD - ATT_HEADS - IDX_HEADS), w.dtype)], axis=1)
    wcat = jnp.concatenate([a_x, a_g, b_u, c_q, c_k, c_v, d_q, d_k, d_v], axis=1).astype(BF16)
    widx = jnp.concatenate([iq_rep, ik_rep, small], axis=1)
    hi = lax.reduce_precision(widx, exponent_bits=8, mantissa_bits=7)
    hi_b = hi.astype(BF16)
    return wcat, jnp.concatenate([hi_b, hi_b, (widx - hi).astype(BF16)], axis=0)


def _layer_weights(l, p):
    row = lambda a: a[l].reshape(1, -1)
    lw = dict(conv_w=p["lru_conv_w"][l], conv_b=row(p["lru_conv_b"]),
              wr=_block_diag(p["lru_wr"][l]).astype(BF16), br=row(p["lru_br"]),
              wi=_block_diag(p["lru_wi"][l]).astype(BF16), bi=row(p["lru_bi"]), lam=row(p["lru_lambda"]))
    sw = dict(ldt=jnp.repeat(p["s5_log_dt"][l], S5_N).reshape(1, S5_STATE),
              are=p["s5_a_re"][l].reshape(1, S5_STATE), aim=p["s5_a_im"][l].reshape(1, S5_STATE),
              bre=_block_diag(jnp.swapaxes(p["s5_b_re"][l], 1, 2)).astype(BF16),
              bim=_block_diag(jnp.swapaxes(p["s5_b_im"][l], 1, 2)).astype(BF16),
              cre=_block_diag(jnp.swapaxes(p["s5_c_re"][l], 1, 2)).astype(BF16),
              cim=_block_diag(jnp.swapaxes(p["s5_c_im"][l], 1, 2)).astype(BF16),
              d=row(p["s5_d"]), gw=p["s5_glu_w"][l].astype(BF16), gb=row(p["s5_glu_b"]))
    pw = dict(gn=row(p["grp_norm"]), wo=p["w_out"][l].astype(BF16), npost=row(p["norm_mix_post"]),
              nfpre=row(p["norm_ffn_pre"]), wg=p["ffn_w_gate"][l].astype(BF16), wu=p["ffn_w_up"][l].astype(BF16),
              wd=p["ffn_w_down"][l].astype(BF16), nfpost=row(p["norm_ffn_post"]))
    return dict(win=_cat_weight(p["w_in"][l]), npre=row(p["norm_mix_pre"]), lru=lw, s5=sw, post=pw,
                fbias=p["fox_f_bias"][l])


def _largest_tile(n, cap):
    t = min(n, cap)
    while n % t:
        t //= 2
    return t


def _recurrent(ax, ag, bu, lru_buf, lru_h0, s5_r0, s5_i0, wts, t_real):
    B, Tp, _ = ax.shape
    tc = _largest_tile(Tp, 256)
    buf8 = jnp.pad(lru_buf, ((0, 0), (SUBLANES - (CONV_W - 1), 0), (0, 0)))
    ya, lru_h = _lru(ax, ag, buf8, lru_h0.reshape(B, 1, GROUP_W), wts["lru"], tc, t_real)
    yb, s5r, s5i = _s5(bu, s5_r0.reshape(B, 1, S5_STATE), s5_i0.reshape(B, 1, S5_STATE), wts["s5"], tc, t_real)
    return ya, yb, lru_h.reshape(B, GROUP_W), s5r.reshape(B, S5_NG, S5_N), s5i.reshape(B, S5_NG, S5_N)


def _prompt_layer(x, mod, wts, B, T):
    N, D = x.shape
    tm = _largest_tile(T, 512)
    (ax, ag, bu, cq, ckf, ckb, cvf, cvb, dq, dkf, dkb, dvf, dvb, iqp, ikp, sm) = _inproj(
        x, mod, T // tm, wts["npre"], *wts["win"], tm)
    seq = lambda a: a.reshape(B, T, a.shape[-1])
    zeros = lambda *s: jnp.zeros(s, F32)
    ax3 = seq(ax)
    ya, yb, lru_h, s5r, s5i = _recurrent(ax3, seq(ag), seq(bu), zeros(B, CONV_W - 1, GROUP_W), zeros(B, GROUP_W),
                                         zeros(B, S5_NG, S5_N), zeros(B, S5_NG, S5_N), wts, T)
    nb = T // LANES
    cf_rows = jnp.swapaxes(seq(sm)[:, :, SM_CF:SM_CF + ATT_HEADS], 1, 2).reshape(B, ATT_HEADS * nb, LANES)
    fb_rows = jnp.repeat(wts["fbias"], nb).reshape(ATT_HEADS * nb, 1)
    lf_rows, f_rows = _fcum(cf_rows, fb_rows, nb)
    f2_col = jnp.swapaxes(f_rows.reshape(B, ATT_HEADS, T), 1, 2)
    logf = jnp.swapaxes(lf_rows.reshape(B, ATT_HEADS, T), 1, 2)
    tr = lambda a: jnp.swapaxes(seq(a), 1, 2)
    tq = kb = _largest_tile(T, 512)
    yc = jnp.swapaxes(_fox(tr(cq), seq(ckb), _with_ones_rows(tr(cvb)), f2_col, tq, kb), 1, 2)
    topk = max(1, min(DSA_TOPK_MAX, T // 4))
    yd = jnp.swapaxes(_dsa(tr(iqp), seq(ikp), tr(sm), tr(dq), seq(dkb), _with_ones_rows(tr(dvb)), tq, kb, topk),
                      1, 2)
    flat = lambda a: a.reshape(N, GROUP_W)
    x2 = _post(x, flat(ya), flat(yb), flat(yc), flat(yd), mod, T // tm, wts["post"], tm)
    heads = lambda a: a.reshape(B, T, ATT_HEADS, ATT_HD)
    state = dict(lru_h=lru_h, lru_conv=ax3[:, T - (CONV_W - 1):, :], s5_re=s5r, s5_im=s5i,
                 fox_k=heads(ckf), fox_v=heads(cvf), fox_logf=logf, dsa_k=heads(dkf), dsa_v=heads(dvf),
                 dsa_kidx=seq(sm)[:, :, SM_IK:SM_IK + IDX_HD])
    return x2, state


def _sample_layer(x, mod, wts, l, B, T, past, caches, page_table):
    N, D = x.shape
    (ax, ag, bu, cq, ckf, ckb, cvf, cvb, dq, dkf, dkb, dvf, dvb, iqp, ikp, sm) = _inproj(
        x, mod, 1, wts["npre"], *wts["win"], N)
    seq = lambda a: a.reshape(B, T, a.shape[-1])
    tp = -(-T // SUBLANES) * SUBLANES
    padt = lambda a, n=tp: jnp.pad(seq(a), ((0, 0), (0, n - T), (0, 0)))
    ax3 = seq(ax)
    ya, yb, lru_h, s5r, s5i = _recurrent(padt(ax), padt(ag), padt(bu), past["lru_conv"], past["lru_h"],
                                         past["s5_re"], past["s5_im"], wts, T)
    sm3 = seq(sm)
    n_pages = page_table.shape[1]
    plen = n_pages * PAGE
    pps = _largest_tile(n_pages, 32)
    iq_rows = iqp.reshape(B, T * IDX_HEADS, LANES)
    wrows = sm3[:, :, SM_IW:SM_IW + IDX_HEADS].reshape(B, T * IDX_HEADS, 1)
    cfn = jnp.pad(jnp.swapaxes(sm3[:, :, SM_CF:SM_CF + ATT_HEADS], 1, 2), ((0, 0), (0, 0), (0, PAGE - T)))
    ikn = jnp.pad(jnp.swapaxes(sm3[:, :, SM_IK:SM_IK + IDX_HD], 1, 2), ((0, 0), (0, 0), (0, PAGE - T)))
    s_past, f_past, s_new, f_new, lf_new = _sa(page_table, l, iq_rows, wrows, cfn,
                                               wts["fbias"].reshape(ATT_HEADS, 1), ikn, caches["idx"],
                                               _largest_tile(n_pages, 64), T)
    topk = max(1, min(DSA_TOPK_MAX, (plen + T) // 4))
    s_allT = jnp.transpose(jnp.concatenate([s_past, s_new], axis=2), (2, 0, 1)).reshape(plen + PAGE, B * T)
    bias_all = jnp.transpose(_sb(s_allT, topk, plen, T).reshape(plen + PAGE, B, T), (1, 2, 0))
    fqc = jnp.pad(f_new[:, :, :T], ((0, 0), (0, 0), (0, SUBLANES - T))).reshape(B, ATT_HEADS * SUBLANES, 1)
    newpage = lambda a: jnp.pad(jnp.transpose(a.reshape(B, T, ATT_HEADS, ATT_HD), (0, 2, 3, 1)),
                                ((0, 0), (0, 0), (0, 0), (0, PAGE - T)))
    yc8, yd8 = _sc(page_table, l, padt(cq, SUBLANES), padt(dq, SUBLANES), fqc, f_past, bias_all[:, :, :plen],
                   f_new, bias_all[:, :, plen:], newpage(ckb), newpage(cvb), newpage(dkb), newpage(dvb),
                   caches["fox_k"], caches["fox_v"], caches["dsa_k"], caches["dsa_v"], pps, T)
    flat = lambda a: a[:, :T, :].reshape(N, GROUP_W)
    x2 = _post(x, flat(ya), flat(yb), flat(yc8), flat(yd8), mod, 1, wts["post"], N)
    heads = lambda a: a.reshape(B, T, ATT_HEADS, ATT_HD)
    state = dict(lru_h=lru_h, lru_conv=ax3[:, T - (CONV_W - 1):, :], s5_re=s5r, s5_im=s5i,
                 fox_k=heads(ckf), fox_v=heads(cvf), fox_logf=jnp.swapaxes(lf_new[:, :, :T], 1, 2),
                 dsa_k=heads(dkf), dsa_v=heads(dvf), dsa_kidx=sm3[:, :, SM_IK:SM_IK + IDX_HD])
    return x2, state


def kernel(x_prompt, x_sample, state_lru_h, state_lru_conv, state_s5_re, state_s5_im, cache_fox_k, cache_fox_v, cache_fox_logf, cache_dsa_k, cache_dsa_v, cache_dsa_kidx, page_table, c_prompt, c_sample, ada_w, ada_b, norm_mix_pre, norm_mix_post, norm_ffn_pre, norm_ffn_post, w_in, lru_conv_w, lru_conv_b, lru_wr, lru_br, lru_wi, lru_bi, lru_lambda, s5_log_dt, s5_a_re, s5_a_im, s5_b_re, s5_b_im, s5_c_re, s5_c_im, s5_d, s5_glu_w, s5_glu_b, fox_f_bias, grp_norm, w_out, ffn_w_gate, ffn_w_up, ffn_w_down):
    p = dict(norm_mix_pre=norm_mix_pre, norm_mix_post=norm_mix_post, norm_ffn_pre=norm_ffn_pre,
             norm_ffn_post=norm_ffn_post, w_in=w_in, lru_conv_w=lru_conv_w, lru_conv_b=lru_conv_b, lru_wr=lru_wr,
             lru_br=lru_br, lru_wi=lru_wi, lru_bi=lru_bi, lru_lambda=lru_lambda, s5_log_dt=s5_log_dt,
             s5_a_re=s5_a_re, s5_a_im=s5_a_im, s5_b_re=s5_b_re, s5_b_im=s5_b_im, s5_c_re=s5_c_re, s5_c_im=s5_c_im,
             s5_d=s5_d, s5_glu_w=s5_glu_w, s5_glu_b=s5_glu_b, fox_f_bias=fox_f_bias, grp_norm=grp_norm,
             w_out=w_out, ffn_w_gate=ffn_w_gate, ffn_w_up=ffn_w_up, ffn_w_down=ffn_w_down)
    B, T, D = x_prompt.shape
    Bs, Ts, _ = x_sample.shape
    L = ada_w.shape[0]
    n_pool = cache_fox_k.shape[1]
    assert T % LANES == 0 and CONV_W - 1 <= Ts <= SUBLANES and cache_fox_k.shape[2] == PAGE

    c_all = jnp.concatenate([c_prompt, c_sample], axis=0)
    c_all = jnp.pad(c_all, ((0, -c_all.shape[0] % SUBLANES), (0, 0)))
    mod_all = _ada(c_all, ada_w, ada_b)

    pool = lambda a: jnp.transpose(a, (0, 1, 3, 4, 2))
    caches = dict(fox_k=pool(cache_fox_k), fox_v=pool(cache_fox_v), dsa_k=pool(cache_dsa_k), dsa_v=pool(cache_dsa_v),
                  idx=jnp.concatenate([jnp.swapaxes(cache_dsa_kidx, 2, 3), jnp.swapaxes(cache_fox_logf, 2, 3)], axis=2))

    xp = x_prompt.reshape(B * T, D)
    xs = x_sample.reshape(Bs * Ts, D)
    st_p, st_s = [], []
    for l in range(L):
        wts = _layer_weights(l, p)
        mod_p = mod_all[l, :B].reshape(B, 1, 6 * D)
        mod_s = jnp.repeat(mod_all[l, B:B + Bs], Ts, axis=0).reshape(1, Bs * Ts, 6 * D)
        xp, new_p = _prompt_layer(xp, mod_p, wts, B, T)
        past = dict(lru_h=state_lru_h[l], lru_conv=state_lru_conv[l], s5_re=state_s5_re[l], s5_im=state_s5_im[l])
        xs, new_s = _sample_layer(xs, mod_s, wts, l, Bs, Ts, past, caches, page_table)
        st_p.append(new_p)
        st_s.append(new_s)

    stk = lambda outs, name: jnp.stack([o[name] for o in outs])
    names = ("lru_h", "lru_conv", "s5_re", "s5_im", "fox_k", "fox_v", "fox_logf", "dsa_k", "dsa_v", "dsa_kidx")
    res = [xp.reshape(B, T, D), xs.reshape(Bs, Ts, D)]
    for name in names:
        res += [stk(st_p, name), stk(st_s, name)]
    return tuple(res)
```
